```python
import math
import jax, jax.numpy as jnp
from jax import lax
import numpy as np

D_MODEL = 2048
BATCH = 4
SEQ = 4096
DEPTH = 1

HEAD_DIM = 128
N_MEM = 256
GRID_W = 64
DIL_PAIRS = ((128, 1), (512, 4), (2048, 16))
A_HEADS_PER_GROUP = 2
A_HEADS = A_HEADS_PER_GROUP * len(DIL_PAIRS)
A_BLOCK = 64
B_Q_HEADS = 6
B_KV_HEADS = 2
B_QBLOCK = 128
ROPE_THETA = 10000.0
C_HEADS = 4
N_BRANCH = 3
REL_BUCKETS = 32
REL_MAX_DIST = 1024
D_FF = -(-8 * D_MODEL // (3 * 256)) * 256
EPS = 1e-6
NEG = -1e30

A_W = A_HEADS * HEAD_DIM
A_OUT_W = A_HEADS_PER_GROUP * HEAD_DIM
B_QW = B_Q_HEADS * HEAD_DIM
B_KVW = B_KV_HEADS * HEAD_DIM
C_W = C_HEADS * HEAD_DIM
IN_SPLITS = (A_W, A_W, A_W, B_QW, B_KVW, B_KVW, C_W, N_BRANCH * D_MODEL)
IN_W = sum(IN_SPLITS)

kernel_name = "hybrid_gated_dilated_gqa_mem_encoder"


def rms_norm(x, g):
    xf = x.astype(jnp.float32)
    y = xf * lax.rsqrt(jnp.mean(xf * xf, axis=-1, keepdims=True) + EPS)
    return (y * g.astype(jnp.float32)).astype(x.dtype)


def split_heads(t, h):
    return t.reshape(t.shape[0], t.shape[1], h, HEAD_DIM)


def t5_bucket(rel):
    nb = REL_BUCKETS // 2
    ret = jnp.where(rel > 0, nb, 0)
    n = jnp.abs(rel)
    max_exact = nb // 2
    large = max_exact + (jnp.log(jnp.maximum(n, 1).astype(jnp.float32) / max_exact)
                         / math.log(REL_MAX_DIST / max_exact) * (nb - max_exact)).astype(jnp.int32)
    large = jnp.minimum(large, nb - 1)
    return ret + jnp.where(n < max_exact, n, large)


def dilated_group(q, k, v, bias_tab, window, dil):
    b, s, h, dh = q.shape
    L = s // dil
    radius = window // (2 * dil)
    n = b * dil

    def to_res(t):
        return t.reshape(b, L, dil, h, dh).transpose(0, 2, 1, 3, 4).reshape(n, L, h, dh)

    qr, kr, vr = to_res(q), to_res(k), to_res(v)
    nb = -(-L // A_BLOCK)
    lp = nb * A_BLOCK
    side = -(-radius // A_BLOCK)
    pad = side * A_BLOCK
    wk = (2 * side + 1) * A_BLOCK
    qb = jnp.pad(qr, ((0, 0), (0, lp - L), (0, 0), (0, 0))).reshape(n, nb, A_BLOCK, h, dh)

    def band(t):
        tp = jnp.pad(t, ((0, 0), (pad, lp - L + pad), (0, 0), (0, 0))).reshape(n, nb + 2 * side, A_BLOCK, h, dh)
        return jnp.concatenate([tp[:, i:i + nb] for i in range(2 * side + 1)], axis=2)

    kb, vb = band(kr), band(vr)
    rel = (jnp.arange(wk) - pad)[None, :] - jnp.arange(A_BLOCK)[:, None]
    bias = bias_tab[t5_bucket(rel * dil)].transpose(2, 0, 1).astype(jnp.float32)
    k_pos = jnp.arange(nb)[:, None] * A_BLOCK - pad + jnp.arange(wk)[None, :]
    valid = (jnp.abs(rel) <= radius)[None] & ((k_pos >= 0) & (k_pos < L))[:, None, :]
    sc = jnp.einsum('nbqhd,nbkhd->nbhqk', qb, kb, preferred_element_type=jnp.float32) / math.sqrt(dh)
    sc = jnp.where(valid[None, :, None], sc + bias[None, None], NEG)
    m = jnp.max(sc, axis=-1, keepdims=True)
    p = jnp.exp(sc - m)
    l = jnp.sum(p, axis=-1, keepdims=True)
    o = jnp.einsum('nbhqk,nbkhd->nbqhd', p, vb.astype(jnp.float32))
    o = o / l[..., 0].transpose(0, 1, 3, 2)[..., None]
    lse = (m + jnp.log(l))[..., 0].transpose(0, 1, 3, 2)
    o = o.reshape(n, lp, h, dh)[:, :L]
    lse = lse.reshape(n, lp, h)[:, :L]
    o = o.reshape(b, dil, L, h, dh).transpose(0, 2, 1, 3, 4).reshape(b, s, h, dh)
    lse = lse.reshape(b, dil, L, h).transpose(0, 2, 1, 3).reshape(b, s, h)
    return o, lse


def axial_rope_tables(s):
    rows = s // GRID_W
    r = jnp.repeat(jnp.arange(rows), GRID_W).astype(jnp.float32)
    c = jnp.tile(jnp.arange(GRID_W), rows).astype(jnp.float32)
    nf = HEAD_DIM // 4
    inv = ROPE_THETA ** (-jnp.arange(nf, dtype=jnp.float32) / nf)
    ang = jnp.concatenate([r[:, None] * inv, c[:, None] * inv], axis=-1)
    return jnp.cos(ang), jnp.sin(ang)


def apply_rope(x, cos, sin):
    b, s, h, dh = x.shape
    xf = x.astype(jnp.float32).reshape(b, s, h, dh // 2, 2)
    x0, x1 = xf[..., 0], xf[..., 1]
    c, sn = cos[None, :, None, :], sin[None, :, None, :]
    out = jnp.stack([x0 * c - x1 * sn, x0 * sn + x1 * c], axis=-1)
    return out.reshape(b, s, h, dh).astype(x.dtype)


def gqa_blocks(q, k, v):
    b, s, hq, dh = q.shape
    g = hq // B_KV_HEADS
    nblk = s // B_QBLOCK
    qb = q.reshape(b, nblk, B_QBLOCK, B_KV_HEADS, g, dh).transpose(1, 0, 2, 3, 4, 5)

    def one_block(qq):
        sc = jnp.einsum('bqkgd,bskd->bkgqs', qq, k, preferred_element_type=jnp.float32) / math.sqrt(dh)
        p = jax.nn.softmax(sc, axis=-1)
        return jnp.einsum('bkgqs,bskd->bqkgd', p, v.astype(jnp.float32)).astype(q.dtype)

    o = lax.map(one_block, qb)
    return o.transpose(1, 0, 2, 3, 4, 5).reshape(b, s, hq * dh)


def setup_inputs(seed: int = 0) -> dict:
    key = jax.random.key(seed)
    ks = jax.random.split(key, 24)
    f32 = jnp.float32

    def w(k, shape, fan_in):
        return jax.random.normal(k, shape, f32) * (fan_in ** -0.5)

    def gain(k, shape):
        return 1.0 + 0.05 * jax.random.normal(k, shape, f32)

    Dp = DEPTH
    return {
        "x": jax.random.normal(ks[0], (BATCH, SEQ, D_MODEL), f32),
        "mem": jax.random.normal(ks[1], (BATCH, N_MEM, D_MODEL), f32),
        "rel_bias": 0.5 * jax.random.normal(ks[2], (REL_BUCKETS, A_HEADS), f32),
        "g_mix": gain(ks[3], (Dp, D_MODEL)),
        "w_in": w(ks[4], (Dp, D_MODEL, IN_W), D_MODEL),
        "g_qa": gain(ks[5], (Dp, HEAD_DIM)),
        "g_ka": gain(ks[6], (Dp, HEAD_DIM)),
        "g_qb": gain(ks[7], (Dp, HEAD_DIM)),
        "g_kb": gain(ks[8], (Dp, HEAD_DIM)),
        "g_mem": gain(ks[9], (Dp, D_MODEL)),
        "w_mem_kv": w(ks[10], (Dp, D_MODEL, 2 * C_W), D_MODEL),
        "g_qc": gain(ks[11], (Dp, HEAD_DIM)),
        "g_kc": gain(ks[12], (Dp, HEAD_DIM)),
        "w_br_a": w(ks[13], (Dp, A_OUT_W, D_MODEL), A_OUT_W),
        "w_br_b": w(ks[14], (Dp, B_QW, D_MODEL), B_QW),
        "w_br_c": w(ks[15], (Dp, C_W, D_MODEL), C_W),
        "w_o": w(ks[16], (Dp, D_MODEL, D_MODEL), D_MODEL),
        "g_ffn": gain(ks[17], (Dp, D_MODEL)),
        "w_ffn_in": w(ks[18], (Dp, D_MODEL, 2 * D_FF), D_MODEL),
        "w_ffn_out": w(ks[19], (Dp, D_FF, D_MODEL), D_FF),
    }


def reference(x, mem, rel_bias, g_mix, w_in, g_qa, g_ka, g_qb, g_kb, g_mem, w_mem_kv,
              g_qc, g_kc, w_br_a, w_br_b, w_br_c, w_o, g_ffn, w_ffn_in, w_ffn_out):
    b, s, _ = x.shape
    cos, sin = axial_rope_tables(s)
    offs = np.cumsum((0,) + IN_SPLITS)
    for layer in range(DEPTH):
        h = rms_norm(x, g_mix[layer])
        proj = h @ w_in[layer]
        qa, ka, va, qb, kb, vb, qc, gt = [proj[..., offs[i]:offs[i + 1]] for i in range(len(IN_SPLITS))]

        qa = rms_norm(split_heads(qa, A_HEADS), g_qa[layer])
        ka = rms_norm(split_heads(ka, A_HEADS), g_ka[layer])
        va = split_heads(va, A_HEADS)
        outs, lses = [], []
        for gi, (win, dil) in enumerate(DIL_PAIRS):
            sl = slice(gi * A_HEADS_PER_GROUP, (gi + 1) * A_HEADS_PER_GROUP)
            o, lse = dilated_group(qa[:, :, sl], ka[:, :, sl], va[:, :, sl], rel_bias[:, sl], win, dil)
            outs.append(o)
            lses.append(lse)
        wgt = jax.nn.softmax(jnp.stack(lses), axis=0)
        oa = jnp.sum(wgt[..., None] * jnp.stack(outs), axis=0).reshape(b, s, A_OUT_W).astype(x.dtype)

        qb = apply_rope(rms_norm(split_heads(qb, B_Q_HEADS), g_qb[layer]), cos, sin)
        kb = apply_rope(rms_norm(split_heads(kb, B_KV_HEADS), g_kb[layer]), cos, sin)
        ob = gqa_blocks(qb, kb, split_heads(vb, B_KV_HEADS))

        mkv = rms_norm(mem, g_mem[layer]) @ w_mem_kv[layer]
        kc = rms_norm(split_heads(mkv[..., :C_W], C_HEADS), g_kc[layer])
        vc = split_heads(mkv[..., C_W:], C_HEADS)
        qc = rms_norm(split_heads(qc, C_HEADS), g_qc[layer])
        sc = jnp.einsum('bqhd,bmhd->bhqm', qc, kc, preferred_element_type=jnp.float32) / math.sqrt(HEAD_DIM)
        pc = jax.nn.softmax(sc, axis=-1)
        oc = jnp.einsum('bhqm,bmhd->bqhd', pc, vc.astype(jnp.float32)).reshape(b, s, C_W).astype(x.dtype)

        gates = jax.nn.sigmoid(gt.astype(jnp.float32)).reshape(b, s, N_BRANCH, D_MODEL)
        merged = (gates[:, :, 0] * (oa @ w_br_a[layer]).astype(jnp.float32)
                  + gates[:, :, 1] * (ob @ w_br_b[layer]).astype(jnp.float32)
                  + gates[:, :, 2] * (oc @ w_br_c[layer]).astype(jnp.float32)).astype(x.dtype)
        x = x + merged @ w_o[layer]

        hf = rms_norm(x, g_ffn[layer]) @ w_ffn_in[layer]
        x = x + (jax.nn.silu(hf[..., :D_FF]) * hf[..., D_FF:]) @ w_ffn_out[layer]
    return x
```

```python
import functools
import math

import numpy as np
import jax
import jax.numpy as jnp
from jax import lax
from jax.experimental import pallas as pl
from jax.experimental.pallas import tpu as pltpu

HEAD_DIM = 128
GRID_W = 64
DIL_PAIRS = ((128, 1), (512, 4), (2048, 16))
A_HEADS_PER_GROUP = 2
A_HEADS = A_HEADS_PER_GROUP * len(DIL_PAIRS)
B_Q_HEADS = 6
B_KV_HEADS = 2
ROPE_THETA = 10000.0
C_HEADS = 4
N_BRANCH = 3
REL_BUCKETS = 32
REL_MAX_DIST = 1024
EPS = 1e-6
NEG = -1e30

A_W = A_HEADS * HEAD_DIM
A_OUT_W = A_HEADS_PER_GROUP * HEAD_DIM
B_QW = B_Q_HEADS * HEAD_DIM
B_KVW = B_KV_HEADS * HEAD_DIM
C_W = C_HEADS * HEAD_DIM

SCALE = 1.0 / math.sqrt(HEAD_DIM)
LOG2E = math.log2(math.e)

A_QROWS = 128
A_KWIN = 256
A_RADIUS = 64
A_NOFF = 3

BF16 = jnp.bfloat16
F32 = jnp.float32

_NT = (((1,), (1,)), ((), ()))


def _cparams(*sem):
    return pltpu.CompilerParams(dimension_semantics=sem)


def _rms_cast_kernel(x_ref, g_ref, o_ref):
    x = x_ref[...]
    ms = jnp.mean(x * x, axis=-1, keepdims=True)
    o_ref[...] = (x * lax.rsqrt(ms + EPS) * g_ref[...]).astype(o_ref.dtype)


def _rms_cast(x2d, g, tm=512):
    m, d = x2d.shape
    return pl.pallas_call(
        _rms_cast_kernel,
        grid=(m // tm,),
        in_specs=[pl.BlockSpec((tm, d), lambda i: (i, 0)),
                  pl.BlockSpec((1, d), lambda i: (0, 0))],
        out_specs=pl.BlockSpec((tm, d), lambda i: (i, 0)),
        out_shape=jax.ShapeDtypeStruct((m, d), BF16),
        compiler_params=_cparams("parallel"),
        name="rms_cast",
    )(x2d, g.reshape(1, d))


def _proj_kernel(*refs, mode, heads):
    if mode == "rope":
        h_ref, w_ref, g_ref, cos_ref, sin_ref, o_ref = refs
    elif mode == "norm":
        h_ref, w_ref, g_ref, o_ref = refs
    else:
        h_ref, w_ref, o_ref = refs
    acc = jnp.dot(h_ref[...], w_ref[...], preferred_element_type=F32)
    for hh in range(heads):
        sl = slice(hh * HEAD_DIM, (hh + 1) * HEAD_DIM)
        y = acc[:, sl]
        if mode != "plain":
            ms = jnp.mean(y * y, axis=-1, keepdims=True)
            y = y * lax.rsqrt(ms + EPS) * g_ref[:, sl]
        if mode == "rope":
            y = y * cos_ref[...] + pltpu.roll(y, HEAD_DIM // 2, 1) * sin_ref[...]
        o_ref[:, sl] = y.astype(o_ref.dtype)


def _proj(h, w, mode, gains=None, cos_t=None, sin_t=None, tm=1024, tn=512):
    m, d = h.shape
    n = w.shape[1]
    tn = min(tn, n)
    in_specs = [pl.BlockSpec((tm, d), lambda i, j: (i, 0)),
                pl.BlockSpec((d, tn), lambda i, j: (0, j))]
    args = [h, w]
    if mode != "plain":
        in_specs.append(pl.BlockSpec((1, tn), lambda i, j: (0, j)))
        args.append(gains.reshape(1, n))
    if mode == "rope":
        s_blocks = cos_t.shape[0] // tm
        in_specs += [pl.BlockSpec((tm, HEAD_DIM), lambda i, j: (i % s_blocks, 0))] * 2
        args += [cos_t, sin_t]
    return pl.pallas_call(
        functools.partial(_proj_kernel, mode=mode, heads=tn // HEAD_DIM),
        grid=(m // tm, n // tn),
        in_specs=in_specs,
        out_specs=pl.BlockSpec((tm, tn), lambda i, j: (i, j)),
        out_shape=jax.ShapeDtypeStruct((m, n), BF16),
        compiler_params=_cparams("parallel", "arbitrary"),
        name="proj_" + mode,
    )(*args)


def _mem_kv_kernel(mem_ref, gm_ref, w_ref, gk_ref, k_ref, v_ref):
    x = mem_ref[...]
    ms = jnp.mean(x * x, axis=-1, keepdims=True)
    hm = (x * lax.rsqrt(ms + EPS) * gm_ref[...]).astype(BF16)
    kv = jnp.dot(hm, w_ref[...], preferred_element_type=F32)
    for hh in range(C_HEADS):
        sl = slice(hh * HEAD_DIM, (hh + 1) * HEAD_DIM)
        y = kv[:, sl]
        ms = jnp.mean(y * y, axis=-1, keepdims=True)
        k_ref[:, sl] = (y * lax.rsqrt(ms + EPS) * gk_ref[...]).astype(BF16)
    v_ref[...] = kv[:, C_W:].astype(BF16)


def _mem_kv(mem2d, g_mem, w_kv, g_kc, tm=256):
    m, d = mem2d.shape
    return pl.pallas_call(
        _mem_kv_kernel,
        grid=(m // tm,),
        in_specs=[pl.BlockSpec((tm, d), lambda i: (i, 0)),
                  pl.BlockSpec((1, d), lambda i: (0, 0)),
                  pl.BlockSpec((d, 2 * C_W), lambda i: (0, 0)),
                  pl.BlockSpec((1, HEAD_DIM), lambda i: (0, 0))],
        out_specs=[pl.BlockSpec((tm, C_W), lambda i: (i, 0))] * 2,
        out_shape=[jax.ShapeDtypeStruct((m, C_W), BF16)] * 2,
        compiler_params=_cparams("parallel"),
        name="mem_kv",
    )(mem2d, g_mem.reshape(1, d), w_kv, g_kc.reshape(1, HEAD_DIM))


def _t5_bucket(rel):
    nb = REL_BUCKETS // 2
    ret = jnp.where(rel > 0, nb, 0)
    n = jnp.abs(rel)
    max_exact = nb // 2
    large = max_exact + (jnp.log(jnp.maximum(n, 1).astype(jnp.float32) / max_exact)
                         / math.log(REL_MAX_DIST / max_exact) * (nb - max_exact)).astype(jnp.int32)
    large = jnp.minimum(large, nb - 1)
    return ret + jnp.where(n < max_exact, n, large)


def _a_bucket_index():
    qi = jnp.arange(A_QROWS, dtype=jnp.int32)[:, None]
    kj = jnp.arange(A_KWIN, dtype=jnp.int32)[None, :]
    out = []
    for _, dil in DIL_PAIRS:
        for off in range(A_NOFF):
            rel = kj - qi - A_RADIUS * off
            out.append(jnp.where(jnp.abs(rel) <= A_RADIUS, _t5_bucket(rel * dil), -1))
    return jnp.stack(out)


def _a_bias_kernel(tab_ref, bucket_ref, o_ref):
    g = pl.program_id(0) // A_NOFF
    bk = bucket_ref[...]
    for hh in range(A_HEADS_PER_GROUP):
        acc = jnp.full(bk.shape, NEG, F32)
        for b in range(REL_BUCKETS):
            acc = jnp.where(bk == b, tab_ref[b, g * A_HEADS_PER_GROUP + hh], acc)
        o_ref[hh] = acc


def _a_bias(rel_bias):
    n = len(DIL_PAIRS) * A_NOFF
    return pl.pallas_call(
        _a_bias_kernel,
        grid=(n,),
        in_specs=[pl.BlockSpec(memory_space=pltpu.SMEM),
                  pl.BlockSpec((None, A_QROWS, A_KWIN), lambda i: (i, 0, 0))],
        out_specs=pl.BlockSpec((None, A_HEADS_PER_GROUP, A_QROWS, A_KWIN), lambda i: (i, 0, 0, 0)),
        out_shape=jax.ShapeDtypeStruct((n, A_HEADS_PER_GROUP, A_QROWS, A_KWIN), F32),
        compiler_params=_cparams("arbitrary"),
        name="a_bias",
    )(rel_bias, _a_bucket_index())


def _mixer_a_kernel(q_ref, k_ref, v_ref, bias_ref, o_ref, lse_ref, *, sub_len, q_rows):
    qblk = pl.program_id(2)
    for i in range(q_rows // A_QROWS):
        q0 = qblk * q_rows + i * A_QROWS
        ks = jnp.clip(q0 - A_RADIUS, 0, sub_len - A_KWIN)
        off = lax.shift_right_logical(q0 - ks, int(math.log2(A_RADIUS)))
        ks = pl.multiple_of(ks, A_RADIUS)
        rows = slice(i * A_QROWS, (i + 1) * A_QROWS)
        for hh in range(A_HEADS_PER_GROUP):
            cols = slice(hh * HEAD_DIM, (hh + 1) * HEAD_DIM)
            q = q_ref[rows, cols]
            k = k_ref[pl.ds(ks, A_KWIN), cols]
            v = v_ref[pl.ds(ks, A_KWIN), cols]
            s = lax.dot_general(q, k, _NT, preferred_element_type=F32) * SCALE + bias_ref[off, hh]
            m = jnp.max(s, axis=-1, keepdims=True)
            p = jnp.exp(s - m)
            l = jnp.sum(p, axis=-1, keepdims=True)
            o = jnp.dot(p.astype(BF16), v, preferred_element_type=F32) / l
            o_ref[rows, cols] = o
            lse_ref[rows, cols] = jnp.broadcast_to(m + jnp.log(l), (A_QROWS, HEAD_DIM))


def _mixer_a_group(qkn, vv, bias_g, gi, dil, seq):
    b = qkn.shape[0]
    sub_len = seq // dil
    q_rows = min(sub_len, 512)
    wq = qkn.shape[2] // A_OUT_W
    wv = vv.shape[2] // A_OUT_W
    k_off = A_W // A_OUT_W
    qk_view = qkn.reshape(b, sub_len, dil * qkn.shape[2])
    v_view = vv.reshape(b, sub_len, dil * vv.shape[2])
    o, lse = pl.pallas_call(
        functools.partial(_mixer_a_kernel, sub_len=sub_len, q_rows=q_rows),
        grid=(b, dil, sub_len // q_rows),
        in_specs=[pl.BlockSpec((None, q_rows, A_OUT_W), lambda bi, r, qi: (bi, qi, r * wq + gi)),
                  pl.BlockSpec((None, sub_len, A_OUT_W), lambda bi, r, qi: (bi, 0, r * wq + k_off + gi)),
                  pl.BlockSpec((None, sub_len, A_OUT_W), lambda bi, r, qi: (bi, 0, r * wv + gi)),
                  pl.BlockSpec((A_NOFF, A_HEADS_PER_GROUP, A_QROWS, A_KWIN), lambda bi, r, qi: (0, 0, 0, 0))],
        out_specs=[pl.BlockSpec((None, q_rows, A_OUT_W), lambda bi, r, qi: (bi, qi, r))] * 2,
        out_shape=[jax.ShapeDtypeStruct((b, sub_len, dil * A_OUT_W), F32)] * 2,
        compiler_params=_cparams("parallel", "parallel", "arbitrary"),
        name=f"mixer_a_d{dil}",
    )(qk_view, qk_view, v_view, bias_g)
    return o.reshape(b, seq, A_OUT_W), lse.reshape(b, seq, A_OUT_W)


def _mixer_b_kernel(q_ref, k_ref, v_ref, o_ref, *, tk, group):
    tq = q_ref.shape[0]
    seq = k_ref.shape[0]
    q = jnp.concatenate([q_ref[:, i * HEAD_DIM:(i + 1) * HEAD_DIM] for i in range(group)], axis=0)
    rows = group * tq
    c = SCALE * LOG2E

    def body(ci, carry):
        m, l, acc = carry
        start = pl.multiple_of(ci * tk, tk)
        k = k_ref[pl.ds(start, tk), :]
        v = v_ref[pl.ds(start, tk), :]
        s = lax.dot_general(q, k, _NT, preferred_element_type=F32)
        m_new = jnp.maximum(m, jnp.max(s, axis=-1, keepdims=True))
        alpha = jnp.exp2((m - m_new) * c)
        p = jnp.exp2((s - m_new) * c)
        l = alpha * l + jnp.sum(p, axis=-1, keepdims=True)
        acc = alpha * acc + jnp.dot(p.astype(BF16), v, preferred_element_type=F32)
        return m_new, l, acc

    init = (jnp.full((rows, 1), NEG, F32), jnp.zeros((rows, 1), F32), jnp.zeros((rows, HEAD_DIM), F32))
    _, l, acc = lax.fori_loop(0, seq // tk, body, init)
    o = acc / l
    for i in range(group):
        o_ref[:, i * HEAD_DIM:(i + 1) * HEAD_DIM] = o[i * tq:(i + 1) * tq].astype(o_ref.dtype)


def _mixer_b(qkr, vv, tq=256, tk=512):
    b, seq, _ = qkr.shape
    group = B_Q_HEADS // B_KV_HEADS
    gw = group * HEAD_DIM
    k_blk = B_QW // HEAD_DIM
    v_blk = A_W // HEAD_DIM
    return pl.pallas_call(
        functools.partial(_mixer_b_kernel, tk=tk, group=group),
        grid=(b, B_KV_HEADS, seq // tq),
        in_specs=[pl.BlockSpec((None, tq, gw), lambda bi, kv, qi: (bi, qi, kv)),
                  pl.BlockSpec((None, seq, HEAD_DIM), lambda bi, kv, qi: (bi, 0, k_blk + kv)),
                  pl.BlockSpec((None, seq, HEAD_DIM), lambda bi, kv, qi: (bi, 0, v_blk + kv))],
        out_specs=pl.BlockSpec((None, tq, gw), lambda bi, kv, qi: (bi, qi, kv)),
        out_shape=jax.ShapeDtypeStruct((b, seq, B_QW), BF16),
        compiler_params=_cparams("parallel", "parallel", "arbitrary"),
        name="mixer_b",
    )(qkr, qkr, vv)


def _mixer_c_kernel(q_ref, k_ref, v_ref, o_ref):
    for hh in range(C_HEADS):
        sl = slice(hh * HEAD_DIM, (hh + 1) * HEAD_DIM)
        s = lax.dot_general(q_ref[:, sl], k_ref[:, sl], _NT, preferred_element_type=F32) * SCALE
        m = jnp.max(s, axis=-1, keepdims=True)
        p = jnp.exp(s - m)
        l = jnp.sum(p, axis=-1, keepdims=True)
        o = jnp.dot(p.astype(BF16), v_ref[:, sl], preferred_element_type=F32) / l
        o_ref[:, sl] = o.astype(o_ref.dtype)


def _mixer_c(qkn, kc, vc, tq=512):
    b, seq, _ = qkn.shape
    n_mem = kc.shape[1]
    qc_blk = (2 * A_W) // C_W
    return pl.pallas_call(
        _mixer_c_kernel,
        grid=(b, seq // tq),
        in_specs=[pl.BlockSpec((None, tq, C_W), lambda bi, qi: (bi, qi, qc_blk)),
                  pl.BlockSpec((None, n_mem, C_W), lambda bi, qi: (bi, 0, 0)),
                  pl.BlockSpec((None, n_mem, C_W), lambda bi, qi: (bi, 0, 0))],
        out_specs=pl.BlockSpec((None, tq, C_W), lambda bi, qi: (bi, qi, 0)),
        out_shape=jax.ShapeDtypeStruct((b, seq, C_W), BF16),
        compiler_params=_cparams("parallel", "arbitrary"),
        name="mixer_c",
    )(qkn, kc, vc)


def _merge_kernel(h_ref, wg0_ref, wg1_ref, wg2_ref,
                  o0_ref, o1_ref, o2_ref, l0_ref, l1_ref, l2_ref, ob_ref, oc_ref,
                  wa_ref, wb_ref, wc_ref, out_ref, oa_ref):
    @pl.when(pl.program_id(1) == 0)
    def _():
        l0, l1, l2 = l0_ref[...], l1_ref[...], l2_ref[...]
        m = jnp.maximum(jnp.maximum(l0, l1), l2)
        w0, w1, w2 = jnp.exp(l0 - m), jnp.exp(l1 - m), jnp.exp(l2 - m)
        oa = (w0 * o0_ref[...] + w1 * o1_ref[...] + w2 * o2_ref[...]) / (w0 + w1 + w2)
        oa_ref[...] = oa.astype(oa_ref.dtype)

    h = h_ref[...]
    ga = jax.nn.sigmoid(jnp.dot(h, wg0_ref[...], preferred_element_type=F32))
    merged = ga * jnp.dot(oa_ref[...], wa_ref[...], preferred_element_type=F32)
    gb = jax.nn.sigmoid(jnp.dot(h, wg1_ref[...], preferred_element_type=F32))
    merged += gb * jnp.dot(ob_ref[...], wb_ref[...], preferred_element_type=F32)
    gc = jax.nn.sigmoid(jnp.dot(h, wg2_ref[...], preferred_element_type=F32))
    merged += gc * jnp.dot(oc_ref[...], wc_ref[...], preferred_element_type=F32)
    out_ref[...] = merged.astype(out_ref.dtype)


def _merge(h, w_gate, a_outs, a_lses, ob, oc, wa, wb, wc, tm=512, tn=512):
    m, d = h.shape
    nj = d // tn
    row = lambda w: pl.BlockSpec((tm, w), lambda i, j: (i, 0))
    col = lambda k: pl.BlockSpec((k, tn), lambda i, j: (0, j))
    gate = lambda br: pl.BlockSpec((d, tn), lambda i, j: (0, br * nj + j))
    return pl.pallas_call(
        _merge_kernel,
        grid=(m // tm, nj),
        in_specs=[row(d), gate(0), gate(1), gate(2)]
                 + [row(A_OUT_W)] * 6 + [row(B_QW), row(C_W)]
                 + [col(A_OUT_W), col(B_QW), col(C_W)],
        out_specs=pl.BlockSpec((tm, tn), lambda i, j: (i, j)),
        out_shape=jax.ShapeDtypeStruct((m, d), BF16),
        scratch_shapes=[pltpu.VMEM((tm, A_OUT_W), BF16)],
        compiler_params=_cparams("parallel", "arbitrary"),
        name="merge",
    )(h, w_gate, w_gate, w_gate, *a_outs, *a_lses, ob, oc, wa, wb, wc)


def _out_proj_kernel(mg_ref, w_ref, x_ref, g_ref, x1_ref, h2_ref):
    x1 = x_ref[...] + jnp.dot(mg_ref[...], w_ref[...], preferred_element_type=F32)
    x1_ref[...] = x1
    ms = jnp.mean(x1 * x1, axis=-1, keepdims=True)
    h2_ref[...] = (x1 * lax.rsqrt(ms + EPS) * g_ref[...]).astype(h2_ref.dtype)


def _out_proj(merged, w_o, x2d, g_ffn, tm=512):
    m, d = x2d.shape
    return pl.pallas_call(
        _out_proj_kernel,
        grid=(m // tm,),
        in_specs=[pl.BlockSpec((tm, d), lambda i: (i, 0)),
                  pl.BlockSpec((d, d), lambda i: (0, 0)),
                  pl.BlockSpec((tm, d), lambda i: (i, 0)),
                  pl.BlockSpec((1, d), lambda i: (0, 0))],
        out_specs=[pl.BlockSpec((tm, d), lambda i: (i, 0))] * 2,
        out_shape=[jax.ShapeDtypeStruct((m, d), F32), jax.ShapeDtypeStruct((m, d), BF16)],
        compiler_params=_cparams("parallel"),
        name="out_proj",
    )(merged, w_o, x2d, g_ffn.reshape(1, d))


def _ffn_kernel(h_ref, wa_ref, wb_ref, wo_ref, x1_ref, out_ref):
    h = h_ref[...]
    a = jnp.dot(h, wa_ref[...], preferred_element_type=F32)
    b = jnp.dot(h, wb_ref[...], preferred_element_type=F32)
    act = (a * jax.nn.sigmoid(a) * b).astype(BF16)
    contrib = jnp.dot(act, wo_ref[...], preferred_element_type=F32)

    @pl.when(pl.program_id(1) == 0)
    def _():
        out_ref[...] = x1_ref[...] + contrib

    @pl.when(pl.program_id(1) > 0)
    def _():
        out_ref[...] += contrib


def _ffn(h2, w_in, w_out, x1, tm=512, tf=512):
    m, d = h2.shape
    d_ff = w_out.shape[0]
    nf = d_ff // tf
    return pl.pallas_call(
        _ffn_kernel,
        grid=(m // tm, nf),
        in_specs=[pl.BlockSpec((tm, d), lambda i, f: (i, 0)),
                  pl.BlockSpec((d, tf), lambda i, f: (0, f)),
                  pl.BlockSpec((d, tf), lambda i, f: (0, nf + f)),
                  pl.BlockSpec((tf, d), lambda i, f: (f, 0)),
                  pl.BlockSpec((tm, d), lambda i, f: (i, 0))],
        out_specs=pl.BlockSpec((tm, d), lambda i, f: (i, 0)),
        out_shape=jax.ShapeDtypeStruct((m, d), F32),
        compiler_params=_cparams("parallel", "arbitrary"),
        name="ffn",
    )(h2, w_in, w_in, w_out, x1)


def _deinterleave_cols(w, heads):
    d = w.shape[0]
    return w.reshape(d, heads, HEAD_DIM // 2, 2).transpose(0, 1, 3, 2).reshape(d, heads * HEAD_DIM)


def _rope_tables(seq):
    rows = seq // GRID_W
    r = jnp.repeat(jnp.arange(rows), GRID_W).astype(F32)
    c = jnp.tile(jnp.arange(GRID_W), rows).astype(F32)
    nf = HEAD_DIM // 4
    inv = ROPE_THETA ** (-jnp.arange(nf, dtype=F32) / nf)
    ang = jnp.concatenate([r[:, None] * inv, c[:, None] * inv], axis=-1)
    cos, sin = jnp.cos(ang), jnp.sin(ang)
    return jnp.concatenate([cos, cos], axis=-1), jnp.concatenate([-sin, sin], axis=-1)


def _layer(x2d, mem2d, bias_a, cos_t, sin_t, b, seq, g_mix, w_in, g_qa, g_ka, g_qb, g_kb, g_mem, w_mem_kv,
           g_qc, g_kc, w_br_a, w_br_b, w_br_c, w_o, g_ffn, w_ffn_in, w_ffn_out):
    d = x2d.shape[1]
    o_qa, o_ka, o_va, o_qb, o_kb, o_vb, o_qc, o_gt = np.cumsum(
        (0, A_W, A_W, A_W, B_QW, B_KVW, B_KVW, C_W))
    seg = lambda lo, hi: w_in[:, lo:hi]
    w_norm = jnp.concatenate([seg(o_qa, o_ka), seg(o_ka, o_va), seg(o_qc, o_gt)], axis=1).astype(BF16)
    w_rope = jnp.concatenate([_deinterleave_cols(seg(o_qb, o_kb), B_Q_HEADS),
                              _deinterleave_cols(seg(o_kb, o_vb), B_KV_HEADS)], axis=1).astype(BF16)
    w_val = jnp.concatenate([seg(o_va, o_qb), seg(o_vb, o_qc)], axis=1).astype(BF16)
    w_gate = w_in[:, o_gt:].astype(BF16)
    g_norm = jnp.concatenate([jnp.tile(g_qa, A_HEADS), jnp.tile(g_ka, A_HEADS), jnp.tile(g_qc, C_HEADS)])
    deint = lambda g: g.reshape(HEAD_DIM // 2, 2).T.reshape(HEAD_DIM)
    g_rope = jnp.concatenate([jnp.tile(deint(g_qb), B_Q_HEADS), jnp.tile(deint(g_kb), B_KV_HEADS)])

    h = _rms_cast(x2d, g_mix)
    qkn = _proj(h, w_norm, "norm", g_norm).reshape(b, seq, -1)
    qkr = _proj(h, w_rope, "rope", g_rope, cos_t, sin_t).reshape(b, seq, -1)
    vv = _proj(h, w_val, "plain").reshape(b, seq, -1)

    a_outs, a_lses = [], []
    for gi, (_, dil) in enumerate(DIL_PAIRS):
        o, lse = _mixer_a_group(qkn, vv, bias_a[gi * A_NOFF:(gi + 1) * A_NOFF], gi, dil, seq)
        a_outs.append(o.reshape(b * seq, A_OUT_W))
        a_lses.append(lse.reshape(b * seq, A_OUT_W))

    ob = _mixer_b(qkr, vv).reshape(b * seq, B_QW)

    kc, vc = _mem_kv(mem2d, g_mem, w_mem_kv.astype(BF16), g_kc)
    n_mem = mem2d.shape[0] // b
    oc = _mixer_c(qkn, kc.reshape(b, n_mem, C_W), vc.reshape(b, n_mem, C_W)).reshape(b * seq, C_W)

    merged = _merge(h, w_gate, a_outs, a_lses, ob, oc,
                    w_br_a.astype(BF16), w_br_b.astype(BF16), w_br_c.astype(BF16))
    x1, h2 = _out_proj(merged, w_o.astype(BF16), x2d, g_ffn)
    return _ffn(h2, w_ffn_in.astype(BF16), w_ffn_out.astype(BF16), x1)


def kernel(x, mem, rel_bias, g_mix, w_in, g_qa, g_ka, g_qb, g_kb, g_mem, w_mem_kv, g_qc, g_kc,
           w_br_a, w_br_b, w_br_c, w_o, g_ffn, w_ffn_in, w_ffn_out):
    b, seq, d = x.shape
    depth = w_in.shape[0]
    cos_t, sin_t = _rope_tables(seq)
    bias_a = _a_bias(rel_bias)
    x2d = x.reshape(b * seq, d)
    mem2d = mem.reshape(-1, d)
    for layer in range(depth):
        x2d = _layer(x2d, mem2d, bias_a, cos_t, sin_t, b, seq,
                     g_mix[layer], w_in[layer], g_qa[layer], g_ka[layer], g_qb[layer], g_kb[layer],
                     g_mem[layer], w_mem_kv[layer], g_qc[layer], g_kc[layer],
                     w_br_a[layer], w_br_b[layer], w_br_c[layer], w_o[layer], g_ffn[layer],
                     w_ffn_in[layer], w_ffn_out[layer])
    return x2d.reshape(b, seq, d)
```

```python
import functools
import math

import numpy as np
import jax
import jax.numpy as jnp
from jax import lax
from jax.experimental import pallas as pl
from jax.experimental.pallas import tpu as pltpu

HEAD_DIM = 128
GRID_W = 64
DIL_PAIRS = ((128, 1), (512, 4), (2048, 16))
A_HEADS_PER_GROUP = 2
A_HEADS = A_HEADS_PER_GROUP * len(DIL_PAIRS)
B_Q_HEADS = 6
B_KV_HEADS = 2
ROPE_THETA = 10000.0
C_HEADS = 4
N_BRANCH = 3
REL_BUCKETS = 32
REL_MAX_DIST = 1024
EPS = 1e-6
NEG = -1e30

A_W = A_HEADS * HEAD_DIM
A_OUT_W = A_HEADS_PER_GROUP * HEAD_DIM
B_QW = B_Q_HEADS * HEAD_DIM
B_KVW = B_KV_HEADS * HEAD_DIM
C_W = C_HEADS * HEAD_DIM

SCALE = 1.0 / math.sqrt(HEAD_DIM)
LOG2E = math.log2(math.e)

A_QROWS = 128
A_KWIN = 256
A_RADIUS = 64
A_NOFF = 3

MXU_N = 256

BF16 = jnp.bfloat16
F32 = jnp.float32

_NT = (((1,), (1,)), ((), ()))


def _cparams(*sem):
    return pltpu.CompilerParams(dimension_semantics=sem)


def _rms_cast_kernel(x_ref, g_ref, o_ref):
    x = x_ref[...]
    ms = jnp.mean(x * x, axis=-1, keepdims=True)
    o_ref[...] = (x * lax.rsqrt(ms + EPS) * g_ref[...]).astype(o_ref.dtype)


def _rms_cast(x2d, g, tm=512):
    m, d = x2d.shape
    return pl.pallas_call(
        _rms_cast_kernel,
        grid=(m // tm,),
        in_specs=[pl.BlockSpec((tm, d), lambda i: (i, 0)),
                  pl.BlockSpec((1, d), lambda i: (0, 0))],
        out_specs=pl.BlockSpec((tm, d), lambda i: (i, 0)),
        out_shape=jax.ShapeDtypeStruct((m, d), BF16),
        compiler_params=_cparams("parallel"),
        name="rms_cast",
    )(x2d, g.reshape(1, d))


def _proj_kernel(*refs, mode, heads):
    if mode == "rope":
        h_ref, w_ref, g_ref, cos_ref, sin_ref, o_ref = refs
    elif mode == "norm":
        h_ref, w_ref, g_ref, o_ref = refs
    else:
        h_ref, w_ref, o_ref = refs
    h = h_ref[...]
    for c0 in range(0, heads * HEAD_DIM, MXU_N):
        acc = jnp.dot(h, w_ref[:, c0:c0 + MXU_N], preferred_element_type=F32)
        for hh in range(MXU_N // HEAD_DIM):
            sl = slice(c0 + hh * HEAD_DIM, c0 + (hh + 1) * HEAD_DIM)
            y = acc[:, hh * HEAD_DIM:(hh + 1) * HEAD_DIM]
            if mode != "plain":
                ms = jnp.mean(y * y, axis=-1, keepdims=True)
                y = y * lax.rsqrt(ms + EPS) * g_ref[:, sl]
            if mode == "rope":
                y = y * cos_ref[...] + pltpu.roll(y, HEAD_DIM // 2, 1) * sin_ref[...]
            o_ref[:, sl] = y.astype(o_ref.dtype)


def _proj(h, w, mode, gains=None, cos_t=None, sin_t=None, tm=1024, tn=512):
    m, d = h.shape
    n = w.shape[1]
    tn = min(tn, n)
    in_specs = [pl.BlockSpec((tm, d), lambda i, j: (i, 0)),
                pl.BlockSpec((d, tn), lambda i, j: (0, j))]
    args = [h, w]
    if mode != "plain":
        in_specs.append(pl.BlockSpec((1, tn), lambda i, j: (0, j)))
        args.append(gains.reshape(1, n))
    if mode == "rope":
        s_blocks = cos_t.shape[0] // tm
        in_specs += [pl.BlockSpec((tm, HEAD_DIM), lambda i, j: (i % s_blocks, 0))] * 2
        args += [cos_t, sin_t]
    return pl.pallas_call(
        functools.partial(_proj_kernel, mode=mode, heads=tn // HEAD_DIM),
        grid=(m // tm, n // tn),
        in_specs=in_specs,
        out_specs=pl.BlockSpec((tm, tn), lambda i, j: (i, j)),
        out_shape=jax.ShapeDtypeStruct((m, n), BF16),
        compiler_params=_cparams("parallel", "arbitrary"),
        name="proj_" + mode,
    )(*args)


def _mem_kv_kernel(mem_ref, gm_ref, w_ref, gk_ref, k_ref, v_ref):
    x = mem_ref[...]
    ms = jnp.mean(x * x, axis=-1, keepdims=True)
    hm = (x * lax.rsqrt(ms + EPS) * gm_ref[...]).astype(BF16)
    kv = jnp.dot(hm, w_ref[...], preferred_element_type=F32)
    for hh in range(C_HEADS):
        sl = slice(hh * HEAD_DIM, (hh + 1) * HEAD_DIM)
        y = kv[:, sl]
        ms = jnp.mean(y * y, axis=-1, keepdims=True)
        k_ref[:, sl] = (y * lax.rsqrt(ms + EPS) * gk_ref[...]).astype(BF16)
    v_ref[...] = kv[:, C_W:].astype(BF16)


def _mem_kv(mem2d, g_mem, w_kv, g_kc, tm=256):
    m, d = mem2d.shape
    return pl.pallas_call(
        _mem_kv_kernel,
        grid=(m // tm,),
        in_specs=[pl.BlockSpec((tm, d), lambda i: (i, 0)),
                  pl.BlockSpec((1, d), lambda i: (0, 0)),
                  pl.BlockSpec((d, 2 * C_W), lambda i: (0, 0)),
                  pl.BlockSpec((1, HEAD_DIM), lambda i: (0, 0))],
        out_specs=[pl.BlockSpec((tm, C_W), lambda i: (i, 0))] * 2,
        out_shape=[jax.ShapeDtypeStruct((m, C_W), BF16)] * 2,
        compiler_params=_cparams("parallel"),
        name="mem_kv",
    )(mem2d, g_mem.reshape(1, d), w_kv, g_kc.reshape(1, HEAD_DIM))


def _t5_bucket(rel):
    nb = REL_BUCKETS // 2
    ret = jnp.where(rel > 0, nb, 0)
    n = jnp.abs(rel)
    max_exact = nb // 2
    large = max_exact + (jnp.log(jnp.maximum(n, 1).astype(jnp.float32) / max_exact)
                         / math.log(REL_MAX_DIST / max_exact) * (nb - max_exact)).astype(jnp.int32)
    large = jnp.minimum(large, nb - 1)
    return ret + jnp.where(n < max_exact, n, large)


def _a_bucket_index():
    qi = jnp.arange(A_QROWS, dtype=jnp.int32)[:, None]
    kj = jnp.arange(A_KWIN, dtype=jnp.int32)[None, :]
    out = []
    for _, dil in DIL_PAIRS:
        for off in range(A_NOFF):
            rel = kj - qi - A_RADIUS * off
            out.append(jnp.where(jnp.abs(rel) <= A_RADIUS, _t5_bucket(rel * dil), -1))
    return jnp.stack(out)


def _a_bias_kernel(tab_ref, bucket_ref, o_ref):
    g = pl.program_id(0) // A_NOFF
    bk = bucket_ref[...]
    for hh in range(A_HEADS_PER_GROUP):
        acc = jnp.full(bk.shape, NEG, F32)
        for b in range(REL_BUCKETS):
            acc = jnp.where(bk == b, tab_ref[b, g * A_HEADS_PER_GROUP + hh], acc)
        o_ref[hh] = acc


def _a_bias(rel_bias):
    n = len(DIL_PAIRS) * A_NOFF
    return pl.pallas_call(
        _a_bias_kernel,
        grid=(n,),
        in_specs=[pl.BlockSpec(memory_space=pltpu.SMEM),
                  pl.BlockSpec((None, A_QROWS, A_KWIN), lambda i: (i, 0, 0))],
        out_specs=pl.BlockSpec((None, A_HEADS_PER_GROUP, A_QROWS, A_KWIN), lambda i: (i, 0, 0, 0)),
        out_shape=jax.ShapeDtypeStruct((n, A_HEADS_PER_GROUP, A_QROWS, A_KWIN), F32),
        compiler_params=_cparams("arbitrary"),
        name="a_bias",
    )(rel_bias, _a_bucket_index())


def _mixer_a_kernel(q_ref, k_ref, v_ref, bias_ref, o_ref, lse_ref, *scratch, dil, seq):
    t_rows = q_ref.shape[0]
    ti = pl.program_id(1)
    sub_len = seq // dil
    lq = t_rows // dil
    if dil > 1:
        stage, qstage, kres, vres = scratch

        @pl.when(ti == 0)
        def _():
            for src, dst in ((k_ref, kres), (v_ref, vres)):
                for hh in range(A_HEADS_PER_GROUP):
                    cols = slice(hh * HEAD_DIM, (hh + 1) * HEAD_DIM)
                    stage[...] = src[:, cols].astype(F32)
                    for r in range(dil):
                        dst[r, :, cols] = stage[pl.ds(r, sub_len, stride=dil), :].astype(BF16)

        for hh in range(A_HEADS_PER_GROUP):
            qstage[hh] = q_ref[:, hh * HEAD_DIM:(hh + 1) * HEAD_DIM].astype(F32)

    for r in range(dil):
        for i in range(lq // A_QROWS):
            q0 = ti * lq + i * A_QROWS
            ks = jnp.clip(q0 - A_RADIUS, 0, sub_len - A_KWIN)
            off = lax.shift_right_logical(q0 - ks, int(math.log2(A_RADIUS)))
            ks = pl.multiple_of(ks, A_RADIUS)
            if dil > 1:
                rows = pl.ds(i * A_QROWS * dil + r, A_QROWS, stride=dil)
            else:
                rows = pl.ds(i * A_QROWS, A_QROWS)
            for hh in range(A_HEADS_PER_GROUP):
                cols = slice(hh * HEAD_DIM, (hh + 1) * HEAD_DIM)
                if dil > 1:
                    q = qstage[hh, rows, :].astype(BF16)
                    k = kres[r, pl.ds(ks, A_KWIN), cols]
                    v = vres[r, pl.ds(ks, A_KWIN), cols]
                else:
                    q = q_ref[rows, cols]
                    k = k_ref[pl.ds(ks, A_KWIN), cols]
                    v = v_ref[pl.ds(ks, A_KWIN), cols]
                s = lax.dot_general(q, k, _NT, preferred_element_type=F32) * SCALE + bias_ref[off, hh]
                m = jnp.max(s, axis=-1, keepdims=True)
                p = jnp.exp(s - m)
                l = jnp.sum(p, axis=-1, keepdims=True)
                o_ref[hh, rows, :] = jnp.dot(p.astype(BF16), v, preferred_element_type=F32) / l
                lse_ref[hh, rows, :] = jnp.broadcast_to(m + jnp.log(l), (A_QROWS, HEAD_DIM))


def _mixer_a_group(qkn, vv, bias_g, gi, dil, t_rows=2048):
    b, seq, _ = qkn.shape
    sub_len = seq // dil
    k_blk = A_W // A_OUT_W
    scratch = []
    if dil > 1:
        scratch = [pltpu.VMEM((seq, HEAD_DIM), F32), pltpu.VMEM((A_HEADS_PER_GROUP, t_rows, HEAD_DIM), F32),
                   pltpu.VMEM((dil, sub_len, A_OUT_W), BF16), pltpu.VMEM((dil, sub_len, A_OUT_W), BF16)]
    return pl.pallas_call(
        functools.partial(_mixer_a_kernel, dil=dil, seq=seq),
        grid=(b, seq // t_rows),
        in_specs=[pl.BlockSpec((None, t_rows, A_OUT_W), lambda bi, ti: (bi, ti, gi)),
                  pl.BlockSpec((None, seq, A_OUT_W), lambda bi, ti: (bi, 0, k_blk + gi)),
                  pl.BlockSpec((None, seq, A_OUT_W), lambda bi, ti: (bi, 0, gi)),
                  pl.BlockSpec((A_NOFF, A_HEADS_PER_GROUP, A_QROWS, A_KWIN), lambda bi, ti: (0, 0, 0, 0))],
        out_specs=[pl.BlockSpec((None, A_HEADS_PER_GROUP, t_rows, HEAD_DIM), lambda bi, ti: (bi, 0, ti, 0))] * 2,
        out_shape=[jax.ShapeDtypeStruct((b, A_HEADS_PER_GROUP, seq, HEAD_DIM), F32)] * 2,
        scratch_shapes=scratch,
        compiler_params=_cparams("parallel", "arbitrary"),
        name=f"mixer_a_d{dil}",
    )(qkn, qkn, vv, bias_g)


def _mixer_b_kernel(q_ref, k_ref, vt_ref, o_ref, *, tk, group):
    tq = q_ref.shape[0]
    seq = k_ref.shape[0]
    q = jnp.concatenate([q_ref[:, i * HEAD_DIM:(i + 1) * HEAD_DIM] for i in range(group)], axis=0)
    rows = group * tq
    c = SCALE * LOG2E
    m = jnp.full((1, rows), NEG, F32)
    l = jnp.zeros((1, rows), F32)
    acc = jnp.zeros((HEAD_DIM, rows), F32)
    for ci in range(seq // tk):
        k = k_ref[ci * tk:(ci + 1) * tk, :]
        vt = vt_ref[:, ci * tk:(ci + 1) * tk]
        st = lax.dot_general(k, q, _NT, preferred_element_type=F32)
        m_new = jnp.maximum(m, jnp.max(st, axis=0, keepdims=True))
        alpha = jnp.exp2((m - m_new) * c)
        pt = jnp.exp2((st - m_new) * c)
        l = alpha * l + jnp.sum(pt, axis=0, keepdims=True)
        acc = alpha * acc + jnp.dot(vt, pt.astype(BF16), preferred_element_type=F32)
        m = m_new
    o = (acc / l).T
    for i in range(group):
        o_ref[:, i * HEAD_DIM:(i + 1) * HEAD_DIM] = o[i * tq:(i + 1) * tq].astype(o_ref.dtype)


def _mixer_b(qkr, vbt, tq=256, tk=512):
    b, seq, _ = qkr.shape
    group = B_Q_HEADS // B_KV_HEADS
    gw = group * HEAD_DIM
    k_blk = B_QW // HEAD_DIM
    return pl.pallas_call(
        functools.partial(_mixer_b_kernel, tk=tk, group=group),
        grid=(b, B_KV_HEADS, seq // tq),
        in_specs=[pl.BlockSpec((None, tq, gw), lambda bi, kv, qi: (bi, qi, kv)),
                  pl.BlockSpec((None, seq, HEAD_DIM), lambda bi, kv, qi: (bi, 0, k_blk + kv)),
                  pl.BlockSpec((None, HEAD_DIM, seq), lambda bi, kv, qi: (bi, kv, 0))],
        out_specs=pl.BlockSpec((None, tq, gw), lambda bi, kv, qi: (bi, qi, kv)),
        out_shape=jax.ShapeDtypeStruct((b, seq, B_QW), BF16),
        compiler_params=_cparams("parallel", "parallel", "arbitrary"),
        name="mixer_b",
    )(qkr, qkr, vbt)


def _mixer_c_kernel(q_ref, k_ref, v_ref, o_ref):
    for hh in range(C_HEADS):
        sl = slice(hh * HEAD_DIM, (hh + 1) * HEAD_DIM)
        s = lax.dot_general(q_ref[:, sl], k_ref[:, sl], _NT, preferred_element_type=F32) * SCALE
        m = jnp.max(s, axis=-1, keepdims=True)
        p = jnp.exp(s - m)
        l = jnp.sum(p, axis=-1, keepdims=True)
        o = jnp.dot(p.astype(BF16), v_ref[:, sl], preferred_element_type=F32) / l
        o_ref[:, sl] = o.astype(o_ref.dtype)


def _mixer_c(qkn, kc, vc, tq=512):
    b, seq, _ = qkn.shape
    n_mem = kc.shape[1]
    qc_blk = (2 * A_W) // C_W
    return pl.pallas_call(
        _mixer_c_kernel,
        grid=(b, seq // tq),
        in_specs=[pl.BlockSpec((None, tq, C_W), lambda bi, qi: (bi, qi, qc_blk)),
                  pl.BlockSpec((None, n_mem, C_W), lambda bi, qi: (bi, 0, 0)),
                  pl.BlockSpec((None, n_mem, C_W), lambda bi, qi: (bi, 0, 0))],
        out_specs=pl.BlockSpec((None, tq, C_W), lambda bi, qi: (bi, qi, 0)),
        out_shape=jax.ShapeDtypeStruct((b, seq, C_W), BF16),
        compiler_params=_cparams("parallel", "arbitrary"),
        name="mixer_c",
    )(qkn, kc, vc)


def _merge_kernel(h_ref, wg0_ref, wg1_ref, wg2_ref,
                  o0_ref, o1_ref, o2_ref, l0_ref, l1_ref, l2_ref, ob_ref, oc_ref,
                  wa_ref, wb_ref, wc_ref, out_ref, oa_ref):
    @pl.when(pl.program_id(1) == 0)
    def _():
        for hh in range(A_HEADS_PER_GROUP):
            l0, l1, l2 = l0_ref[hh], l1_ref[hh], l2_ref[hh]
            m = jnp.maximum(jnp.maximum(l0, l1), l2)
            w0, w1, w2 = jnp.exp(l0 - m), jnp.exp(l1 - m), jnp.exp(l2 - m)
            oa = (w0 * o0_ref[hh] + w1 * o1_ref[hh] + w2 * o2_ref[hh]) / (w0 + w1 + w2)
            oa_ref[:, hh * HEAD_DIM:(hh + 1) * HEAD_DIM] = oa.astype(oa_ref.dtype)

    h = h_ref[...]
    ga = jax.nn.sigmoid(jnp.dot(h, wg0_ref[...], preferred_element_type=F32))
    merged = ga * jnp.dot(oa_ref[...], wa_ref[...], preferred_element_type=F32)
    gb = jax.nn.sigmoid(jnp.dot(h, wg1_ref[...], preferred_element_type=F32))
    merged += gb * jnp.dot(ob_ref[...], wb_ref[...], preferred_element_type=F32)
    gc = jax.nn.sigmoid(jnp.dot(h, wg2_ref[...], preferred_element_type=F32))
    merged += gc * jnp.dot(oc_ref[...], wc_ref[...], preferred_element_type=F32)
    out_ref[...] = merged.astype(out_ref.dtype)


def _merge(h, w_gate, a_outs, a_lses, ob, oc, wa, wb, wc, tm=512, tn=512):
    m, d = h.shape
    nj = d // tn
    sb = a_outs[0].shape[2] // tm
    row = lambda w: pl.BlockSpec((tm, w), lambda i, j: (i, 0))
    a_spec = pl.BlockSpec((None, A_HEADS_PER_GROUP, tm, HEAD_DIM), lambda i, j: (i // sb, 0, i % sb, 0))
    col = lambda k: pl.BlockSpec((k, tn), lambda i, j: (0, j))
    gate = lambda br: pl.BlockSpec((d, tn), lambda i, j: (0, br * nj + j))
    return pl.pallas_call(
        _merge_kernel,
        grid=(m // tm, nj),
        in_specs=[row(d), gate(0), gate(1), gate(2)]
                 + [a_spec] * 6 + [row(B_QW), row(C_W)]
                 + [col(A_OUT_W), col(B_QW), col(C_W)],
        out_specs=pl.BlockSpec((tm, tn), lambda i, j: (i, j)),
        out_shape=jax.ShapeDtypeStruct((m, d), BF16),
        scratch_shapes=[pltpu.VMEM((tm, A_OUT_W), BF16)],
        compiler_params=_cparams("parallel", "arbitrary"),
        name="merge",
    )(h, w_gate, w_gate, w_gate, *a_outs, *a_lses, ob, oc, wa, wb, wc)


def _out_proj_kernel(mg_ref, w_ref, x_ref, g_ref, x1_ref, h2_ref):
    x1 = x_ref[...] + jnp.dot(mg_ref[...], w_ref[...], preferred_element_type=F32)
    x1_ref[...] = x1
    ms = jnp.mean(x1 * x1, axis=-1, keepdims=True)
    h2_ref[...] = (x1 * lax.rsqrt(ms + EPS) * g_ref[...]).astype(h2_ref.dtype)


def _out_proj(merged, w_o, x2d, g_ffn, tm=512):
    m, d = x2d.shape
    return pl.pallas_call(
        _out_proj_kernel,
        grid=(m // tm,),
        in_specs=[pl.BlockSpec((tm, d), lambda i: (i, 0)),
                  pl.BlockSpec((d, d), lambda i: (0, 0)),
                  pl.BlockSpec((tm, d), lambda i: (i, 0)),
                  pl.BlockSpec((1, d), lambda i: (0, 0))],
        out_specs=[pl.BlockSpec((tm, d), lambda i: (i, 0))] * 2,
        out_shape=[jax.ShapeDtypeStruct((m, d), F32), jax.ShapeDtypeStruct((m, d), BF16)],
        compiler_params=_cparams("parallel"),
        name="out_proj",
    )(merged, w_o, x2d, g_ffn.reshape(1, d))


def _ffn_kernel(h_ref, wa_ref, wb_ref, wo_ref, x1_ref, out_ref):
    @pl.when(pl.program_id(1) == 0)
    def _():
        out_ref[...] = x1_ref[...]

    h = h_ref[...]
    a = jnp.dot(h, wa_ref[...], preferred_element_type=F32)
    b = jnp.dot(h, wb_ref[...], preferred_element_type=F32)
    act = (a * jax.nn.sigmoid(a) * b).astype(BF16)
    out_ref[...] += jnp.dot(act, wo_ref[...], preferred_element_type=F32)


def _ffn(h2, w_in, w_out, x1, tm=512, tf=512):
    m, d = h2.shape
    d_ff = w_out.shape[0]
    nf = d_ff // tf
    return pl.pallas_call(
        _ffn_kernel,
        grid=(m // tm, nf),
        in_specs=[pl.BlockSpec((tm, d), lambda i, f: (i, 0)),
                  pl.BlockSpec((d, tf), lambda i, f: (0, f)),
                  pl.BlockSpec((d, tf), lambda i, f: (0, nf + f)),
                  pl.BlockSpec((tf, d), lambda i, f: (f, 0)),
                  pl.BlockSpec((tm, d), lambda i, f: (i, 0))],
        out_specs=pl.BlockSpec((tm, d), lambda i, f: (i, 0)),
        out_shape=jax.ShapeDtypeStruct((m, d), F32),
        compiler_params=_cparams("parallel", "arbitrary"),
        name="ffn",
    )(h2, w_in, w_in, w_out, x1)


def _deinterleave_cols(w, heads):
    d = w.shape[0]
    return w.reshape(d, heads, HEAD_DIM // 2, 2).transpose(0, 1, 3, 2).reshape(d, heads * HEAD_DIM)


def _rope_tables(seq):
    rows = seq // GRID_W
    r = jnp.repeat(jnp.arange(rows), GRID_W).astype(F32)
    c = jnp.tile(jnp.arange(GRID_W), rows).astype(F32)
    nf = HEAD_DIM // 4
    inv = ROPE_THETA ** (-jnp.arange(nf, dtype=F32) / nf)
    ang = jnp.concatenate([r[:, None] * inv, c[:, None] * inv], axis=-1)
    cos, sin = jnp.cos(ang), jnp.sin(ang)
    return jnp.concatenate([cos, cos], axis=-1), jnp.concatenate([-sin, sin], axis=-1)


def _layer(x2d, mem2d, bias_a, cos_t, sin_t, b, seq, g_mix, w_in, g_qa, g_ka, g_qb, g_kb, g_mem, w_mem_kv,
           g_qc, g_kc, w_br_a, w_br_b, w_br_c, w_o, g_ffn, w_ffn_in, w_ffn_out):
    d = x2d.shape[1]
    o_qa, o_ka, o_va, o_qb, o_kb, o_vb, o_qc, o_gt = np.cumsum(
        (0, A_W, A_W, A_W, B_QW, B_KVW, B_KVW, C_W))
    seg = lambda lo, hi: w_in[:, lo:hi]
    w_norm = jnp.concatenate([seg(o_qa, o_ka), seg(o_ka, o_va), seg(o_qc, o_gt)], axis=1).astype(BF16)
    w_rope = jnp.concatenate([_deinterleave_cols(seg(o_qb, o_kb), B_Q_HEADS),
                              _deinterleave_cols(seg(o_kb, o_vb), B_KV_HEADS)], axis=1).astype(BF16)
    w_val = jnp.concatenate([seg(o_va, o_qb), seg(o_vb, o_qc)], axis=1).astype(BF16)
    w_gate = w_in[:, o_gt:].astype(BF16)
    g_norm = jnp.concatenate([jnp.tile(g_qa, A_HEADS), jnp.tile(g_ka, A_HEADS), jnp.tile(g_qc, C_HEADS)])
    deint = lambda g: g.reshape(HEAD_DIM // 2, 2).T.reshape(HEAD_DIM)
    g_rope = jnp.concatenate([jnp.tile(deint(g_qb), B_Q_HEADS), jnp.tile(deint(g_kb), B_KV_HEADS)])

    h = _rms_cast(x2d, g_mix)
    qkn = _proj(h, w_norm, "norm", g_norm).reshape(b, seq, -1)
    qkr = _proj(h, w_rope, "rope", g_rope, cos_t, sin_t).reshape(b, seq, -1)
    vv = _proj(h, w_val, "plain").reshape(b, seq, -1)

    a_outs, a_lses = [], []
    for gi, (_, dil) in enumerate(DIL_PAIRS):
        o, lse = _mixer_a_group(qkn, vv, bias_a[gi * A_NOFF:(gi + 1) * A_NOFF], gi, dil)
        a_outs.append(o)
        a_lses.append(lse)

    vbt = jnp.swapaxes(vv[:, :, A_W:], 1, 2)
    ob = _mixer_b(qkr, vbt).reshape(b * seq, B_QW)

    kc, vc = _mem_kv(mem2d, g_mem, w_mem_kv.astype(BF16), g_kc)
    n_mem = mem2d.shape[0] // b
    oc = _mixer_c(qkn, kc.reshape(b, n_mem, C_W), vc.reshape(b, n_mem, C_W)).reshape(b * seq, C_W)

    merged = _merge(h, w_gate, a_outs, a_lses, ob, oc,
                    w_br_a.astype(BF16), w_br_b.astype(BF16), w_br_c.astype(BF16))
    x1, h2 = _out_proj(merged, w_o.astype(BF16), x2d, g_ffn)
    return _ffn(h2, w_ffn_in.astype(BF16), w_ffn_out.astype(BF16), x1)


def kernel(x, mem, rel_bias, g_mix, w_in, g_qa, g_ka, g_qb, g_kb, g_mem, w_mem_kv, g_qc, g_kc,
           w_br_a, w_br_b, w_br_c, w_o, g_ffn, w_ffn_in, w_ffn_out):
    b, seq, d = x.shape
    depth = w_in.shape[0]
    cos_t, sin_t = _rope_tables(seq)
    bias_a = _a_bias(rel_bias)
    x2d = x.reshape(b * seq, d)
    mem2d = mem.reshape(-1, d)
    for layer in range(depth):
        x2d = _layer(x2d, mem2d, bias_a, cos_t, sin_t, b, seq,
                     g_mix[layer], w_in[layer], g_qa[layer], g_ka[layer], g_qb[layer], g_kb[layer],
                     g_mem[layer], w_mem_kv[layer], g_qc[layer], g_kc[layer],
                     w_br_a[layer], w_br_b[layer], w_br_c[layer], w_o[layer], g_ffn[layer],
                     w_ffn_in[layer], w_ffn_out[layer])
    return x2d.reshape(b, seq, d)
```

```python
import functools
import math

import numpy as np
import jax
import jax.numpy as jnp
from jax import lax
from jax.experimental import pallas as pl
from jax.experimental.pallas import tpu as pltpu

HEAD_DIM = 128
GRID_W = 64
DIL_PAIRS = ((128, 1), (512, 4), (2048, 16))
A_HEADS_PER_GROUP = 2
A_HEADS = A_HEADS_PER_GROUP * len(DIL_PAIRS)
B_Q_HEADS = 6
B_KV_HEADS = 2
ROPE_THETA = 10000.0
C_HEADS = 4
N_BRANCH = 3
REL_BUCKETS = 32
REL_MAX_DIST = 1024
EPS = 1e-6
NEG = -1e30

A_W = A_HEADS * HEAD_DIM
A_OUT_W = A_HEADS_PER_GROUP * HEAD_DIM
B_QW = B_Q_HEADS * HEAD_DIM
B_KVW = B_KV_HEADS * HEAD_DIM
C_W = C_HEADS * HEAD_DIM

SCALE = 1.0 / math.sqrt(HEAD_DIM)
LOG2E = math.log2(math.e)

A_QROWS = 128
A_KWIN = 256
A_RADIUS = 64
A_NOFF = 3

MXU_N = 256

BF16 = jnp.bfloat16
F32 = jnp.float32

_NT = (((1,), (1,)), ((), ()))


def _cparams(*sem, vmem_mb=None):
    limit = None if vmem_mb is None else vmem_mb * 1024 * 1024
    return pltpu.CompilerParams(dimension_semantics=sem, vmem_limit_bytes=limit)


def _rms_cast_kernel(x_ref, g_ref, o_ref):
    x = x_ref[...]
    ms = jnp.mean(x * x, axis=-1, keepdims=True)
    o_ref[...] = (x * lax.rsqrt(ms + EPS) * g_ref[...]).astype(o_ref.dtype)


def _rms_cast(x2d, g, tm=512):
    m, d = x2d.shape
    return pl.pallas_call(
        _rms_cast_kernel,
        grid=(m // tm,),
        in_specs=[pl.BlockSpec((tm, d), lambda i: (i, 0)),
                  pl.BlockSpec((1, d), lambda i: (0, 0))],
        out_specs=pl.BlockSpec((tm, d), lambda i: (i, 0)),
        out_shape=jax.ShapeDtypeStruct((m, d), BF16),
        compiler_params=_cparams("parallel"),
        name="rms_cast",
    )(x2d, g.reshape(1, d))


def _proj_kernel(*refs, mode, heads):
    if mode == "rope":
        h_ref, w_ref, g_ref, cos_ref, sin_ref, o_ref = refs
    elif mode == "norm":
        h_ref, w_ref, g_ref, o_ref = refs
    else:
        h_ref, w_ref, o_ref = refs
    acc = jnp.dot(h_ref[...], w_ref[...], preferred_element_type=F32)
    for hh in range(heads):
        sl = slice(hh * HEAD_DIM, (hh + 1) * HEAD_DIM)
        y = acc[:, sl]
        if mode != "plain":
            ms = jnp.mean(y * y, axis=-1, keepdims=True)
            y = y * lax.rsqrt(ms + EPS) * g_ref[:, sl]
        if mode == "rope":
            y = y * cos_ref[...] + pltpu.roll(y, HEAD_DIM // 2, 1) * sin_ref[...]
        o_ref[:, sl] = y.astype(o_ref.dtype)


def _proj(h, w, mode, gains=None, cos_t=None, sin_t=None, tm=1024, tn=512):
    m, d = h.shape
    n = w.shape[1]
    tn = min(tn, n)
    assert n % tn == 0 and m % tm == 0
    in_specs = [pl.BlockSpec((tm, d), lambda i, j: (i, 0)),
                pl.BlockSpec((d, tn), lambda i, j: (0, j))]
    args = [h, w]
    if mode != "plain":
        in_specs.append(pl.BlockSpec((1, tn), lambda i, j: (0, j)))
        args.append(gains.reshape(1, n))
    if mode == "rope":
        s_blocks = cos_t.shape[0] // tm
        in_specs += [pl.BlockSpec((tm, HEAD_DIM), lambda i, j: (i % s_blocks, 0))] * 2
        args += [cos_t, sin_t]
    return pl.pallas_call(
        functools.partial(_proj_kernel, mode=mode, heads=tn // HEAD_DIM),
        grid=(m // tm, n // tn),
        in_specs=in_specs,
        out_specs=pl.BlockSpec((tm, tn), lambda i, j: (i, j)),
        out_shape=jax.ShapeDtypeStruct((m, n), BF16),
        compiler_params=_cparams("parallel", "arbitrary"),
        name="proj_" + mode,
    )(*args)


def _proj_t_kernel(wt_ref, h_ref, g_ref, cos_ref, sin_ref, q_ref, v_ref, *, t_sub):
    n_q = q_ref.shape[0] // HEAD_DIM
    n_v = v_ref.shape[0] // HEAD_DIM
    half = HEAD_DIM // 2
    for t0 in range(0, h_ref.shape[0], t_sub):
        tok = slice(t0, t0 + t_sub)
        yt = lax.dot_general(wt_ref[...], h_ref[tok, :], _NT, preferred_element_type=F32)
        for hh in range(n_q):
            y = yt[hh * HEAD_DIM:(hh + 1) * HEAD_DIM]
            ms = jnp.mean(y * y, axis=0, keepdims=True)
            y = y * lax.rsqrt(ms + EPS) * g_ref[...]
            partner = jnp.concatenate([y[half:], y[:half]], axis=0)
            y = y * cos_ref[:, tok] + partner * sin_ref[:, tok]
            q_ref[hh * HEAD_DIM:(hh + 1) * HEAD_DIM, tok] = y.astype(q_ref.dtype)
        for hh in range(n_v):
            rows = slice((n_q + hh) * HEAD_DIM, (n_q + hh + 1) * HEAD_DIM)
            v_ref[hh * HEAD_DIM:(hh + 1) * HEAD_DIM, tok] = yt[rows].astype(v_ref.dtype)


def _proj_t(h, wt, gain_col, cos_tt, sin_tt, b, seq, n_q_rows, tm=1024, t_sub=256):
    m, d = h.shape
    n = wt.shape[0]
    n_v_rows = n - n_q_rows
    sb = seq // tm
    return pl.pallas_call(
        functools.partial(_proj_t_kernel, t_sub=t_sub),
        grid=(m // tm,),
        in_specs=[pl.BlockSpec((n, d), lambda i: (0, 0)),
                  pl.BlockSpec((tm, d), lambda i: (i, 0)),
                  pl.BlockSpec((HEAD_DIM, t_sub), lambda i: (0, 0)),
                  pl.BlockSpec((HEAD_DIM, tm), lambda i: (0, i % sb)),
                  pl.BlockSpec((HEAD_DIM, tm), lambda i: (0, i % sb))],
        out_specs=[pl.BlockSpec((None, n_q_rows, tm), lambda i: (i // sb, 0, i % sb)),
                   pl.BlockSpec((None, n_v_rows, tm), lambda i: (i // sb, 0, i % sb))],
        out_shape=[jax.ShapeDtypeStruct((b, n_q_rows, seq), BF16),
                   jax.ShapeDtypeStruct((b, n_v_rows, seq), BF16)],
        compiler_params=_cparams("parallel"),
        name="proj_t",
    )(wt, h, jnp.broadcast_to(gain_col[:, None], (HEAD_DIM, t_sub)), cos_tt, sin_tt)


def _mem_kv_kernel(mem_ref, gm_ref, w_ref, gk_ref, k_ref, v_ref):
    x = mem_ref[...]
    ms = jnp.mean(x * x, axis=-1, keepdims=True)
    hm = (x * lax.rsqrt(ms + EPS) * gm_ref[...]).astype(BF16)
    kv = jnp.dot(hm, w_ref[...], preferred_element_type=F32)
    for hh in range(C_HEADS):
        sl = slice(hh * HEAD_DIM, (hh + 1) * HEAD_DIM)
        y = kv[:, sl]
        ms = jnp.mean(y * y, axis=-1, keepdims=True)
        k_ref[:, sl] = (y * lax.rsqrt(ms + EPS) * gk_ref[...]).astype(BF16)
    v_ref[...] = kv[:, C_W:].astype(BF16)


def _mem_kv(mem2d, g_mem, w_kv, g_kc, tm=256):
    m, d = mem2d.shape
    return pl.pallas_call(
        _mem_kv_kernel,
        grid=(m // tm,),
        in_specs=[pl.BlockSpec((tm, d), lambda i: (i, 0)),
                  pl.BlockSpec((1, d), lambda i: (0, 0)),
                  pl.BlockSpec((d, 2 * C_W), lambda i: (0, 0)),
                  pl.BlockSpec((1, HEAD_DIM), lambda i: (0, 0))],
        out_specs=[pl.BlockSpec((tm, C_W), lambda i: (i, 0))] * 2,
        out_shape=[jax.ShapeDtypeStruct((m, C_W), BF16)] * 2,
        compiler_params=_cparams("parallel"),
        name="mem_kv",
    )(mem2d, g_mem.reshape(1, d), w_kv, g_kc.reshape(1, HEAD_DIM))


def _t5_bucket(rel):
    nb = REL_BUCKETS // 2
    ret = jnp.where(rel > 0, nb, 0)
    n = jnp.abs(rel)
    max_exact = nb // 2
    large = max_exact + (jnp.log(jnp.maximum(n, 1).astype(jnp.float32) / max_exact)
                         / math.log(REL_MAX_DIST / max_exact) * (nb - max_exact)).astype(jnp.int32)
    large = jnp.minimum(large, nb - 1)
    return ret + jnp.where(n < max_exact, n, large)


def _a_bucket_index():
    qi = jnp.arange(A_QROWS, dtype=jnp.int32)[:, None]
    kj = jnp.arange(A_KWIN, dtype=jnp.int32)[None, :]
    out = []
    for _, dil in DIL_PAIRS:
        for off in range(A_NOFF):
            rel = kj - qi - A_RADIUS * off
            out.append(jnp.where(jnp.abs(rel) <= A_RADIUS, _t5_bucket(rel * dil), -1))
    return jnp.stack(out)


def _a_bias_kernel(tab_ref, bucket_ref, o_ref):
    g = pl.program_id(0) // A_NOFF
    bk = bucket_ref[...]
    for hh in range(A_HEADS_PER_GROUP):
        acc = jnp.full(bk.shape, NEG, F32)
        for b in range(REL_BUCKETS):
            acc = jnp.where(bk == b, tab_ref[b, g * A_HEADS_PER_GROUP + hh], acc)
        o_ref[hh] = acc


def _a_bias(rel_bias):
    n = len(DIL_PAIRS) * A_NOFF
    return pl.pallas_call(
        _a_bias_kernel,
        grid=(n,),
        in_specs=[pl.BlockSpec(memory_space=pltpu.SMEM),
                  pl.BlockSpec((None, A_QROWS, A_KWIN), lambda i: (i, 0, 0))],
        out_specs=pl.BlockSpec((None, A_HEADS_PER_GROUP, A_QROWS, A_KWIN), lambda i: (i, 0, 0, 0)),
        out_shape=jax.ShapeDtypeStruct((n, A_HEADS_PER_GROUP, A_QROWS, A_KWIN), F32),
        compiler_params=_cparams("arbitrary"),
        name="a_bias",
    )(rel_bias, _a_bucket_index())


def _mixer_a_kernel(q_ref, k_ref, v_ref, bias_ref, o_ref, lse_ref, *scratch, dil, seq):
    t_rows = q_ref.shape[0]
    ti = pl.program_id(1)
    sub_len = seq // dil
    lq = t_rows // dil
    if dil > 1:
        stage, qstage, kres, vres = scratch

        @pl.when(ti == 0)
        def _():
            for src, dst in ((k_ref, kres), (v_ref, vres)):
                for hh in range(A_HEADS_PER_GROUP):
                    cols = slice(hh * HEAD_DIM, (hh + 1) * HEAD_DIM)
                    stage[...] = src[:, cols].astype(F32)
                    for r in range(dil):
                        dst[r, :, cols] = stage[pl.ds(r, sub_len, stride=dil), :].astype(BF16)

        for hh in range(A_HEADS_PER_GROUP):
            qstage[hh] = q_ref[:, hh * HEAD_DIM:(hh + 1) * HEAD_DIM].astype(F32)

    for r in range(dil):
        for i in range(lq // A_QROWS):
            q0 = ti * lq + i * A_QROWS
            ks = jnp.clip(q0 - A_RADIUS, 0, sub_len - A_KWIN)
            off = lax.shift_right_logical(q0 - ks, int(math.log2(A_RADIUS)))
            ks = pl.multiple_of(ks, A_RADIUS)
            if dil > 1:
                rows = pl.ds(i * A_QROWS * dil + r, A_QROWS, stride=dil)
            else:
                rows = pl.ds(i * A_QROWS, A_QROWS)
            for hh in range(A_HEADS_PER_GROUP):
                cols = slice(hh * HEAD_DIM, (hh + 1) * HEAD_DIM)
                if dil > 1:
                    q = qstage[hh, rows, :].astype(BF16)
                    k = kres[r, pl.ds(ks, A_KWIN), cols]
                    v = vres[r, pl.ds(ks, A_KWIN), cols]
                else:
                    q = q_ref[rows, cols]
                    k = k_ref[pl.ds(ks, A_KWIN), cols]
                    v = v_ref[pl.ds(ks, A_KWIN), cols]
                s = lax.dot_general(q, k, _NT, preferred_element_type=F32) * SCALE + bias_ref[off, hh]
                m = jnp.max(s, axis=-1, keepdims=True)
                p = jnp.exp(s - m)
                l = jnp.sum(p, axis=-1, keepdims=True)
                o_ref[hh, rows, :] = jnp.dot(p.astype(BF16), v, preferred_element_type=F32) / l
                lse_ref[hh, rows, :] = jnp.broadcast_to(m + jnp.log(l), (A_QROWS, HEAD_DIM))


def _mixer_a_group(qkn, vv, bias_g, gi, dil, t_rows=2048):
    b, seq, _ = qkn.shape
    sub_len = seq // dil
    k_blk = A_W // A_OUT_W
    scratch = []
    if dil > 1:
        scratch = [pltpu.VMEM((seq, HEAD_DIM), F32), pltpu.VMEM((A_HEADS_PER_GROUP, t_rows, HEAD_DIM), F32),
                   pltpu.VMEM((dil, sub_len, A_OUT_W), BF16), pltpu.VMEM((dil, sub_len, A_OUT_W), BF16)]
    return pl.pallas_call(
        functools.partial(_mixer_a_kernel, dil=dil, seq=seq),
        grid=(b, seq // t_rows),
        in_specs=[pl.BlockSpec((None, t_rows, A_OUT_W), lambda bi, ti: (bi, ti, gi)),
                  pl.BlockSpec((None, seq, A_OUT_W), lambda bi, ti: (bi, 0, k_blk + gi)),
                  pl.BlockSpec((None, seq, A_OUT_W), lambda bi, ti: (bi, 0, gi)),
                  pl.BlockSpec((A_NOFF, A_HEADS_PER_GROUP, A_QROWS, A_KWIN), lambda bi, ti: (0, 0, 0, 0))],
        out_specs=[pl.BlockSpec((None, A_HEADS_PER_GROUP, t_rows, HEAD_DIM), lambda bi, ti: (bi, 0, ti, 0))] * 2,
        out_shape=[jax.ShapeDtypeStruct((b, A_HEADS_PER_GROUP, seq, HEAD_DIM), F32)] * 2,
        scratch_shapes=scratch,
        compiler_params=_cparams("parallel", "arbitrary"),
        name=f"mixer_a_d{dil}",
    )(qkn, qkn, vv, bias_g)


def _mixer_b_kernel(qt_ref, k_ref, vt_ref, o_ref, *, tk, group):
    tq = qt_ref.shape[1]
    seq = k_ref.shape[0]
    c = SCALE * LOG2E
    rows = group * tq
    qt = jnp.concatenate([qt_ref[i * HEAD_DIM:(i + 1) * HEAD_DIM, :] for i in range(group)], axis=1)
    n_chunks = seq // tk

    def scores(ci):
        return jnp.dot(k_ref[ci * tk:(ci + 1) * tk, :], qt, preferred_element_type=F32)

    m = jnp.full((1, rows), NEG, F32)
    l = jnp.zeros((1, rows), F32)
    acc = jnp.zeros((HEAD_DIM, rows), F32)
    st = scores(0)
    for ci in range(n_chunks):
        st_next = scores(ci + 1) if ci + 1 < n_chunks else None
        m_new = jnp.maximum(m, jnp.max(st, axis=0, keepdims=True))
        alpha = jnp.exp2((m - m_new) * c)
        pt = jnp.exp2((st - m_new) * c)
        l = alpha * l + jnp.sum(pt, axis=0, keepdims=True)
        vt = vt_ref[:, ci * tk:(ci + 1) * tk]
        acc = alpha * acc + jnp.dot(vt, pt.astype(BF16), preferred_element_type=F32)
        m, st = m_new, st_next
    o = acc / l
    for i in range(group):
        o_ref[:, i * HEAD_DIM:(i + 1) * HEAD_DIM] = o[:, i * tq:(i + 1) * tq].T.astype(o_ref.dtype)


def _mixer_b(qbt, kb, vbt, tq=256, tk=512):
    b, seq, _ = kb.shape
    group = B_Q_HEADS // B_KV_HEADS
    gw = group * HEAD_DIM
    return pl.pallas_call(
        functools.partial(_mixer_b_kernel, tk=tk, group=group),
        grid=(b, B_KV_HEADS, seq // tq),
        in_specs=[pl.BlockSpec((None, gw, tq), lambda bi, kv, qi: (bi, kv, qi)),
                  pl.BlockSpec((None, seq, HEAD_DIM), lambda bi, kv, qi: (bi, 0, kv)),
                  pl.BlockSpec((None, HEAD_DIM, seq), lambda bi, kv, qi: (bi, kv, 0))],
        out_specs=pl.BlockSpec((None, tq, gw), lambda bi, kv, qi: (bi, qi, kv)),
        out_shape=jax.ShapeDtypeStruct((b, seq, B_QW), BF16),
        compiler_params=_cparams("parallel", "parallel", "arbitrary"),
        name="mixer_b",
    )(qbt, kb, vbt)


def _mixer_c_kernel(q_ref, k_ref, v_ref, o_ref):
    for hh in range(C_HEADS):
        sl = slice(hh * HEAD_DIM, (hh + 1) * HEAD_DIM)
        s = lax.dot_general(q_ref[:, sl], k_ref[:, sl], _NT, preferred_element_type=F32) * SCALE
        m = jnp.max(s, axis=-1, keepdims=True)
        p = jnp.exp(s - m)
        l = jnp.sum(p, axis=-1, keepdims=True)
        o = jnp.dot(p.astype(BF16), v_ref[:, sl], preferred_element_type=F32) / l
        o_ref[:, sl] = o.astype(o_ref.dtype)


def _mixer_c(qkn, kc, vc, tq=512):
    b, seq, _ = qkn.shape
    n_mem = kc.shape[1]
    qc_blk = (2 * A_W) // C_W
    return pl.pallas_call(
        _mixer_c_kernel,
        grid=(b, seq // tq),
        in_specs=[pl.BlockSpec((None, tq, C_W), lambda bi, qi: (bi, qi, qc_blk)),
                  pl.BlockSpec((None, n_mem, C_W), lambda bi, qi: (bi, 0, 0)),
                  pl.BlockSpec((None, n_mem, C_W), lambda bi, qi: (bi, 0, 0))],
        out_specs=pl.BlockSpec((None, tq, C_W), lambda bi, qi: (bi, qi, 0)),
        out_shape=jax.ShapeDtypeStruct((b, seq, C_W), BF16),
        compiler_params=_cparams("parallel", "arbitrary"),
        name="mixer_c",
    )(qkn, kc, vc)


def _merge_kernel(h_ref, wg0_ref, wg1_ref, wg2_ref,
                  o0_ref, o1_ref, o2_ref, l0_ref, l1_ref, l2_ref, ob_ref, oc_ref,
                  wa_ref, wb_ref, wc_ref, out_ref, oa_ref):
    @pl.when(pl.program_id(1) == 0)
    def _():
        for hh in range(A_HEADS_PER_GROUP):
            l0, l1, l2 = l0_ref[hh], l1_ref[hh], l2_ref[hh]
            m = jnp.maximum(jnp.maximum(l0, l1), l2)
            w0, w1, w2 = jnp.exp(l0 - m), jnp.exp(l1 - m), jnp.exp(l2 - m)
            oa = (w0 * o0_ref[hh] + w1 * o1_ref[hh] + w2 * o2_ref[hh]) / (w0 + w1 + w2)
            oa_ref[:, hh * HEAD_DIM:(hh + 1) * HEAD_DIM] = oa.astype(oa_ref.dtype)

    h = h_ref[...]
    ga = jax.nn.sigmoid(jnp.dot(h, wg0_ref[...], preferred_element_type=F32))
    merged = ga * jnp.dot(oa_ref[...], wa_ref[...], preferred_element_type=F32)
    gb = jax.nn.sigmoid(jnp.dot(h, wg1_ref[...], preferred_element_type=F32))
    merged += gb * jnp.dot(ob_ref[...], wb_ref[...], preferred_element_type=F32)
    gc = jax.nn.sigmoid(jnp.dot(h, wg2_ref[...], preferred_element_type=F32))
    merged += gc * jnp.dot(oc_ref[...], wc_ref[...], preferred_element_type=F32)
    out_ref[...] = merged.astype(out_ref.dtype)


def _merge(h, w_gate, a_outs, a_lses, ob, oc, wa, wb, wc, tm=512, tn=512):
    m, d = h.shape
    nj = d // tn
    sb = a_outs[0].shape[2] // tm
    row = lambda w: pl.BlockSpec((tm, w), lambda i, j: (i, 0))
    a_spec = pl.BlockSpec((None, A_HEADS_PER_GROUP, tm, HEAD_DIM), lambda i, j: (i // sb, 0, i % sb, 0))
    col = lambda k: pl.BlockSpec((k, tn), lambda i, j: (0, j))
    gate = lambda br: pl.BlockSpec((d, tn), lambda i, j: (0, br * nj + j))
    return pl.pallas_call(
        _merge_kernel,
        grid=(m // tm, nj),
        in_specs=[row(d), gate(0), gate(1), gate(2)]
                 + [a_spec] * 6 + [row(B_QW), row(C_W)]
                 + [col(A_OUT_W), col(B_QW), col(C_W)],
        out_specs=pl.BlockSpec((tm, tn), lambda i, j: (i, j)),
        out_shape=jax.ShapeDtypeStruct((m, d), BF16),
        scratch_shapes=[pltpu.VMEM((tm, A_OUT_W), BF16)],
        compiler_params=_cparams("parallel", "arbitrary"),
        name="merge",
    )(h, w_gate, w_gate, w_gate, *a_outs, *a_lses, ob, oc, wa, wb, wc)


def _out_proj_kernel(mg_ref, w_ref, x_ref, g_ref, x1_ref, h2_ref):
    x1 = x_ref[...] + jnp.dot(mg_ref[...], w_ref[...], preferred_element_type=F32)
    x1_ref[...] = x1
    ms = jnp.mean(x1 * x1, axis=-1, keepdims=True)
    h2_ref[...] = (x1 * lax.rsqrt(ms + EPS) * g_ref[...]).astype(h2_ref.dtype)


def _out_proj(merged, w_o, x2d, g_ffn, tm=512):
    m, d = x2d.shape
    return pl.pallas_call(
        _out_proj_kernel,
        grid=(m // tm,),
        in_specs=[pl.BlockSpec((tm, d), lambda i: (i, 0)),
                  pl.BlockSpec((d, d), lambda i: (0, 0)),
                  pl.BlockSpec((tm, d), lambda i: (i, 0)),
                  pl.BlockSpec((1, d), lambda i: (0, 0))],
        out_specs=[pl.BlockSpec((tm, d), lambda i: (i, 0))] * 2,
        out_shape=[jax.ShapeDtypeStruct((m, d), F32), jax.ShapeDtypeStruct((m, d), BF16)],
        compiler_params=_cparams("parallel"),
        name="out_proj",
    )(merged, w_o, x2d, g_ffn.reshape(1, d))


def _ffn_kernel(h_ref, wa_ref, wb_ref, wo_ref, x1_ref, out_ref, *, n_slabs):
    f = pl.program_id(1)
    slab_w = x1_ref.shape[1]

    @pl.when(f == 0)
    def _():
        out_ref[...] = jnp.zeros_like(out_ref)

    for s in range(n_slabs):
        @pl.when(f == s)
        def _():
            out_ref[:, s * slab_w:(s + 1) * slab_w] += x1_ref[...]

    h = h_ref[...]
    a = jnp.dot(h, wa_ref[...], preferred_element_type=F32)
    b = jnp.dot(h, wb_ref[...], preferred_element_type=F32)
    act = (a * jax.nn.sigmoid(a) * b).astype(BF16)
    out_ref[...] += jnp.dot(act, wo_ref[...], preferred_element_type=F32)


def _ffn(h2, w_in, w_out, x1, tm=1024, tf=512, slab_w=256):
    m, d = h2.shape
    d_ff = w_out.shape[0]
    nf = d_ff // tf
    n_slabs = d // slab_w
    assert n_slabs <= nf
    return pl.pallas_call(
        functools.partial(_ffn_kernel, n_slabs=n_slabs),
        grid=(m // tm, nf),
        in_specs=[pl.BlockSpec((tm, d), lambda i, f: (i, 0)),
                  pl.BlockSpec((d, tf), lambda i, f: (0, f)),
                  pl.BlockSpec((d, tf), lambda i, f: (0, nf + f)),
                  pl.BlockSpec((tf, d), lambda i, f: (f, 0)),
                  pl.BlockSpec((tm, slab_w), lambda i, f: (i, jnp.minimum(f, n_slabs - 1)))],
        out_specs=pl.BlockSpec((tm, d), lambda i, f: (i, 0)),
        out_shape=jax.ShapeDtypeStruct((m, d), F32),
        compiler_params=_cparams("parallel", "arbitrary", vmem_mb=48),
        name="ffn",
    )(h2, w_in, w_in, w_out, x1)


def _deinterleave_cols(w, heads):
    d = w.shape[0]
    return w.reshape(d, heads, HEAD_DIM // 2, 2).transpose(0, 1, 3, 2).reshape(d, heads * HEAD_DIM)


def _rope_tables(seq):
    rows = seq // GRID_W
    r = jnp.repeat(jnp.arange(rows), GRID_W).astype(F32)
    c = jnp.tile(jnp.arange(GRID_W), rows).astype(F32)
    nf = HEAD_DIM // 4
    inv = ROPE_THETA ** (-jnp.arange(nf, dtype=F32) / nf)
    ang = jnp.concatenate([r[:, None] * inv, c[:, None] * inv], axis=-1)
    cos, sin = jnp.cos(ang), jnp.sin(ang)
    return jnp.concatenate([cos, cos], axis=-1), jnp.concatenate([-sin, sin], axis=-1)


def _layer(x2d, mem2d, bias_a, cos_t, sin_t, b, seq, g_mix, w_in, g_qa, g_ka, g_qb, g_kb, g_mem, w_mem_kv,
           g_qc, g_kc, w_br_a, w_br_b, w_br_c, w_o, g_ffn, w_ffn_in, w_ffn_out):
    d = x2d.shape[1]
    o_qa, o_ka, o_va, o_qb, o_kb, o_vb, o_qc, o_gt = np.cumsum(
        (0, A_W, A_W, A_W, B_QW, B_KVW, B_KVW, C_W))
    seg = lambda lo, hi: w_in[:, lo:hi]
    w_norm = jnp.concatenate([seg(o_qa, o_ka), seg(o_ka, o_va), seg(o_qc, o_gt)], axis=1).astype(BF16)
    w_kb = _deinterleave_cols(seg(o_kb, o_vb), B_KV_HEADS).astype(BF16)
    wt_qv = jnp.concatenate([_deinterleave_cols(seg(o_qb, o_kb), B_Q_HEADS), seg(o_vb, o_qc)], axis=1).T.astype(BF16)
    w_va = seg(o_va, o_qb).astype(BF16)
    w_gate = w_in[:, o_gt:].astype(BF16)
    g_norm = jnp.concatenate([jnp.tile(g_qa, A_HEADS), jnp.tile(g_ka, A_HEADS), jnp.tile(g_qc, C_HEADS)])
    deint = lambda g: g.reshape(HEAD_DIM // 2, 2).T.reshape(HEAD_DIM)

    h = _rms_cast(x2d, g_mix)
    qkn = _proj(h, w_norm, "norm", g_norm).reshape(b, seq, -1)
    kb = _proj(h, w_kb, "rope", jnp.tile(deint(g_kb), B_KV_HEADS), cos_t, sin_t).reshape(b, seq, -1)
    va = _proj(h, w_va, "plain", tn=A_W).reshape(b, seq, -1)
    qbt, vbt = _proj_t(h, wt_qv, deint(g_qb), cos_t.T, sin_t.T, b, seq, B_QW)

    a_outs, a_lses = [], []
    for gi, (_, dil) in enumerate(DIL_PAIRS):
        o, lse = _mixer_a_group(qkn, va, bias_a[gi * A_NOFF:(gi + 1) * A_NOFF], gi, dil)
        a_outs.append(o)
        a_lses.append(lse)

    ob = _mixer_b(qbt, kb, vbt).reshape(b * seq, B_QW)

    kc, vc = _mem_kv(mem2d, g_mem, w_mem_kv.astype(BF16), g_kc)
    n_mem = mem2d.shape[0] // b
    oc = _mixer_c(qkn, kc.reshape(b, n_mem, C_W), vc.reshape(b, n_mem, C_W)).reshape(b * seq, C_W)

    merged = _merge(h, w_gate, a_outs, a_lses, ob, oc,
                    w_br_a.astype(BF16), w_br_b.astype(BF16), w_br_c.astype(BF16))
    x1, h2 = _out_proj(merged, w_o.astype(BF16), x2d, g_ffn)
    return _ffn(h2, w_ffn_in.astype(BF16), w_ffn_out.astype(BF16), x1)


def kernel(x, mem, rel_bias, g_mix, w_in, g_qa, g_ka, g_qb, g_kb, g_mem, w_mem_kv, g_qc, g_kc,
           w_br_a, w_br_b, w_br_c, w_o, g_ffn, w_ffn_in, w_ffn_out):
    b, seq, d = x.shape
    depth = w_in.shape[0]
    cos_t, sin_t = _rope_tables(seq)
    bias_a = _a_bias(rel_bias)
    x2d = x.reshape(b * seq, d)
    mem2d = mem.reshape(-1, d)
    for layer in range(depth):
        x2d = _layer(x2d, mem2d, bias_a, cos_t, sin_t, b, seq,
                     g_mix[layer], w_in[layer], g_qa[layer], g_ka[layer], g_qb[layer], g_kb[layer],
                     g_mem[layer], w_mem_kv[layer], g_qc[layer], g_kc[layer],
                     w_br_a[layer], w_br_b[layer], w_br_c[layer], w_o[layer], g_ffn[layer],
                     w_ffn_in[layer], w_ffn_out[layer])
    return x2d.reshape(b, seq, d)
```

```python
import functools
import math

import numpy as np
import jax
import jax.numpy as jnp
from jax import lax
from jax.experimental import pallas as pl
from jax.experimental.pallas import tpu as pltpu

HEAD_DIM = 128
GRID_W = 64
DIL_PAIRS = ((128, 1), (512, 4), (2048, 16))
A_HEADS_PER_GROUP = 2
A_HEADS = A_HEADS_PER_GROUP * len(DIL_PAIRS)
B_Q_HEADS = 6
B_KV_HEADS = 2
ROPE_THETA = 10000.0
C_HEADS = 4
N_BRANCH = 3
REL_BUCKETS = 32
REL_MAX_DIST = 1024
EPS = 1e-6
NEG = -1e30

A_W = A_HEADS * HEAD_DIM
A_OUT_W = A_HEADS_PER_GROUP * HEAD_DIM
B_QW = B_Q_HEADS * HEAD_DIM
B_KVW = B_KV_HEADS * HEAD_DIM
C_W = C_HEADS * HEAD_DIM

SCALE = 1.0 / math.sqrt(HEAD_DIM)
LOG2E = math.log2(math.e)

A_QROWS = 128
A_KWIN = 256
A_RADIUS = 64
A_NOFF = 3
A_BATCH = 8

MXU_N = 256

BF16 = jnp.bfloat16
F32 = jnp.float32

_NT = (((1,), (1,)), ((), ()))


def _cparams(*sem, vmem_mb=None):
    limit = None if vmem_mb is None else vmem_mb * 1024 * 1024
    return pltpu.CompilerParams(dimension_semantics=sem, vmem_limit_bytes=limit)


def _rms_cast_kernel(x_ref, g_ref, o_ref):
    x = x_ref[...]
    ms = jnp.mean(x * x, axis=-1, keepdims=True)
    o_ref[...] = (x * lax.rsqrt(ms + EPS) * g_ref[...]).astype(o_ref.dtype)


def _rms_cast(x2d, g, tm=512):
    m, d = x2d.shape
    return pl.pallas_call(
        _rms_cast_kernel,
        grid=(m // tm,),
        in_specs=[pl.BlockSpec((tm, d), lambda i: (i, 0)),
                  pl.BlockSpec((1, d), lambda i: (0, 0))],
        out_specs=pl.BlockSpec((tm, d), lambda i: (i, 0)),
        out_shape=jax.ShapeDtypeStruct((m, d), BF16),
        compiler_params=_cparams("parallel"),
        name="rms_cast",
    )(x2d, g.reshape(1, d))


def _proj_kernel(*refs, mode, heads):
    if mode == "rope":
        h_ref, w_ref, g_ref, cos_ref, sin_ref, o_ref = refs
    elif mode == "norm":
        h_ref, w_ref, g_ref, o_ref = refs
    else:
        h_ref, w_ref, o_ref = refs
    acc = jnp.dot(h_ref[...], w_ref[...], preferred_element_type=F32)
    for hh in range(heads):
        sl = slice(hh * HEAD_DIM, (hh + 1) * HEAD_DIM)
        y = acc[:, sl]
        if mode != "plain":
            ms = jnp.mean(y * y, axis=-1, keepdims=True)
            y = y * lax.rsqrt(ms + EPS) * g_ref[:, sl]
        if mode == "rope":
            y = y * cos_ref[...] + pltpu.roll(y, HEAD_DIM // 2, 1) * sin_ref[...]
        o_ref[:, sl] = y.astype(o_ref.dtype)


def _proj(h, w, mode, gains=None, cos_t=None, sin_t=None, tm=1024, tn=512):
    m, d = h.shape
    n = w.shape[1]
    tn = min(tn, n)
    assert n % tn == 0 and m % tm == 0
    in_specs = [pl.BlockSpec((tm, d), lambda i, j: (i, 0)),
                pl.BlockSpec((d, tn), lambda i, j: (0, j))]
    args = [h, w]
    if mode != "plain":
        in_specs.append(pl.BlockSpec((1, tn), lambda i, j: (0, j)))
        args.append(gains.reshape(1, n))
    if mode == "rope":
        s_blocks = cos_t.shape[0] // tm
        in_specs += [pl.BlockSpec((tm, HEAD_DIM), lambda i, j: (i % s_blocks, 0))] * 2
        args += [cos_t, sin_t]
    return pl.pallas_call(
        functools.partial(_proj_kernel, mode=mode, heads=tn // HEAD_DIM),
        grid=(m // tm, n // tn),
        in_specs=in_specs,
        out_specs=pl.BlockSpec((tm, tn), lambda i, j: (i, j)),
        out_shape=jax.ShapeDtypeStruct((m, n), BF16),
        compiler_params=_cparams("parallel", "arbitrary"),
        name="proj_" + mode,
    )(*args)


def _proj_t_kernel(wt_ref, h_ref, g_ref, cos_ref, sin_ref, q_ref, v_ref, *, t_sub):
    n_q = q_ref.shape[0] // HEAD_DIM
    n_v = v_ref.shape[0] // HEAD_DIM
    half = HEAD_DIM // 2
    for t0 in range(0, h_ref.shape[0], t_sub):
        tok = slice(t0, t0 + t_sub)
        yt = lax.dot_general(wt_ref[...], h_ref[tok, :], _NT, preferred_element_type=F32)
        for hh in range(n_q):
            y = yt[hh * HEAD_DIM:(hh + 1) * HEAD_DIM]
            ms = jnp.mean(y * y, axis=0, keepdims=True)
            y = y * lax.rsqrt(ms + EPS) * g_ref[...]
            partner = jnp.concatenate([y[half:], y[:half]], axis=0)
            y = y * cos_ref[:, tok] + partner * sin_ref[:, tok]
            q_ref[hh * HEAD_DIM:(hh + 1) * HEAD_DIM, tok] = y.astype(q_ref.dtype)
        for hh in range(n_v):
            rows = slice((n_q + hh) * HEAD_DIM, (n_q + hh + 1) * HEAD_DIM)
            v_ref[hh * HEAD_DIM:(hh + 1) * HEAD_DIM, tok] = yt[rows].astype(v_ref.dtype)


def _proj_t(h, wt, gain_col, cos_tt, sin_tt, b, seq, n_q_rows, tm=1024, t_sub=256):
    m, d = h.shape
    n = wt.shape[0]
    n_v_rows = n - n_q_rows
    sb = seq // tm
    return pl.pallas_call(
        functools.partial(_proj_t_kernel, t_sub=t_sub),
        grid=(m // tm,),
        in_specs=[pl.BlockSpec((n, d), lambda i: (0, 0)),
                  pl.BlockSpec((tm, d), lambda i: (i, 0)),
                  pl.BlockSpec((HEAD_DIM, t_sub), lambda i: (0, 0)),
                  pl.BlockSpec((HEAD_DIM, tm), lambda i: (0, i % sb)),
                  pl.BlockSpec((HEAD_DIM, tm), lambda i: (0, i % sb))],
        out_specs=[pl.BlockSpec((None, n_q_rows, tm), lambda i: (i // sb, 0, i % sb)),
                   pl.BlockSpec((None, n_v_rows, tm), lambda i: (i // sb, 0, i % sb))],
        out_shape=[jax.ShapeDtypeStruct((b, n_q_rows, seq), BF16),
                   jax.ShapeDtypeStruct((b, n_v_rows, seq), BF16)],
        compiler_params=_cparams("parallel"),
        name="proj_t",
    )(wt, h, jnp.broadcast_to(gain_col[:, None], (HEAD_DIM, t_sub)), cos_tt, sin_tt)


def _mem_kv_kernel(mem_ref, gm_ref, w_ref, gk_ref, k_ref, v_ref):
    x = mem_ref[...]
    ms = jnp.mean(x * x, axis=-1, keepdims=True)
    hm = (x * lax.rsqrt(ms + EPS) * gm_ref[...]).astype(BF16)
    kv = jnp.dot(hm, w_ref[...], preferred_element_type=F32)
    for hh in range(C_HEADS):
        sl = slice(hh * HEAD_DIM, (hh + 1) * HEAD_DIM)
        y = kv[:, sl]
        ms = jnp.mean(y * y, axis=-1, keepdims=True)
        k_ref[:, sl] = (y * lax.rsqrt(ms + EPS) * gk_ref[...]).astype(BF16)
    v_ref[...] = kv[:, C_W:].astype(BF16)


def _mem_kv(mem2d, g_mem, w_kv, g_kc, tm=256):
    m, d = mem2d.shape
    return pl.pallas_call(
        _mem_kv_kernel,
        grid=(m // tm,),
        in_specs=[pl.BlockSpec((tm, d), lambda i: (i, 0)),
                  pl.BlockSpec((1, d), lambda i: (0, 0)),
                  pl.BlockSpec((d, 2 * C_W), lambda i: (0, 0)),
                  pl.BlockSpec((1, HEAD_DIM), lambda i: (0, 0))],
        out_specs=[pl.BlockSpec((tm, C_W), lambda i: (i, 0))] * 2,
        out_shape=[jax.ShapeDtypeStruct((m, C_W), BF16)] * 2,
        compiler_params=_cparams("parallel"),
        name="mem_kv",
    )(mem2d, g_mem.reshape(1, d), w_kv, g_kc.reshape(1, HEAD_DIM))


def _t5_bucket(rel):
    nb = REL_BUCKETS // 2
    ret = jnp.where(rel > 0, nb, 0)
    n = jnp.abs(rel)
    max_exact = nb // 2
    large = max_exact + (jnp.log(jnp.maximum(n, 1).astype(jnp.float32) / max_exact)
                         / math.log(REL_MAX_DIST / max_exact) * (nb - max_exact)).astype(jnp.int32)
    large = jnp.minimum(large, nb - 1)
    return ret + jnp.where(n < max_exact, n, large)


def _a_bucket_index():
    qi = jnp.arange(A_QROWS, dtype=jnp.int32)[:, None]
    kj = jnp.arange(A_KWIN, dtype=jnp.int32)[None, :]
    out = []
    for _, dil in DIL_PAIRS:
        for off in range(A_NOFF):
            rel = kj - qi - A_RADIUS * off
            out.append(jnp.where(jnp.abs(rel) <= A_RADIUS, _t5_bucket(rel * dil), -1))
    return jnp.stack(out)


def _a_bias_kernel(tab_ref, bucket_ref, o_ref):
    g = pl.program_id(0) // A_NOFF
    bk = bucket_ref[...]
    for hh in range(A_HEADS_PER_GROUP):
        acc = jnp.full(bk.shape, NEG, F32)
        for b in range(REL_BUCKETS):
            acc = jnp.where(bk == b, tab_ref[b, g * A_HEADS_PER_GROUP + hh], acc)
        o_ref[hh] = acc


def _a_bias(rel_bias):
    n = len(DIL_PAIRS) * A_NOFF
    return pl.pallas_call(
        _a_bias_kernel,
        grid=(n,),
        in_specs=[pl.BlockSpec(memory_space=pltpu.SMEM),
                  pl.BlockSpec((None, A_QROWS, A_KWIN), lambda i: (i, 0, 0))],
        out_specs=pl.BlockSpec((None, A_HEADS_PER_GROUP, A_QROWS, A_KWIN), lambda i: (i, 0, 0, 0)),
        out_shape=jax.ShapeDtypeStruct((n, A_HEADS_PER_GROUP, A_QROWS, A_KWIN), F32),
        compiler_params=_cparams("arbitrary"),
        name="a_bias",
    )(rel_bias, _a_bucket_index())


def _mixer_a_kernel(q_ref, k_ref, v_ref, bias_ref, o_ref, lse_ref, *scratch, dil, seq):
    t_rows = q_ref.shape[0]
    ti = pl.program_id(1)
    sub_len = seq // dil
    lq = t_rows // dil
    if dil > 1:
        stage, qstage, kres, vres = scratch

        @pl.when(ti == 0)
        def _():
            for src, dst in ((k_ref, kres), (v_ref, vres)):
                for hh in range(A_HEADS_PER_GROUP):
                    cols = slice(hh * HEAD_DIM, (hh + 1) * HEAD_DIM)
                    stage[...] = src[:, cols].astype(F32)
                    for r in range(dil):
                        dst[r, :, cols] = stage[pl.ds(r, sub_len, stride=dil), :].astype(BF16)

        for hh in range(A_HEADS_PER_GROUP):
            qstage[hh] = q_ref[:, hh * HEAD_DIM:(hh + 1) * HEAD_DIM].astype(F32)

    def scores(r, i, hh):
        q0 = ti * lq + i * A_QROWS
        ks = jnp.clip(q0 - A_RADIUS, 0, sub_len - A_KWIN)
        off = lax.shift_right_logical(q0 - ks, int(math.log2(A_RADIUS)))
        ks = pl.multiple_of(ks, A_RADIUS)
        cols = slice(hh * HEAD_DIM, (hh + 1) * HEAD_DIM)
        if dil > 1:
            rows = pl.ds(i * A_QROWS * dil + r, A_QROWS, stride=dil)
            q = qstage[hh, rows, :].astype(BF16)
            k = kres[r, pl.ds(ks, A_KWIN), cols]
            v = vres[r, pl.ds(ks, A_KWIN), cols]
        else:
            rows = pl.ds(i * A_QROWS, A_QROWS)
            q = q_ref[rows, cols]
            k = k_ref[pl.ds(ks, A_KWIN), cols]
            v = v_ref[pl.ds(ks, A_KWIN), cols]
        s = lax.dot_general(q, k, _NT, preferred_element_type=F32) * SCALE + bias_ref[off, hh]
        return rows, s, v

    def softmax(s):
        m = jnp.max(s, axis=-1, keepdims=True)
        p = jnp.exp(s - m)
        l = jnp.sum(p, axis=-1, keepdims=True)
        return p.astype(BF16), l, m + jnp.log(l)

    items = [(r, i, hh) for r in range(dil) for i in range(lq // A_QROWS) for hh in range(A_HEADS_PER_GROUP)]
    for b0 in range(0, len(items), A_BATCH):
        batch = items[b0:b0 + A_BATCH]
        staged = [scores(*it) for it in batch]
        probs = [softmax(s) for _, s, _ in staged]
        for (_, _, hh), (rows, _, v), (p, l, lse) in zip(batch, staged, probs):
            o_ref[hh, rows, :] = jnp.dot(p, v, preferred_element_type=F32) / l
            lse_ref[hh, rows, :] = jnp.broadcast_to(lse, (A_QROWS, HEAD_DIM))


def _mixer_a_group(qkn, vv, bias_g, gi, dil, t_rows=2048):
    b, seq, _ = qkn.shape
    sub_len = seq // dil
    k_blk = A_W // A_OUT_W
    scratch = []
    if dil > 1:
        scratch = [pltpu.VMEM((seq, HEAD_DIM), F32), pltpu.VMEM((A_HEADS_PER_GROUP, t_rows, HEAD_DIM), F32),
                   pltpu.VMEM((dil, sub_len, A_OUT_W), BF16), pltpu.VMEM((dil, sub_len, A_OUT_W), BF16)]
    return pl.pallas_call(
        functools.partial(_mixer_a_kernel, dil=dil, seq=seq),
        grid=(b, seq // t_rows),
        in_specs=[pl.BlockSpec((None, t_rows, A_OUT_W), lambda bi, ti: (bi, ti, gi)),
                  pl.BlockSpec((None, seq, A_OUT_W), lambda bi, ti: (bi, 0, k_blk + gi)),
                  pl.BlockSpec((None, seq, A_OUT_W), lambda bi, ti: (bi, 0, gi)),
                  pl.BlockSpec((A_NOFF, A_HEADS_PER_GROUP, A_QROWS, A_KWIN), lambda bi, ti: (0, 0, 0, 0))],
        out_specs=[pl.BlockSpec((None, A_HEADS_PER_GROUP, t_rows, HEAD_DIM), lambda bi, ti: (bi, 0, ti, 0))] * 2,
        out_shape=[jax.ShapeDtypeStruct((b, A_HEADS_PER_GROUP, seq, HEAD_DIM), F32)] * 2,
        scratch_shapes=scratch,
        compiler_params=_cparams("parallel", "arbitrary"),
        name=f"mixer_a_d{dil}",
    )(qkn, qkn, vv, bias_g)


def _mixer_b_kernel(qt_ref, k_ref, vt_ref, o_ref, s_scr, *, tk, group, w):
    tq = qt_ref.shape[1]
    seq = k_ref.shape[0]
    n_chunks = seq // tk
    c = SCALE * LOG2E
    units = [(i, j) for i in range(group) for j in range(tq // w)]

    def pass_a(u, ci, m):
        i, j = units[u]
        qt = qt_ref[i * HEAD_DIM:(i + 1) * HEAD_DIM, j * w:(j + 1) * w]
        st = jnp.dot(k_ref[ci * tk:(ci + 1) * tk, :], qt, preferred_element_type=F32)
        s_scr[u % 2, ci * tk:(ci + 1) * tk, :] = st
        return jnp.maximum(m, jnp.max(st, axis=0, keepdims=True))

    def pass_b(u, ci, m, l, acc):
        pt = jnp.exp2((s_scr[u % 2, ci * tk:(ci + 1) * tk, :] - m) * c)
        l = l + jnp.sum(pt, axis=0, keepdims=True)
        acc = acc + jnp.dot(vt_ref[:, ci * tk:(ci + 1) * tk], pt.astype(BF16), preferred_element_type=F32)
        return l, acc

    m_prev = None
    for s in range(len(units) + 1):
        m_cur = jnp.full((1, w), NEG, F32)
        l = jnp.zeros((1, w), F32)
        acc = jnp.zeros((HEAD_DIM, w), F32)
        for ci in range(n_chunks):
            if s < len(units):
                m_cur = pass_a(s, ci, m_cur)
            if s > 0:
                l, acc = pass_b(s - 1, ci, m_prev, l, acc)
        if s > 0:
            i, j = units[s - 1]
            o_ref[j * w:(j + 1) * w, i * HEAD_DIM:(i + 1) * HEAD_DIM] = (acc / l).T.astype(o_ref.dtype)
        m_prev = m_cur


def _mixer_b(qbt, kb, vbt, tq=512, tk=512, w=256):
    b, seq, _ = kb.shape
    group = B_Q_HEADS // B_KV_HEADS
    gw = group * HEAD_DIM
    return pl.pallas_call(
        functools.partial(_mixer_b_kernel, tk=tk, group=group, w=w),
        grid=(b, B_KV_HEADS, seq // tq),
        scratch_shapes=[pltpu.VMEM((2, seq, w), F32)],
        in_specs=[pl.BlockSpec((None, gw, tq), lambda bi, kv, qi: (bi, kv, qi)),
                  pl.BlockSpec((None, seq, HEAD_DIM), lambda bi, kv, qi: (bi, 0, kv)),
                  pl.BlockSpec((None, HEAD_DIM, seq), lambda bi, kv, qi: (bi, kv, 0))],
        out_specs=pl.BlockSpec((None, tq, gw), lambda bi, kv, qi: (bi, qi, kv)),
        out_shape=jax.ShapeDtypeStruct((b, seq, B_QW), BF16),
        compiler_params=_cparams("parallel", "parallel", "arbitrary"),
        name="mixer_b",
    )(qbt, kb, vbt)


def _mixer_c_kernel(q_ref, k_ref, v_ref, o_ref):
    for hh in range(C_HEADS):
        sl = slice(hh * HEAD_DIM, (hh + 1) * HEAD_DIM)
        s = lax.dot_general(q_ref[:, sl], k_ref[:, sl], _NT, preferred_element_type=F32) * SCALE
        m = jnp.max(s, axis=-1, keepdims=True)
        p = jnp.exp(s - m)
        l = jnp.sum(p, axis=-1, keepdims=True)
        o = jnp.dot(p.astype(BF16), v_ref[:, sl], preferred_element_type=F32) / l
        o_ref[:, sl] = o.astype(o_ref.dtype)


def _mixer_c(qkn, kc, vc, tq=512):
    b, seq, _ = qkn.shape
    n_mem = kc.shape[1]
    qc_blk = (2 * A_W) // C_W
    return pl.pallas_call(
        _mixer_c_kernel,
        grid=(b, seq // tq),
        in_specs=[pl.BlockSpec((None, tq, C_W), lambda bi, qi: (bi, qi, qc_blk)),
                  pl.BlockSpec((None, n_mem, C_W), lambda bi, qi: (bi, 0, 0)),
                  pl.BlockSpec((None, n_mem, C_W), lambda bi, qi: (bi, 0, 0))],
        out_specs=pl.BlockSpec((None, tq, C_W), lambda bi, qi: (bi, qi, 0)),
        out_shape=jax.ShapeDtypeStruct((b, seq, C_W), BF16),
        compiler_params=_cparams("parallel", "arbitrary"),
        name="mixer_c",
    )(qkn, kc, vc)


def _merge_kernel(h_ref, wg0_ref, wg1_ref, wg2_ref,
                  o0_ref, o1_ref, o2_ref, l0_ref, l1_ref, l2_ref, ob_ref, oc_ref,
                  wa_ref, wb_ref, wc_ref, out_ref, oa_ref):
    @pl.when(pl.program_id(1) == 0)
    def _():
        for hh in range(A_HEADS_PER_GROUP):
            l0, l1, l2 = l0_ref[hh], l1_ref[hh], l2_ref[hh]
            m = jnp.maximum(jnp.maximum(l0, l1), l2)
            w0, w1, w2 = jnp.exp(l0 - m), jnp.exp(l1 - m), jnp.exp(l2 - m)
            oa = (w0 * o0_ref[hh] + w1 * o1_ref[hh] + w2 * o2_ref[hh]) / (w0 + w1 + w2)
            oa_ref[:, hh * HEAD_DIM:(hh + 1) * HEAD_DIM] = oa.astype(oa_ref.dtype)

    h = h_ref[...]
    ga = jax.nn.sigmoid(jnp.dot(h, wg0_ref[...], preferred_element_type=F32))
    merged = ga * jnp.dot(oa_ref[...], wa_ref[...], preferred_element_type=F32)
    gb = jax.nn.sigmoid(jnp.dot(h, wg1_ref[...], preferred_element_type=F32))
    merged += gb * jnp.dot(ob_ref[...], wb_ref[...], preferred_element_type=F32)
    gc = jax.nn.sigmoid(jnp.dot(h, wg2_ref[...], preferred_element_type=F32))
    merged += gc * jnp.dot(oc_ref[...], wc_ref[...], preferred_element_type=F32)
    out_ref[...] = merged.astype(out_ref.dtype)


def _merge(h, w_gate, a_outs, a_lses, ob, oc, wa, wb, wc, tm=1024, tn=512):
    m, d = h.shape
    nj = d // tn
    sb = a_outs[0].shape[2] // tm
    row = lambda w: pl.BlockSpec((tm, w), lambda i, j: (i, 0))
    a_spec = pl.BlockSpec((None, A_HEADS_PER_GROUP, tm, HEAD_DIM), lambda i, j: (i // sb, 0, i % sb, 0))
    col = lambda k: pl.BlockSpec((k, tn), lambda i, j: (0, j))
    gate = lambda br: pl.BlockSpec((d, tn), lambda i, j: (0, br * nj + j))
    return pl.pallas_call(
        _merge_kernel,
        grid=(m // tm, nj),
        in_specs=[row(d), gate(0), gate(1), gate(2)]
                 + [a_spec] * 6 + [row(B_QW), row(C_W)]
                 + [col(A_OUT_W), col(B_QW), col(C_W)],
        out_specs=pl.BlockSpec((tm, tn), lambda i, j: (i, j)),
        out_shape=jax.ShapeDtypeStruct((m, d), BF16),
        scratch_shapes=[pltpu.VMEM((tm, A_OUT_W), BF16)],
        compiler_params=_cparams("parallel", "arbitrary", vmem_mb=56),
        name="merge",
    )(h, w_gate, w_gate, w_gate, *a_outs, *a_lses, ob, oc, wa, wb, wc)


def _out_proj_kernel(mg_ref, w_ref, x_ref, g_ref, x1_ref, h2_ref):
    x1 = x_ref[...] + jnp.dot(mg_ref[...], w_ref[...], preferred_element_type=F32)
    x1_ref[...] = x1
    ms = jnp.mean(x1 * x1, axis=-1, keepdims=True)
    h2_ref[...] = (x1 * lax.rsqrt(ms + EPS) * g_ref[...]).astype(h2_ref.dtype)


def _out_proj(merged, w_o, x2d, g_ffn, tm=512):
    m, d = x2d.shape
    return pl.pallas_call(
        _out_proj_kernel,
        grid=(m // tm,),
        in_specs=[pl.BlockSpec((tm, d), lambda i: (i, 0)),
                  pl.BlockSpec((d, d), lambda i: (0, 0)),
                  pl.BlockSpec((tm, d), lambda i: (i, 0)),
                  pl.BlockSpec((1, d), lambda i: (0, 0))],
        out_specs=[pl.BlockSpec((tm, d), lambda i: (i, 0))] * 2,
        out_shape=[jax.ShapeDtypeStruct((m, d), F32), jax.ShapeDtypeStruct((m, d), BF16)],
        compiler_params=_cparams("parallel"),
        name="out_proj",
    )(merged, w_o, x2d, g_ffn.reshape(1, d))


def _ffn_kernel(h_ref, wa_ref, wb_ref, wo_ref, x1_ref, out_ref, *, n_slabs):
    f = pl.program_id(1)
    slab_w = x1_ref.shape[1]

    @pl.when(f == 0)
    def _():
        out_ref[...] = jnp.zeros_like(out_ref)

    for s in range(n_slabs):
        @pl.when(f == s)
        def _():
            out_ref[:, s * slab_w:(s + 1) * slab_w] += x1_ref[...]

    h = h_ref[...]
    a = jnp.dot(h, wa_ref[...], preferred_element_type=F32)
    b = jnp.dot(h, wb_ref[...], preferred_element_type=F32)
    act = (a * jax.nn.sigmoid(a) * b).astype(BF16)
    out_ref[...] += jnp.dot(act, wo_ref[...], preferred_element_type=F32)


def _ffn(h2, w_in, w_out, x1, tm=1024, tf=512, slab_w=256):
    m, d = h2.shape
    d_ff = w_out.shape[0]
    nf = d_ff // tf
    n_slabs = d // slab_w
    assert n_slabs <= nf
    return pl.pallas_call(
        functools.partial(_ffn_kernel, n_slabs=n_slabs),
        grid=(m // tm, nf),
        in_specs=[pl.BlockSpec((tm, d), lambda i, f: (i, 0)),
                  pl.BlockSpec((d, tf), lambda i, f: (0, f)),
                  pl.BlockSpec((d, tf), lambda i, f: (0, nf + f)),
                  pl.BlockSpec((tf, d), lambda i, f: (f, 0)),
                  pl.BlockSpec((tm, slab_w), lambda i, f: (i, jnp.minimum(f, n_slabs - 1)))],
        out_specs=pl.BlockSpec((tm, d), lambda i, f: (i, 0)),
        out_shape=jax.ShapeDtypeStruct((m, d), F32),
        compiler_params=_cparams("parallel", "arbitrary", vmem_mb=48),
        name="ffn",
    )(h2, w_in, w_in, w_out, x1)


def _deinterleave_cols(w, heads):
    d = w.shape[0]
    return w.reshape(d, heads, HEAD_DIM // 2, 2).transpose(0, 1, 3, 2).reshape(d, heads * HEAD_DIM)


def _rope_tables(seq):
    rows = seq // GRID_W
    r = jnp.repeat(jnp.arange(rows), GRID_W).astype(F32)
    c = jnp.tile(jnp.arange(GRID_W), rows).astype(F32)
    nf = HEAD_DIM // 4
    inv = ROPE_THETA ** (-jnp.arange(nf, dtype=F32) / nf)
    ang = jnp.concatenate([r[:, None] * inv, c[:, None] * inv], axis=-1)
    cos, sin = jnp.cos(ang), jnp.sin(ang)
    return jnp.concatenate([cos, cos], axis=-1), jnp.concatenate([-sin, sin], axis=-1)


def _layer(x2d, mem2d, bias_a, cos_t, sin_t, b, seq, g_mix, w_in, g_qa, g_ka, g_qb, g_kb, g_mem, w_mem_kv,
           g_qc, g_kc, w_br_a, w_br_b, w_br_c, w_o, g_ffn, w_ffn_in, w_ffn_out):
    d = x2d.shape[1]
    o_qa, o_ka, o_va, o_qb, o_kb, o_vb, o_qc, o_gt = np.cumsum(
        (0, A_W, A_W, A_W, B_QW, B_KVW, B_KVW, C_W))
    seg = lambda lo, hi: w_in[:, lo:hi]
    w_norm = jnp.concatenate([seg(o_qa, o_ka), seg(o_ka, o_va), seg(o_qc, o_gt)], axis=1).astype(BF16)
    w_kb = _deinterleave_cols(seg(o_kb, o_vb), B_KV_HEADS).astype(BF16)
    wt_qv = jnp.concatenate([_deinterleave_cols(seg(o_qb, o_kb), B_Q_HEADS), seg(o_vb, o_qc)], axis=1).T.astype(BF16)
    w_va = seg(o_va, o_qb).astype(BF16)
    w_gate = w_in[:, o_gt:].astype(BF16)
    g_norm = jnp.concatenate([jnp.tile(g_qa, A_HEADS), jnp.tile(g_ka, A_HEADS), jnp.tile(g_qc, C_HEADS)])
    deint = lambda g: g.reshape(HEAD_DIM // 2, 2).T.reshape(HEAD_DIM)

    h = _rms_cast(x2d, g_mix)
    qkn = _proj(h, w_norm, "norm", g_norm).reshape(b, seq, -1)
    kb = _proj(h, w_kb, "rope", jnp.tile(deint(g_kb), B_KV_HEADS), cos_t, sin_t).reshape(b, seq, -1)
    va = _proj(h, w_va, "plain", tn=A_W).reshape(b, seq, -1)
    qbt, vbt = _proj_t(h, wt_qv, deint(g_qb), cos_t.T, sin_t.T, b, seq, B_QW)

    a_outs, a_lses = [], []
    for gi, (_, dil) in enumerate(DIL_PAIRS):
        o, lse = _mixer_a_group(qkn, va, bias_a[gi * A_NOFF:(gi + 1) * A_NOFF], gi, dil)
        a_outs.append(o)
        a_lses.append(lse)

    ob = _mixer_b(qbt, kb, vbt).reshape(b * seq, B_QW)

    kc, vc = _mem_kv(mem2d, g_mem, w_mem_kv.astype(BF16), g_kc)
    n_mem = mem2d.shape[0] // b
    oc = _mixer_c(qkn, kc.reshape(b, n_mem, C_W), vc.reshape(b, n_mem, C_W)).reshape(b * seq, C_W)

    merged = _merge(h, w_gate, a_outs, a_lses, ob, oc,
                    w_br_a.astype(BF16), w_br_b.astype(BF16), w_br_c.astype(BF16))
    x1, h2 = _out_proj(merged, w_o.astype(BF16), x2d, g_ffn)
    return _ffn(h2, w_ffn_in.astype(BF16), w_ffn_out.astype(BF16), x1)


def kernel(x, mem, rel_bias, g_mix, w_in, g_qa, g_ka, g_qb, g_kb, g_mem, w_mem_kv, g_qc, g_kc,
           w_br_a, w_br_b, w_br_c, w_o, g_ffn, w_ffn_in, w_ffn_out):
    b, seq, d = x.shape
    depth = w_in.shape[0]
    cos_t, sin_t = _rope_tables(seq)
    bias_a = _a_bias(rel_bias)
    x2d = x.reshape(b * seq, d)
    mem2d = mem.reshape(-1, d)
    for layer in range(depth):
        x2d = _layer(x2d, mem2d, bias_a, cos_t, sin_t, b, seq,
                     g_mix[layer], w_in[layer], g_qa[layer], g_ka[layer], g_qb[layer], g_kb[layer],
                     g_mem[layer], w_mem_kv[layer], g_qc[layer], g_kc[layer],
                     w_br_a[layer], w_br_b[layer], w_br_c[layer], w_o[layer], g_ffn[layer],
                     w_ffn_in[layer], w_ffn_out[layer])
    return x2d.reshape(b, seq, d)
```

```python
import functools
import math

import numpy as np
import jax
import jax.numpy as jnp
from jax import lax
from jax.experimental import pallas as pl
from jax.experimental.pallas import tpu as pltpu

HEAD_DIM = 128
GRID_W = 64
DIL_PAIRS = ((128, 1), (512, 4), (2048, 16))
A_HEADS_PER_GROUP = 2
A_HEADS = A_HEADS_PER_GROUP * len(DIL_PAIRS)
B_Q_HEADS = 6
B_KV_HEADS = 2
ROPE_THETA = 10000.0
C_HEADS = 4
N_BRANCH = 3
REL_BUCKETS = 32
REL_MAX_DIST = 1024
EPS = 1e-6
NEG = -1e30

A_W = A_HEADS * HEAD_DIM
A_OUT_W = A_HEADS_PER_GROUP * HEAD_DIM
B_QW = B_Q_HEADS * HEAD_DIM
B_KVW = B_KV_HEADS * HEAD_DIM
C_W = C_HEADS * HEAD_DIM

SCALE = 1.0 / math.sqrt(HEAD_DIM)
LOG2E = math.log2(math.e)

A_QROWS = 128
A_KWIN = 256
A_RADIUS = 64
A_NOFF = 3
A_BATCH = 8

MXU_N = 256

BF16 = jnp.bfloat16
F32 = jnp.float32

_NT = (((1,), (1,)), ((), ()))


def _cparams(*sem, vmem_mb=None):
    limit = None if vmem_mb is None else vmem_mb * 1024 * 1024
    return pltpu.CompilerParams(dimension_semantics=sem, vmem_limit_bytes=limit)


def _proj_norm_kernel(x_ref, gm_ref, w_ref, g_ref, o_ref, h_ref):
    @pl.when(pl.program_id(1) == 0)
    def _():
        x = x_ref[...]
        ms = jnp.mean(x * x, axis=-1, keepdims=True)
        h_ref[...] = (x * lax.rsqrt(ms + EPS) * gm_ref[...]).astype(h_ref.dtype)

    acc = jnp.dot(h_ref[...], w_ref[...], preferred_element_type=F32)
    for hh in range(o_ref.shape[1] // HEAD_DIM):
        sl = slice(hh * HEAD_DIM, (hh + 1) * HEAD_DIM)
        y = acc[:, sl]
        ms = jnp.mean(y * y, axis=-1, keepdims=True)
        o_ref[:, sl] = (y * lax.rsqrt(ms + EPS) * g_ref[:, sl]).astype(o_ref.dtype)


def _proj_norm(x2d, g_mix, w, gains, tm=1024, tn=512):
    m, d = x2d.shape
    n = w.shape[1]
    assert n % tn == 0 and m % tm == 0
    return pl.pallas_call(
        _proj_norm_kernel,
        grid=(m // tm, n // tn),
        in_specs=[pl.BlockSpec((tm, d), lambda i, j: (i, 0)),
                  pl.BlockSpec((1, d), lambda i, j: (0, 0)),
                  pl.BlockSpec((d, tn), lambda i, j: (0, j)),
                  pl.BlockSpec((1, tn), lambda i, j: (0, j))],
        out_specs=[pl.BlockSpec((tm, tn), lambda i, j: (i, j)),
                   pl.BlockSpec((tm, d), lambda i, j: (i, 0))],
        out_shape=[jax.ShapeDtypeStruct((m, n), BF16), jax.ShapeDtypeStruct((m, d), BF16)],
        compiler_params=_cparams("parallel", "arbitrary", vmem_mb=48),
        name="proj_norm",
    )(x2d, g_mix.reshape(1, d), w, gains.reshape(1, n))


def _proj_plain_kernel(h_ref, w_ref, o_ref):
    o_ref[...] = jnp.dot(h_ref[...], w_ref[...], preferred_element_type=F32).astype(o_ref.dtype)


def _proj_plain(h, w, tm=1024):
    m, d = h.shape
    n = w.shape[1]
    assert m % tm == 0
    return pl.pallas_call(
        _proj_plain_kernel,
        grid=(m // tm,),
        in_specs=[pl.BlockSpec((tm, d), lambda i: (i, 0)),
                  pl.BlockSpec((d, n), lambda i: (0, 0))],
        out_specs=pl.BlockSpec((tm, n), lambda i: (i, 0)),
        out_shape=jax.ShapeDtypeStruct((m, n), BF16),
        compiler_params=_cparams("parallel"),
        name="proj_plain",
    )(h, w)


def _proj_t_kernel(wt_ref, h_ref, gq_ref, gk_ref, cos_ref, sin_ref, q_ref, k_ref, v_ref, *, t_sub):
    n_q = q_ref.shape[0] // HEAD_DIM
    n_k = k_ref.shape[1] // HEAD_DIM
    n_v = v_ref.shape[0] // HEAD_DIM
    half = HEAD_DIM // 2

    def norm_rope(y, g_ref, tok):
        ms = jnp.mean(y * y, axis=0, keepdims=True)
        y = y * lax.rsqrt(ms + EPS) * g_ref[...]
        partner = jnp.concatenate([y[half:], y[:half]], axis=0)
        return y * cos_ref[:, tok] + partner * sin_ref[:, tok]

    for t0 in range(0, h_ref.shape[0], t_sub):
        tok = slice(t0, t0 + t_sub)
        yt = lax.dot_general(wt_ref[...], h_ref[tok, :], _NT, preferred_element_type=F32)
        head = lambda hh: yt[hh * HEAD_DIM:(hh + 1) * HEAD_DIM]
        for hh in range(n_q):
            q_ref[hh * HEAD_DIM:(hh + 1) * HEAD_DIM, tok] = norm_rope(head(hh), gq_ref, tok).astype(q_ref.dtype)
        for hh in range(n_k):
            y = norm_rope(head(n_q + hh), gk_ref, tok)
            k_ref[tok, hh * HEAD_DIM:(hh + 1) * HEAD_DIM] = y.T.astype(k_ref.dtype)
        for hh in range(n_v):
            v_ref[hh * HEAD_DIM:(hh + 1) * HEAD_DIM, tok] = head(n_q + n_k + hh).astype(v_ref.dtype)


def _proj_t(h, wt, gq_col, gk_col, cos_tt, sin_tt, b, seq, tm=1024, t_sub=256):
    m, d = h.shape
    n = wt.shape[0]
    sb = seq // tm
    col = lambda g: jnp.broadcast_to(g[:, None], (HEAD_DIM, t_sub))
    lane_tile = lambda rows: pl.BlockSpec((None, rows, tm), lambda i: (i // sb, 0, i % sb))
    return pl.pallas_call(
        functools.partial(_proj_t_kernel, t_sub=t_sub),
        grid=(m // tm,),
        in_specs=[pl.BlockSpec((n, d), lambda i: (0, 0)),
                  pl.BlockSpec((tm, d), lambda i: (i, 0)),
                  pl.BlockSpec((HEAD_DIM, t_sub), lambda i: (0, 0)),
                  pl.BlockSpec((HEAD_DIM, t_sub), lambda i: (0, 0)),
                  pl.BlockSpec((HEAD_DIM, tm), lambda i: (0, i % sb)),
                  pl.BlockSpec((HEAD_DIM, tm), lambda i: (0, i % sb))],
        out_specs=[lane_tile(B_QW),
                   pl.BlockSpec((tm, B_KVW), lambda i: (i, 0)),
                   lane_tile(B_KVW)],
        out_shape=[jax.ShapeDtypeStruct((b, B_QW, seq), BF16),
                   jax.ShapeDtypeStruct((m, B_KVW), BF16),
                   jax.ShapeDtypeStruct((b, B_KVW, seq), BF16)],
        compiler_params=_cparams("parallel"),
        name="proj_t",
    )(wt, h, col(gq_col), col(gk_col), cos_tt, sin_tt)


def _mem_kv_kernel(mem_ref, gm_ref, w_ref, gk_ref, k_ref, v_ref):
    x = mem_ref[...]
    ms = jnp.mean(x * x, axis=-1, keepdims=True)
    hm = (x * lax.rsqrt(ms + EPS) * gm_ref[...]).astype(BF16)
    kv = jnp.dot(hm, w_ref[...], preferred_element_type=F32)
    for hh in range(C_HEADS):
        sl = slice(hh * HEAD_DIM, (hh + 1) * HEAD_DIM)
        y = kv[:, sl]
        ms = jnp.mean(y * y, axis=-1, keepdims=True)
        k_ref[:, sl] = (y * lax.rsqrt(ms + EPS) * gk_ref[...]).astype(BF16)
    v_ref[...] = kv[:, C_W:].astype(BF16)


def _mem_kv(mem2d, g_mem, w_kv, g_kc, tm=256):
    m, d = mem2d.shape
    return pl.pallas_call(
        _mem_kv_kernel,
        grid=(m // tm,),
        in_specs=[pl.BlockSpec((tm, d), lambda i: (i, 0)),
                  pl.BlockSpec((1, d), lambda i: (0, 0)),
                  pl.BlockSpec((d, 2 * C_W), lambda i: (0, 0)),
                  pl.BlockSpec((1, HEAD_DIM), lambda i: (0, 0))],
        out_specs=[pl.BlockSpec((tm, C_W), lambda i: (i, 0))] * 2,
        out_shape=[jax.ShapeDtypeStruct((m, C_W), BF16)] * 2,
        compiler_params=_cparams("parallel"),
        name="mem_kv",
    )(mem2d, g_mem.reshape(1, d), w_kv, g_kc.reshape(1, HEAD_DIM))


def _t5_bucket(rel):
    nb = REL_BUCKETS // 2
    ret = jnp.where(rel > 0, nb, 0)
    n = jnp.abs(rel)
    max_exact = nb // 2
    large = max_exact + (jnp.log(jnp.maximum(n, 1).astype(jnp.float32) / max_exact)
                         / math.log(REL_MAX_DIST / max_exact) * (nb - max_exact)).astype(jnp.int32)
    large = jnp.minimum(large, nb - 1)
    return ret + jnp.where(n < max_exact, n, large)


def _a_bucket_index():
    qi = jnp.arange(A_QROWS, dtype=jnp.int32)[:, None]
    kj = jnp.arange(A_KWIN, dtype=jnp.int32)[None, :]
    out = []
    for _, dil in DIL_PAIRS:
        for off in range(A_NOFF):
            rel = kj - qi - A_RADIUS * off
            out.append(jnp.where(jnp.abs(rel) <= A_RADIUS, _t5_bucket(rel * dil), -1))
    return jnp.stack(out)


def _a_bias_kernel(tab_ref, bucket_ref, o_ref):
    g = pl.program_id(0) // A_NOFF
    bk = bucket_ref[...]
    for hh in range(A_HEADS_PER_GROUP):
        acc = jnp.full(bk.shape, NEG, F32)
        for b in range(REL_BUCKETS):
            acc = jnp.where(bk == b, tab_ref[b, g * A_HEADS_PER_GROUP + hh], acc)
        o_ref[hh] = acc


def _a_bias(rel_bias):
    n = len(DIL_PAIRS) * A_NOFF
    return pl.pallas_call(
        _a_bias_kernel,
        grid=(n,),
        in_specs=[pl.BlockSpec(memory_space=pltpu.SMEM),
                  pl.BlockSpec((None, A_QROWS, A_KWIN), lambda i: (i, 0, 0))],
        out_specs=pl.BlockSpec((None, A_HEADS_PER_GROUP, A_QROWS, A_KWIN), lambda i: (i, 0, 0, 0)),
        out_shape=jax.ShapeDtypeStruct((n, A_HEADS_PER_GROUP, A_QROWS, A_KWIN), F32),
        compiler_params=_cparams("arbitrary"),
        name="a_bias",
    )(rel_bias, _a_bucket_index())


def _mixer_a_kernel(q_ref, k_ref, v_ref, bias_ref, o_ref, lse_ref, *scratch, dil, seq):
    t_rows = q_ref.shape[0]
    ti = pl.program_id(1)
    sub_len = seq // dil
    lq = t_rows // dil
    if dil > 1:
        stage, qstage, kres, vres = scratch

        @pl.when(ti == 0)
        def _():
            for src, dst in ((k_ref, kres), (v_ref, vres)):
                for hh in range(A_HEADS_PER_GROUP):
                    cols = slice(hh * HEAD_DIM, (hh + 1) * HEAD_DIM)
                    stage[...] = src[:, cols].astype(F32)
                    for r in range(dil):
                        dst[r, :, cols] = stage[pl.ds(r, sub_len, stride=dil), :].astype(BF16)

        for hh in range(A_HEADS_PER_GROUP):
            qstage[hh] = q_ref[:, hh * HEAD_DIM:(hh + 1) * HEAD_DIM].astype(F32)

    def scores(r, i, hh):
        q0 = ti * lq + i * A_QROWS
        ks = jnp.clip(q0 - A_RADIUS, 0, sub_len - A_KWIN)
        off = lax.shift_right_logical(q0 - ks, int(math.log2(A_RADIUS)))
        ks = pl.multiple_of(ks, A_RADIUS)
        cols = slice(hh * HEAD_DIM, (hh + 1) * HEAD_DIM)
        if dil > 1:
            rows = pl.ds(i * A_QROWS * dil + r, A_QROWS, stride=dil)
            q = qstage[hh, rows, :].astype(BF16)
            k = kres[r, pl.ds(ks, A_KWIN), cols]
            v = vres[r, pl.ds(ks, A_KWIN), cols]
        else:
            rows = pl.ds(i * A_QROWS, A_QROWS)
            q = q_ref[rows, cols]
            k = k_ref[pl.ds(ks, A_KWIN), cols]
            v = v_ref[pl.ds(ks, A_KWIN), cols]
        s = lax.dot_general(q, k, _NT, preferred_element_type=F32) * SCALE + bias_ref[off, hh]
        return rows, s, v

    def softmax(s):
        m = jnp.max(s, axis=-1, keepdims=True)
        p = jnp.exp(s - m)
        l = jnp.sum(p, axis=-1, keepdims=True)
        return p.astype(BF16), l, m + jnp.log(l)

    items = [(r, i, hh) for r in range(dil) for i in range(lq // A_QROWS) for hh in range(A_HEADS_PER_GROUP)]
    for b0 in range(0, len(items), A_BATCH):
        batch = items[b0:b0 + A_BATCH]
        staged = [scores(*it) for it in batch]
        probs = [softmax(s) for _, s, _ in staged]
        for (_, _, hh), (rows, _, v), (p, l, lse) in zip(batch, staged, probs):
            o_ref[hh, rows, :] = jnp.dot(p, v, preferred_element_type=F32) / l
            lse_ref[hh, rows, :] = jnp.broadcast_to(lse, (A_QROWS, HEAD_DIM))


def _mixer_a_group(qkn, vv, bias_g, gi, dil, t_rows=2048):
    b, seq, _ = qkn.shape
    sub_len = seq // dil
    k_blk = A_W // A_OUT_W
    scratch = []
    if dil > 1:
        scratch = [pltpu.VMEM((seq, HEAD_DIM), F32), pltpu.VMEM((A_HEADS_PER_GROUP, t_rows, HEAD_DIM), F32),
                   pltpu.VMEM((dil, sub_len, A_OUT_W), BF16), pltpu.VMEM((dil, sub_len, A_OUT_W), BF16)]
    return pl.pallas_call(
        functools.partial(_mixer_a_kernel, dil=dil, seq=seq),
        grid=(b, seq // t_rows),
        in_specs=[pl.BlockSpec((None, t_rows, A_OUT_W), lambda bi, ti: (bi, ti, gi)),
                  pl.BlockSpec((None, seq, A_OUT_W), lambda bi, ti: (bi, 0, k_blk + gi)),
                  pl.BlockSpec((None, seq, A_OUT_W), lambda bi, ti: (bi, 0, gi)),
                  pl.BlockSpec((A_NOFF, A_HEADS_PER_GROUP, A_QROWS, A_KWIN), lambda bi, ti: (0, 0, 0, 0))],
        out_specs=[pl.BlockSpec((None, A_HEADS_PER_GROUP, t_rows, HEAD_DIM), lambda bi, ti: (bi, 0, ti, 0))] * 2,
        out_shape=[jax.ShapeDtypeStruct((b, A_HEADS_PER_GROUP, seq, HEAD_DIM), F32)] * 2,
        scratch_shapes=scratch,
        compiler_params=_cparams("parallel", "arbitrary"),
        name=f"mixer_a_d{dil}",
    )(qkn, qkn, vv, bias_g)


def _mixer_b_kernel(qt_ref, k_ref, vt_ref, o_ref, s_scr, *, tk, group, w):
    tq = qt_ref.shape[1]
    seq = k_ref.shape[0]
    n_chunks = seq // tk
    units = [(i, j) for i in range(group) for j in range(tq // w)]

    def pass_a(u, ci, m):
        i, j = units[u]
        qt = qt_ref[i * HEAD_DIM:(i + 1) * HEAD_DIM, j * w:(j + 1) * w]
        st = jnp.dot(k_ref[ci * tk:(ci + 1) * tk, :], qt, preferred_element_type=F32)
        s_scr[u % 2, ci * tk:(ci + 1) * tk, :] = st
        return jnp.maximum(m, jnp.max(st, axis=0, keepdims=True))

    def pass_b(u, ci, m, l, acc):
        pt = jnp.exp2(s_scr[u % 2, ci * tk:(ci + 1) * tk, :] - m)
        l = l + jnp.sum(pt, axis=0, keepdims=True)
        acc = acc + jnp.dot(vt_ref[:, ci * tk:(ci + 1) * tk], pt.astype(BF16), preferred_element_type=F32)
        return l, acc

    m_prev = None
    for s in range(len(units) + 1):
        m_cur = jnp.full((1, w), NEG, F32)
        l = jnp.zeros((1, w), F32)
        acc = jnp.zeros((HEAD_DIM, w), F32)
        for ci in range(n_chunks):
            if s < len(units):
                m_cur = pass_a(s, ci, m_cur)
            if s > 0:
                l, acc = pass_b(s - 1, ci, m_prev, l, acc)
        if s > 0:
            i, j = units[s - 1]
            o_ref[j * w:(j + 1) * w, i * HEAD_DIM:(i + 1) * HEAD_DIM] = (acc / l).T.astype(o_ref.dtype)
        m_prev = m_cur


def _mixer_b(qbt, kb, vbt, tq=512, tk=512, w=256):
    b, seq, _ = kb.shape
    group = B_Q_HEADS // B_KV_HEADS
    gw = group * HEAD_DIM
    return pl.pallas_call(
        functools.partial(_mixer_b_kernel, tk=tk, group=group, w=w),
        grid=(b, B_KV_HEADS, seq // tq),
        scratch_shapes=[pltpu.VMEM((2, seq, w), F32)],
        in_specs=[pl.BlockSpec((None, gw, tq), lambda bi, kv, qi: (bi, kv, qi)),
                  pl.BlockSpec((None, seq, HEAD_DIM), lambda bi, kv, qi: (bi, 0, kv)),
                  pl.BlockSpec((None, HEAD_DIM, seq), lambda bi, kv, qi: (bi, kv, 0))],
        out_specs=pl.BlockSpec((None, tq, gw), lambda bi, kv, qi: (bi, qi, kv)),
        out_shape=jax.ShapeDtypeStruct((b, seq, B_QW), BF16),
        compiler_params=_cparams("parallel", "parallel", "arbitrary"),
        name="mixer_b",
    )(qbt, kb, vbt)


def _mixer_c_kernel(q_ref, k_ref, v_ref, o_ref):
    for hh in range(C_HEADS):
        sl = slice(hh * HEAD_DIM, (hh + 1) * HEAD_DIM)
        s = lax.dot_general(q_ref[:, sl], k_ref[:, sl], _NT, preferred_element_type=F32) * SCALE
        m = jnp.max(s, axis=-1, keepdims=True)
        p = jnp.exp(s - m)
        l = jnp.sum(p, axis=-1, keepdims=True)
        o = jnp.dot(p.astype(BF16), v_ref[:, sl], preferred_element_type=F32) / l
        o_ref[:, sl] = o.astype(o_ref.dtype)


def _mixer_c(qkn, kc, vc, tq=512):
    b, seq, _ = qkn.shape
    n_mem = kc.shape[1]
    qc_blk = (2 * A_W) // C_W
    return pl.pallas_call(
        _mixer_c_kernel,
        grid=(b, seq // tq),
        in_specs=[pl.BlockSpec((None, tq, C_W), lambda bi, qi: (bi, qi, qc_blk)),
                  pl.BlockSpec((None, n_mem, C_W), lambda bi, qi: (bi, 0, 0)),
                  pl.BlockSpec((None, n_mem, C_W), lambda bi, qi: (bi, 0, 0))],
        out_specs=pl.BlockSpec((None, tq, C_W), lambda bi, qi: (bi, qi, 0)),
        out_shape=jax.ShapeDtypeStruct((b, seq, C_W), BF16),
        compiler_params=_cparams("parallel", "arbitrary"),
        name="mixer_c",
    )(qkn, kc, vc)


def _merge_kernel(h_ref, wg0_ref, wg1_ref, wg2_ref,
                  o0_ref, o1_ref, o2_ref, l0_ref, l1_ref, l2_ref, ob_ref, oc_ref,
                  wa_ref, wb_ref, wc_ref, out_ref, oa_ref):
    @pl.when(pl.program_id(1) == 0)
    def _():
        for hh in range(A_HEADS_PER_GROUP):
            l0, l1, l2 = l0_ref[hh], l1_ref[hh], l2_ref[hh]
            m = jnp.maximum(jnp.maximum(l0, l1), l2)
            w0, w1, w2 = jnp.exp(l0 - m), jnp.exp(l1 - m), jnp.exp(l2 - m)
            oa = (w0 * o0_ref[hh] + w1 * o1_ref[hh] + w2 * o2_ref[hh]) / (w0 + w1 + w2)
            oa_ref[:, hh * HEAD_DIM:(hh + 1) * HEAD_DIM] = oa.astype(oa_ref.dtype)

    h = h_ref[...]
    ga = jax.nn.sigmoid(jnp.dot(h, wg0_ref[...], preferred_element_type=F32))
    merged = ga * jnp.dot(oa_ref[...], wa_ref[...], preferred_element_type=F32)
    gb = jax.nn.sigmoid(jnp.dot(h, wg1_ref[...], preferred_element_type=F32))
    merged += gb * jnp.dot(ob_ref[...], wb_ref[...], preferred_element_type=F32)
    gc = jax.nn.sigmoid(jnp.dot(h, wg2_ref[...], preferred_element_type=F32))
    merged += gc * jnp.dot(oc_ref[...], wc_ref[...], preferred_element_type=F32)
    out_ref[...] = merged.astype(out_ref.dtype)


def _merge(h, w_gate, a_outs, a_lses, ob, oc, wa, wb, wc, tm=1024, tn=512):
    m, d = h.shape
    nj = d // tn
    sb = a_outs[0].shape[2] // tm
    row = lambda w: pl.BlockSpec((tm, w), lambda i, j: (i, 0))
    a_spec = pl.BlockSpec((None, A_HEADS_PER_GROUP, tm, HEAD_DIM), lambda i, j: (i // sb, 0, i % sb, 0))
    col = lambda k: pl.BlockSpec((k, tn), lambda i, j: (0, j))
    gate = lambda br: pl.BlockSpec((d, tn), lambda i, j: (0, br * nj + j))
    return pl.pallas_call(
        _merge_kernel,
        grid=(m // tm, nj),
        in_specs=[row(d), gate(0), gate(1), gate(2)]
                 + [a_spec] * 6 + [row(B_QW), row(C_W)]
                 + [col(A_OUT_W), col(B_QW), col(C_W)],
        out_specs=pl.BlockSpec((tm, tn), lambda i, j: (i, j)),
        out_shape=jax.ShapeDtypeStruct((m, d), BF16),
        scratch_shapes=[pltpu.VMEM((tm, A_OUT_W), BF16)],
        compiler_params=_cparams("parallel", "arbitrary", vmem_mb=56),
        name="merge",
    )(h, w_gate, w_gate, w_gate, *a_outs, *a_lses, ob, oc, wa, wb, wc)


def _out_proj_kernel(mg_ref, w_ref, x_ref, g_ref, x1_ref, h2_ref):
    x1 = x_ref[...] + jnp.dot(mg_ref[...], w_ref[...], preferred_element_type=F32)
    x1_ref[...] = x1
    ms = jnp.mean(x1 * x1, axis=-1, keepdims=True)
    h2_ref[...] = (x1 * lax.rsqrt(ms + EPS) * g_ref[...]).astype(h2_ref.dtype)


def _out_proj(merged, w_o, x2d, g_ffn, tm=512):
    m, d = x2d.shape
    return pl.pallas_call(
        _out_proj_kernel,
        grid=(m // tm,),
        in_specs=[pl.BlockSpec((tm, d), lambda i: (i, 0)),
                  pl.BlockSpec((d, d), lambda i: (0, 0)),
                  pl.BlockSpec((tm, d), lambda i: (i, 0)),
                  pl.BlockSpec((1, d), lambda i: (0, 0))],
        out_specs=[pl.BlockSpec((tm, d), lambda i: (i, 0))] * 2,
        out_shape=[jax.ShapeDtypeStruct((m, d), F32), jax.ShapeDtypeStruct((m, d), BF16)],
        compiler_params=_cparams("parallel"),
        name="out_proj",
    )(merged, w_o, x2d, g_ffn.reshape(1, d))


def _ffn_kernel(h_ref, wa_ref, wb_ref, wo_ref, x1_ref, out_ref, *, n_slabs):
    f = pl.program_id(1)
    slab_w = x1_ref.shape[1]

    @pl.when(f == 0)
    def _():
        out_ref[...] = jnp.zeros_like(out_ref)

    for s in range(n_slabs):
        @pl.when(f == s)
        def _():
            out_ref[:, s * slab_w:(s + 1) * slab_w] += x1_ref[...]

    h = h_ref[...]
    a = jnp.dot(h, wa_ref[...], preferred_element_type=F32)
    b = jnp.dot(h, wb_ref[...], preferred_element_type=F32)
    act = (a * jax.nn.sigmoid(a) * b).astype(BF16)
    out_ref[...] += jnp.dot(act, wo_ref[...], preferred_element_type=F32)


def _ffn(h2, w_in, w_out, x1, tm=1024, tf=512, slab_w=256):
    m, d = h2.shape
    d_ff = w_out.shape[0]
    nf = d_ff // tf
    n_slabs = d // slab_w
    assert n_slabs <= nf
    return pl.pallas_call(
        functools.partial(_ffn_kernel, n_slabs=n_slabs),
        grid=(m // tm, nf),
        in_specs=[pl.BlockSpec((tm, d), lambda i, f: (i, 0)),
                  pl.BlockSpec((d, tf), lambda i, f: (0, f)),
                  pl.BlockSpec((d, tf), lambda i, f: (0, nf + f)),
                  pl.BlockSpec((tf, d), lambda i, f: (f, 0)),
                  pl.BlockSpec((tm, slab_w), lambda i, f: (i, jnp.minimum(f, n_slabs - 1)))],
        out_specs=pl.BlockSpec((tm, d), lambda i, f: (i, 0)),
        out_shape=jax.ShapeDtypeStruct((m, d), F32),
        compiler_params=_cparams("parallel", "arbitrary", vmem_mb=48),
        name="ffn",
    )(h2, w_in, w_in, w_out, x1)


def _deinterleave_cols(w, heads):
    d = w.shape[0]
    return w.reshape(d, heads, HEAD_DIM // 2, 2).transpose(0, 1, 3, 2).reshape(d, heads * HEAD_DIM)


def _rope_tables(seq):
    rows = seq // GRID_W
    r = jnp.repeat(jnp.arange(rows), GRID_W).astype(F32)
    c = jnp.tile(jnp.arange(GRID_W), rows).astype(F32)
    nf = HEAD_DIM // 4
    inv = ROPE_THETA ** (-jnp.arange(nf, dtype=F32) / nf)
    ang = jnp.concatenate([r[:, None] * inv, c[:, None] * inv], axis=-1)
    cos, sin = jnp.cos(ang).T, jnp.sin(ang).T
    return jnp.concatenate([cos, cos], axis=0), jnp.concatenate([-sin, sin], axis=0)


def _layer(x2d, mem2d, bias_a, cos_t, sin_t, b, seq, g_mix, w_in, g_qa, g_ka, g_qb, g_kb, g_mem, w_mem_kv,
           g_qc, g_kc, w_br_a, w_br_b, w_br_c, w_o, g_ffn, w_ffn_in, w_ffn_out):
    d = x2d.shape[1]
    o_qa, o_ka, o_va, o_qb, o_kb, o_vb, o_qc, o_gt = np.cumsum(
        (0, A_W, A_W, A_W, B_QW, B_KVW, B_KVW, C_W))
    seg = lambda lo, hi: w_in[:, lo:hi]
    w_norm = jnp.concatenate([seg(o_qa, o_ka), seg(o_ka, o_va), seg(o_qc, o_gt)], axis=1).astype(BF16)
    wt_b = jnp.concatenate([_deinterleave_cols(seg(o_qb, o_kb), B_Q_HEADS),
                            _deinterleave_cols(seg(o_kb, o_vb), B_KV_HEADS),
                            seg(o_vb, o_qc)], axis=1).T.astype(BF16)
    w_va = seg(o_va, o_qb).astype(BF16)
    w_gate = w_in[:, o_gt:].astype(BF16)
    g_norm = jnp.concatenate([jnp.tile(g_qa, A_HEADS), jnp.tile(g_ka, A_HEADS), jnp.tile(g_qc, C_HEADS)])
    deint = lambda g: g.reshape(HEAD_DIM // 2, 2).T.reshape(HEAD_DIM)

    qkn, h = _proj_norm(x2d, g_mix, w_norm, g_norm)
    qkn = qkn.reshape(b, seq, -1)
    va = _proj_plain(h, w_va).reshape(b, seq, -1)
    qbt, kb, vbt = _proj_t(h, wt_b, deint(g_qb) * (SCALE * LOG2E), deint(g_kb), cos_t, sin_t, b, seq)
    kb = kb.reshape(b, seq, B_KVW)

    a_outs, a_lses = [], []
    for gi, (_, dil) in enumerate(DIL_PAIRS):
        o, lse = _mixer_a_group(qkn, va, bias_a[gi * A_NOFF:(gi + 1) * A_NOFF], gi, dil)
        a_outs.append(o)
        a_lses.append(lse)

    ob = _mixer_b(qbt, kb, vbt).reshape(b * seq, B_QW)

    kc, vc = _mem_kv(mem2d, g_mem, w_mem_kv.astype(BF16), g_kc)
    n_mem = mem2d.shape[0] // b
    oc = _mixer_c(qkn, kc.reshape(b, n_mem, C_W), vc.reshape(b, n_mem, C_W)).reshape(b * seq, C_W)

    merged = _merge(h, w_gate, a_outs, a_lses, ob, oc,
                    w_br_a.astype(BF16), w_br_b.astype(BF16), w_br_c.astype(BF16))
    x1, h2 = _out_proj(merged, w_o.astype(BF16), x2d, g_ffn)
    return _ffn(h2, w_ffn_in.astype(BF16), w_ffn_out.astype(BF16), x1)


def kernel(x, mem, rel_bias, g_mix, w_in, g_qa, g_ka, g_qb, g_kb, g_mem, w_mem_kv, g_qc, g_kc,
           w_br_a, w_br_b, w_br_c, w_o, g_ffn, w_ffn_in, w_ffn_out):
    b, seq, d = x.shape
    depth = w_in.shape[0]
    cos_t, sin_t = _rope_tables(seq)
    bias_a = _a_bias(rel_bias)
    x2d = x.reshape(b * seq, d)
    mem2d = mem.reshape(-1, d)
    for layer in range(depth):
        x2d = _layer(x2d, mem2d, bias_a, cos_t, sin_t, b, seq,
                     g_mix[layer], w_in[layer], g_qa[layer], g_ka[layer], g_qb[layer], g_kb[layer],
                     g_mem[layer], w_mem_kv[layer], g_qc[layer], g_kc[layer],
                     w_br_a[layer], w_br_b[layer], w_br_c[layer], w_o[layer], g_ffn[layer],
                     w_ffn_in[layer], w_ffn_out[layer])
    return x2d.reshape(b, seq, d)
```

```python
import functools
import math

import numpy as np
import jax
import jax.numpy as jnp
from jax import lax
from jax.experimental import pallas as pl
from jax.experimental.pallas import tpu as pltpu

HEAD_DIM = 128
GRID_W = 64
DIL_PAIRS = ((128, 1), (512, 4), (2048, 16))
A_HEADS_PER_GROUP = 2
A_HEADS = A_HEADS_PER_GROUP * len(DIL_PAIRS)
B_Q_HEADS = 6
B_KV_HEADS = 2
ROPE_THETA = 10000.0
C_HEADS = 4
N_BRANCH = 3
REL_BUCKETS = 32
REL_MAX_DIST = 1024
EPS = 1e-6
NEG = -1e30

A_W = A_HEADS * HEAD_DIM
A_OUT_W = A_HEADS_PER_GROUP * HEAD_DIM
B_QW = B_Q_HEADS * HEAD_DIM
B_KVW = B_KV_HEADS * HEAD_DIM
C_W = C_HEADS * HEAD_DIM

SCALE = 1.0 / math.sqrt(HEAD_DIM)
LOG2E = math.log2(math.e)

A_QROWS = 128
A_KWIN = 256
A_RADIUS = 64
A_NOFF = 3
A_BATCH = 8

MXU_N = 256

BF16 = jnp.bfloat16
F32 = jnp.float32

_NT = (((1,), (1,)), ((), ()))


def _software_pipeline(n_chunks, matmul, epilogue):
    acc = matmul(0)
    for c in range(n_chunks):
        nxt = matmul(c + 1) if c + 1 < n_chunks else None
        epilogue(c, acc)
        acc = nxt


def _cparams(*sem, vmem_mb=None):
    limit = None if vmem_mb is None else vmem_mb * 1024 * 1024
    return pltpu.CompilerParams(dimension_semantics=sem, vmem_limit_bytes=limit)


def _proj_norm_kernel(x_ref, gm_ref, w_ref, g_ref, o_ref, h_ref, *, r_sub):
    @pl.when(pl.program_id(1) == 0)
    def _():
        x = x_ref[...]
        ms = jnp.mean(x * x, axis=-1, keepdims=True)
        h_ref[...] = (x * lax.rsqrt(ms + EPS) * gm_ref[...]).astype(h_ref.dtype)

    def matmul(c):
        return jnp.dot(h_ref[c * r_sub:(c + 1) * r_sub, :], w_ref[...], preferred_element_type=F32)

    def epilogue(c, acc):
        for hh in range(o_ref.shape[1] // HEAD_DIM):
            sl = slice(hh * HEAD_DIM, (hh + 1) * HEAD_DIM)
            y = acc[:, sl]
            ms = jnp.mean(y * y, axis=-1, keepdims=True)
            o_ref[c * r_sub:(c + 1) * r_sub, sl] = (y * lax.rsqrt(ms + EPS) * g_ref[:, sl]).astype(o_ref.dtype)

    _software_pipeline(h_ref.shape[0] // r_sub, matmul, epilogue)


def _proj_norm(x2d, g_mix, w, gains, col_block, tm=1024, tn=512, r_sub=256):
    m, d = x2d.shape
    n = gains.shape[0]
    assert n % tn == 0 and m % tm == 0
    return pl.pallas_call(
        functools.partial(_proj_norm_kernel, r_sub=r_sub),
        grid=(m // tm, n // tn),
        in_specs=[pl.BlockSpec((tm, d), lambda i, j: (i, 0)),
                  pl.BlockSpec((1, d), lambda i, j: (0, 0)),
                  pl.BlockSpec((d, tn), lambda i, j: (0, col_block(j))),
                  pl.BlockSpec((1, tn), lambda i, j: (0, j))],
        out_specs=[pl.BlockSpec((tm, tn), lambda i, j: (i, j)),
                   pl.BlockSpec((tm, d), lambda i, j: (i, 0))],
        out_shape=[jax.ShapeDtypeStruct((m, n), BF16), jax.ShapeDtypeStruct((m, d), BF16)],
        compiler_params=_cparams("parallel", "arbitrary", vmem_mb=48),
        name="proj_norm",
    )(x2d, g_mix.reshape(1, d), w, gains.reshape(1, n))


def _proj_plain_kernel(h_ref, w_ref, o_ref):
    o_ref[...] = jnp.dot(h_ref[...], w_ref[...], preferred_element_type=F32).astype(o_ref.dtype)


def _proj_plain(h, w, n, col_block, tm=1024):
    m, d = h.shape
    assert m % tm == 0
    return pl.pallas_call(
        _proj_plain_kernel,
        grid=(m // tm,),
        in_specs=[pl.BlockSpec((tm, d), lambda i: (i, 0)),
                  pl.BlockSpec((d, n), lambda i: (0, col_block))],
        out_specs=pl.BlockSpec((tm, n), lambda i: (i, 0)),
        out_shape=jax.ShapeDtypeStruct((m, n), BF16),
        compiler_params=_cparams("parallel"),
        name="proj_plain",
    )(h, w)


def _proj_t_kernel(wt_ref, h_ref, gq_ref, gk_ref, cos_ref, sin_ref, q_ref, k_ref, v_ref, *, t_sub):
    n_q = q_ref.shape[0] // HEAD_DIM
    n_k = k_ref.shape[1] // HEAD_DIM
    n_v = v_ref.shape[0] // HEAD_DIM
    half = HEAD_DIM // 2

    def norm_rope(y, g_ref, tok):
        ms = jnp.mean(y * y, axis=0, keepdims=True)
        y = y * lax.rsqrt(ms + EPS) * g_ref[...]
        partner = jnp.concatenate([y[half:], y[:half]], axis=0)
        return y * cos_ref[:, tok] + partner * sin_ref[:, tok]

    def matmul(c):
        return lax.dot_general(wt_ref[...], h_ref[c * t_sub:(c + 1) * t_sub, :], _NT,
                               preferred_element_type=F32)

    def epilogue(c, yt):
        tok = slice(c * t_sub, (c + 1) * t_sub)
        head = lambda hh: yt[hh * HEAD_DIM:(hh + 1) * HEAD_DIM]
        for hh in range(n_q):
            q_ref[hh * HEAD_DIM:(hh + 1) * HEAD_DIM, tok] = norm_rope(head(hh), gq_ref, tok).astype(q_ref.dtype)
        for hh in range(n_k):
            y = norm_rope(head(n_q + hh), gk_ref, tok)
            k_ref[tok, hh * HEAD_DIM:(hh + 1) * HEAD_DIM] = y.T.astype(k_ref.dtype)
        for hh in range(n_v):
            v_ref[hh * HEAD_DIM:(hh + 1) * HEAD_DIM, tok] = head(n_q + n_k + hh).astype(v_ref.dtype)

    _software_pipeline(h_ref.shape[0] // t_sub, matmul, epilogue)


def _proj_t(h, wt, gq_col, gk_col, cos_tt, sin_tt, b, seq, tm=1024, t_sub=256):
    m, d = h.shape
    n = wt.shape[0]
    sb = seq // tm
    col = lambda g: jnp.broadcast_to(g[:, None], (HEAD_DIM, t_sub))
    lane_tile = lambda rows: pl.BlockSpec((None, rows, tm), lambda i: (i // sb, 0, i % sb))
    return pl.pallas_call(
        functools.partial(_proj_t_kernel, t_sub=t_sub),
        grid=(m // tm,),
        in_specs=[pl.BlockSpec((n, d), lambda i: (0, 0)),
                  pl.BlockSpec((tm, d), lambda i: (i, 0)),
                  pl.BlockSpec((HEAD_DIM, t_sub), lambda i: (0, 0)),
                  pl.BlockSpec((HEAD_DIM, t_sub), lambda i: (0, 0)),
                  pl.BlockSpec((HEAD_DIM, tm), lambda i: (0, i % sb)),
                  pl.BlockSpec((HEAD_DIM, tm), lambda i: (0, i % sb))],
        out_specs=[lane_tile(B_QW),
                   pl.BlockSpec((tm, B_KVW), lambda i: (i, 0)),
                   lane_tile(B_KVW)],
        out_shape=[jax.ShapeDtypeStruct((b, B_QW, seq), BF16),
                   jax.ShapeDtypeStruct((m, B_KVW), BF16),
                   jax.ShapeDtypeStruct((b, B_KVW, seq), BF16)],
        compiler_params=_cparams("parallel"),
        name="proj_t",
    )(wt, h, col(gq_col), col(gk_col), cos_tt, sin_tt)


def _mem_kv_kernel(mem_ref, gm_ref, w_ref, gk_ref, k_ref, v_ref):
    x = mem_ref[...]
    ms = jnp.mean(x * x, axis=-1, keepdims=True)
    hm = (x * lax.rsqrt(ms + EPS) * gm_ref[...]).astype(BF16)
    kv = jnp.dot(hm, w_ref[...], preferred_element_type=F32)
    for hh in range(C_HEADS):
        sl = slice(hh * HEAD_DIM, (hh + 1) * HEAD_DIM)
        y = kv[:, sl]
        ms = jnp.mean(y * y, axis=-1, keepdims=True)
        k_ref[:, sl] = (y * lax.rsqrt(ms + EPS) * gk_ref[...]).astype(BF16)
    v_ref[...] = kv[:, C_W:].astype(BF16)


def _mem_kv(mem2d, g_mem, w_kv, g_kc, tm=256):
    m, d = mem2d.shape
    return pl.pallas_call(
        _mem_kv_kernel,
        grid=(m // tm,),
        in_specs=[pl.BlockSpec((tm, d), lambda i: (i, 0)),
                  pl.BlockSpec((1, d), lambda i: (0, 0)),
                  pl.BlockSpec((d, 2 * C_W), lambda i: (0, 0)),
                  pl.BlockSpec((1, HEAD_DIM), lambda i: (0, 0))],
        out_specs=[pl.BlockSpec((tm, C_W), lambda i: (i, 0))] * 2,
        out_shape=[jax.ShapeDtypeStruct((m, C_W), BF16)] * 2,
        compiler_params=_cparams("parallel"),
        name="mem_kv",
    )(mem2d, g_mem.reshape(1, d), w_kv, g_kc.reshape(1, HEAD_DIM))


def _t5_bucket(rel):
    nb = REL_BUCKETS // 2
    ret = np.where(rel > 0, nb, 0)
    n = np.abs(rel)
    max_exact = nb // 2
    large = max_exact + (np.log(np.maximum(n, 1).astype(np.float32) / np.float32(max_exact))
                         / np.float32(math.log(REL_MAX_DIST / max_exact))
                         * np.float32(nb - max_exact)).astype(np.int32)
    large = np.minimum(large, nb - 1)
    return ret + np.where(n < max_exact, n, large)


def _a_bucket_index():
    qi = np.arange(A_QROWS, dtype=np.int32)[:, None]
    kj = np.arange(A_KWIN, dtype=np.int32)[None, :]
    out = []
    for _, dil in DIL_PAIRS:
        for off in range(A_NOFF):
            rel = kj - qi - A_RADIUS * off
            out.append(np.where(np.abs(rel) <= A_RADIUS, _t5_bucket(rel * dil), -1))
    return np.stack(out).astype(np.int32)


def _a_bias_kernel(tab_ref, bucket_ref, o_ref):
    g = pl.program_id(0) // A_NOFF
    bk = bucket_ref[...]
    for hh in range(A_HEADS_PER_GROUP):
        acc = jnp.full(bk.shape, NEG, F32)
        for b in range(REL_BUCKETS):
            acc = jnp.where(bk == b, tab_ref[b, g * A_HEADS_PER_GROUP + hh], acc)
        o_ref[hh] = acc


def _a_bias(rel_bias):
    n = len(DIL_PAIRS) * A_NOFF
    return pl.pallas_call(
        _a_bias_kernel,
        grid=(n,),
        in_specs=[pl.BlockSpec(memory_space=pltpu.SMEM),
                  pl.BlockSpec((None, A_QROWS, A_KWIN), lambda i: (i, 0, 0))],
        out_specs=pl.BlockSpec((None, A_HEADS_PER_GROUP, A_QROWS, A_KWIN), lambda i: (i, 0, 0, 0)),
        out_shape=jax.ShapeDtypeStruct((n, A_HEADS_PER_GROUP, A_QROWS, A_KWIN), F32),
        compiler_params=_cparams("arbitrary"),
        name="a_bias",
    )(rel_bias, _a_bucket_index())


def _mixer_a_kernel(q_ref, k_ref, v_ref, bias_ref, o_ref, lse_ref, *scratch, dil, seq):
    t_rows = q_ref.shape[0]
    ti = pl.program_id(1)
    sub_len = seq // dil
    lq = t_rows // dil
    if dil > 1:
        stage, qstage, kres, vres = scratch

        @pl.when(ti == 0)
        def _():
            for src, dst in ((k_ref, kres), (v_ref, vres)):
                for hh in range(A_HEADS_PER_GROUP):
                    cols = slice(hh * HEAD_DIM, (hh + 1) * HEAD_DIM)
                    stage[...] = src[:, cols].astype(F32)
                    for r in range(dil):
                        dst[r, :, cols] = stage[pl.ds(r, sub_len, stride=dil), :].astype(BF16)

        for hh in range(A_HEADS_PER_GROUP):
            qstage[hh] = q_ref[:, hh * HEAD_DIM:(hh + 1) * HEAD_DIM].astype(F32)

    def scores(r, i, hh):
        q0 = ti * lq + i * A_QROWS
        ks = jnp.clip(q0 - A_RADIUS, 0, sub_len - A_KWIN)
        off = lax.shift_right_logical(q0 - ks, int(math.log2(A_RADIUS)))
        ks = pl.multiple_of(ks, A_RADIUS)
        cols = slice(hh * HEAD_DIM, (hh + 1) * HEAD_DIM)
        if dil > 1:
            rows = pl.ds(i * A_QROWS * dil + r, A_QROWS, stride=dil)
            q = qstage[hh, rows, :].astype(BF16)
            k = kres[r, pl.ds(ks, A_KWIN), cols]
            v = vres[r, pl.ds(ks, A_KWIN), cols]
        else:
            rows = pl.ds(i * A_QROWS, A_QROWS)
            q = q_ref[rows, cols]
            k = k_ref[pl.ds(ks, A_KWIN), cols]
            v = v_ref[pl.ds(ks, A_KWIN), cols]
        s = lax.dot_general(q, k, _NT, preferred_element_type=F32) * SCALE + bias_ref[off, hh]
        return rows, s, v

    def softmax(s):
        m = jnp.max(s, axis=-1, keepdims=True)
        p = jnp.exp(s - m)
        l = jnp.sum(p, axis=-1, keepdims=True)
        return p.astype(BF16), l, m + jnp.log(l)

    items = [(r, i, hh) for r in range(dil) for i in range(lq // A_QROWS) for hh in range(A_HEADS_PER_GROUP)]
    for b0 in range(0, len(items), A_BATCH):
        batch = items[b0:b0 + A_BATCH]
        staged = [scores(*it) for it in batch]
        probs = [softmax(s) for _, s, _ in staged]
        for (_, _, hh), (rows, _, v), (p, l, lse) in zip(batch, staged, probs):
            o_ref[hh, rows, :] = jnp.dot(p, v, preferred_element_type=F32) / l
            lse_ref[hh, rows, :] = jnp.broadcast_to(lse, (A_QROWS, HEAD_DIM))


def _mixer_a_group(qkn, vv, bias_g, gi, dil, t_rows=2048):
    b, seq, _ = qkn.shape
    sub_len = seq // dil
    k_blk = A_W // A_OUT_W
    scratch = []
    if dil > 1:
        scratch = [pltpu.VMEM((seq, HEAD_DIM), F32), pltpu.VMEM((A_HEADS_PER_GROUP, t_rows, HEAD_DIM), F32),
                   pltpu.VMEM((dil, sub_len, A_OUT_W), BF16), pltpu.VMEM((dil, sub_len, A_OUT_W), BF16)]
    return pl.pallas_call(
        functools.partial(_mixer_a_kernel, dil=dil, seq=seq),
        grid=(b, seq // t_rows),
        in_specs=[pl.BlockSpec((None, t_rows, A_OUT_W), lambda bi, ti: (bi, ti, gi)),
                  pl.BlockSpec((None, seq, A_OUT_W), lambda bi, ti: (bi, 0, k_blk + gi)),
                  pl.BlockSpec((None, seq, A_OUT_W), lambda bi, ti: (bi, 0, gi)),
                  pl.BlockSpec((A_NOFF, A_HEADS_PER_GROUP, A_QROWS, A_KWIN), lambda bi, ti: (gi, 0, 0, 0))],
        out_specs=[pl.BlockSpec((None, A_HEADS_PER_GROUP, t_rows, HEAD_DIM), lambda bi, ti: (bi, 0, ti, 0))] * 2,
        out_shape=[jax.ShapeDtypeStruct((b, A_HEADS_PER_GROUP, seq, HEAD_DIM), F32)] * 2,
        scratch_shapes=scratch,
        compiler_params=_cparams("parallel", "arbitrary"),
        name=f"mixer_a_d{dil}",
    )(qkn, qkn, vv, bias_g)


def _mixer_b_kernel(qt_ref, k_ref, vt_ref, o_ref, s_scr, *, tk, group, w):
    tq = qt_ref.shape[1]
    seq = k_ref.shape[0]
    n_chunks = seq // tk
    units = [(i, j) for i in range(group) for j in range(tq // w)]

    def pass_a(u, ci, m):
        i, j = units[u]
        qt = qt_ref[i * HEAD_DIM:(i + 1) * HEAD_DIM, j * w:(j + 1) * w]
        st = jnp.dot(k_ref[ci * tk:(ci + 1) * tk, :], qt, preferred_element_type=F32)
        s_scr[u % 2, ci * tk:(ci + 1) * tk, :] = st
        return jnp.maximum(m, jnp.max(st, axis=0, keepdims=True))

    def pass_b(u, ci, m, l, acc):
        pt = jnp.exp2(s_scr[u % 2, ci * tk:(ci + 1) * tk, :] - m)
        l = l + jnp.sum(pt, axis=0, keepdims=True)
        acc = acc + jnp.dot(vt_ref[:, ci * tk:(ci + 1) * tk], pt.astype(BF16), preferred_element_type=F32)
        return l, acc

    m_prev = None
    for s in range(len(units) + 1):
        m_cur = jnp.full((1, w), NEG, F32)
        l = jnp.zeros((1, w), F32)
        acc = jnp.zeros((HEAD_DIM, w), F32)
        for ci in range(n_chunks):
            if s < len(units):
                m_cur = pass_a(s, ci, m_cur)
            if s > 0:
                l, acc = pass_b(s - 1, ci, m_prev, l, acc)
        if s > 0:
            i, j = units[s - 1]
            o_ref[j * w:(j + 1) * w, i * HEAD_DIM:(i + 1) * HEAD_DIM] = (acc / l).T.astype(o_ref.dtype)
        m_prev = m_cur


def _mixer_b(qbt, kb, vbt, tq=512, tk=512, w=256):
    b, seq, _ = kb.shape
    group = B_Q_HEADS // B_KV_HEADS
    gw = group * HEAD_DIM
    return pl.pallas_call(
        functools.partial(_mixer_b_kernel, tk=tk, group=group, w=w),
        grid=(b, B_KV_HEADS, seq // tq),
        scratch_shapes=[pltpu.VMEM((2, seq, w), F32)],
        in_specs=[pl.BlockSpec((None, gw, tq), lambda bi, kv, qi: (bi, kv, qi)),
                  pl.BlockSpec((None, seq, HEAD_DIM), lambda bi, kv, qi: (bi, 0, kv)),
                  pl.BlockSpec((None, HEAD_DIM, seq), lambda bi, kv, qi: (bi, kv, 0))],
        out_specs=pl.BlockSpec((None, tq, gw), lambda bi, kv, qi: (bi, qi, kv)),
        out_shape=jax.ShapeDtypeStruct((b, seq, B_QW), BF16),
        compiler_params=_cparams("parallel", "parallel", "arbitrary"),
        name="mixer_b",
    )(qbt, kb, vbt)


def _mixer_c_kernel(q_ref, k_ref, v_ref, o_ref):
    for hh in range(C_HEADS):
        sl = slice(hh * HEAD_DIM, (hh + 1) * HEAD_DIM)
        s = lax.dot_general(q_ref[:, sl], k_ref[:, sl], _NT, preferred_element_type=F32) * SCALE
        m = jnp.max(s, axis=-1, keepdims=True)
        p = jnp.exp(s - m)
        l = jnp.sum(p, axis=-1, keepdims=True)
        o = jnp.dot(p.astype(BF16), v_ref[:, sl], preferred_element_type=F32) / l
        o_ref[:, sl] = o.astype(o_ref.dtype)


def _mixer_c(qkn, kc, vc, tq=512):
    b, seq, _ = qkn.shape
    n_mem = kc.shape[1]
    qc_blk = (2 * A_W) // C_W
    return pl.pallas_call(
        _mixer_c_kernel,
        grid=(b, seq // tq),
        in_specs=[pl.BlockSpec((None, tq, C_W), lambda bi, qi: (bi, qi, qc_blk)),
                  pl.BlockSpec((None, n_mem, C_W), lambda bi, qi: (bi, 0, 0)),
                  pl.BlockSpec((None, n_mem, C_W), lambda bi, qi: (bi, 0, 0))],
        out_specs=pl.BlockSpec((None, tq, C_W), lambda bi, qi: (bi, qi, 0)),
        out_shape=jax.ShapeDtypeStruct((b, seq, C_W), BF16),
        compiler_params=_cparams("parallel", "arbitrary"),
        name="mixer_c",
    )(qkn, kc, vc)


def _merge_kernel(h_ref, wg0_ref, wg1_ref, wg2_ref,
                  o0_ref, o1_ref, o2_ref, l0_ref, l1_ref, l2_ref, ob_ref, oc_ref,
                  wa_ref, wb_ref, wc_ref, out_ref, oa_ref):
    @pl.when(pl.program_id(1) == 0)
    def _():
        for hh in range(A_HEADS_PER_GROUP):
            l0, l1, l2 = l0_ref[hh], l1_ref[hh], l2_ref[hh]
            m = jnp.maximum(jnp.maximum(l0, l1), l2)
            w0, w1, w2 = jnp.exp(l0 - m), jnp.exp(l1 - m), jnp.exp(l2 - m)
            oa = (w0 * o0_ref[hh] + w1 * o1_ref[hh] + w2 * o2_ref[hh]) / (w0 + w1 + w2)
            oa_ref[:, hh * HEAD_DIM:(hh + 1) * HEAD_DIM] = oa.astype(oa_ref.dtype)

    h = h_ref[...]
    ga = jax.nn.sigmoid(jnp.dot(h, wg0_ref[...], preferred_element_type=F32))
    merged = ga * jnp.dot(oa_ref[...], wa_ref[...], preferred_element_type=F32)
    gb = jax.nn.sigmoid(jnp.dot(h, wg1_ref[...], preferred_element_type=F32))
    merged += gb * jnp.dot(ob_ref[...], wb_ref[...], preferred_element_type=F32)
    gc = jax.nn.sigmoid(jnp.dot(h, wg2_ref[...], preferred_element_type=F32))
    merged += gc * jnp.dot(oc_ref[...], wc_ref[...], preferred_element_type=F32)
    out_ref[...] = merged.astype(out_ref.dtype)


def _merge(h, w_gate, gate_col0, a_outs, a_lses, ob, oc, wa, wb, wc, tm=1024, tn=512):
    m, d = h.shape
    nj = d // tn
    gate_block0 = gate_col0 // tn
    sb = a_outs[0].shape[2] // tm
    row = lambda w: pl.BlockSpec((tm, w), lambda i, j: (i, 0))
    a_spec = pl.BlockSpec((None, A_HEADS_PER_GROUP, tm, HEAD_DIM), lambda i, j: (i // sb, 0, i % sb, 0))
    col = lambda k: pl.BlockSpec((k, tn), lambda i, j: (0, j))
    gate = lambda br: pl.BlockSpec((d, tn), lambda i, j: (0, gate_block0 + br * nj + j))
    return pl.pallas_call(
        _merge_kernel,
        grid=(m // tm, nj),
        in_specs=[row(d), gate(0), gate(1), gate(2)]
                 + [a_spec] * 6 + [row(B_QW), row(C_W)]
                 + [col(A_OUT_W), col(B_QW), col(C_W)],
        out_specs=pl.BlockSpec((tm, tn), lambda i, j: (i, j)),
        out_shape=jax.ShapeDtypeStruct((m, d), BF16),
        scratch_shapes=[pltpu.VMEM((tm, A_OUT_W), BF16)],
        compiler_params=_cparams("parallel", "arbitrary", vmem_mb=56),
        name="merge",
    )(h, w_gate, w_gate, w_gate, *a_outs, *a_lses, ob, oc, wa, wb, wc)


def _out_proj_kernel(mg_ref, w_ref, x_ref, g_ref, x1_ref, h2_ref, *, r_sub):
    def matmul(c):
        return jnp.dot(mg_ref[c * r_sub:(c + 1) * r_sub, :], w_ref[...], preferred_element_type=F32)

    def epilogue(c, acc):
        rows = slice(c * r_sub, (c + 1) * r_sub)
        x1 = x_ref[rows, :] + acc
        x1_ref[rows, :] = x1
        ms = jnp.mean(x1 * x1, axis=-1, keepdims=True)
        h2_ref[rows, :] = (x1 * lax.rsqrt(ms + EPS) * g_ref[...]).astype(h2_ref.dtype)

    _software_pipeline(mg_ref.shape[0] // r_sub, matmul, epilogue)


def _out_proj(merged, w_o, x2d, g_ffn, tm=512, r_sub=128):
    m, d = x2d.shape
    return pl.pallas_call(
        functools.partial(_out_proj_kernel, r_sub=r_sub),
        grid=(m // tm,),
        in_specs=[pl.BlockSpec((tm, d), lambda i: (i, 0)),
                  pl.BlockSpec((d, d), lambda i: (0, 0)),
                  pl.BlockSpec((tm, d), lambda i: (i, 0)),
                  pl.BlockSpec((1, d), lambda i: (0, 0))],
        out_specs=[pl.BlockSpec((tm, d), lambda i: (i, 0))] * 2,
        out_shape=[jax.ShapeDtypeStruct((m, d), F32), jax.ShapeDtypeStruct((m, d), BF16)],
        compiler_params=_cparams("parallel"),
        name="out_proj",
    )(merged, w_o, x2d, g_ffn.reshape(1, d))


def _ffn_kernel(h_ref, wa_ref, wb_ref, wo_ref, x1_ref, out_ref, *, n_slabs):
    f = pl.program_id(1)
    slab_w = x1_ref.shape[1]

    @pl.when(f == 0)
    def _():
        out_ref[...] = jnp.zeros_like(out_ref)

    for s in range(n_slabs):
        @pl.when(f == s)
        def _():
            out_ref[:, s * slab_w:(s + 1) * slab_w] += x1_ref[...]

    h = h_ref[...]
    a = jnp.dot(h, wa_ref[...], preferred_element_type=F32)
    b = jnp.dot(h, wb_ref[...], preferred_element_type=F32)
    act = (a * jax.nn.sigmoid(a) * b).astype(BF16)
    out_ref[...] += jnp.dot(act, wo_ref[...], preferred_element_type=F32)


def _ffn(h2, w_in, w_out, x1, tm=1024, tf=512, slab_w=256):
    m, d = h2.shape
    d_ff = w_out.shape[0]
    nf = d_ff // tf
    n_slabs = d // slab_w
    assert n_slabs <= nf
    return pl.pallas_call(
        functools.partial(_ffn_kernel, n_slabs=n_slabs),
        grid=(m // tm, nf),
        in_specs=[pl.BlockSpec((tm, d), lambda i, f: (i, 0)),
                  pl.BlockSpec((d, tf), lambda i, f: (0, f)),
                  pl.BlockSpec((d, tf), lambda i, f: (0, nf + f)),
                  pl.BlockSpec((tf, d), lambda i, f: (f, 0)),
                  pl.BlockSpec((tm, slab_w), lambda i, f: (i, jnp.minimum(f, n_slabs - 1)))],
        out_specs=pl.BlockSpec((tm, d), lambda i, f: (i, 0)),
        out_shape=jax.ShapeDtypeStruct((m, d), F32),
        compiler_params=_cparams("parallel", "arbitrary", vmem_mb=48),
        name="ffn",
    )(h2, w_in, w_in, w_out, x1)


def _deinterleave_cols(w, heads):
    d = w.shape[0]
    return w.reshape(d, heads, HEAD_DIM // 2, 2).transpose(0, 1, 3, 2).reshape(d, heads * HEAD_DIM)


def _rope_tables(seq):
    rows = seq // GRID_W
    r = np.repeat(np.arange(rows), GRID_W).astype(np.float64)
    c = np.tile(np.arange(GRID_W), rows).astype(np.float64)
    nf = HEAD_DIM // 4
    inv = ROPE_THETA ** (-np.arange(nf, dtype=np.float64) / nf)
    ang = np.concatenate([r[:, None] * inv, c[:, None] * inv], axis=-1)
    cos, sin = np.cos(ang).T, np.sin(ang).T
    return (np.concatenate([cos, cos], axis=0).astype(np.float32),
            np.concatenate([-sin, sin], axis=0).astype(np.float32))


def _layer(x2d, mem2d, bias_a, cos_t, sin_t, b, seq, g_mix, w_in, g_qa, g_ka, g_qb, g_kb, g_mem, w_mem_kv,
           g_qc, g_kc, w_br_a, w_br_b, w_br_c, w_o, g_ffn, w_ffn_in, w_ffn_out):
    d = x2d.shape[1]
    o_qa, o_ka, o_va, o_qb, o_kb, o_vb, o_qc, o_gt = np.cumsum(
        (0, A_W, A_W, A_W, B_QW, B_KVW, B_KVW, C_W))
    seg = lambda lo, hi: w_in[:, lo:hi]
    w_all = w_in.astype(BF16)
    wt_b = jnp.concatenate([_deinterleave_cols(seg(o_qb, o_kb), B_Q_HEADS),
                            _deinterleave_cols(seg(o_kb, o_vb), B_KV_HEADS),
                            seg(o_vb, o_qc)], axis=1).T.astype(BF16)
    g_norm = jnp.concatenate([jnp.tile(g_qa, A_HEADS), jnp.tile(g_ka, A_HEADS), jnp.tile(g_qc, C_HEADS)])
    deint = lambda g: g.reshape(HEAD_DIM // 2, 2).T.reshape(HEAD_DIM)

    n_qk_tiles = (2 * A_W) // C_W
    qkn, h = _proj_norm(x2d, g_mix, w_all, g_norm,
                        lambda j: jnp.where(j < n_qk_tiles, j, int(o_qc) // C_W), tn=C_W)
    qkn = qkn.reshape(b, seq, -1)
    va = _proj_plain(h, w_all, A_W, int(o_va) // A_W).reshape(b, seq, -1)
    qbt, kb, vbt = _proj_t(h, wt_b, deint(g_qb) * (SCALE * LOG2E), deint(g_kb), cos_t, sin_t, b, seq)
    kb = kb.reshape(b, seq, B_KVW)

    a_outs, a_lses = [], []
    for gi, (_, dil) in enumerate(DIL_PAIRS):
        o, lse = _mixer_a_group(qkn, va, bias_a, gi, dil)
        a_outs.append(o)
        a_lses.append(lse)

    ob = _mixer_b(qbt, kb, vbt).reshape(b * seq, B_QW)

    kc, vc = _mem_kv(mem2d, g_mem, w_mem_kv.astype(BF16), g_kc)
    n_mem = mem2d.shape[0] // b
    oc = _mixer_c(qkn, kc.reshape(b, n_mem, C_W), vc.reshape(b, n_mem, C_W)).reshape(b * seq, C_W)

    merged = _merge(h, w_all, int(o_gt), a_outs, a_lses, ob, oc,
                    w_br_a.astype(BF16), w_br_b.astype(BF16), w_br_c.astype(BF16))
    x1, h2 = _out_proj(merged, w_o.astype(BF16), x2d, g_ffn)
    return _ffn(h2, w_ffn_in.astype(BF16), w_ffn_out.astype(BF16), x1)


def kernel(x, mem, rel_bias, g_mix, w_in, g_qa, g_ka, g_qb, g_kb, g_mem, w_mem_kv, g_qc, g_kc,
           w_br_a, w_br_b, w_br_c, w_o, g_ffn, w_ffn_in, w_ffn_out):
    b, seq, d = x.shape
    depth = w_in.shape[0]
    cos_t, sin_t = _rope_tables(seq)
    bias_a = _a_bias(rel_bias)
    x2d = x.reshape(b * seq, d)
    mem2d = mem.reshape(-1, d)
    for layer in range(depth):
        x2d = _layer(x2d, mem2d, bias_a, cos_t, sin_t, b, seq,
                     g_mix[layer], w_in[layer], g_qa[layer], g_ka[layer], g_qb[layer], g_kb[layer],
                     g_mem[layer], w_mem_kv[layer], g_qc[layer], g_kc[layer],
                     w_br_a[layer], w_br_b[layer], w_br_c[layer], w_o[layer], g_ffn[layer],
                     w_ffn_in[layer], w_ffn_out[layer])
    return x2d.reshape(b, seq, d)
```

```python
import functools
import math

import numpy as np
import jax
import jax.numpy as jnp
from jax import lax
from jax.experimental import pallas as pl
from jax.experimental.pallas import tpu as pltpu

HEAD_DIM = 128
GRID_W = 64
DIL_PAIRS = ((128, 1), (512, 4), (2048, 16))
A_HEADS_PER_GROUP = 2
A_HEADS = A_HEADS_PER_GROUP * len(DIL_PAIRS)
B_Q_HEADS = 6
B_KV_HEADS = 2
ROPE_THETA = 10000.0
C_HEADS = 4
N_BRANCH = 3
REL_BUCKETS = 32
REL_MAX_DIST = 1024
EPS = 1e-6
NEG = -1e30

A_W = A_HEADS * HEAD_DIM
A_OUT_W = A_HEADS_PER_GROUP * HEAD_DIM
B_QW = B_Q_HEADS * HEAD_DIM
B_KVW = B_KV_HEADS * HEAD_DIM
C_W = C_HEADS * HEAD_DIM

SCALE = 1.0 / math.sqrt(HEAD_DIM)
LOG2E = math.log2(math.e)

A_QROWS = 128
A_KWIN = 256
A_RADIUS = 64
A_NOFF = 3
A_BATCH = 8


BF16 = jnp.bfloat16
F32 = jnp.float32

_NT = (((1,), (1,)), ((), ()))


def _software_pipeline(n_chunks, matmul, epilogue):
    acc = matmul(0)
    for c in range(n_chunks):
        nxt = matmul(c + 1) if c + 1 < n_chunks else None
        epilogue(c, acc)
        acc = nxt


def _cparams(*sem, vmem_mb=None):
    limit = None if vmem_mb is None else vmem_mb * 1024 * 1024
    return pltpu.CompilerParams(dimension_semantics=sem, vmem_limit_bytes=limit)


def _proj_norm_kernel(x_ref, gm_ref, w_ref, g_ref, o_ref, h_ref, *, r_sub):
    @pl.when(pl.program_id(1) == 0)
    def _():
        x = x_ref[...]
        ms = jnp.mean(x * x, axis=-1, keepdims=True)
        h_ref[...] = (x * lax.rsqrt(ms + EPS) * gm_ref[...]).astype(h_ref.dtype)

    def matmul(c):
        return jnp.dot(h_ref[c * r_sub:(c + 1) * r_sub, :], w_ref[...], preferred_element_type=F32)

    def epilogue(c, acc):
        for hh in range(o_ref.shape[1] // HEAD_DIM):
            sl = slice(hh * HEAD_DIM, (hh + 1) * HEAD_DIM)
            y = acc[:, sl]
            ms = jnp.mean(y * y, axis=-1, keepdims=True)
            o_ref[c * r_sub:(c + 1) * r_sub, sl] = (y * lax.rsqrt(ms + EPS) * g_ref[:, sl]).astype(o_ref.dtype)

    _software_pipeline(h_ref.shape[0] // r_sub, matmul, epilogue)


def _proj_norm(x2d, g_mix, w, gains, col_block, tm=1024, tn=512, r_sub=256):
    m, d = x2d.shape
    n = gains.shape[0]
    assert n % tn == 0 and m % tm == 0
    return pl.pallas_call(
        functools.partial(_proj_norm_kernel, r_sub=r_sub),
        grid=(m // tm, n // tn),
        in_specs=[pl.BlockSpec((tm, d), lambda i, j: (i, 0)),
                  pl.BlockSpec((1, d), lambda i, j: (0, 0)),
                  pl.BlockSpec((d, tn), lambda i, j: (0, col_block(j))),
                  pl.BlockSpec((1, tn), lambda i, j: (0, j))],
        out_specs=[pl.BlockSpec((tm, tn), lambda i, j: (i, j)),
                   pl.BlockSpec((tm, d), lambda i, j: (i, 0))],
        out_shape=[jax.ShapeDtypeStruct((m, n), BF16), jax.ShapeDtypeStruct((m, d), BF16)],
        compiler_params=_cparams("parallel", "arbitrary", vmem_mb=48),
        name="proj_norm",
    )(x2d, g_mix.reshape(1, d), w, gains.reshape(1, n))


def _proj_plain_kernel(h_ref, w_ref, o_ref):
    o_ref[...] = jnp.dot(h_ref[...], w_ref[...], preferred_element_type=F32).astype(o_ref.dtype)


def _proj_plain(h, w, n, col_block, tm=1024):
    m, d = h.shape
    assert m % tm == 0
    return pl.pallas_call(
        _proj_plain_kernel,
        grid=(m // tm,),
        in_specs=[pl.BlockSpec((tm, d), lambda i: (i, 0)),
                  pl.BlockSpec((d, n), lambda i: (0, col_block))],
        out_specs=pl.BlockSpec((tm, n), lambda i: (i, 0)),
        out_shape=jax.ShapeDtypeStruct((m, n), BF16),
        compiler_params=_cparams("parallel"),
        name="proj_plain",
    )(h, w)


def _proj_t_kernel(wt_ref, h_ref, gq_ref, gk_ref, cos_ref, sin_ref, q_ref, k_ref, v_ref, *, t_sub):
    n_q = q_ref.shape[0] // HEAD_DIM
    n_k = k_ref.shape[1] // HEAD_DIM
    n_v = v_ref.shape[0] // HEAD_DIM
    half = HEAD_DIM // 2

    def norm_rope(y, g_ref, tok):
        ms = jnp.mean(y * y, axis=0, keepdims=True)
        y = y * lax.rsqrt(ms + EPS) * g_ref[...]
        partner = jnp.concatenate([y[half:], y[:half]], axis=0)
        return y * cos_ref[:, tok] + partner * sin_ref[:, tok]

    def matmul(c):
        return lax.dot_general(wt_ref[...], h_ref[c * t_sub:(c + 1) * t_sub, :], _NT,
                               preferred_element_type=F32)

    def epilogue(c, yt):
        tok = slice(c * t_sub, (c + 1) * t_sub)
        head = lambda hh: yt[hh * HEAD_DIM:(hh + 1) * HEAD_DIM]
        for hh in range(n_q):
            q_ref[hh * HEAD_DIM:(hh + 1) * HEAD_DIM, tok] = norm_rope(head(hh), gq_ref, tok).astype(q_ref.dtype)
        for hh in range(n_k):
            y = norm_rope(head(n_q + hh), gk_ref, tok)
            k_ref[tok, hh * HEAD_DIM:(hh + 1) * HEAD_DIM] = y.T.astype(k_ref.dtype)
        for hh in range(n_v):
            v_ref[hh * HEAD_DIM:(hh + 1) * HEAD_DIM, tok] = head(n_q + n_k + hh).astype(v_ref.dtype)

    _software_pipeline(h_ref.shape[0] // t_sub, matmul, epilogue)


def _proj_t(h, wt, gq_col, gk_col, cos_tt, sin_tt, b, seq, tm=1024, t_sub=256):
    m, d = h.shape
    n = wt.shape[0]
    sb = seq // tm
    col = lambda g: jnp.broadcast_to(g[:, None], (HEAD_DIM, t_sub))
    lane_tile = lambda rows: pl.BlockSpec((None, rows, tm), lambda i: (i // sb, 0, i % sb))
    return pl.pallas_call(
        functools.partial(_proj_t_kernel, t_sub=t_sub),
        grid=(m // tm,),
        in_specs=[pl.BlockSpec((n, d), lambda i: (0, 0)),
                  pl.BlockSpec((tm, d), lambda i: (i, 0)),
                  pl.BlockSpec((HEAD_DIM, t_sub), lambda i: (0, 0)),
                  pl.BlockSpec((HEAD_DIM, t_sub), lambda i: (0, 0)),
                  pl.BlockSpec((HEAD_DIM, tm), lambda i: (0, i % sb)),
                  pl.BlockSpec((HEAD_DIM, tm), lambda i: (0, i % sb))],
        out_specs=[lane_tile(B_QW),
                   pl.BlockSpec((tm, B_KVW), lambda i: (i, 0)),
                   lane_tile(B_KVW)],
        out_shape=[jax.ShapeDtypeStruct((b, B_QW, seq), BF16),
                   jax.ShapeDtypeStruct((m, B_KVW), BF16),
                   jax.ShapeDtypeStruct((b, B_KVW, seq), BF16)],
        compiler_params=_cparams("parallel"),
        name="proj_t",
    )(wt, h, col(gq_col), col(gk_col), cos_tt, sin_tt)


def _mem_kv_kernel(mem_ref, gm_ref, w_ref, gk_ref, k_ref, v_ref):
    x = mem_ref[...]
    ms = jnp.mean(x * x, axis=-1, keepdims=True)
    hm = (x * lax.rsqrt(ms + EPS) * gm_ref[...]).astype(BF16)
    kv = jnp.dot(hm, w_ref[...], preferred_element_type=F32)
    for hh in range(C_HEADS):
        sl = slice(hh * HEAD_DIM, (hh + 1) * HEAD_DIM)
        y = kv[:, sl]
        ms = jnp.mean(y * y, axis=-1, keepdims=True)
        k_ref[:, sl] = (y * lax.rsqrt(ms + EPS) * gk_ref[...]).astype(BF16)
    v_ref[...] = kv[:, C_W:].astype(BF16)


def _mem_kv(mem2d, g_mem, w_kv, g_kc, tm=256):
    m, d = mem2d.shape
    return pl.pallas_call(
        _mem_kv_kernel,
        grid=(m // tm,),
        in_specs=[pl.BlockSpec((tm, d), lambda i: (i, 0)),
                  pl.BlockSpec((1, d), lambda i: (0, 0)),
                  pl.BlockSpec((d, 2 * C_W), lambda i: (0, 0)),
                  pl.BlockSpec((1, HEAD_DIM), lambda i: (0, 0))],
        out_specs=[pl.BlockSpec((tm, C_W), lambda i: (i, 0))] * 2,
        out_shape=[jax.ShapeDtypeStruct((m, C_W), BF16)] * 2,
        compiler_params=_cparams("parallel"),
        name="mem_kv",
    )(mem2d, g_mem.reshape(1, d), w_kv, g_kc.reshape(1, HEAD_DIM))


def _t5_bucket(rel):
    nb = REL_BUCKETS // 2
    ret = np.where(rel > 0, nb, 0)
    n = np.abs(rel)
    max_exact = nb // 2
    large = max_exact + (np.log(np.maximum(n, 1).astype(np.float32) / np.float32(max_exact))
                         / np.float32(math.log(REL_MAX_DIST / max_exact))
                         * np.float32(nb - max_exact)).astype(np.int32)
    large = np.minimum(large, nb - 1)
    return ret + np.where(n < max_exact, n, large)


def _a_bucket_index():
    qi = np.arange(A_QROWS, dtype=np.int32)[:, None]
    kj = np.arange(A_KWIN, dtype=np.int32)[None, :]
    out = []
    for _, dil in DIL_PAIRS:
        for off in range(A_NOFF):
            rel = kj - qi - A_RADIUS * off
            out.append(np.where(np.abs(rel) <= A_RADIUS, _t5_bucket(rel * dil), -1))
    return np.stack(out).astype(np.int32)


def _a_bias_kernel(tab_ref, bucket_ref, o_ref):
    g = pl.program_id(0) // A_NOFF
    bk = bucket_ref[...]
    for hh in range(A_HEADS_PER_GROUP):
        acc = jnp.full(bk.shape, NEG, F32)
        for b in range(REL_BUCKETS):
            acc = jnp.where(bk == b, tab_ref[b, g * A_HEADS_PER_GROUP + hh], acc)
        o_ref[hh] = acc


def _a_bias(rel_bias):
    n = len(DIL_PAIRS) * A_NOFF
    return pl.pallas_call(
        _a_bias_kernel,
        grid=(n,),
        in_specs=[pl.BlockSpec(memory_space=pltpu.SMEM),
                  pl.BlockSpec((None, A_QROWS, A_KWIN), lambda i: (i, 0, 0))],
        out_specs=pl.BlockSpec((None, A_HEADS_PER_GROUP, A_QROWS, A_KWIN), lambda i: (i, 0, 0, 0)),
        out_shape=jax.ShapeDtypeStruct((n, A_HEADS_PER_GROUP, A_QROWS, A_KWIN), F32),
        compiler_params=_cparams("arbitrary"),
        name="a_bias",
    )(rel_bias, _a_bucket_index())


def _mixer_a_kernel(q_ref, k_ref, v_ref, bias_ref, oa_ref, stage, qstage, o_acc, lse_acc, *residue_kv, seq):
    t_rows = q_ref.shape[0]
    ti = pl.program_id(1)
    n_groups = len(DIL_PAIRS)

    def softmax(s):
        m = jnp.max(s, axis=-1, keepdims=True)
        p = jnp.exp(s - m)
        l = jnp.sum(p, axis=-1, keepdims=True)
        return p.astype(BF16), l, m + jnp.log(l)

    def run_group(gi, dil, kres, vres):
        sub_len = seq // dil
        lq = t_rows // dil
        gcols = lambda hh: slice(gi * A_OUT_W + hh * HEAD_DIM, gi * A_OUT_W + (hh + 1) * HEAD_DIM)
        if dil > 1:
            @pl.when(ti == 0)
            def _():
                for src, dst in ((k_ref, kres), (v_ref, vres)):
                    for hh in range(A_HEADS_PER_GROUP):
                        stage[...] = src[:, gcols(hh)].astype(F32)
                        for r in range(dil):
                            dst[r, :, hh * HEAD_DIM:(hh + 1) * HEAD_DIM] = (
                                stage[pl.ds(r, sub_len, stride=dil), :].astype(BF16))

            for hh in range(A_HEADS_PER_GROUP):
                qstage[hh] = q_ref[:, gcols(hh)].astype(F32)

        def scores(r, i, hh):
            q0 = ti * lq + i * A_QROWS
            ks = jnp.clip(q0 - A_RADIUS, 0, sub_len - A_KWIN)
            off = lax.shift_right_logical(q0 - ks, int(math.log2(A_RADIUS)))
            ks = pl.multiple_of(ks, A_RADIUS)
            if dil > 1:
                cols = slice(hh * HEAD_DIM, (hh + 1) * HEAD_DIM)
                rows = pl.ds(i * A_QROWS * dil + r, A_QROWS, stride=dil)
                q = qstage[hh, rows, :].astype(BF16)
                k = kres[r, pl.ds(ks, A_KWIN), cols]
                v = vres[r, pl.ds(ks, A_KWIN), cols]
            else:
                rows = pl.ds(i * A_QROWS, A_QROWS)
                q = q_ref[rows, gcols(hh)]
                k = k_ref[pl.ds(ks, A_KWIN), gcols(hh)]
                v = v_ref[pl.ds(ks, A_KWIN), gcols(hh)]
            s = lax.dot_general(q, k, _NT, preferred_element_type=F32) * SCALE + bias_ref[gi * A_NOFF + off, hh]
            return rows, s, v

        def fold(hh, rows, o, lse):
            lse = jnp.broadcast_to(lse, (A_QROWS, HEAD_DIM))
            if gi > 0:
                prev_o, prev_lse = o_acc[hh, rows, :], lse_acc[hh, rows, :]
                m = jnp.maximum(prev_lse, lse)
                w_prev, w_new = jnp.exp(prev_lse - m), jnp.exp(lse - m)
                den = w_prev + w_new
                o = (w_prev * prev_o + w_new * o) / den
                lse = m + jnp.log(den)
            o_acc[hh, rows, :] = o
            if gi + 1 < n_groups:
                lse_acc[hh, rows, :] = lse

        items = [(r, i, hh) for r in range(dil) for i in range(lq // A_QROWS) for hh in range(A_HEADS_PER_GROUP)]
        for b0 in range(0, len(items), A_BATCH):
            batch = items[b0:b0 + A_BATCH]
            staged = [scores(*it) for it in batch]
            probs = [softmax(s) for _, s, _ in staged]
            for (_, _, hh), (rows, _, v), (p, l, lse) in zip(batch, staged, probs):
                fold(hh, rows, jnp.dot(p, v, preferred_element_type=F32) / l, lse)

    strided = [gi for gi, (_, dil) in enumerate(DIL_PAIRS) if dil > 1]
    for gi, (_, dil) in enumerate(DIL_PAIRS):
        kres, vres = (residue_kv[2 * strided.index(gi):2 * strided.index(gi) + 2] if dil > 1 else (None, None))
        run_group(gi, dil, kres, vres)
    for hh in range(A_HEADS_PER_GROUP):
        oa_ref[:, hh * HEAD_DIM:(hh + 1) * HEAD_DIM] = o_acc[hh].astype(oa_ref.dtype)


def _mixer_a(qkn, va, bias_a, t_rows=2048):
    b, seq, _ = qkn.shape
    head_buf = lambda rows: pltpu.VMEM((A_HEADS_PER_GROUP, rows, HEAD_DIM), F32)
    scratch = [pltpu.VMEM((seq, HEAD_DIM), F32), head_buf(t_rows), head_buf(t_rows), head_buf(t_rows)]
    for _, dil in DIL_PAIRS:
        if dil > 1:
            scratch += [pltpu.VMEM((dil, seq // dil, A_OUT_W), BF16)] * 2
    return pl.pallas_call(
        functools.partial(_mixer_a_kernel, seq=seq),
        grid=(b, seq // t_rows),
        in_specs=[pl.BlockSpec((None, t_rows, A_W), lambda bi, ti: (bi, ti, 0)),
                  pl.BlockSpec((None, seq, A_W), lambda bi, ti: (bi, 0, 1)),
                  pl.BlockSpec((None, seq, A_W), lambda bi, ti: (bi, 0, 0)),
                  pl.BlockSpec(bias_a.shape, lambda bi, ti: (0, 0, 0, 0), pipeline_mode=pl.Buffered(1))],
        out_specs=pl.BlockSpec((None, t_rows, A_OUT_W), lambda bi, ti: (bi, ti, 0)),
        out_shape=jax.ShapeDtypeStruct((b, seq, A_OUT_W), BF16),
        scratch_shapes=scratch,
        compiler_params=_cparams("parallel", "arbitrary", vmem_mb=62),
        name="mixer_a",
    )(qkn, qkn, va, bias_a)


def _mixer_b_kernel(qt_ref, k_ref, vt_ref, o_ref, s_scr, *, tk, group, w):
    tq = qt_ref.shape[1]
    seq = k_ref.shape[0]
    n_chunks = seq // tk
    units = [(i, j) for i in range(group) for j in range(tq // w)]

    def pass_a(u, ci, m):
        i, j = units[u]
        qt = qt_ref[i * HEAD_DIM:(i + 1) * HEAD_DIM, j * w:(j + 1) * w]
        st = jnp.dot(k_ref[ci * tk:(ci + 1) * tk, :], qt, preferred_element_type=F32)
        s_scr[u % 2, ci * tk:(ci + 1) * tk, :] = st
        return jnp.maximum(m, jnp.max(st, axis=0, keepdims=True))

    def pass_b(u, ci, m, l, acc):
        pt = jnp.exp2(s_scr[u % 2, ci * tk:(ci + 1) * tk, :] - m)
        l = l + jnp.sum(pt, axis=0, keepdims=True)
        acc = acc + jnp.dot(vt_ref[:, ci * tk:(ci + 1) * tk], pt.astype(BF16), preferred_element_type=F32)
        return l, acc

    m_prev = None
    for s in range(len(units) + 1):
        m_cur = jnp.full((1, w), NEG, F32)
        l = jnp.zeros((1, w), F32)
        acc = jnp.zeros((HEAD_DIM, w), F32)
        for ci in range(n_chunks):
            if s < len(units):
                m_cur = pass_a(s, ci, m_cur)
            if s > 0:
                l, acc = pass_b(s - 1, ci, m_prev, l, acc)
        if s > 0:
            i, j = units[s - 1]
            o_ref[j * w:(j + 1) * w, i * HEAD_DIM:(i + 1) * HEAD_DIM] = (acc / l).T.astype(o_ref.dtype)
        m_prev = m_cur


def _mixer_b(qbt, kb, vbt, tq=512, tk=512, w=256):
    b, seq, _ = kb.shape
    group = B_Q_HEADS // B_KV_HEADS
    gw = group * HEAD_DIM
    return pl.pallas_call(
        functools.partial(_mixer_b_kernel, tk=tk, group=group, w=w),
        grid=(b, B_KV_HEADS, seq // tq),
        scratch_shapes=[pltpu.VMEM((2, seq, w), F32)],
        in_specs=[pl.BlockSpec((None, gw, tq), lambda bi, kv, qi: (bi, kv, qi)),
                  pl.BlockSpec((None, seq, HEAD_DIM), lambda bi, kv, qi: (bi, 0, kv)),
                  pl.BlockSpec((None, HEAD_DIM, seq), lambda bi, kv, qi: (bi, kv, 0))],
        out_specs=pl.BlockSpec((None, tq, gw), lambda bi, kv, qi: (bi, qi, kv)),
        out_shape=jax.ShapeDtypeStruct((b, seq, B_QW), BF16),
        compiler_params=_cparams("parallel", "parallel", "arbitrary"),
        name="mixer_b",
    )(qbt, kb, vbt)


def _mixer_c_kernel(q_ref, k_ref, v_ref, o_ref):
    for hh in range(C_HEADS):
        sl = slice(hh * HEAD_DIM, (hh + 1) * HEAD_DIM)
        s = lax.dot_general(q_ref[:, sl], k_ref[:, sl], _NT, preferred_element_type=F32) * SCALE
        m = jnp.max(s, axis=-1, keepdims=True)
        p = jnp.exp(s - m)
        l = jnp.sum(p, axis=-1, keepdims=True)
        o = jnp.dot(p.astype(BF16), v_ref[:, sl], preferred_element_type=F32) / l
        o_ref[:, sl] = o.astype(o_ref.dtype)


def _mixer_c(qkn, kc, vc, tq=512):
    b, seq, _ = qkn.shape
    n_mem = kc.shape[1]
    qc_blk = (2 * A_W) // C_W
    return pl.pallas_call(
        _mixer_c_kernel,
        grid=(b, seq // tq),
        in_specs=[pl.BlockSpec((None, tq, C_W), lambda bi, qi: (bi, qi, qc_blk)),
                  pl.BlockSpec((None, n_mem, C_W), lambda bi, qi: (bi, 0, 0)),
                  pl.BlockSpec((None, n_mem, C_W), lambda bi, qi: (bi, 0, 0))],
        out_specs=pl.BlockSpec((None, tq, C_W), lambda bi, qi: (bi, qi, 0)),
        out_shape=jax.ShapeDtypeStruct((b, seq, C_W), BF16),
        compiler_params=_cparams("parallel", "arbitrary"),
        name="mixer_c",
    )(qkn, kc, vc)


def _merge_kernel(h_ref, wg0_ref, wg1_ref, wg2_ref, oa_ref, ob_ref, oc_ref, wa_ref, wb_ref, wc_ref, out_ref):
    h = h_ref[...]
    ga = jax.nn.sigmoid(jnp.dot(h, wg0_ref[...], preferred_element_type=F32))
    merged = ga * jnp.dot(oa_ref[...], wa_ref[...], preferred_element_type=F32)
    gb = jax.nn.sigmoid(jnp.dot(h, wg1_ref[...], preferred_element_type=F32))
    merged += gb * jnp.dot(ob_ref[...], wb_ref[...], preferred_element_type=F32)
    gc = jax.nn.sigmoid(jnp.dot(h, wg2_ref[...], preferred_element_type=F32))
    merged += gc * jnp.dot(oc_ref[...], wc_ref[...], preferred_element_type=F32)
    out_ref[...] = merged.astype(out_ref.dtype)


def _merge(h, w_gate, gate_col0, oa, ob, oc, wa, wb, wc, tm=1024, tn=512):
    m, d = h.shape
    nj = d // tn
    gate_block0 = gate_col0 // tn
    row = lambda w: pl.BlockSpec((tm, w), lambda i, j: (i, 0))
    col = lambda k: pl.BlockSpec((k, tn), lambda i, j: (0, j))
    gate = lambda br: pl.BlockSpec((d, tn), lambda i, j: (0, gate_block0 + br * nj + j))
    return pl.pallas_call(
        _merge_kernel,
        grid=(m // tm, nj),
        in_specs=[row(d), gate(0), gate(1), gate(2), row(A_OUT_W), row(B_QW), row(C_W),
                  col(A_OUT_W), col(B_QW), col(C_W)],
        out_specs=pl.BlockSpec((tm, tn), lambda i, j: (i, j)),
        out_shape=jax.ShapeDtypeStruct((m, d), BF16),
        compiler_params=_cparams("parallel", "arbitrary", vmem_mb=48),
        name="merge",
    )(h, w_gate, w_gate, w_gate, oa, ob, oc, wa, wb, wc)


def _out_proj_kernel(mg_ref, w_ref, x_ref, g_ref, x1_ref, h2_ref, *, r_sub):
    def matmul(c):
        return jnp.dot(mg_ref[c * r_sub:(c + 1) * r_sub, :], w_ref[...], preferred_element_type=F32)

    def epilogue(c, acc):
        rows = slice(c * r_sub, (c + 1) * r_sub)
        x1 = x_ref[rows, :] + acc
        x1_ref[rows, :] = x1
        ms = jnp.mean(x1 * x1, axis=-1, keepdims=True)
        h2_ref[rows, :] = (x1 * lax.rsqrt(ms + EPS) * g_ref[...]).astype(h2_ref.dtype)

    _software_pipeline(mg_ref.shape[0] // r_sub, matmul, epilogue)


def _out_proj(merged, w_o, x2d, g_ffn, tm=512, r_sub=512):
    m, d = x2d.shape
    return pl.pallas_call(
        functools.partial(_out_proj_kernel, r_sub=r_sub),
        grid=(m // tm,),
        in_specs=[pl.BlockSpec((tm, d), lambda i: (i, 0)),
                  pl.BlockSpec((d, d), lambda i: (0, 0)),
                  pl.BlockSpec((tm, d), lambda i: (i, 0)),
                  pl.BlockSpec((1, d), lambda i: (0, 0))],
        out_specs=[pl.BlockSpec((tm, d), lambda i: (i, 0))] * 2,
        out_shape=[jax.ShapeDtypeStruct((m, d), F32), jax.ShapeDtypeStruct((m, d), BF16)],
        compiler_params=_cparams("parallel"),
        name="out_proj",
    )(merged, w_o, x2d, g_ffn.reshape(1, d))


def _ffn_kernel(h_ref, wa_ref, wb_ref, wo_ref, x1_ref, out_ref, *, n_slabs):
    f = pl.program_id(1)
    slab_w = x1_ref.shape[1]

    @pl.when(f == 0)
    def _():
        out_ref[...] = jnp.zeros_like(out_ref)

    for s in range(n_slabs):
        @pl.when(f == s)
        def _():
            out_ref[:, s * slab_w:(s + 1) * slab_w] += x1_ref[...]

    h = h_ref[...]
    a = jnp.dot(h, wa_ref[...], preferred_element_type=F32)
    b = jnp.dot(h, wb_ref[...], preferred_element_type=F32)
    act = (a * jax.nn.sigmoid(a) * b).astype(BF16)
    out_ref[...] += jnp.dot(act, wo_ref[...], preferred_element_type=F32)


def _ffn(h2, w_in, w_out, x1, tm=1024, tf=512, slab_w=256):
    m, d = h2.shape
    d_ff = w_out.shape[0]
    nf = d_ff // tf
    n_slabs = d // slab_w
    assert n_slabs <= nf
    return pl.pallas_call(
        functools.partial(_ffn_kernel, n_slabs=n_slabs),
        grid=(m // tm, nf),
        in_specs=[pl.BlockSpec((tm, d), lambda i, f: (i, 0)),
                  pl.BlockSpec((d, tf), lambda i, f: (0, f)),
                  pl.BlockSpec((d, tf), lambda i, f: (0, nf + f)),
                  pl.BlockSpec((tf, d), lambda i, f: (f, 0)),
                  pl.BlockSpec((tm, slab_w), lambda i, f: (i, jnp.minimum(f, n_slabs - 1)))],
        out_specs=pl.BlockSpec((tm, d), lambda i, f: (i, 0)),
        out_shape=jax.ShapeDtypeStruct((m, d), F32),
        compiler_params=_cparams("parallel", "arbitrary", vmem_mb=48),
        name="ffn",
    )(h2, w_in, w_in, w_out, x1)


def _deinterleave_cols(w, heads):
    d = w.shape[0]
    return w.reshape(d, heads, HEAD_DIM // 2, 2).transpose(0, 1, 3, 2).reshape(d, heads * HEAD_DIM)


def _rope_tables(seq):
    rows = seq // GRID_W
    r = np.repeat(np.arange(rows), GRID_W).astype(np.float64)
    c = np.tile(np.arange(GRID_W), rows).astype(np.float64)
    nf = HEAD_DIM // 4
    inv = ROPE_THETA ** (-np.arange(nf, dtype=np.float64) / nf)
    ang = np.concatenate([r[:, None] * inv, c[:, None] * inv], axis=-1)
    cos, sin = np.cos(ang).T, np.sin(ang).T
    return (np.concatenate([cos, cos], axis=0).astype(np.float32),
            np.concatenate([-sin, sin], axis=0).astype(np.float32))


def _layer(x2d, mem2d, bias_a, cos_t, sin_t, b, seq, g_mix, w_in, g_qa, g_ka, g_qb, g_kb, g_mem, w_mem_kv,
           g_qc, g_kc, w_br_a, w_br_b, w_br_c, w_o, g_ffn, w_ffn_in, w_ffn_out):
    d = x2d.shape[1]
    o_qa, o_ka, o_va, o_qb, o_kb, o_vb, o_qc, o_gt = np.cumsum(
        (0, A_W, A_W, A_W, B_QW, B_KVW, B_KVW, C_W))
    seg = lambda lo, hi: w_in[:, lo:hi]
    w_all = w_in.astype(BF16)
    wt_b = jnp.concatenate([_deinterleave_cols(seg(o_qb, o_kb), B_Q_HEADS),
                            _deinterleave_cols(seg(o_kb, o_vb), B_KV_HEADS),
                            seg(o_vb, o_qc)], axis=1).T.astype(BF16)
    g_norm = jnp.concatenate([jnp.tile(g_qa, A_HEADS), jnp.tile(g_ka, A_HEADS), jnp.tile(g_qc, C_HEADS)])
    deint = lambda g: g.reshape(HEAD_DIM // 2, 2).T.reshape(HEAD_DIM)

    n_qk_tiles = (2 * A_W) // C_W
    qkn, h = _proj_norm(x2d, g_mix, w_all, g_norm,
                        lambda j: jnp.where(j < n_qk_tiles, j, int(o_qc) // C_W), tn=C_W)
    qkn = qkn.reshape(b, seq, -1)
    va = _proj_plain(h, w_all, A_W, int(o_va) // A_W).reshape(b, seq, -1)
    qbt, kb, vbt = _proj_t(h, wt_b, deint(g_qb) * (SCALE * LOG2E), deint(g_kb), cos_t, sin_t, b, seq)
    kb = kb.reshape(b, seq, B_KVW)

    oa = _mixer_a(qkn, va, bias_a).reshape(b * seq, A_OUT_W)
    ob = _mixer_b(qbt, kb, vbt).reshape(b * seq, B_QW)

    kc, vc = _mem_kv(mem2d, g_mem, w_mem_kv.astype(BF16), g_kc)
    n_mem = mem2d.shape[0] // b
    oc = _mixer_c(qkn, kc.reshape(b, n_mem, C_W), vc.reshape(b, n_mem, C_W)).reshape(b * seq, C_W)

    merged = _merge(h, w_all, int(o_gt), oa, ob, oc,
                    w_br_a.astype(BF16), w_br_b.astype(BF16), w_br_c.astype(BF16))
    x1, h2 = _out_proj(merged, w_o.astype(BF16), x2d, g_ffn)
    return _ffn(h2, w_ffn_in.astype(BF16), w_ffn_out.astype(BF16), x1)


def kernel(x, mem, rel_bias, g_mix, w_in, g_qa, g_ka, g_qb, g_kb, g_mem, w_mem_kv, g_qc, g_kc,
           w_br_a, w_br_b, w_br_c, w_o, g_ffn, w_ffn_in, w_ffn_out):
    b, seq, d = x.shape
    depth = w_in.shape[0]
    cos_t, sin_t = _rope_tables(seq)
    bias_a = _a_bias(rel_bias)
    x2d = x.reshape(b * seq, d)
    mem2d = mem.reshape(-1, d)
    for layer in range(depth):
        x2d = _layer(x2d, mem2d, bias_a, cos_t, sin_t, b, seq,
                     g_mix[layer], w_in[layer], g_qa[layer], g_ka[layer], g_qb[layer], g_kb[layer],
                     g_mem[layer], w_mem_kv[layer], g_qc[layer], g_kc[layer],
                     w_br_a[layer], w_br_b[layer], w_br_c[layer], w_o[layer], g_ffn[layer],
                     w_ffn_in[layer], w_ffn_out[layer])
    return x2d.reshape(b, seq, d)
```

```python
import functools
import math

import numpy as np
import jax
import jax.numpy as jnp
from jax import lax
from jax.experimental import pallas as pl
from jax.experimental.pallas import tpu as pltpu

HEAD_DIM = 128
GRID_W = 64
DIL_PAIRS = ((128, 1), (512, 4), (2048, 16))
A_HEADS_PER_GROUP = 2
A_HEADS = A_HEADS_PER_GROUP * len(DIL_PAIRS)
B_Q_HEADS = 6
B_KV_HEADS = 2
ROPE_THETA = 10000.0
C_HEADS = 4
N_BRANCH = 3
REL_BUCKETS = 32
REL_MAX_DIST = 1024
EPS = 1e-6
NEG = -1e30

A_W = A_HEADS * HEAD_DIM
A_OUT_W = A_HEADS_PER_GROUP * HEAD_DIM
B_QW = B_Q_HEADS * HEAD_DIM
B_KVW = B_KV_HEADS * HEAD_DIM
C_W = C_HEADS * HEAD_DIM

SCALE = 1.0 / math.sqrt(HEAD_DIM)
LOG2E = math.log2(math.e)

A_QROWS = 128
A_KWIN = 256
A_RADIUS = 64
A_NOFF = 3
A_BATCH = 8


BF16 = jnp.bfloat16
F32 = jnp.float32

_NT = (((1,), (1,)), ((), ()))


def _software_pipeline(n_chunks, matmul, epilogue):
    acc = matmul(0)
    for c in range(n_chunks):
        nxt = matmul(c + 1) if c + 1 < n_chunks else None
        epilogue(c, acc)
        acc = nxt


def _cast_jobs(jobs, n_steps, step_index):
    in_specs, out_specs, out_shapes = [], [], []
    for src, col_block, width in jobs:
        rows = src.shape[0]
        rt = rows // n_steps
        assert rt * n_steps == rows and rt % 16 == 0 and src.shape[1] % width == 0
        in_specs.append(pl.BlockSpec((rt, width), lambda *g, cb=col_block: (step_index(*g), cb)))
        out_specs.append(pl.BlockSpec((rt, width), lambda *g: (step_index(*g), 0)))
        out_shapes.append(jax.ShapeDtypeStruct((rows, width), BF16))
    return in_specs, out_specs, out_shapes


def _run_cast_jobs(src_refs, dst_refs):
    for src, dst in zip(src_refs, dst_refs):
        dst[...] = src[...].astype(dst.dtype)


def _cparams(*sem, vmem_mb=None):
    limit = None if vmem_mb is None else vmem_mb * 1024 * 1024
    return pltpu.CompilerParams(dimension_semantics=sem, vmem_limit_bytes=limit)


def _proj_norm_kernel(x_ref, gm_ref, w_ref, g_ref, o_ref, h_ref, *, r_sub):
    @pl.when(pl.program_id(1) == 0)
    def _():
        x = x_ref[...]
        ms = jnp.mean(x * x, axis=-1, keepdims=True)
        h_ref[...] = (x * lax.rsqrt(ms + EPS) * gm_ref[...]).astype(h_ref.dtype)

    def matmul(c):
        return jnp.dot(h_ref[c * r_sub:(c + 1) * r_sub, :], w_ref[...], preferred_element_type=F32)

    def epilogue(c, acc):
        for hh in range(o_ref.shape[1] // HEAD_DIM):
            sl = slice(hh * HEAD_DIM, (hh + 1) * HEAD_DIM)
            y = acc[:, sl]
            ms = jnp.mean(y * y, axis=-1, keepdims=True)
            o_ref[c * r_sub:(c + 1) * r_sub, sl] = (y * lax.rsqrt(ms + EPS) * g_ref[:, sl]).astype(o_ref.dtype)

    _software_pipeline(h_ref.shape[0] // r_sub, matmul, epilogue)


def _proj_norm(x2d, g_mix, w, gains, col_block, tm=1024, tn=512, r_sub=256):
    m, d = x2d.shape
    n = gains.shape[0]
    assert n % tn == 0 and m % tm == 0
    return pl.pallas_call(
        functools.partial(_proj_norm_kernel, r_sub=r_sub),
        grid=(m // tm, n // tn),
        in_specs=[pl.BlockSpec((tm, d), lambda i, j: (i, 0)),
                  pl.BlockSpec((1, d), lambda i, j: (0, 0)),
                  pl.BlockSpec((d, tn), lambda i, j: (0, col_block(j))),
                  pl.BlockSpec((1, tn), lambda i, j: (0, j))],
        out_specs=[pl.BlockSpec((tm, tn), lambda i, j: (i, j)),
                   pl.BlockSpec((tm, d), lambda i, j: (i, 0))],
        out_shape=[jax.ShapeDtypeStruct((m, n), BF16), jax.ShapeDtypeStruct((m, d), BF16)],
        compiler_params=_cparams("parallel", "arbitrary", vmem_mb=48),
        name="proj_norm",
    )(x2d, g_mix.reshape(1, d), w, gains.reshape(1, n))


def _proj_plain_kernel(h_ref, w_ref, o_ref):
    o_ref[...] = jnp.dot(h_ref[...], w_ref[...], preferred_element_type=F32).astype(o_ref.dtype)


def _proj_plain(h, w, n, col_block, tm=1024):
    m, d = h.shape
    assert m % tm == 0
    return pl.pallas_call(
        _proj_plain_kernel,
        grid=(m // tm,),
        in_specs=[pl.BlockSpec((tm, d), lambda i: (i, 0)),
                  pl.BlockSpec((d, n), lambda i: (0, col_block))],
        out_specs=pl.BlockSpec((tm, n), lambda i: (i, 0)),
        out_shape=jax.ShapeDtypeStruct((m, n), BF16),
        compiler_params=_cparams("parallel"),
        name="proj_plain",
    )(h, w)


def _proj_t_kernel(wt_ref, h_ref, gq_ref, gk_ref, cos_ref, sin_ref, *rest, t_sub, n_cast):
    q_ref, k_ref, v_ref = rest[n_cast:n_cast + 3]
    _run_cast_jobs(rest[:n_cast], rest[n_cast + 3:])
    n_q = q_ref.shape[0] // HEAD_DIM
    n_k = k_ref.shape[1] // HEAD_DIM
    n_v = v_ref.shape[0] // HEAD_DIM
    half = HEAD_DIM // 2

    def norm_rope(y, g_ref, tok):
        ms = jnp.mean(y * y, axis=0, keepdims=True)
        y = y * lax.rsqrt(ms + EPS) * g_ref[...]
        partner = jnp.concatenate([y[half:], y[:half]], axis=0)
        return y * cos_ref[:, tok] + partner * sin_ref[:, tok]

    def matmul(c):
        return lax.dot_general(wt_ref[...], h_ref[c * t_sub:(c + 1) * t_sub, :], _NT,
                               preferred_element_type=F32)

    def epilogue(c, yt):
        tok = slice(c * t_sub, (c + 1) * t_sub)
        head = lambda hh: yt[hh * HEAD_DIM:(hh + 1) * HEAD_DIM]
        for hh in range(n_q):
            q_ref[hh * HEAD_DIM:(hh + 1) * HEAD_DIM, tok] = norm_rope(head(hh), gq_ref, tok).astype(q_ref.dtype)
        for hh in range(n_k):
            y = norm_rope(head(n_q + hh), gk_ref, tok)
            k_ref[tok, hh * HEAD_DIM:(hh + 1) * HEAD_DIM] = y.T.astype(k_ref.dtype)
        for hh in range(n_v):
            v_ref[hh * HEAD_DIM:(hh + 1) * HEAD_DIM, tok] = head(n_q + n_k + hh).astype(v_ref.dtype)

    _software_pipeline(h_ref.shape[0] // t_sub, matmul, epilogue)


def _proj_t(h, wt, gq_col, gk_col, cos_tt, sin_tt, b, seq, cast_jobs, tm=1024, t_sub=256):
    m, d = h.shape
    n = wt.shape[0]
    sb = seq // tm
    col = lambda g: jnp.broadcast_to(g[:, None], (HEAD_DIM, t_sub))
    lane_tile = lambda rows: pl.BlockSpec((None, rows, tm), lambda i: (i // sb, 0, i % sb))
    c_in, c_out, c_shapes = _cast_jobs(cast_jobs, m // tm, lambda i: i)
    return pl.pallas_call(
        functools.partial(_proj_t_kernel, t_sub=t_sub, n_cast=len(cast_jobs)),
        grid=(m // tm,),
        in_specs=[pl.BlockSpec((n, d), lambda i: (0, 0)),
                  pl.BlockSpec((tm, d), lambda i: (i, 0)),
                  pl.BlockSpec((HEAD_DIM, t_sub), lambda i: (0, 0)),
                  pl.BlockSpec((HEAD_DIM, t_sub), lambda i: (0, 0)),
                  pl.BlockSpec((HEAD_DIM, tm), lambda i: (0, i % sb)),
                  pl.BlockSpec((HEAD_DIM, tm), lambda i: (0, i % sb))] + c_in,
        out_specs=[lane_tile(B_QW),
                   pl.BlockSpec((tm, B_KVW), lambda i: (i, 0)),
                   lane_tile(B_KVW)] + c_out,
        out_shape=[jax.ShapeDtypeStruct((b, B_QW, seq), BF16),
                   jax.ShapeDtypeStruct((m, B_KVW), BF16),
                   jax.ShapeDtypeStruct((b, B_KVW, seq), BF16)] + c_shapes,
        compiler_params=_cparams("parallel", vmem_mb=48),
        name="proj_t",
    )(wt, h, col(gq_col), col(gk_col), cos_tt, sin_tt, *[src for src, _, _ in cast_jobs])


def _mem_kv_kernel(mem_ref, gm_ref, w_ref, gk_ref, k_ref, v_ref):
    x = mem_ref[...]
    ms = jnp.mean(x * x, axis=-1, keepdims=True)
    hm = (x * lax.rsqrt(ms + EPS) * gm_ref[...]).astype(BF16)
    kv = jnp.dot(hm, w_ref[...], preferred_element_type=F32)
    for hh in range(C_HEADS):
        sl = slice(hh * HEAD_DIM, (hh + 1) * HEAD_DIM)
        y = kv[:, sl]
        ms = jnp.mean(y * y, axis=-1, keepdims=True)
        k_ref[:, sl] = (y * lax.rsqrt(ms + EPS) * gk_ref[...]).astype(BF16)
    v_ref[...] = kv[:, C_W:].astype(BF16)


def _mem_kv(mem2d, g_mem, w_kv, g_kc, tm=256):
    m, d = mem2d.shape
    return pl.pallas_call(
        _mem_kv_kernel,
        grid=(m // tm,),
        in_specs=[pl.BlockSpec((tm, d), lambda i: (i, 0)),
                  pl.BlockSpec((1, d), lambda i: (0, 0)),
                  pl.BlockSpec((d, 2 * C_W), lambda i: (0, 0)),
                  pl.BlockSpec((1, HEAD_DIM), lambda i: (0, 0))],
        out_specs=[pl.BlockSpec((tm, C_W), lambda i: (i, 0))] * 2,
        out_shape=[jax.ShapeDtypeStruct((m, C_W), BF16)] * 2,
        compiler_params=_cparams("parallel"),
        name="mem_kv",
    )(mem2d, g_mem.reshape(1, d), w_kv, g_kc.reshape(1, HEAD_DIM))


def _t5_bucket(rel):
    nb = REL_BUCKETS // 2
    ret = np.where(rel > 0, nb, 0)
    n = np.abs(rel)
    max_exact = nb // 2
    large = max_exact + (np.log(np.maximum(n, 1).astype(np.float32) / np.float32(max_exact))
                         / np.float32(math.log(REL_MAX_DIST / max_exact))
                         * np.float32(nb - max_exact)).astype(np.int32)
    large = np.minimum(large, nb - 1)
    return ret + np.where(n < max_exact, n, large)


def _a_bucket_index():
    qi = np.arange(A_QROWS, dtype=np.int32)[:, None]
    kj = np.arange(A_KWIN, dtype=np.int32)[None, :]
    out = []
    for _, dil in DIL_PAIRS:
        for off in range(A_NOFF):
            rel = kj - qi - A_RADIUS * off
            out.append(np.where(np.abs(rel) <= A_RADIUS, _t5_bucket(rel * dil), -1))
    return np.stack(out).astype(np.int32)


def _a_bias_kernel(tab_ref, bucket_ref, o_ref):
    g = pl.program_id(0) // A_NOFF
    bk = bucket_ref[...]
    for hh in range(A_HEADS_PER_GROUP):
        acc = jnp.full(bk.shape, NEG, F32)
        for b in range(REL_BUCKETS):
            acc = jnp.where(bk == b, tab_ref[b, g * A_HEADS_PER_GROUP + hh], acc)
        o_ref[hh] = acc


def _a_bias(rel_bias):
    n = len(DIL_PAIRS) * A_NOFF
    return pl.pallas_call(
        _a_bias_kernel,
        grid=(n,),
        in_specs=[pl.BlockSpec(memory_space=pltpu.SMEM),
                  pl.BlockSpec((None, A_QROWS, A_KWIN), lambda i: (i, 0, 0))],
        out_specs=pl.BlockSpec((None, A_HEADS_PER_GROUP, A_QROWS, A_KWIN), lambda i: (i, 0, 0, 0)),
        out_shape=jax.ShapeDtypeStruct((n, A_HEADS_PER_GROUP, A_QROWS, A_KWIN), F32),
        compiler_params=_cparams("arbitrary"),
        name="a_bias",
    )(rel_bias, _a_bucket_index())


def _mixer_a_kernel(q_ref, k_ref, v_ref, bias_ref, oa_ref, stage, qstage, o_acc, lse_acc, *residue_kv, seq):
    t_rows = q_ref.shape[0]
    ti = pl.program_id(1)
    n_groups = len(DIL_PAIRS)

    def softmax(s):
        m = jnp.max(s, axis=-1, keepdims=True)
        p = jnp.exp(s - m)
        l = jnp.sum(p, axis=-1, keepdims=True)
        return p.astype(BF16), l, m + jnp.log(l)

    def run_group(gi, dil, kres, vres):
        sub_len = seq // dil
        lq = t_rows // dil
        gcols = lambda hh: slice(gi * A_OUT_W + hh * HEAD_DIM, gi * A_OUT_W + (hh + 1) * HEAD_DIM)
        if dil > 1:
            @pl.when(ti == 0)
            def _():
                for src, dst in ((k_ref, kres), (v_ref, vres)):
                    for hh in range(A_HEADS_PER_GROUP):
                        stage[...] = src[:, gcols(hh)].astype(F32)
                        for r in range(dil):
                            dst[r, :, hh * HEAD_DIM:(hh + 1) * HEAD_DIM] = (
                                stage[pl.ds(r, sub_len, stride=dil), :].astype(BF16))

            for hh in range(A_HEADS_PER_GROUP):
                qstage[hh] = q_ref[:, gcols(hh)].astype(F32)

        def scores(r, i, hh):
            q0 = ti * lq + i * A_QROWS
            ks = jnp.clip(q0 - A_RADIUS, 0, sub_len - A_KWIN)
            off = lax.shift_right_logical(q0 - ks, int(math.log2(A_RADIUS)))
            ks = pl.multiple_of(ks, A_RADIUS)
            if dil > 1:
                cols = slice(hh * HEAD_DIM, (hh + 1) * HEAD_DIM)
                rows = pl.ds(i * A_QROWS * dil + r, A_QROWS, stride=dil)
                q = qstage[hh, rows, :].astype(BF16)
                k = kres[r, pl.ds(ks, A_KWIN), cols]
                v = vres[r, pl.ds(ks, A_KWIN), cols]
            else:
                rows = pl.ds(i * A_QROWS, A_QROWS)
                q = q_ref[rows, gcols(hh)]
                k = k_ref[pl.ds(ks, A_KWIN), gcols(hh)]
                v = v_ref[pl.ds(ks, A_KWIN), gcols(hh)]
            s = lax.dot_general(q, k, _NT, preferred_element_type=F32) * SCALE + bias_ref[gi * A_NOFF + off, hh]
            return rows, s, v

        def fold(hh, rows, o, lse):
            lse = jnp.broadcast_to(lse, (A_QROWS, HEAD_DIM))
            if gi > 0:
                prev_o, prev_lse = o_acc[hh, rows, :], lse_acc[hh, rows, :]
                m = jnp.maximum(prev_lse, lse)
                w_prev, w_new = jnp.exp(prev_lse - m), jnp.exp(lse - m)
                den = w_prev + w_new
                o = (w_prev * prev_o + w_new * o) / den
                lse = m + jnp.log(den)
            o_acc[hh, rows, :] = o
            if gi + 1 < n_groups:
                lse_acc[hh, rows, :] = lse

        items = [(r, i, hh) for r in range(dil) for i in range(lq // A_QROWS) for hh in range(A_HEADS_PER_GROUP)]
        for b0 in range(0, len(items), A_BATCH):
            batch = items[b0:b0 + A_BATCH]
            staged = [scores(*it) for it in batch]
            probs = [softmax(s) for _, s, _ in staged]
            for (_, _, hh), (rows, _, v), (p, l, lse) in zip(batch, staged, probs):
                fold(hh, rows, jnp.dot(p, v, preferred_element_type=F32) / l, lse)

    strided = [gi for gi, (_, dil) in enumerate(DIL_PAIRS) if dil > 1]
    for gi, (_, dil) in enumerate(DIL_PAIRS):
        kres, vres = (residue_kv[2 * strided.index(gi):2 * strided.index(gi) + 2] if dil > 1 else (None, None))
        run_group(gi, dil, kres, vres)
    for hh in range(A_HEADS_PER_GROUP):
        oa_ref[:, hh * HEAD_DIM:(hh + 1) * HEAD_DIM] = o_acc[hh].astype(oa_ref.dtype)


def _mixer_a(qkn, va, bias_a, t_rows=2048):
    b, seq, _ = qkn.shape
    head_buf = lambda rows: pltpu.VMEM((A_HEADS_PER_GROUP, rows, HEAD_DIM), F32)
    scratch = [pltpu.VMEM((seq, HEAD_DIM), F32), head_buf(t_rows), head_buf(t_rows), head_buf(t_rows)]
    for _, dil in DIL_PAIRS:
        if dil > 1:
            scratch += [pltpu.VMEM((dil, seq // dil, A_OUT_W), BF16)] * 2
    return pl.pallas_call(
        functools.partial(_mixer_a_kernel, seq=seq),
        grid=(b, seq // t_rows),
        in_specs=[pl.BlockSpec((None, t_rows, A_W), lambda bi, ti: (bi, ti, 0)),
                  pl.BlockSpec((None, seq, A_W), lambda bi, ti: (bi, 0, 1)),
                  pl.BlockSpec((None, seq, A_W), lambda bi, ti: (bi, 0, 0)),
                  pl.BlockSpec(bias_a.shape, lambda bi, ti: (0, 0, 0, 0), pipeline_mode=pl.Buffered(1))],
        out_specs=pl.BlockSpec((None, t_rows, A_OUT_W), lambda bi, ti: (bi, ti, 0)),
        out_shape=jax.ShapeDtypeStruct((b, seq, A_OUT_W), BF16),
        scratch_shapes=scratch,
        compiler_params=_cparams("parallel", "arbitrary", vmem_mb=62),
        name="mixer_a",
    )(qkn, qkn, va, bias_a)


def _mixer_b_kernel(qt_ref, k_ref, vt_ref, *rest, tk, group, w, n_cast):
    o_ref, s_scr = rest[n_cast], rest[-1]
    _run_cast_jobs(rest[:n_cast], rest[n_cast + 1:-1])
    tq = qt_ref.shape[1]
    seq = k_ref.shape[0]
    n_chunks = seq // tk
    units = [(i, j) for i in range(group) for j in range(tq // w)]

    def pass_a(u, ci, m):
        i, j = units[u]
        qt = qt_ref[i * HEAD_DIM:(i + 1) * HEAD_DIM, j * w:(j + 1) * w]
        st = jnp.dot(k_ref[ci * tk:(ci + 1) * tk, :], qt, preferred_element_type=F32)
        s_scr[u % 2, ci * tk:(ci + 1) * tk, :] = st
        return jnp.maximum(m, jnp.max(st, axis=0, keepdims=True))

    def pass_b(u, ci, m, l, acc):
        pt = jnp.exp2(s_scr[u % 2, ci * tk:(ci + 1) * tk, :] - m)
        l = l + jnp.sum(pt, axis=0, keepdims=True)
        acc = acc + jnp.dot(vt_ref[:, ci * tk:(ci + 1) * tk], pt.astype(BF16), preferred_element_type=F32)
        return l, acc

    m_prev = None
    for s in range(len(units) + 1):
        m_cur = jnp.full((1, w), NEG, F32)
        l = jnp.zeros((1, w), F32)
        acc = jnp.zeros((HEAD_DIM, w), F32)
        for ci in range(n_chunks):
            if s < len(units):
                m_cur = pass_a(s, ci, m_cur)
            if s > 0:
                l, acc = pass_b(s - 1, ci, m_prev, l, acc)
        if s > 0:
            i, j = units[s - 1]
            o_ref[j * w:(j + 1) * w, i * HEAD_DIM:(i + 1) * HEAD_DIM] = (acc / l).T.astype(o_ref.dtype)
        m_prev = m_cur


def _mixer_b(qbt, kb, vbt, cast_jobs, tq=512, tk=512, w=256):
    b, seq, _ = kb.shape
    group = B_Q_HEADS // B_KV_HEADS
    gw = group * HEAD_DIM
    nq = seq // tq
    n_steps = b * B_KV_HEADS * nq
    c_in, c_out, c_shapes = _cast_jobs(cast_jobs, n_steps, lambda bi, kv, qi: (bi * B_KV_HEADS + kv) * nq + qi)
    return pl.pallas_call(
        functools.partial(_mixer_b_kernel, tk=tk, group=group, w=w, n_cast=len(cast_jobs)),
        grid=(b, B_KV_HEADS, nq),
        scratch_shapes=[pltpu.VMEM((2, seq, w), F32)],
        in_specs=[pl.BlockSpec((None, gw, tq), lambda bi, kv, qi: (bi, kv, qi)),
                  pl.BlockSpec((None, seq, HEAD_DIM), lambda bi, kv, qi: (bi, 0, kv)),
                  pl.BlockSpec((None, HEAD_DIM, seq), lambda bi, kv, qi: (bi, kv, 0))] + c_in,
        out_specs=[pl.BlockSpec((None, tq, gw), lambda bi, kv, qi: (bi, qi, kv))] + c_out,
        out_shape=[jax.ShapeDtypeStruct((b, seq, B_QW), BF16)] + c_shapes,
        compiler_params=_cparams("parallel", "parallel", "arbitrary"),
        name="mixer_b",
    )(qbt, kb, vbt, *[src for src, _, _ in cast_jobs])


def _mixer_c_kernel(q_ref, k_ref, v_ref, o_ref):
    for hh in range(C_HEADS):
        sl = slice(hh * HEAD_DIM, (hh + 1) * HEAD_DIM)
        s = lax.dot_general(q_ref[:, sl], k_ref[:, sl], _NT, preferred_element_type=F32) * SCALE
        m = jnp.max(s, axis=-1, keepdims=True)
        p = jnp.exp(s - m)
        l = jnp.sum(p, axis=-1, keepdims=True)
        o = jnp.dot(p.astype(BF16), v_ref[:, sl], preferred_element_type=F32) / l
        o_ref[:, sl] = o.astype(o_ref.dtype)


def _mixer_c(qkn, kc, vc, tq=512):
    b, seq, _ = qkn.shape
    n_mem = kc.shape[1]
    qc_blk = (2 * A_W) // C_W
    return pl.pallas_call(
        _mixer_c_kernel,
        grid=(b, seq // tq),
        in_specs=[pl.BlockSpec((None, tq, C_W), lambda bi, qi: (bi, qi, qc_blk)),
                  pl.BlockSpec((None, n_mem, C_W), lambda bi, qi: (bi, 0, 0)),
                  pl.BlockSpec((None, n_mem, C_W), lambda bi, qi: (bi, 0, 0))],
        out_specs=pl.BlockSpec((None, tq, C_W), lambda bi, qi: (bi, qi, 0)),
        out_shape=jax.ShapeDtypeStruct((b, seq, C_W), BF16),
        compiler_params=_cparams("parallel", "arbitrary"),
        name="mixer_c",
    )(qkn, kc, vc)


def _merge_kernel(h_ref, wg0_ref, wg1_ref, wg2_ref, oa_ref, ob_ref, oc_ref, wa_ref, wb_ref, wc_ref, out_ref):
    h = h_ref[...]
    ga = jax.nn.sigmoid(jnp.dot(h, wg0_ref[...], preferred_element_type=F32))
    merged = ga * jnp.dot(oa_ref[...], wa_ref[...], preferred_element_type=F32)
    gb = jax.nn.sigmoid(jnp.dot(h, wg1_ref[...], preferred_element_type=F32))
    merged += gb * jnp.dot(ob_ref[...], wb_ref[...], preferred_element_type=F32)
    gc = jax.nn.sigmoid(jnp.dot(h, wg2_ref[...], preferred_element_type=F32))
    merged += gc * jnp.dot(oc_ref[...], wc_ref[...], preferred_element_type=F32)
    out_ref[...] = merged.astype(out_ref.dtype)


def _merge(h, w_gates, oa, ob, oc, wa, wb, wc, tm=1024, tn=512):
    m, d = h.shape
    nj = d // tn
    row = lambda w: pl.BlockSpec((tm, w), lambda i, j: (i, 0))
    col = lambda k: pl.BlockSpec((k, tn), lambda i, j: (0, j))
    return pl.pallas_call(
        _merge_kernel,
        grid=(m // tm, nj),
        in_specs=[row(d), col(d), col(d), col(d), row(A_OUT_W), row(B_QW), row(C_W),
                  col(A_OUT_W), col(B_QW), col(C_W)],
        out_specs=pl.BlockSpec((tm, tn), lambda i, j: (i, j)),
        out_shape=jax.ShapeDtypeStruct((m, d), BF16),
        compiler_params=_cparams("parallel", "arbitrary", vmem_mb=48),
        name="merge",
    )(h, *w_gates, oa, ob, oc, wa, wb, wc)


def _out_proj_kernel(mg_ref, w_ref, x_ref, g_ref, x1_ref, h2_ref, *, r_sub):
    def matmul(c):
        return jnp.dot(mg_ref[c * r_sub:(c + 1) * r_sub, :], w_ref[...], preferred_element_type=F32)

    def epilogue(c, acc):
        rows = slice(c * r_sub, (c + 1) * r_sub)
        x1 = x_ref[rows, :] + acc
        x1_ref[rows, :] = x1
        ms = jnp.mean(x1 * x1, axis=-1, keepdims=True)
        h2_ref[rows, :] = (x1 * lax.rsqrt(ms + EPS) * g_ref[...]).astype(h2_ref.dtype)

    _software_pipeline(mg_ref.shape[0] // r_sub, matmul, epilogue)


def _out_proj(merged, w_o, x2d, g_ffn, tm=512, r_sub=512):
    m, d = x2d.shape
    return pl.pallas_call(
        functools.partial(_out_proj_kernel, r_sub=r_sub),
        grid=(m // tm,),
        in_specs=[pl.BlockSpec((tm, d), lambda i: (i, 0)),
                  pl.BlockSpec((d, d), lambda i: (0, 0)),
                  pl.BlockSpec((tm, d), lambda i: (i, 0)),
                  pl.BlockSpec((1, d), lambda i: (0, 0))],
        out_specs=[pl.BlockSpec((tm, d), lambda i: (i, 0))] * 2,
        out_shape=[jax.ShapeDtypeStruct((m, d), F32), jax.ShapeDtypeStruct((m, d), BF16)],
        compiler_params=_cparams("parallel"),
        name="out_proj",
    )(merged, w_o, x2d, g_ffn.reshape(1, d))


def _ffn_kernel(h_ref, wa_ref, wb_ref, wo_ref, x1_ref, out_ref, *, n_slabs):
    f = pl.program_id(1)
    slab_w = x1_ref.shape[1]

    @pl.when(f == 0)
    def _():
        out_ref[...] = jnp.zeros_like(out_ref)

    for s in range(n_slabs):
        @pl.when(f == s)
        def _():
            out_ref[:, s * slab_w:(s + 1) * slab_w] += x1_ref[...]

    h = h_ref[...]
    a = jnp.dot(h, wa_ref[...], preferred_element_type=F32)
    b = jnp.dot(h, wb_ref[...], preferred_element_type=F32)
    act = (a * jax.nn.sigmoid(a) * b).astype(BF16)
    out_ref[...] += jnp.dot(act, wo_ref[...], preferred_element_type=F32)


def _ffn(h2, w_a, w_b, w_out, x1, tm=1024, tf=512, slab_w=256):
    m, d = h2.shape
    d_ff = w_out.shape[0]
    nf = d_ff // tf
    n_slabs = d // slab_w
    assert n_slabs <= nf
    return pl.pallas_call(
        functools.partial(_ffn_kernel, n_slabs=n_slabs),
        grid=(m // tm, nf),
        in_specs=[pl.BlockSpec((tm, d), lambda i, f: (i, 0)),
                  pl.BlockSpec((d, tf), lambda i, f: (0, f)),
                  pl.BlockSpec((d, tf), lambda i, f: (0, f)),
                  pl.BlockSpec((tf, d), lambda i, f: (f, 0)),
                  pl.BlockSpec((tm, slab_w), lambda i, f: (i, jnp.minimum(f, n_slabs - 1)))],
        out_specs=pl.BlockSpec((tm, d), lambda i, f: (i, 0)),
        out_shape=jax.ShapeDtypeStruct((m, d), F32),
        compiler_params=_cparams("parallel", "arbitrary", vmem_mb=48),
        name="ffn",
    )(h2, w_a, w_b, w_out, x1)


def _deinterleave_cols(w, heads):
    d = w.shape[0]
    return w.reshape(d, heads, HEAD_DIM // 2, 2).transpose(0, 1, 3, 2).reshape(d, heads * HEAD_DIM)


def _rope_tables(seq):
    rows = seq // GRID_W
    r = np.repeat(np.arange(rows), GRID_W).astype(np.float64)
    c = np.tile(np.arange(GRID_W), rows).astype(np.float64)
    nf = HEAD_DIM // 4
    inv = ROPE_THETA ** (-np.arange(nf, dtype=np.float64) / nf)
    ang = np.concatenate([r[:, None] * inv, c[:, None] * inv], axis=-1)
    cos, sin = np.cos(ang).T, np.sin(ang).T
    return (np.concatenate([cos, cos], axis=0).astype(np.float32),
            np.concatenate([-sin, sin], axis=0).astype(np.float32))


def _layer(x2d, mem2d, bias_a, cos_t, sin_t, b, seq, g_mix, w_in, g_qa, g_ka, g_qb, g_kb, g_mem, w_mem_kv,
           g_qc, g_kc, w_br_a, w_br_b, w_br_c, w_o, g_ffn, w_ffn_in, w_ffn_out):
    d = x2d.shape[1]
    o_qa, o_ka, o_va, o_qb, o_kb, o_vb, o_qc, o_gt = np.cumsum(
        (0, A_W, A_W, A_W, B_QW, B_KVW, B_KVW, C_W))
    seg = lambda lo, hi: w_in[:, lo:hi]
    d_ff = w_ffn_out.shape[0]
    w_all = w_in[:, :o_gt].astype(BF16)
    wt_b = jnp.concatenate([_deinterleave_cols(seg(o_qb, o_kb), B_Q_HEADS),
                            _deinterleave_cols(seg(o_kb, o_vb), B_KV_HEADS),
                            seg(o_vb, o_qc)], axis=1).T.astype(BF16)
    g_norm = jnp.concatenate([jnp.tile(g_qa, A_HEADS), jnp.tile(g_ka, A_HEADS), jnp.tile(g_qc, C_HEADS)])
    deint = lambda g: g.reshape(HEAD_DIM // 2, 2).T.reshape(HEAD_DIM)

    n_qk_tiles = (2 * A_W) // C_W
    qkn, h = _proj_norm(x2d, g_mix, w_all, g_norm,
                        lambda j: jnp.where(j < n_qk_tiles, j, int(o_qc) // C_W), tn=C_W)
    qkn = qkn.reshape(b, seq, -1)
    va = _proj_plain(h, w_all, A_W, int(o_va) // A_W).reshape(b, seq, -1)
    late_weights = [(w_ffn_out, 0, d), (w_o, 0, d), (w_br_a, 0, d), (w_br_b, 0, d), (w_br_c, 0, d)]
    qbt, kb, vbt, w_out_b, w_o_b, wa_b, wb_b, wc_b = _proj_t(
        h, wt_b, deint(g_qb) * (SCALE * LOG2E), deint(g_kb), cos_t, sin_t, b, seq, late_weights)
    kb = kb.reshape(b, seq, B_KVW)

    oa = _mixer_a(qkn, va, bias_a).reshape(b * seq, A_OUT_W)
    gate_jobs = [(w_in, int(o_gt) // d + br, d) for br in range(N_BRANCH)]
    ob, w_ffa, w_ffb, *w_gates = _mixer_b(qbt, kb, vbt, [(w_ffn_in, 0, d_ff), (w_ffn_in, 1, d_ff)] + gate_jobs)
    ob = ob.reshape(b * seq, B_QW)

    kc, vc = _mem_kv(mem2d, g_mem, w_mem_kv.astype(BF16), g_kc)
    n_mem = mem2d.shape[0] // b
    oc = _mixer_c(qkn, kc.reshape(b, n_mem, C_W), vc.reshape(b, n_mem, C_W)).reshape(b * seq, C_W)

    merged = _merge(h, w_gates, oa, ob, oc, wa_b, wb_b, wc_b)
    x1, h2 = _out_proj(merged, w_o_b, x2d, g_ffn)
    return _ffn(h2, w_ffa, w_ffb, w_out_b, x1)


def kernel(x, mem, rel_bias, g_mix, w_in, g_qa, g_ka, g_qb, g_kb, g_mem, w_mem_kv, g_qc, g_kc,
           w_br_a, w_br_b, w_br_c, w_o, g_ffn, w_ffn_in, w_ffn_out):
    b, seq, d = x.shape
    depth = w_in.shape[0]
    cos_t, sin_t = _rope_tables(seq)
    bias_a = _a_bias(rel_bias)
    x2d = x.reshape(b * seq, d)
    mem2d = mem.reshape(-1, d)
    for layer in range(depth):
        x2d = _layer(x2d, mem2d, bias_a, cos_t, sin_t, b, seq,
                     g_mix[layer], w_in[layer], g_qa[layer], g_ka[layer], g_qb[layer], g_kb[layer],
                     g_mem[layer], w_mem_kv[layer], g_qc[layer], g_kc[layer],
                     w_br_a[layer], w_br_b[layer], w_br_c[layer], w_o[layer], g_ffn[layer],
                     w_ffn_in[layer], w_ffn_out[layer])
    return x2d.reshape(b, seq, d)
```

```python
import functools
import math

import numpy as np
import jax
import jax.numpy as jnp
from jax import lax
from jax.experimental import pallas as pl
from jax.experimental.pallas import tpu as pltpu

HEAD_DIM = 128
GRID_W = 64
DIL_PAIRS = ((128, 1), (512, 4), (2048, 16))
A_HEADS_PER_GROUP = 2
A_HEADS = A_HEADS_PER_GROUP * len(DIL_PAIRS)
B_Q_HEADS = 6
B_KV_HEADS = 2
ROPE_THETA = 10000.0
C_HEADS = 4
N_BRANCH = 3
REL_BUCKETS = 32
REL_MAX_DIST = 1024
EPS = 1e-6
NEG = -1e30

A_W = A_HEADS * HEAD_DIM
A_OUT_W = A_HEADS_PER_GROUP * HEAD_DIM
B_QW = B_Q_HEADS * HEAD_DIM
B_KVW = B_KV_HEADS * HEAD_DIM
C_W = C_HEADS * HEAD_DIM

SCALE = 1.0 / math.sqrt(HEAD_DIM)
LOG2E = math.log2(math.e)

A_QROWS = 128
A_KWIN = 256
A_RADIUS = 64
A_NOFF = 3
A_BATCH = 8


BF16 = jnp.bfloat16
F32 = jnp.float32

_NT = (((1,), (1,)), ((), ()))


def _software_pipeline(n_chunks, matmul, epilogue):
    acc = matmul(0)
    for c in range(n_chunks):
        nxt = matmul(c + 1) if c + 1 < n_chunks else None
        epilogue(c, acc)
        acc = nxt


def _cast_jobs(jobs, n_steps, step_index):
    in_specs, out_specs, out_shapes = [], [], []
    for src, col_block, width in jobs:
        rows = src.shape[0]
        rt = rows // n_steps
        assert rt * n_steps == rows and rt % 16 == 0 and src.shape[1] % width == 0
        in_specs.append(pl.BlockSpec((rt, width), lambda *g, cb=col_block: (step_index(*g), cb)))
        out_specs.append(pl.BlockSpec((rt, width), lambda *g: (step_index(*g), 0)))
        out_shapes.append(jax.ShapeDtypeStruct((rows, width), BF16))
    return in_specs, out_specs, out_shapes


def _run_cast_jobs(src_refs, dst_refs):
    for src, dst in zip(src_refs, dst_refs):
        dst[...] = src[...].astype(dst.dtype)


def _cast_kernel(src_ref, dst_ref):
    dst_ref[...] = src_ref[...].astype(dst_ref.dtype)


def _cast_cols(w, n_cols, row_tile=256):
    rows = w.shape[0]
    assert rows % row_tile == 0
    return pl.pallas_call(
        _cast_kernel,
        grid=(rows // row_tile,),
        in_specs=[pl.BlockSpec((row_tile, n_cols), lambda i: (i, 0))],
        out_specs=pl.BlockSpec((row_tile, n_cols), lambda i: (i, 0)),
        out_shape=jax.ShapeDtypeStruct((rows, n_cols), BF16),
        compiler_params=_cparams("parallel"),
        name="cast_head_weights",
    )(w)


def _cparams(*sem, vmem_mb=None):
    limit = None if vmem_mb is None else vmem_mb * 1024 * 1024
    return pltpu.CompilerParams(dimension_semantics=sem, vmem_limit_bytes=limit)


def _proj_norm_kernel(x_hbm, gm_ref, w_ref, g_ref, o_ref, h_ref, xbuf, sem, *, r_sub, n_j):
    i, j = pl.program_id(0), pl.program_id(1)
    n_i = pl.num_programs(0)
    tm = h_ref.shape[0]
    piece = tm // n_j

    def x_copy(tile, p):
        slot = tile % 2
        rows = pl.ds(tile * tm + p * piece, piece)
        return pltpu.make_async_copy(x_hbm.at[rows, :], xbuf.at[slot, pl.ds(p * piece, piece), :], sem.at[slot, p])

    @pl.when((i == 0) & (j == 0))
    def _():
        for p in range(n_j):
            x_copy(0, p).start()

    @pl.when(i + 1 < n_i)
    def _():
        for p in range(n_j):
            @pl.when(j == p)
            def _():
                x_copy(i + 1, p).start()

    @pl.when(j == 0)
    def _():
        for p in range(n_j):
            x_copy(i, p).wait()
        x = xbuf[i % 2]
        ms = jnp.mean(x * x, axis=-1, keepdims=True)
        h_ref[...] = (x * lax.rsqrt(ms + EPS) * gm_ref[...]).astype(h_ref.dtype)

    def matmul(c):
        return jnp.dot(h_ref[c * r_sub:(c + 1) * r_sub, :], w_ref[...], preferred_element_type=F32)

    def epilogue(c, acc):
        for hh in range(o_ref.shape[1] // HEAD_DIM):
            sl = slice(hh * HEAD_DIM, (hh + 1) * HEAD_DIM)
            y = acc[:, sl]
            ms = jnp.mean(y * y, axis=-1, keepdims=True)
            o_ref[c * r_sub:(c + 1) * r_sub, sl] = (y * lax.rsqrt(ms + EPS) * g_ref[:, sl]).astype(o_ref.dtype)

    _software_pipeline(h_ref.shape[0] // r_sub, matmul, epilogue)


def _proj_norm(x2d, g_mix, w, gains, col_block, tm=1024, tn=512, r_sub=256):
    m, d = x2d.shape
    n = gains.shape[0]
    n_j = n // tn
    assert n % tn == 0 and m % tm == 0 and tm % (8 * n_j) == 0
    return pl.pallas_call(
        functools.partial(_proj_norm_kernel, r_sub=r_sub, n_j=n_j),
        grid=(m // tm, n_j),
        in_specs=[pl.BlockSpec(memory_space=pl.ANY),
                  pl.BlockSpec((1, d), lambda i, j: (0, 0)),
                  pl.BlockSpec((d, tn), lambda i, j: (0, col_block(j))),
                  pl.BlockSpec((1, tn), lambda i, j: (0, j))],
        out_specs=[pl.BlockSpec((tm, tn), lambda i, j: (i, j)),
                   pl.BlockSpec((tm, d), lambda i, j: (i, 0))],
        out_shape=[jax.ShapeDtypeStruct((m, n), BF16), jax.ShapeDtypeStruct((m, d), BF16)],
        scratch_shapes=[pltpu.VMEM((2, tm, d), F32), pltpu.SemaphoreType.DMA((2, n_j))],
        compiler_params=_cparams("arbitrary", "arbitrary", vmem_mb=48),
        name="proj_norm",
    )(x2d, g_mix.reshape(1, d), w, gains.reshape(1, n))


def _proj_plain_kernel(h_ref, w_ref, o_ref):
    o_ref[...] = jnp.dot(h_ref[...], w_ref[...], preferred_element_type=F32).astype(o_ref.dtype)


def _proj_plain(h, w, n, col_block, tm=1024):
    m, d = h.shape
    assert m % tm == 0
    return pl.pallas_call(
        _proj_plain_kernel,
        grid=(m // tm,),
        in_specs=[pl.BlockSpec((tm, d), lambda i: (i, 0)),
                  pl.BlockSpec((d, n), lambda i: (0, col_block))],
        out_specs=pl.BlockSpec((tm, n), lambda i: (i, 0)),
        out_shape=jax.ShapeDtypeStruct((m, n), BF16),
        compiler_params=_cparams("parallel"),
        name="proj_plain",
    )(h, w)


def _proj_t_kernel(wt_ref, h_ref, gq_ref, gk_ref, cos_ref, sin_ref, *rest, t_sub, n_cast):
    q_ref, k_ref, v_ref = rest[n_cast:n_cast + 3]
    _run_cast_jobs(rest[:n_cast], rest[n_cast + 3:])
    n_q = q_ref.shape[0] // HEAD_DIM
    n_k = k_ref.shape[1] // HEAD_DIM
    n_v = v_ref.shape[0] // HEAD_DIM
    half = HEAD_DIM // 2

    def norm_rope(y, g_ref, tok):
        ms = jnp.mean(y * y, axis=0, keepdims=True)
        y = y * lax.rsqrt(ms + EPS) * g_ref[...]
        partner = jnp.concatenate([y[half:], y[:half]], axis=0)
        return y * cos_ref[:, tok] + partner * sin_ref[:, tok]

    def matmul(c):
        return lax.dot_general(wt_ref[...], h_ref[c * t_sub:(c + 1) * t_sub, :], _NT,
                               preferred_element_type=F32)

    def epilogue(c, yt):
        tok = slice(c * t_sub, (c + 1) * t_sub)
        head = lambda hh: yt[hh * HEAD_DIM:(hh + 1) * HEAD_DIM]
        for hh in range(n_q):
            q_ref[hh * HEAD_DIM:(hh + 1) * HEAD_DIM, tok] = norm_rope(head(hh), gq_ref, tok).astype(q_ref.dtype)
        for hh in range(n_k):
            y = norm_rope(head(n_q + hh), gk_ref, tok)
            k_ref[tok, hh * HEAD_DIM:(hh + 1) * HEAD_DIM] = y.T.astype(k_ref.dtype)
        for hh in range(n_v):
            v_ref[hh * HEAD_DIM:(hh + 1) * HEAD_DIM, tok] = head(n_q + n_k + hh).astype(v_ref.dtype)

    _software_pipeline(h_ref.shape[0] // t_sub, matmul, epilogue)


def _proj_t(h, wt, gq_col, gk_col, cos_tt, sin_tt, b, seq, cast_jobs, tm=1024, t_sub=256):
    m, d = h.shape
    n = wt.shape[0]
    sb = seq // tm
    col = lambda g: jnp.broadcast_to(g[:, None], (HEAD_DIM, t_sub))
    lane_tile = lambda rows: pl.BlockSpec((None, rows, tm), lambda i: (i // sb, 0, i % sb))
    c_in, c_out, c_shapes = _cast_jobs(cast_jobs, m // tm, lambda i: i)
    return pl.pallas_call(
        functools.partial(_proj_t_kernel, t_sub=t_sub, n_cast=len(cast_jobs)),
        grid=(m // tm,),
        in_specs=[pl.BlockSpec((n, d), lambda i: (0, 0)),
                  pl.BlockSpec((tm, d), lambda i: (i, 0)),
                  pl.BlockSpec((HEAD_DIM, t_sub), lambda i: (0, 0)),
                  pl.BlockSpec((HEAD_DIM, t_sub), lambda i: (0, 0)),
                  pl.BlockSpec((HEAD_DIM, tm), lambda i: (0, i % sb)),
                  pl.BlockSpec((HEAD_DIM, tm), lambda i: (0, i % sb))] + c_in,
        out_specs=[lane_tile(B_QW),
                   pl.BlockSpec((tm, B_KVW), lambda i: (i, 0)),
                   lane_tile(B_KVW)] + c_out,
        out_shape=[jax.ShapeDtypeStruct((b, B_QW, seq), BF16),
                   jax.ShapeDtypeStruct((m, B_KVW), BF16),
                   jax.ShapeDtypeStruct((b, B_KVW, seq), BF16)] + c_shapes,
        compiler_params=_cparams("parallel", vmem_mb=48),
        name="proj_t",
    )(wt, h, col(gq_col), col(gk_col), cos_tt, sin_tt, *[src for src, _, _ in cast_jobs])


def _mem_kv_kernel(mem_ref, gm_ref, w_ref, gk_ref, k_ref, v_ref):
    x = mem_ref[...]
    ms = jnp.mean(x * x, axis=-1, keepdims=True)
    hm = (x * lax.rsqrt(ms + EPS) * gm_ref[...]).astype(BF16)
    kv = jnp.dot(hm, w_ref[...], preferred_element_type=F32)
    for hh in range(C_HEADS):
        sl = slice(hh * HEAD_DIM, (hh + 1) * HEAD_DIM)
        y = kv[:, sl]
        ms = jnp.mean(y * y, axis=-1, keepdims=True)
        k_ref[:, sl] = (y * lax.rsqrt(ms + EPS) * gk_ref[...]).astype(BF16)
    v_ref[...] = kv[:, C_W:].astype(BF16)


def _mem_kv(mem2d, g_mem, w_kv, g_kc, tm=256):
    m, d = mem2d.shape
    return pl.pallas_call(
        _mem_kv_kernel,
        grid=(m // tm,),
        in_specs=[pl.BlockSpec((tm, d), lambda i: (i, 0)),
                  pl.BlockSpec((1, d), lambda i: (0, 0)),
                  pl.BlockSpec((d, 2 * C_W), lambda i: (0, 0)),
                  pl.BlockSpec((1, HEAD_DIM), lambda i: (0, 0))],
        out_specs=[pl.BlockSpec((tm, C_W), lambda i: (i, 0))] * 2,
        out_shape=[jax.ShapeDtypeStruct((m, C_W), BF16)] * 2,
        compiler_params=_cparams("parallel"),
        name="mem_kv",
    )(mem2d, g_mem.reshape(1, d), w_kv, g_kc.reshape(1, HEAD_DIM))


def _t5_bucket(rel):
    nb = REL_BUCKETS // 2
    ret = np.where(rel > 0, nb, 0)
    n = np.abs(rel)
    max_exact = nb // 2
    large = max_exact + (np.log(np.maximum(n, 1).astype(np.float32) / np.float32(max_exact))
                         / np.float32(math.log(REL_MAX_DIST / max_exact))
                         * np.float32(nb - max_exact)).astype(np.int32)
    large = np.minimum(large, nb - 1)
    return ret + np.where(n < max_exact, n, large)


def _a_bucket_index():
    qi = np.arange(A_QROWS, dtype=np.int32)[:, None]
    kj = np.arange(A_KWIN, dtype=np.int32)[None, :]
    out = []
    for _, dil in DIL_PAIRS:
        for off in range(A_NOFF):
            rel = kj - qi - A_RADIUS * off
            out.append(np.where(np.abs(rel) <= A_RADIUS, _t5_bucket(rel * dil), -1))
    return np.stack(out).astype(np.int32)


def _a_bias_kernel(tab_ref, bucket_ref, o_ref):
    g = pl.program_id(0) // A_NOFF
    bk = bucket_ref[...]
    for hh in range(A_HEADS_PER_GROUP):
        acc = jnp.full(bk.shape, NEG, F32)
        for b in range(REL_BUCKETS):
            acc = jnp.where(bk == b, tab_ref[b, g * A_HEADS_PER_GROUP + hh], acc)
        o_ref[hh] = acc


def _a_bias(rel_bias):
    n = len(DIL_PAIRS) * A_NOFF
    return pl.pallas_call(
        _a_bias_kernel,
        grid=(n,),
        in_specs=[pl.BlockSpec(memory_space=pltpu.SMEM),
                  pl.BlockSpec((None, A_QROWS, A_KWIN), lambda i: (i, 0, 0))],
        out_specs=pl.BlockSpec((None, A_HEADS_PER_GROUP, A_QROWS, A_KWIN), lambda i: (i, 0, 0, 0)),
        out_shape=jax.ShapeDtypeStruct((n, A_HEADS_PER_GROUP, A_QROWS, A_KWIN), F32),
        compiler_params=_cparams("arbitrary"),
        name="a_bias",
    )(rel_bias, _a_bucket_index())


def _mixer_a_kernel(q_ref, k_ref, v_ref, bias_ref, oa_ref, stage, qstage, o_acc, lse_acc, *residue_kv, seq):
    t_rows = q_ref.shape[0]
    ti = pl.program_id(1)
    n_groups = len(DIL_PAIRS)

    def softmax(s):
        m = jnp.max(s, axis=-1, keepdims=True)
        p = jnp.exp(s - m)
        l = jnp.sum(p, axis=-1, keepdims=True)
        return p.astype(BF16), l, m + jnp.log(l)

    def run_group(gi, dil, kres, vres):
        sub_len = seq // dil
        lq = t_rows // dil
        gcols = lambda hh: slice(gi * A_OUT_W + hh * HEAD_DIM, gi * A_OUT_W + (hh + 1) * HEAD_DIM)
        if dil > 1:
            @pl.when(ti == 0)
            def _():
                for src, dst in ((k_ref, kres), (v_ref, vres)):
                    for hh in range(A_HEADS_PER_GROUP):
                        stage[...] = src[:, gcols(hh)].astype(F32)
                        for r in range(dil):
                            dst[r, :, hh * HEAD_DIM:(hh + 1) * HEAD_DIM] = (
                                stage[pl.ds(r, sub_len, stride=dil), :].astype(BF16))

            for hh in range(A_HEADS_PER_GROUP):
                qstage[hh] = q_ref[:, gcols(hh)].astype(F32)

        def scores(r, i, hh):
            q0 = ti * lq + i * A_QROWS
            ks = jnp.clip(q0 - A_RADIUS, 0, sub_len - A_KWIN)
            off = lax.shift_right_logical(q0 - ks, int(math.log2(A_RADIUS)))
            ks = pl.multiple_of(ks, A_RADIUS)
            if dil > 1:
                cols = slice(hh * HEAD_DIM, (hh + 1) * HEAD_DIM)
                rows = pl.ds(i * A_QROWS * dil + r, A_QROWS, stride=dil)
                q = qstage[hh, rows, :].astype(BF16)
                k = kres[r, pl.ds(ks, A_KWIN), cols]
                v = vres[r, pl.ds(ks, A_KWIN), cols]
            else:
                rows = pl.ds(i * A_QROWS, A_QROWS)
                q = q_ref[rows, gcols(hh)]
                k = k_ref[pl.ds(ks, A_KWIN), gcols(hh)]
                v = v_ref[pl.ds(ks, A_KWIN), gcols(hh)]
            s = lax.dot_general(q, k, _NT, preferred_element_type=F32) * SCALE + bias_ref[gi * A_NOFF + off, hh]
            return rows, s, v

        def fold(hh, rows, o, lse):
            lse = jnp.broadcast_to(lse, (A_QROWS, HEAD_DIM))
            if gi > 0:
                prev_o, prev_lse = o_acc[hh, rows, :], lse_acc[hh, rows, :]
                m = jnp.maximum(prev_lse, lse)
                w_prev, w_new = jnp.exp(prev_lse - m), jnp.exp(lse - m)
                den = w_prev + w_new
                o = (w_prev * prev_o + w_new * o) / den
                lse = m + jnp.log(den)
            o_acc[hh, rows, :] = o
            if gi + 1 < n_groups:
                lse_acc[hh, rows, :] = lse

        items = [(r, i, hh) for r in range(dil) for i in range(lq // A_QROWS) for hh in range(A_HEADS_PER_GROUP)]
        for b0 in range(0, len(items), A_BATCH):
            batch = items[b0:b0 + A_BATCH]
            staged = [scores(*it) for it in batch]
            probs = [softmax(s) for _, s, _ in staged]
            for (_, _, hh), (rows, _, v), (p, l, lse) in zip(batch, staged, probs):
                fold(hh, rows, jnp.dot(p, v, preferred_element_type=F32) / l, lse)

    strided = [gi for gi, (_, dil) in enumerate(DIL_PAIRS) if dil > 1]
    for gi, (_, dil) in enumerate(DIL_PAIRS):
        kres, vres = (residue_kv[2 * strided.index(gi):2 * strided.index(gi) + 2] if dil > 1 else (None, None))
        run_group(gi, dil, kres, vres)
    for hh in range(A_HEADS_PER_GROUP):
        oa_ref[:, hh * HEAD_DIM:(hh + 1) * HEAD_DIM] = o_acc[hh].astype(oa_ref.dtype)


def _mixer_a(qkn, va, bias_a, t_rows=2048):
    b, seq, _ = qkn.shape
    head_buf = lambda rows: pltpu.VMEM((A_HEADS_PER_GROUP, rows, HEAD_DIM), F32)
    scratch = [pltpu.VMEM((seq, HEAD_DIM), F32), head_buf(t_rows), head_buf(t_rows), head_buf(t_rows)]
    for _, dil in DIL_PAIRS:
        if dil > 1:
            scratch += [pltpu.VMEM((dil, seq // dil, A_OUT_W), BF16)] * 2
    return pl.pallas_call(
        functools.partial(_mixer_a_kernel, seq=seq),
        grid=(b, seq // t_rows),
        in_specs=[pl.BlockSpec((None, t_rows, A_W), lambda bi, ti: (bi, ti, 0)),
                  pl.BlockSpec((None, seq, A_W), lambda bi, ti: (bi, 0, 1)),
                  pl.BlockSpec((None, seq, A_W), lambda bi, ti: (bi, 0, 0)),
                  pl.BlockSpec(bias_a.shape, lambda bi, ti: (0, 0, 0, 0), pipeline_mode=pl.Buffered(1))],
        out_specs=pl.BlockSpec((None, t_rows, A_OUT_W), lambda bi, ti: (bi, ti, 0)),
        out_shape=jax.ShapeDtypeStruct((b, seq, A_OUT_W), BF16),
        scratch_shapes=scratch,
        compiler_params=_cparams("parallel", "arbitrary", vmem_mb=62),
        name="mixer_a",
    )(qkn, qkn, va, bias_a)


def _mixer_b_kernel(qt_ref, k_ref, vt_ref, *rest, tk, group, w, n_cast):
    o_ref, s_scr = rest[n_cast], rest[-1]
    _run_cast_jobs(rest[:n_cast], rest[n_cast + 1:-1])
    tq = qt_ref.shape[1]
    seq = k_ref.shape[0]
    n_chunks = seq // tk
    units = [(i, j) for i in range(group) for j in range(tq // w)]

    def pass_a(u, ci, m):
        i, j = units[u]
        qt = qt_ref[i * HEAD_DIM:(i + 1) * HEAD_DIM, j * w:(j + 1) * w]
        st = jnp.dot(k_ref[ci * tk:(ci + 1) * tk, :], qt, preferred_element_type=F32)
        s_scr[u % 2, ci * tk:(ci + 1) * tk, :] = st
        return jnp.maximum(m, jnp.max(st, axis=0, keepdims=True))

    def pass_b(u, ci, m, l, acc):
        pt = jnp.exp2(s_scr[u % 2, ci * tk:(ci + 1) * tk, :] - m)
        l = l + jnp.sum(pt, axis=0, keepdims=True)
        acc = acc + jnp.dot(vt_ref[:, ci * tk:(ci + 1) * tk], pt.astype(BF16), preferred_element_type=F32)
        return l, acc

    m_prev = None
    for s in range(len(units) + 1):
        m_cur = jnp.full((1, w), NEG, F32)
        l = jnp.zeros((1, w), F32)
        acc = jnp.zeros((HEAD_DIM, w), F32)
        for ci in range(n_chunks):
            if s < len(units):
                m_cur = pass_a(s, ci, m_cur)
            if s > 0:
                l, acc = pass_b(s - 1, ci, m_prev, l, acc)
        if s > 0:
            i, j = units[s - 1]
            o_ref[j * w:(j + 1) * w, i * HEAD_DIM:(i + 1) * HEAD_DIM] = (acc / l).T.astype(o_ref.dtype)
        m_prev = m_cur


def _mixer_b(qbt, kb, vbt, cast_jobs, tq=512, tk=512, w=256):
    b, seq, _ = kb.shape
    group = B_Q_HEADS // B_KV_HEADS
    gw = group * HEAD_DIM
    nq = seq // tq
    n_steps = b * B_KV_HEADS * nq
    c_in, c_out, c_shapes = _cast_jobs(cast_jobs, n_steps, lambda bi, kv, qi: (bi * B_KV_HEADS + kv) * nq + qi)
    return pl.pallas_call(
        functools.partial(_mixer_b_kernel, tk=tk, group=group, w=w, n_cast=len(cast_jobs)),
        grid=(b, B_KV_HEADS, nq),
        scratch_shapes=[pltpu.VMEM((2, seq, w), F32)],
        in_specs=[pl.BlockSpec((None, gw, tq), lambda bi, kv, qi: (bi, kv, qi)),
                  pl.BlockSpec((None, seq, HEAD_DIM), lambda bi, kv, qi: (bi, 0, kv)),
                  pl.BlockSpec((None, HEAD_DIM, seq), lambda bi, kv, qi: (bi, kv, 0))] + c_in,
        out_specs=[pl.BlockSpec((None, tq, gw), lambda bi, kv, qi: (bi, qi, kv))] + c_out,
        out_shape=[jax.ShapeDtypeStruct((b, seq, B_QW), BF16)] + c_shapes,
        compiler_params=_cparams("parallel", "parallel", "arbitrary"),
        name="mixer_b",
    )(qbt, kb, vbt, *[src for src, _, _ in cast_jobs])


def _mixer_c_kernel(q_ref, k_ref, v_ref, o_ref):
    for hh in range(C_HEADS):
        sl = slice(hh * HEAD_DIM, (hh + 1) * HEAD_DIM)
        s = lax.dot_general(q_ref[:, sl], k_ref[:, sl], _NT, preferred_element_type=F32) * SCALE
        m = jnp.max(s, axis=-1, keepdims=True)
        p = jnp.exp(s - m)
        l = jnp.sum(p, axis=-1, keepdims=True)
        o = jnp.dot(p.astype(BF16), v_ref[:, sl], preferred_element_type=F32) / l
        o_ref[:, sl] = o.astype(o_ref.dtype)


def _mixer_c(qkn, kc, vc, tq=512):
    b, seq, _ = qkn.shape
    n_mem = kc.shape[1]
    qc_blk = (2 * A_W) // C_W
    return pl.pallas_call(
        _mixer_c_kernel,
        grid=(b, seq // tq),
        in_specs=[pl.BlockSpec((None, tq, C_W), lambda bi, qi: (bi, qi, qc_blk)),
                  pl.BlockSpec((None, n_mem, C_W), lambda bi, qi: (bi, 0, 0)),
                  pl.BlockSpec((None, n_mem, C_W), lambda bi, qi: (bi, 0, 0))],
        out_specs=pl.BlockSpec((None, tq, C_W), lambda bi, qi: (bi, qi, 0)),
        out_shape=jax.ShapeDtypeStruct((b, seq, C_W), BF16),
        compiler_params=_cparams("parallel", "arbitrary"),
        name="mixer_c",
    )(qkn, kc, vc)


def _merge_kernel(h_ref, wg0_ref, wg1_ref, wg2_ref, oa_ref, ob_ref, oc_ref, wa_ref, wb_ref, wc_ref, out_ref):
    h = h_ref[...]
    ga = jax.nn.sigmoid(jnp.dot(h, wg0_ref[...], preferred_element_type=F32))
    merged = ga * jnp.dot(oa_ref[...], wa_ref[...], preferred_element_type=F32)
    gb = jax.nn.sigmoid(jnp.dot(h, wg1_ref[...], preferred_element_type=F32))
    merged += gb * jnp.dot(ob_ref[...], wb_ref[...], preferred_element_type=F32)
    gc = jax.nn.sigmoid(jnp.dot(h, wg2_ref[...], preferred_element_type=F32))
    merged += gc * jnp.dot(oc_ref[...], wc_ref[...], preferred_element_type=F32)
    out_ref[...] = merged.astype(out_ref.dtype)


def _merge(h, w_gates, oa, ob, oc, wa, wb, wc, tm=1024, tn=512):
    m, d = h.shape
    nj = d // tn
    row = lambda w: pl.BlockSpec((tm, w), lambda i, j: (i, 0))
    col = lambda k: pl.BlockSpec((k, tn), lambda i, j: (0, j))
    return pl.pallas_call(
        _merge_kernel,
        grid=(m // tm, nj),
        in_specs=[row(d), col(d), col(d), col(d), row(A_OUT_W), row(B_QW), row(C_W),
                  col(A_OUT_W), col(B_QW), col(C_W)],
        out_specs=pl.BlockSpec((tm, tn), lambda i, j: (i, j)),
        out_shape=jax.ShapeDtypeStruct((m, d), BF16),
        compiler_params=_cparams("parallel", "arbitrary", vmem_mb=48),
        name="merge",
    )(h, *w_gates, oa, ob, oc, wa, wb, wc)


def _out_proj_kernel(mg_ref, w_ref, x_ref, g_ref, x1_ref, h2_ref, *, r_sub):
    def matmul(c):
        return jnp.dot(mg_ref[c * r_sub:(c + 1) * r_sub, :], w_ref[...], preferred_element_type=F32)

    def epilogue(c, acc):
        rows = slice(c * r_sub, (c + 1) * r_sub)
        x1 = x_ref[rows, :] + acc
        x1_ref[rows, :] = x1
        ms = jnp.mean(x1 * x1, axis=-1, keepdims=True)
        h2_ref[rows, :] = (x1 * lax.rsqrt(ms + EPS) * g_ref[...]).astype(h2_ref.dtype)

    _software_pipeline(mg_ref.shape[0] // r_sub, matmul, epilogue)


def _out_proj(merged, w_o, x2d, g_ffn, tm=512, r_sub=512):
    m, d = x2d.shape
    return pl.pallas_call(
        functools.partial(_out_proj_kernel, r_sub=r_sub),
        grid=(m // tm,),
        in_specs=[pl.BlockSpec((tm, d), lambda i: (i, 0)),
                  pl.BlockSpec((d, d), lambda i: (0, 0)),
                  pl.BlockSpec((tm, d), lambda i: (i, 0)),
                  pl.BlockSpec((1, d), lambda i: (0, 0))],
        out_specs=[pl.BlockSpec((tm, d), lambda i: (i, 0))] * 2,
        out_shape=[jax.ShapeDtypeStruct((m, d), F32), jax.ShapeDtypeStruct((m, d), BF16)],
        compiler_params=_cparams("parallel"),
        name="out_proj",
    )(merged, w_o, x2d, g_ffn.reshape(1, d))


def _ffn_kernel(h_ref, wa_ref, wb_ref, wo_ref, x1_ref, out_ref, *, n_slabs):
    f = pl.program_id(1)
    slab_w = x1_ref.shape[1]

    @pl.when(f == 0)
    def _():
        out_ref[...] = jnp.zeros_like(out_ref)

    for s in range(n_slabs):
        @pl.when(f == s)
        def _():
            out_ref[:, s * slab_w:(s + 1) * slab_w] += x1_ref[...]

    h = h_ref[...]
    a = jnp.dot(h, wa_ref[...], preferred_element_type=F32)
    b = jnp.dot(h, wb_ref[...], preferred_element_type=F32)
    act = (a * jax.nn.sigmoid(a) * b).astype(BF16)
    out_ref[...] += jnp.dot(act, wo_ref[...], preferred_element_type=F32)


def _ffn(h2, w_a, w_b, w_out, x1, tm=1024, tf=512, slab_w=256):
    m, d = h2.shape
    d_ff = w_out.shape[0]
    nf = d_ff // tf
    n_slabs = d // slab_w
    assert n_slabs <= nf
    return pl.pallas_call(
        functools.partial(_ffn_kernel, n_slabs=n_slabs),
        grid=(m // tm, nf),
        in_specs=[pl.BlockSpec((tm, d), lambda i, f: (i, 0)),
                  pl.BlockSpec((d, tf), lambda i, f: (0, f)),
                  pl.BlockSpec((d, tf), lambda i, f: (0, f)),
                  pl.BlockSpec((tf, d), lambda i, f: (f, 0)),
                  pl.BlockSpec((tm, slab_w), lambda i, f: (i, jnp.minimum(f, n_slabs - 1)))],
        out_specs=pl.BlockSpec((tm, d), lambda i, f: (i, 0)),
        out_shape=jax.ShapeDtypeStruct((m, d), F32),
        compiler_params=_cparams("parallel", "arbitrary", vmem_mb=48),
        name="ffn",
    )(h2, w_a, w_b, w_out, x1)


def _deinterleave_cols(w, heads):
    d = w.shape[0]
    return w.reshape(d, heads, HEAD_DIM // 2, 2).transpose(0, 1, 3, 2).reshape(d, heads * HEAD_DIM)


def _rope_tables(seq):
    rows = seq // GRID_W
    r = np.repeat(np.arange(rows), GRID_W).astype(np.float64)
    c = np.tile(np.arange(GRID_W), rows).astype(np.float64)
    nf = HEAD_DIM // 4
    inv = ROPE_THETA ** (-np.arange(nf, dtype=np.float64) / nf)
    ang = np.concatenate([r[:, None] * inv, c[:, None] * inv], axis=-1)
    cos, sin = np.cos(ang).T, np.sin(ang).T
    return (np.concatenate([cos, cos], axis=0).astype(np.float32),
            np.concatenate([-sin, sin], axis=0).astype(np.float32))


def _layer(x2d, mem2d, bias_a, cos_t, sin_t, b, seq, g_mix, w_in, g_qa, g_ka, g_qb, g_kb, g_mem, w_mem_kv,
           g_qc, g_kc, w_br_a, w_br_b, w_br_c, w_o, g_ffn, w_ffn_in, w_ffn_out):
    d = x2d.shape[1]
    o_qa, o_ka, o_va, o_qb, o_kb, o_vb, o_qc, o_gt = np.cumsum(
        (0, A_W, A_W, A_W, B_QW, B_KVW, B_KVW, C_W))
    seg = lambda lo, hi: w_in[:, lo:hi]
    d_ff = w_ffn_out.shape[0]
    w_all = _cast_cols(w_in, int(o_gt))
    wt_b = jnp.concatenate([_deinterleave_cols(seg(o_qb, o_kb), B_Q_HEADS),
                            _deinterleave_cols(seg(o_kb, o_vb), B_KV_HEADS),
                            seg(o_vb, o_qc)], axis=1).T.astype(BF16)
    g_norm = jnp.concatenate([jnp.tile(g_qa, A_HEADS), jnp.tile(g_ka, A_HEADS), jnp.tile(g_qc, C_HEADS)])
    deint = lambda g: g.reshape(HEAD_DIM // 2, 2).T.reshape(HEAD_DIM)

    n_qk_tiles = (2 * A_W) // C_W
    qkn, h = _proj_norm(x2d, g_mix, w_all, g_norm,
                        lambda j: jnp.where(j < n_qk_tiles, j, int(o_qc) // C_W), tn=C_W)
    qkn = qkn.reshape(b, seq, -1)
    va = _proj_plain(h, w_all, A_W, int(o_va) // A_W).reshape(b, seq, -1)
    late_weights = [(w_ffn_out, 0, d), (w_o, 0, d), (w_br_a, 0, d), (w_br_b, 0, d), (w_br_c, 0, d)]
    qbt, kb, vbt, w_out_b, w_o_b, wa_b, wb_b, wc_b = _proj_t(
        h, wt_b, deint(g_qb) * (SCALE * LOG2E), deint(g_kb), cos_t, sin_t, b, seq, late_weights)
    kb = kb.reshape(b, seq, B_KVW)

    oa = _mixer_a(qkn, va, bias_a).reshape(b * seq, A_OUT_W)
    gate_jobs = [(w_in, int(o_gt) // d + br, d) for br in range(N_BRANCH)]
    ob, w_ffa, w_ffb, *w_gates = _mixer_b(qbt, kb, vbt, [(w_ffn_in, 0, d_ff), (w_ffn_in, 1, d_ff)] + gate_jobs)
    ob = ob.reshape(b * seq, B_QW)

    kc, vc = _mem_kv(mem2d, g_mem, w_mem_kv.astype(BF16), g_kc)
    n_mem = mem2d.shape[0] // b
    oc = _mixer_c(qkn, kc.reshape(b, n_mem, C_W), vc.reshape(b, n_mem, C_W)).reshape(b * seq, C_W)

    merged = _merge(h, w_gates, oa, ob, oc, wa_b, wb_b, wc_b)
    x1, h2 = _out_proj(merged, w_o_b, x2d, g_ffn)
    return _ffn(h2, w_ffa, w_ffb, w_out_b, x1)


def kernel(x, mem, rel_bias, g_mix, w_in, g_qa, g_ka, g_qb, g_kb, g_mem, w_mem_kv, g_qc, g_kc,
           w_br_a, w_br_b, w_br_c, w_o, g_ffn, w_ffn_in, w_ffn_out):
    b, seq, d = x.shape
    depth = w_in.shape[0]
    cos_t, sin_t = _rope_tables(seq)
    bias_a = _a_bias(rel_bias)
    x2d = x.reshape(b * seq, d)
    mem2d = mem.reshape(-1, d)
    for layer in range(depth):
        x2d = _layer(x2d, mem2d, bias_a, cos_t, sin_t, b, seq,
                     g_mix[layer], w_in[layer], g_qa[layer], g_ka[layer], g_qb[layer], g_kb[layer],
                     g_mem[layer], w_mem_kv[layer], g_qc[layer], g_kc[layer],
                     w_br_a[layer], w_br_b[layer], w_br_c[layer], w_o[layer], g_ffn[layer],
                     w_ffn_in[layer], w_ffn_out[layer])
    return x2d.reshape(b, seq, d)
```

```python
import functools
import math

import numpy as np
import jax
import jax.numpy as jnp
from jax import lax
from jax.experimental import pallas as pl
from jax.experimental.pallas import tpu as pltpu

HEAD_DIM = 128
GRID_W = 64
DIL_PAIRS = ((128, 1), (512, 4), (2048, 16))
A_HEADS_PER_GROUP = 2
A_HEADS = A_HEADS_PER_GROUP * len(DIL_PAIRS)
B_Q_HEADS = 6
B_KV_HEADS = 2
ROPE_THETA = 10000.0
C_HEADS = 4
N_BRANCH = 3
REL_BUCKETS = 32
REL_MAX_DIST = 1024
EPS = 1e-6
NEG = -1e30

A_W = A_HEADS * HEAD_DIM
A_OUT_W = A_HEADS_PER_GROUP * HEAD_DIM
B_QW = B_Q_HEADS * HEAD_DIM
B_KVW = B_KV_HEADS * HEAD_DIM
C_W = C_HEADS * HEAD_DIM

SCALE = 1.0 / math.sqrt(HEAD_DIM)
LOG2E = math.log2(math.e)

A_QROWS = 128
A_KWIN = 256
A_RADIUS = 64
A_NOFF = 3
A_BATCH = 8


BF16 = jnp.bfloat16
F32 = jnp.float32

_NT = (((1,), (1,)), ((), ()))


def _software_pipeline(n_chunks, matmul, epilogue):
    acc = matmul(0)
    for c in range(n_chunks):
        nxt = matmul(c + 1) if c + 1 < n_chunks else None
        epilogue(c, acc)
        acc = nxt


def _cast_jobs(jobs, n_steps, step_index):
    in_specs, out_specs, out_shapes = [], [], []
    for src, col_block, width in jobs:
        rows = src.shape[0]
        rt = rows // n_steps
        assert rt * n_steps == rows and rt % 16 == 0 and src.shape[1] % width == 0
        in_specs.append(pl.BlockSpec((rt, width), lambda *g, cb=col_block: (step_index(*g), cb)))
        out_specs.append(pl.BlockSpec((rt, width), lambda *g: (step_index(*g), 0)))
        out_shapes.append(jax.ShapeDtypeStruct((rows, width), BF16))
    return in_specs, out_specs, out_shapes


def _run_cast_jobs(src_refs, dst_refs):
    for src, dst in zip(src_refs, dst_refs):
        dst[...] = src[...].astype(dst.dtype)


def _cast_kernel(src_ref, dst_ref):
    dst_ref[...] = src_ref[...].astype(dst_ref.dtype)


def _cast_cols(w, n_cols, row_tile=256):
    rows = w.shape[0]
    assert rows % row_tile == 0
    return pl.pallas_call(
        _cast_kernel,
        grid=(rows // row_tile,),
        in_specs=[pl.BlockSpec((row_tile, n_cols), lambda i: (i, 0))],
        out_specs=pl.BlockSpec((row_tile, n_cols), lambda i: (i, 0)),
        out_shape=jax.ShapeDtypeStruct((rows, n_cols), BF16),
        compiler_params=_cparams("parallel"),
        name="cast_head_weights",
    )(w)


def _cparams(*sem, vmem_mb=None):
    limit = None if vmem_mb is None else vmem_mb * 1024 * 1024
    return pltpu.CompilerParams(dimension_semantics=sem, vmem_limit_bytes=limit)


def _proj_norm_kernel(x_hbm, gm_ref, w_ref, g_ref, o_ref, h_ref, xbuf, sem, *, r_sub, n_j):
    i, j = pl.program_id(0), pl.program_id(1)
    n_i = pl.num_programs(0)
    tm = h_ref.shape[0]
    piece = tm // n_j

    def x_copy(tile, p):
        slot = tile % 2
        rows = pl.ds(tile * tm + p * piece, piece)
        return pltpu.make_async_copy(x_hbm.at[rows, :], xbuf.at[slot, pl.ds(p * piece, piece), :], sem.at[slot, p])

    @pl.when((i == 0) & (j == 0))
    def _():
        for p in range(n_j):
            x_copy(0, p).start()

    @pl.when(i + 1 < n_i)
    def _():
        for p in range(n_j):
            @pl.when(j == p)
            def _():
                x_copy(i + 1, p).start()

    @pl.when(j == 0)
    def _():
        for p in range(n_j):
            x_copy(i, p).wait()
        x = xbuf[i % 2]
        ms = jnp.mean(x * x, axis=-1, keepdims=True)
        h_ref[...] = (x * lax.rsqrt(ms + EPS) * gm_ref[...]).astype(h_ref.dtype)

    def matmul(c):
        return jnp.dot(h_ref[c * r_sub:(c + 1) * r_sub, :], w_ref[...], preferred_element_type=F32)

    def epilogue(c, acc):
        for hh in range(o_ref.shape[1] // HEAD_DIM):
            sl = slice(hh * HEAD_DIM, (hh + 1) * HEAD_DIM)
            y = acc[:, sl]
            ms = jnp.mean(y * y, axis=-1, keepdims=True)
            o_ref[c * r_sub:(c + 1) * r_sub, sl] = (y * lax.rsqrt(ms + EPS) * g_ref[:, sl]).astype(o_ref.dtype)

    _software_pipeline(h_ref.shape[0] // r_sub, matmul, epilogue)


def _proj_norm(x2d, g_mix, w, gains, col_block, tm=1024, tn=512, r_sub=256):
    m, d = x2d.shape
    n = gains.shape[0]
    n_j = n // tn
    assert n % tn == 0 and m % tm == 0 and tm % (8 * n_j) == 0
    return pl.pallas_call(
        functools.partial(_proj_norm_kernel, r_sub=r_sub, n_j=n_j),
        grid=(m // tm, n_j),
        in_specs=[pl.BlockSpec(memory_space=pl.ANY),
                  pl.BlockSpec((1, d), lambda i, j: (0, 0)),
                  pl.BlockSpec((d, tn), lambda i, j: (0, col_block(j))),
                  pl.BlockSpec((1, tn), lambda i, j: (0, j))],
        out_specs=[pl.BlockSpec((tm, tn), lambda i, j: (i, j)),
                   pl.BlockSpec((tm, d), lambda i, j: (i, 0))],
        out_shape=[jax.ShapeDtypeStruct((m, n), BF16), jax.ShapeDtypeStruct((m, d), BF16)],
        scratch_shapes=[pltpu.VMEM((2, tm, d), F32), pltpu.SemaphoreType.DMA((2, n_j))],
        compiler_params=_cparams("arbitrary", "arbitrary", vmem_mb=48),
        name="proj_norm",
    )(x2d, g_mix.reshape(1, d), w, gains.reshape(1, n))


def _proj_plain_kernel(h_ref, w_ref, o_ref):
    o_ref[...] = jnp.dot(h_ref[...], w_ref[...], preferred_element_type=F32).astype(o_ref.dtype)


def _proj_plain(h, w, n, col_block, tm=1024):
    m, d = h.shape
    assert m % tm == 0
    return pl.pallas_call(
        _proj_plain_kernel,
        grid=(m // tm,),
        in_specs=[pl.BlockSpec((tm, d), lambda i: (i, 0)),
                  pl.BlockSpec((d, n), lambda i: (0, col_block))],
        out_specs=pl.BlockSpec((tm, n), lambda i: (i, 0)),
        out_shape=jax.ShapeDtypeStruct((m, n), BF16),
        compiler_params=_cparams("parallel"),
        name="proj_plain",
    )(h, w)


def _proj_t_kernel(wt_ref, h_ref, gq_ref, gk_ref, cos_ref, sin_ref, *rest, t_sub, n_cast):
    q_ref, k_ref, v_ref = rest[n_cast:n_cast + 3]
    _run_cast_jobs(rest[:n_cast], rest[n_cast + 3:])
    n_q = q_ref.shape[0] // HEAD_DIM
    n_k = k_ref.shape[1] // HEAD_DIM
    n_v = v_ref.shape[0] // HEAD_DIM
    half = HEAD_DIM // 2

    def norm_rope(y, g_ref, tok):
        ms = jnp.mean(y * y, axis=0, keepdims=True)
        y = y * lax.rsqrt(ms + EPS) * g_ref[...]
        partner = jnp.concatenate([y[half:], y[:half]], axis=0)
        return y * cos_ref[:, tok] + partner * sin_ref[:, tok]

    def matmul(c):
        return lax.dot_general(wt_ref[...], h_ref[c * t_sub:(c + 1) * t_sub, :], _NT,
                               preferred_element_type=F32)

    def epilogue(c, yt):
        tok = slice(c * t_sub, (c + 1) * t_sub)
        head = lambda hh: yt[hh * HEAD_DIM:(hh + 1) * HEAD_DIM]
        for hh in range(n_q):
            q_ref[hh * HEAD_DIM:(hh + 1) * HEAD_DIM, tok] = norm_rope(head(hh), gq_ref, tok).astype(q_ref.dtype)
        for hh in range(n_k):
            y = norm_rope(head(n_q + hh), gk_ref, tok)
            k_ref[tok, hh * HEAD_DIM:(hh + 1) * HEAD_DIM] = y.T.astype(k_ref.dtype)
        for hh in range(n_v):
            v_ref[hh * HEAD_DIM:(hh + 1) * HEAD_DIM, tok] = head(n_q + n_k + hh).astype(v_ref.dtype)

    _software_pipeline(h_ref.shape[0] // t_sub, matmul, epilogue)


def _proj_t(h, wt, gq_col, gk_col, cos_tt, sin_tt, b, seq, cast_jobs, tm=1024, t_sub=256):
    m, d = h.shape
    n = wt.shape[0]
    sb = seq // tm
    col = lambda g: jnp.broadcast_to(g[:, None], (HEAD_DIM, t_sub))
    lane_tile = lambda rows: pl.BlockSpec((None, rows, tm), lambda i: (i // sb, 0, i % sb))
    c_in, c_out, c_shapes = _cast_jobs(cast_jobs, m // tm, lambda i: i)
    return pl.pallas_call(
        functools.partial(_proj_t_kernel, t_sub=t_sub, n_cast=len(cast_jobs)),
        grid=(m // tm,),
        in_specs=[pl.BlockSpec((n, d), lambda i: (0, 0)),
                  pl.BlockSpec((tm, d), lambda i: (i, 0)),
                  pl.BlockSpec((HEAD_DIM, t_sub), lambda i: (0, 0)),
                  pl.BlockSpec((HEAD_DIM, t_sub), lambda i: (0, 0)),
                  pl.BlockSpec((HEAD_DIM, tm), lambda i: (0, i % sb)),
                  pl.BlockSpec((HEAD_DIM, tm), lambda i: (0, i % sb))] + c_in,
        out_specs=[lane_tile(B_QW),
                   pl.BlockSpec((tm, B_KVW), lambda i: (i, 0)),
                   lane_tile(B_KVW)] + c_out,
        out_shape=[jax.ShapeDtypeStruct((b, B_QW, seq), BF16),
                   jax.ShapeDtypeStruct((m, B_KVW), BF16),
                   jax.ShapeDtypeStruct((b, B_KVW, seq), BF16)] + c_shapes,
        compiler_params=_cparams("parallel", vmem_mb=48),
        name="proj_t",
    )(wt, h, col(gq_col), col(gk_col), cos_tt, sin_tt, *[src for src, _, _ in cast_jobs])


def _mem_kv_kernel(mem_ref, gm_ref, w_ref, gk_ref, k_ref, v_ref):
    x = mem_ref[...]
    ms = jnp.mean(x * x, axis=-1, keepdims=True)
    hm = (x * lax.rsqrt(ms + EPS) * gm_ref[...]).astype(BF16)
    kv = jnp.dot(hm, w_ref[...], preferred_element_type=F32)
    for hh in range(C_HEADS):
        sl = slice(hh * HEAD_DIM, (hh + 1) * HEAD_DIM)
        y = kv[:, sl]
        ms = jnp.mean(y * y, axis=-1, keepdims=True)
        k_ref[:, sl] = (y * lax.rsqrt(ms + EPS) * gk_ref[...]).astype(BF16)
    v_ref[...] = kv[:, C_W:].astype(BF16)


def _mem_kv(mem2d, g_mem, w_kv, g_kc, tm=256):
    m, d = mem2d.shape
    return pl.pallas_call(
        _mem_kv_kernel,
        grid=(m // tm,),
        in_specs=[pl.BlockSpec((tm, d), lambda i: (i, 0)),
                  pl.BlockSpec((1, d), lambda i: (0, 0)),
                  pl.BlockSpec((d, 2 * C_W), lambda i: (0, 0)),
                  pl.BlockSpec((1, HEAD_DIM), lambda i: (0, 0))],
        out_specs=[pl.BlockSpec((tm, C_W), lambda i: (i, 0))] * 2,
        out_shape=[jax.ShapeDtypeStruct((m, C_W), BF16)] * 2,
        compiler_params=_cparams("parallel"),
        name="mem_kv",
    )(mem2d, g_mem.reshape(1, d), w_kv, g_kc.reshape(1, HEAD_DIM))


def _t5_bucket(rel):
    nb = REL_BUCKETS // 2
    ret = np.where(rel > 0, nb, 0)
    n = np.abs(rel)
    max_exact = nb // 2
    large = max_exact + (np.log(np.maximum(n, 1).astype(np.float32) / np.float32(max_exact))
                         / np.float32(math.log(REL_MAX_DIST / max_exact))
                         * np.float32(nb - max_exact)).astype(np.int32)
    large = np.minimum(large, nb - 1)
    return ret + np.where(n < max_exact, n, large)


def _a_bucket_index():
    qi = np.arange(A_QROWS, dtype=np.int32)[:, None]
    kj = np.arange(A_KWIN, dtype=np.int32)[None, :]
    out = []
    for _, dil in DIL_PAIRS:
        for off in range(A_NOFF):
            rel = kj - qi - A_RADIUS * off
            out.append(np.where(np.abs(rel) <= A_RADIUS, _t5_bucket(rel * dil), -1))
    return np.stack(out).astype(np.int32)


def _a_bias_kernel(tab_ref, bucket_ref, o_ref):
    g = pl.program_id(0) // A_NOFF
    bk = bucket_ref[...]
    for hh in range(A_HEADS_PER_GROUP):
        acc = jnp.full(bk.shape, NEG, F32)
        for b in range(REL_BUCKETS):
            acc = jnp.where(bk == b, tab_ref[b, g * A_HEADS_PER_GROUP + hh], acc)
        o_ref[hh] = acc


def _a_bias(rel_bias):
    n = len(DIL_PAIRS) * A_NOFF
    return pl.pallas_call(
        _a_bias_kernel,
        grid=(n,),
        in_specs=[pl.BlockSpec(memory_space=pltpu.SMEM),
                  pl.BlockSpec((None, A_QROWS, A_KWIN), lambda i: (i, 0, 0))],
        out_specs=pl.BlockSpec((None, A_HEADS_PER_GROUP, A_QROWS, A_KWIN), lambda i: (i, 0, 0, 0)),
        out_shape=jax.ShapeDtypeStruct((n, A_HEADS_PER_GROUP, A_QROWS, A_KWIN), F32),
        compiler_params=_cparams("arbitrary"),
        name="a_bias",
    )(rel_bias, _a_bucket_index())


def _mixer_a_kernel(q_ref, k_ref, v_ref, bias_ref, oa_ref, stage, qstage, o_acc, lse_acc, *residue_kv, seq):
    t_rows = q_ref.shape[0]
    ti = pl.program_id(1)
    n_groups = len(DIL_PAIRS)

    def softmax(s):
        m = jnp.max(s, axis=-1, keepdims=True)
        p = jnp.exp(s - m)
        l = jnp.sum(p, axis=-1, keepdims=True)
        return p.astype(BF16), l, m + jnp.log(l)

    def run_group(gi, dil, kres, vres):
        sub_len = seq // dil
        lq = t_rows // dil
        gcols = lambda hh: slice(gi * A_OUT_W + hh * HEAD_DIM, gi * A_OUT_W + (hh + 1) * HEAD_DIM)
        if dil > 1:
            @pl.when(ti == 0)
            def _():
                for src, dst in ((k_ref, kres), (v_ref, vres)):
                    for hh in range(A_HEADS_PER_GROUP):
                        stage[...] = src[:, gcols(hh)].astype(F32)
                        for r in range(dil):
                            dst[r, :, hh * HEAD_DIM:(hh + 1) * HEAD_DIM] = (
                                stage[pl.ds(r, sub_len, stride=dil), :].astype(BF16))

            for hh in range(A_HEADS_PER_GROUP):
                qstage[hh] = q_ref[:, gcols(hh)].astype(F32)

        def scores(r, i, hh):
            q0 = ti * lq + i * A_QROWS
            ks = jnp.clip(q0 - A_RADIUS, 0, sub_len - A_KWIN)
            off = lax.shift_right_logical(q0 - ks, int(math.log2(A_RADIUS)))
            ks = pl.multiple_of(ks, A_RADIUS)
            if dil > 1:
                cols = slice(hh * HEAD_DIM, (hh + 1) * HEAD_DIM)
                rows = pl.ds(i * A_QROWS * dil + r, A_QROWS, stride=dil)
                q = qstage[hh, rows, :].astype(BF16)
                k = kres[r, pl.ds(ks, A_KWIN), cols]
                v = vres[r, pl.ds(ks, A_KWIN), cols]
            else:
                rows = pl.ds(i * A_QROWS, A_QROWS)
                q = q_ref[rows, gcols(hh)]
                k = k_ref[pl.ds(ks, A_KWIN), gcols(hh)]
                v = v_ref[pl.ds(ks, A_KWIN), gcols(hh)]
            s = lax.dot_general(q, k, _NT, preferred_element_type=F32) * SCALE + bias_ref[gi * A_NOFF + off, hh]
            return rows, s, v

        def fold(hh, rows, o, lse):
            lse = jnp.broadcast_to(lse, (A_QROWS, HEAD_DIM))
            if gi > 0:
                prev_o, prev_lse = o_acc[hh, rows, :], lse_acc[hh, rows, :]
                m = jnp.maximum(prev_lse, lse)
                w_prev, w_new = jnp.exp(prev_lse - m), jnp.exp(lse - m)
                den = w_prev + w_new
                o = (w_prev * prev_o + w_new * o) / den
                lse = m + jnp.log(den)
            o_acc[hh, rows, :] = o
            if gi + 1 < n_groups:
                lse_acc[hh, rows, :] = lse

        items = [(r, i, hh) for r in range(dil) for i in range(lq // A_QROWS) for hh in range(A_HEADS_PER_GROUP)]
        for b0 in range(0, len(items), A_BATCH):
            batch = items[b0:b0 + A_BATCH]
            staged = [scores(*it) for it in batch]
            probs = [softmax(s) for _, s, _ in staged]
            for (_, _, hh), (rows, _, v), (p, l, lse) in zip(batch, staged, probs):
                fold(hh, rows, jnp.dot(p, v, preferred_element_type=F32) / l, lse)

    strided = [gi for gi, (_, dil) in enumerate(DIL_PAIRS) if dil > 1]
    for gi, (_, dil) in enumerate(DIL_PAIRS):
        kres, vres = (residue_kv[2 * strided.index(gi):2 * strided.index(gi) + 2] if dil > 1 else (None, None))
        run_group(gi, dil, kres, vres)
    for hh in range(A_HEADS_PER_GROUP):
        oa_ref[:, hh * HEAD_DIM:(hh + 1) * HEAD_DIM] = o_acc[hh].astype(oa_ref.dtype)


def _mixer_a(qkn, va, bias_a, t_rows=2048):
    b, seq, _ = qkn.shape
    head_buf = lambda rows: pltpu.VMEM((A_HEADS_PER_GROUP, rows, HEAD_DIM), F32)
    scratch = [pltpu.VMEM((seq, HEAD_DIM), F32), head_buf(t_rows), head_buf(t_rows), head_buf(t_rows)]
    for _, dil in DIL_PAIRS:
        if dil > 1:
            scratch += [pltpu.VMEM((dil, seq // dil, A_OUT_W), BF16)] * 2
    return pl.pallas_call(
        functools.partial(_mixer_a_kernel, seq=seq),
        grid=(b, seq // t_rows),
        in_specs=[pl.BlockSpec((None, t_rows, A_W), lambda bi, ti: (bi, ti, 0)),
                  pl.BlockSpec((None, seq, A_W), lambda bi, ti: (bi, 0, 1)),
                  pl.BlockSpec((None, seq, A_W), lambda bi, ti: (bi, 0, 0)),
                  pl.BlockSpec(bias_a.shape, lambda bi, ti: (0, 0, 0, 0), pipeline_mode=pl.Buffered(1))],
        out_specs=pl.BlockSpec((None, t_rows, A_OUT_W), lambda bi, ti: (bi, ti, 0)),
        out_shape=jax.ShapeDtypeStruct((b, seq, A_OUT_W), BF16),
        scratch_shapes=scratch,
        compiler_params=_cparams("parallel", "arbitrary", vmem_mb=62),
        name="mixer_a",
    )(qkn, qkn, va, bias_a)


def _mixer_b_kernel(qt_ref, k_ref, vt_ref, *rest, tk, group, w, n_cast):
    o_ref, s_scr = rest[n_cast], rest[-1]
    _run_cast_jobs(rest[:n_cast], rest[n_cast + 1:-1])
    tq = qt_ref.shape[1]
    seq = k_ref.shape[0]
    n_chunks = seq // tk
    units = [(i, j) for i in range(group) for j in range(tq // w)]

    def pass_a(u, ci, m):
        i, j = units[u]
        qt = qt_ref[i * HEAD_DIM:(i + 1) * HEAD_DIM, j * w:(j + 1) * w]
        st = jnp.dot(k_ref[ci * tk:(ci + 1) * tk, :], qt, preferred_element_type=F32)
        s_scr[u % 2, ci * tk:(ci + 1) * tk, :] = st
        return jnp.maximum(m, jnp.max(st, axis=0, keepdims=True))

    def pass_b(u, ci, m, l, acc):
        pt = jnp.exp2(s_scr[u % 2, ci * tk:(ci + 1) * tk, :] - m)
        l = l + jnp.sum(pt, axis=0, keepdims=True)
        acc = acc + jnp.dot(vt_ref[:, ci * tk:(ci + 1) * tk], pt.astype(BF16), preferred_element_type=F32)
        return l, acc

    m_prev = None
    for s in range(len(units) + 1):
        m_cur = jnp.full((1, w), NEG, F32)
        l = jnp.zeros((1, w), F32)
        acc = jnp.zeros((HEAD_DIM, w), F32)
        for ci in range(n_chunks):
            if s < len(units):
                m_cur = pass_a(s, ci, m_cur)
            if s > 0:
                l, acc = pass_b(s - 1, ci, m_prev, l, acc)
        if s > 0:
            i, j = units[s - 1]
            o_ref[j * w:(j + 1) * w, i * HEAD_DIM:(i + 1) * HEAD_DIM] = (acc / l).T.astype(o_ref.dtype)
        m_prev = m_cur


def _mixer_b(qbt, kb, vbt, cast_jobs, tq=512, tk=512, w=256):
    b, seq, _ = kb.shape
    group = B_Q_HEADS // B_KV_HEADS
    gw = group * HEAD_DIM
    nq = seq // tq
    n_steps = b * B_KV_HEADS * nq
    c_in, c_out, c_shapes = _cast_jobs(cast_jobs, n_steps, lambda bi, kv, qi: (bi * B_KV_HEADS + kv) * nq + qi)
    return pl.pallas_call(
        functools.partial(_mixer_b_kernel, tk=tk, group=group, w=w, n_cast=len(cast_jobs)),
        grid=(b, B_KV_HEADS, nq),
        scratch_shapes=[pltpu.VMEM((2, seq, w), F32)],
        in_specs=[pl.BlockSpec((None, gw, tq), lambda bi, kv, qi: (bi, kv, qi)),
                  pl.BlockSpec((None, seq, HEAD_DIM), lambda bi, kv, qi: (bi, 0, kv)),
                  pl.BlockSpec((None, HEAD_DIM, seq), lambda bi, kv, qi: (bi, kv, 0))] + c_in,
        out_specs=[pl.BlockSpec((None, tq, gw), lambda bi, kv, qi: (bi, qi, kv))] + c_out,
        out_shape=[jax.ShapeDtypeStruct((b, seq, B_QW), BF16)] + c_shapes,
        compiler_params=_cparams("parallel", "parallel", "arbitrary"),
        name="mixer_b",
    )(qbt, kb, vbt, *[src for src, _, _ in cast_jobs])


def _mixer_c_kernel(q_ref, k_ref, v_ref, o_ref):
    for hh in range(C_HEADS):
        sl = slice(hh * HEAD_DIM, (hh + 1) * HEAD_DIM)
        s = lax.dot_general(q_ref[:, sl], k_ref[:, sl], _NT, preferred_element_type=F32) * SCALE
        m = jnp.max(s, axis=-1, keepdims=True)
        p = jnp.exp(s - m)
        l = jnp.sum(p, axis=-1, keepdims=True)
        o = jnp.dot(p.astype(BF16), v_ref[:, sl], preferred_element_type=F32) / l
        o_ref[:, sl] = o.astype(o_ref.dtype)


def _mixer_c(qkn, kc, vc, tq=1024):
    b, seq, _ = qkn.shape
    n_mem = kc.shape[1]
    qc_blk = (2 * A_W) // C_W
    return pl.pallas_call(
        _mixer_c_kernel,
        grid=(b, seq // tq),
        in_specs=[pl.BlockSpec((None, tq, C_W), lambda bi, qi: (bi, qi, qc_blk)),
                  pl.BlockSpec((None, n_mem, C_W), lambda bi, qi: (bi, 0, 0)),
                  pl.BlockSpec((None, n_mem, C_W), lambda bi, qi: (bi, 0, 0))],
        out_specs=pl.BlockSpec((None, tq, C_W), lambda bi, qi: (bi, qi, 0)),
        out_shape=jax.ShapeDtypeStruct((b, seq, C_W), BF16),
        compiler_params=_cparams("parallel", "arbitrary"),
        name="mixer_c",
    )(qkn, kc, vc)


def _merge_kernel(h_ref, wg0_ref, wg1_ref, wg2_ref, oa_ref, ob_ref, oc_ref, wa_ref, wb_ref, wc_ref, out_ref):
    h = h_ref[...]
    ga = jax.nn.sigmoid(jnp.dot(h, wg0_ref[...], preferred_element_type=F32))
    merged = ga * jnp.dot(oa_ref[...], wa_ref[...], preferred_element_type=F32)
    gb = jax.nn.sigmoid(jnp.dot(h, wg1_ref[...], preferred_element_type=F32))
    merged += gb * jnp.dot(ob_ref[...], wb_ref[...], preferred_element_type=F32)
    gc = jax.nn.sigmoid(jnp.dot(h, wg2_ref[...], preferred_element_type=F32))
    merged += gc * jnp.dot(oc_ref[...], wc_ref[...], preferred_element_type=F32)
    out_ref[...] = merged.astype(out_ref.dtype)


def _merge(h, w_gates, oa, ob, oc, wa, wb, wc, tm=1024, tn=512):
    m, d = h.shape
    nj = d // tn
    row = lambda w: pl.BlockSpec((tm, w), lambda i, j: (i, 0))
    col = lambda k: pl.BlockSpec((k, tn), lambda i, j: (0, j))
    return pl.pallas_call(
        _merge_kernel,
        grid=(m // tm, nj),
        in_specs=[row(d), col(d), col(d), col(d), row(A_OUT_W), row(B_QW), row(C_W),
                  col(A_OUT_W), col(B_QW), col(C_W)],
        out_specs=pl.BlockSpec((tm, tn), lambda i, j: (i, j)),
        out_shape=jax.ShapeDtypeStruct((m, d), BF16),
        compiler_params=_cparams("parallel", "arbitrary", vmem_mb=48),
        name="merge",
    )(h, *w_gates, oa, ob, oc, wa, wb, wc)


def _out_proj_kernel(mg_ref, w_ref, x_ref, g_ref, x1_ref, h2_ref, *, r_sub):
    def matmul(c):
        return jnp.dot(mg_ref[c * r_sub:(c + 1) * r_sub, :], w_ref[...], preferred_element_type=F32)

    def epilogue(c, acc):
        rows = slice(c * r_sub, (c + 1) * r_sub)
        x1 = x_ref[rows, :] + acc
        x1_ref[rows, :] = x1
        ms = jnp.mean(x1 * x1, axis=-1, keepdims=True)
        h2_ref[rows, :] = (x1 * lax.rsqrt(ms + EPS) * g_ref[...]).astype(h2_ref.dtype)

    _software_pipeline(mg_ref.shape[0] // r_sub, matmul, epilogue)


def _out_proj(merged, w_o, x2d, g_ffn, tm=512, r_sub=512):
    m, d = x2d.shape
    return pl.pallas_call(
        functools.partial(_out_proj_kernel, r_sub=r_sub),
        grid=(m // tm,),
        in_specs=[pl.BlockSpec((tm, d), lambda i: (i, 0)),
                  pl.BlockSpec((d, d), lambda i: (0, 0)),
                  pl.BlockSpec((tm, d), lambda i: (i, 0)),
                  pl.BlockSpec((1, d), lambda i: (0, 0))],
        out_specs=[pl.BlockSpec((tm, d), lambda i: (i, 0))] * 2,
        out_shape=[jax.ShapeDtypeStruct((m, d), F32), jax.ShapeDtypeStruct((m, d), BF16)],
        compiler_params=_cparams("parallel"),
        name="out_proj",
    )(merged, w_o, x2d, g_ffn.reshape(1, d))


def _ffn_kernel(h_ref, wa_ref, wb_ref, wo_ref, x1_ref, out_ref, *, n_slabs):
    f = pl.program_id(1)
    slab_w = x1_ref.shape[1]

    @pl.when(f == 0)
    def _():
        out_ref[...] = jnp.zeros_like(out_ref)

    for s in range(n_slabs):
        @pl.when(f == s)
        def _():
            out_ref[:, s * slab_w:(s + 1) * slab_w] += x1_ref[...]

    h = h_ref[...]
    a = jnp.dot(h, wa_ref[...], preferred_element_type=F32)
    b = jnp.dot(h, wb_ref[...], preferred_element_type=F32)
    act = (a * jax.nn.sigmoid(a) * b).astype(BF16)
    out_ref[...] += jnp.dot(act, wo_ref[...], preferred_element_type=F32)


def _ffn(h2, w_a, w_b, w_out, x1, tm=1024, tf=512, slab_w=256):
    m, d = h2.shape
    d_ff = w_out.shape[0]
    nf = d_ff // tf
    n_slabs = d // slab_w
    assert n_slabs <= nf
    return pl.pallas_call(
        functools.partial(_ffn_kernel, n_slabs=n_slabs),
        grid=(m // tm, nf),
        in_specs=[pl.BlockSpec((tm, d), lambda i, f: (i, 0)),
                  pl.BlockSpec((d, tf), lambda i, f: (0, f)),
                  pl.BlockSpec((d, tf), lambda i, f: (0, f)),
                  pl.BlockSpec((tf, d), lambda i, f: (f, 0)),
                  pl.BlockSpec((tm, slab_w), lambda i, f: (i, jnp.minimum(f, n_slabs - 1)))],
        out_specs=pl.BlockSpec((tm, d), lambda i, f: (i, 0)),
        out_shape=jax.ShapeDtypeStruct((m, d), F32),
        compiler_params=_cparams("parallel", "arbitrary", vmem_mb=48),
        name="ffn",
    )(h2, w_a, w_b, w_out, x1)


def _deinterleave_cols(w, heads):
    d = w.shape[0]
    return w.reshape(d, heads, HEAD_DIM // 2, 2).transpose(0, 1, 3, 2).reshape(d, heads * HEAD_DIM)


def _rope_tables(seq):
    rows = seq // GRID_W
    r = np.repeat(np.arange(rows), GRID_W).astype(np.float64)
    c = np.tile(np.arange(GRID_W), rows).astype(np.float64)
    nf = HEAD_DIM // 4
    inv = ROPE_THETA ** (-np.arange(nf, dtype=np.float64) / nf)
    ang = np.concatenate([r[:, None] * inv, c[:, None] * inv], axis=-1)
    cos, sin = np.cos(ang).T, np.sin(ang).T
    return (np.concatenate([cos, cos], axis=0).astype(np.float32),
            np.concatenate([-sin, sin], axis=0).astype(np.float32))


def _layer(x2d, mem2d, bias_a, cos_t, sin_t, b, seq, g_mix, w_in, g_qa, g_ka, g_qb, g_kb, g_mem, w_mem_kv,
           g_qc, g_kc, w_br_a, w_br_b, w_br_c, w_o, g_ffn, w_ffn_in, w_ffn_out):
    d = x2d.shape[1]
    o_qa, o_ka, o_va, o_qb, o_kb, o_vb, o_qc, o_gt = np.cumsum(
        (0, A_W, A_W, A_W, B_QW, B_KVW, B_KVW, C_W))
    d_ff = w_ffn_out.shape[0]
    w_all = _cast_cols(w_in, int(o_gt))
    seg = lambda lo, hi: w_all[:, lo:hi]
    wt_b = jnp.concatenate([_deinterleave_cols(seg(o_qb, o_kb), B_Q_HEADS),
                            _deinterleave_cols(seg(o_kb, o_vb), B_KV_HEADS),
                            seg(o_vb, o_qc)], axis=1).T
    g_norm = jnp.concatenate([jnp.tile(g_qa, A_HEADS), jnp.tile(g_ka, A_HEADS), jnp.tile(g_qc, C_HEADS)])
    deint = lambda g: g.reshape(HEAD_DIM // 2, 2).T.reshape(HEAD_DIM)

    n_qk_tiles = (2 * A_W) // C_W
    qkn, h = _proj_norm(x2d, g_mix, w_all, g_norm,
                        lambda j: jnp.where(j < n_qk_tiles, j, int(o_qc) // C_W), tn=C_W)
    qkn = qkn.reshape(b, seq, -1)
    va = _proj_plain(h, w_all, A_W, int(o_va) // A_W).reshape(b, seq, -1)
    late_weights = [(w_ffn_out, 0, d), (w_o, 0, d), (w_br_a, 0, d), (w_br_b, 0, d), (w_br_c, 0, d)]
    qbt, kb, vbt, w_out_b, w_o_b, wa_b, wb_b, wc_b = _proj_t(
        h, wt_b, deint(g_qb) * (SCALE * LOG2E), deint(g_kb), cos_t, sin_t, b, seq, late_weights)
    kb = kb.reshape(b, seq, B_KVW)

    oa = _mixer_a(qkn, va, bias_a).reshape(b * seq, A_OUT_W)
    gate_jobs = [(w_in, int(o_gt) // d + br, d) for br in range(N_BRANCH)]
    ob, w_ffa, w_ffb, *w_gates = _mixer_b(qbt, kb, vbt, [(w_ffn_in, 0, d_ff), (w_ffn_in, 1, d_ff)] + gate_jobs)
    ob = ob.reshape(b * seq, B_QW)

    kc, vc = _mem_kv(mem2d, g_mem, w_mem_kv.astype(BF16), g_kc)
    n_mem = mem2d.shape[0] // b
    oc = _mixer_c(qkn, kc.reshape(b, n_mem, C_W), vc.reshape(b, n_mem, C_W)).reshape(b * seq, C_W)

    merged = _merge(h, w_gates, oa, ob, oc, wa_b, wb_b, wc_b)
    x1, h2 = _out_proj(merged, w_o_b, x2d, g_ffn)
    return _ffn(h2, w_ffa, w_ffb, w_out_b, x1)


def kernel(x, mem, rel_bias, g_mix, w_in, g_qa, g_ka, g_qb, g_kb, g_mem, w_mem_kv, g_qc, g_kc,
           w_br_a, w_br_b, w_br_c, w_o, g_ffn, w_ffn_in, w_ffn_out):
    b, seq, d = x.shape
    depth = w_in.shape[0]
    cos_t, sin_t = _rope_tables(seq)
    bias_a = _a_bias(rel_bias)
    x2d = x.reshape(b * seq, d)
    mem2d = mem.reshape(-1, d)
    for layer in range(depth):
        x2d = _layer(x2d, mem2d, bias_a, cos_t, sin_t, b, seq,
                     g_mix[layer], w_in[layer], g_qa[layer], g_ka[layer], g_qb[layer], g_kb[layer],
                     g_mem[layer], w_mem_kv[layer], g_qc[layer], g_kc[layer],
                     w_br_a[layer], w_br_b[layer], w_br_c[layer], w_o[layer], g_ffn[layer],
                     w_ffn_in[layer], w_ffn_out[layer])
    return x2d.reshape(b, seq, d)
```

```python
import functools
import math

import numpy as np
import jax
import jax.numpy as jnp
from jax import lax
from jax.experimental import pallas as pl
from jax.experimental.pallas import tpu as pltpu

HEAD_DIM = 128
GRID_W = 64
DIL_PAIRS = ((128, 1), (512, 4), (2048, 16))
A_HEADS_PER_GROUP = 2
A_HEADS = A_HEADS_PER_GROUP * len(DIL_PAIRS)
B_Q_HEADS = 6
B_KV_HEADS = 2
ROPE_THETA = 10000.0
C_HEADS = 4
N_BRANCH = 3
REL_BUCKETS = 32
REL_MAX_DIST = 1024
EPS = 1e-6
NEG = -1e30

A_W = A_HEADS * HEAD_DIM
A_OUT_W = A_HEADS_PER_GROUP * HEAD_DIM
B_QW = B_Q_HEADS * HEAD_DIM
B_KVW = B_KV_HEADS * HEAD_DIM
C_W = C_HEADS * HEAD_DIM

SCALE = 1.0 / math.sqrt(HEAD_DIM)
LOG2E = math.log2(math.e)

A_QROWS = 128
A_KWIN = 256
A_RADIUS = 64
A_NOFF = 3
A_BATCH = 8

BF16 = jnp.bfloat16
F32 = jnp.float32
BF16_TILE_ROWS = 16
F32_TILE_ROWS = 8

_NT = (((1,), (1,)), ((), ()))


def _software_pipeline(n_chunks, matmul, epilogue):
    acc = matmul(0)
    for c in range(n_chunks):
        nxt = matmul(c + 1) if c + 1 < n_chunks else None
        epilogue(c, acc)
        acc = nxt


def _cast_jobs(jobs, n_steps, step_index):
    in_specs, out_specs, out_shapes = [], [], []
    for src, col_block, width in jobs:
        rows = src.shape[0]
        rt = rows // n_steps
        assert rt * n_steps == rows and rt % BF16_TILE_ROWS == 0 and src.shape[1] % width == 0
        in_specs.append(pl.BlockSpec((rt, width), lambda *g, cb=col_block: (step_index(*g), cb)))
        out_specs.append(pl.BlockSpec((rt, width), lambda *g: (step_index(*g), 0)))
        out_shapes.append(jax.ShapeDtypeStruct((rows, width), BF16))
    return in_specs, out_specs, out_shapes


def _run_cast_jobs(src_refs, dst_refs):
    for src, dst in zip(src_refs, dst_refs):
        dst[...] = src[...].astype(dst.dtype)


def _cast_kernel(src_ref, dst_ref):
    dst_ref[...] = src_ref[...].astype(dst_ref.dtype)


def _cast_cols(w, n_cols, row_tile=256):
    rows = w.shape[0]
    assert rows % row_tile == 0
    return pl.pallas_call(
        _cast_kernel,
        grid=(rows // row_tile,),
        in_specs=[pl.BlockSpec((row_tile, n_cols), lambda i: (i, 0))],
        out_specs=pl.BlockSpec((row_tile, n_cols), lambda i: (i, 0)),
        out_shape=jax.ShapeDtypeStruct((rows, n_cols), BF16),
        compiler_params=_cparams("parallel"),
        name="cast_head_weights",
    )(w)


def _cparams(*sem, vmem_mb=None):
    limit = None if vmem_mb is None else vmem_mb * 1024 * 1024
    return pltpu.CompilerParams(dimension_semantics=sem, vmem_limit_bytes=limit)


def _proj_norm_kernel(x_hbm, gm_ref, w_ref, g_ref, o_ref, h_ref, xbuf, sem, *, r_sub, n_j):
    i, j = pl.program_id(0), pl.program_id(1)
    n_i = pl.num_programs(0)
    tm = h_ref.shape[0]
    piece = tm // n_j

    def x_copy(tile, p):
        slot = tile % 2
        rows = pl.ds(tile * tm + p * piece, piece)
        return pltpu.make_async_copy(x_hbm.at[rows, :], xbuf.at[slot, pl.ds(p * piece, piece), :], sem.at[slot, p])

    @pl.when((i == 0) & (j == 0))
    def _():
        for p in range(n_j):
            x_copy(0, p).start()

    @pl.when(i + 1 < n_i)
    def _():
        for p in range(n_j):
            @pl.when(j == p)
            def _():
                x_copy(i + 1, p).start()

    @pl.when(j == 0)
    def _():
        for p in range(n_j):
            x_copy(i, p).wait()
        x = xbuf[i % 2]
        ms = jnp.mean(x * x, axis=-1, keepdims=True)
        h_ref[...] = (x * lax.rsqrt(ms + EPS) * gm_ref[...]).astype(h_ref.dtype)

    def matmul(c):
        return jnp.dot(h_ref[c * r_sub:(c + 1) * r_sub, :], w_ref[...], preferred_element_type=F32)

    def epilogue(c, acc):
        for hh in range(o_ref.shape[1] // HEAD_DIM):
            sl = slice(hh * HEAD_DIM, (hh + 1) * HEAD_DIM)
            y = acc[:, sl]
            ms = jnp.mean(y * y, axis=-1, keepdims=True)
            o_ref[c * r_sub:(c + 1) * r_sub, sl] = (y * lax.rsqrt(ms + EPS) * g_ref[:, sl]).astype(o_ref.dtype)

    _software_pipeline(h_ref.shape[0] // r_sub, matmul, epilogue)


def _proj_norm(x2d, g_mix, w, gains, col_block, tm=1024, tn=512, r_sub=256):
    m, d = x2d.shape
    n = gains.shape[0]
    n_j = n // tn
    assert n % tn == 0 and m % tm == 0 and tm % (F32_TILE_ROWS * n_j) == 0
    return pl.pallas_call(
        functools.partial(_proj_norm_kernel, r_sub=r_sub, n_j=n_j),
        grid=(m // tm, n_j),
        in_specs=[pl.BlockSpec(memory_space=pl.ANY),
                  pl.BlockSpec((1, d), lambda i, j: (0, 0)),
                  pl.BlockSpec((d, tn), lambda i, j: (0, col_block(j))),
                  pl.BlockSpec((1, tn), lambda i, j: (0, j))],
        out_specs=[pl.BlockSpec((tm, tn), lambda i, j: (i, j)),
                   pl.BlockSpec((tm, d), lambda i, j: (i, 0))],
        out_shape=[jax.ShapeDtypeStruct((m, n), BF16), jax.ShapeDtypeStruct((m, d), BF16)],
        scratch_shapes=[pltpu.VMEM((2, tm, d), F32), pltpu.SemaphoreType.DMA((2, n_j))],
        compiler_params=_cparams("arbitrary", "arbitrary", vmem_mb=48),
        name="proj_norm",
    )(x2d, g_mix.reshape(1, d), w, gains.reshape(1, n))


def _proj_plain_kernel(h_ref, w_ref, o_ref):
    o_ref[...] = jnp.dot(h_ref[...], w_ref[...], preferred_element_type=F32).astype(o_ref.dtype)


def _proj_plain(h, w, n, col_block, tm=2048):
    m, d = h.shape
    assert m % tm == 0
    return pl.pallas_call(
        _proj_plain_kernel,
        grid=(m // tm,),
        in_specs=[pl.BlockSpec((tm, d), lambda i: (i, 0)),
                  pl.BlockSpec((d, n), lambda i: (0, col_block))],
        out_specs=pl.BlockSpec((tm, n), lambda i: (i, 0)),
        out_shape=jax.ShapeDtypeStruct((m, n), BF16),
        compiler_params=_cparams("parallel"),
        name="proj_plain",
    )(h, w)


def _proj_t_kernel(wt_ref, h_ref, gq_ref, gk_ref, cos_ref, sin_ref, *rest, t_sub, n_cast):
    q_ref, k_ref, v_ref = rest[n_cast:n_cast + 3]
    _run_cast_jobs(rest[:n_cast], rest[n_cast + 3:])
    n_q = q_ref.shape[0] // HEAD_DIM
    n_k = k_ref.shape[1] // HEAD_DIM
    n_v = v_ref.shape[0] // HEAD_DIM
    half = HEAD_DIM // 2

    def norm_rope(y, g_ref, tok):
        ms = jnp.mean(y * y, axis=0, keepdims=True)
        y = y * lax.rsqrt(ms + EPS) * g_ref[...]
        partner = jnp.concatenate([y[half:], y[:half]], axis=0)
        return y * cos_ref[:, tok] + partner * sin_ref[:, tok]

    def matmul(c):
        return lax.dot_general(wt_ref[...], h_ref[c * t_sub:(c + 1) * t_sub, :], _NT,
                               preferred_element_type=F32)

    def epilogue(c, yt):
        tok = slice(c * t_sub, (c + 1) * t_sub)
        head = lambda hh: yt[hh * HEAD_DIM:(hh + 1) * HEAD_DIM]
        for hh in range(n_q):
            q_ref[hh * HEAD_DIM:(hh + 1) * HEAD_DIM, tok] = norm_rope(head(hh), gq_ref, tok).astype(q_ref.dtype)
        for hh in range(n_k):
            y = norm_rope(head(n_q + hh), gk_ref, tok)
            k_ref[tok, hh * HEAD_DIM:(hh + 1) * HEAD_DIM] = y.T.astype(k_ref.dtype)
        for hh in range(n_v):
            v_ref[hh * HEAD_DIM:(hh + 1) * HEAD_DIM, tok] = head(n_q + n_k + hh).astype(v_ref.dtype)

    _software_pipeline(h_ref.shape[0] // t_sub, matmul, epilogue)


def _proj_t(h, wt, gq_col, gk_col, cos_tt, sin_tt, b, seq, cast_jobs, tm=1024, t_sub=256):
    m, d = h.shape
    n = wt.shape[0]
    sb = seq // tm
    col = lambda g: jnp.broadcast_to(g[:, None], (HEAD_DIM, t_sub))
    lane_tile = lambda rows: pl.BlockSpec((None, rows, tm), lambda i: (i // sb, 0, i % sb))
    c_in, c_out, c_shapes = _cast_jobs(cast_jobs, m // tm, lambda i: i)
    return pl.pallas_call(
        functools.partial(_proj_t_kernel, t_sub=t_sub, n_cast=len(cast_jobs)),
        grid=(m // tm,),
        in_specs=[pl.BlockSpec((n, d), lambda i: (0, 0)),
                  pl.BlockSpec((tm, d), lambda i: (i, 0)),
                  pl.BlockSpec((HEAD_DIM, t_sub), lambda i: (0, 0)),
                  pl.BlockSpec((HEAD_DIM, t_sub), lambda i: (0, 0)),
                  pl.BlockSpec((HEAD_DIM, tm), lambda i: (0, i % sb)),
                  pl.BlockSpec((HEAD_DIM, tm), lambda i: (0, i % sb))] + c_in,
        out_specs=[lane_tile(B_QW),
                   pl.BlockSpec((tm, B_KVW), lambda i: (i, 0)),
                   lane_tile(B_KVW)] + c_out,
        out_shape=[jax.ShapeDtypeStruct((b, B_QW, seq), BF16),
                   jax.ShapeDtypeStruct((m, B_KVW), BF16),
                   jax.ShapeDtypeStruct((b, B_KVW, seq), BF16)] + c_shapes,
        compiler_params=_cparams("parallel", vmem_mb=48),
        name="proj_t",
    )(wt, h, col(gq_col), col(gk_col), cos_tt, sin_tt, *[src for src, _, _ in cast_jobs])


def _mem_kv_kernel(mem_ref, gm_ref, w_ref, gk_ref, k_ref, v_ref):
    x = mem_ref[...]
    ms = jnp.mean(x * x, axis=-1, keepdims=True)
    hm = (x * lax.rsqrt(ms + EPS) * gm_ref[...]).astype(BF16)
    kv = jnp.dot(hm, w_ref[...], preferred_element_type=F32)
    for hh in range(C_HEADS):
        sl = slice(hh * HEAD_DIM, (hh + 1) * HEAD_DIM)
        y = kv[:, sl]
        ms = jnp.mean(y * y, axis=-1, keepdims=True)
        k_ref[:, sl] = (y * lax.rsqrt(ms + EPS) * gk_ref[...]).astype(BF16)
    v_ref[...] = kv[:, C_W:].astype(BF16)


def _mem_kv(mem2d, g_mem, w_kv, g_kc, tm=256):
    m, d = mem2d.shape
    return pl.pallas_call(
        _mem_kv_kernel,
        grid=(m // tm,),
        in_specs=[pl.BlockSpec((tm, d), lambda i: (i, 0)),
                  pl.BlockSpec((1, d), lambda i: (0, 0)),
                  pl.BlockSpec((d, 2 * C_W), lambda i: (0, 0)),
                  pl.BlockSpec((1, HEAD_DIM), lambda i: (0, 0))],
        out_specs=[pl.BlockSpec((tm, C_W), lambda i: (i, 0))] * 2,
        out_shape=[jax.ShapeDtypeStruct((m, C_W), BF16)] * 2,
        compiler_params=_cparams("parallel"),
        name="mem_kv",
    )(mem2d, g_mem.reshape(1, d), w_kv, g_kc.reshape(1, HEAD_DIM))


def _t5_bucket(rel):
    nb = REL_BUCKETS // 2
    ret = np.where(rel > 0, nb, 0)
    n = np.abs(rel)
    max_exact = nb // 2
    large = max_exact + (np.log(np.maximum(n, 1).astype(np.float32) / np.float32(max_exact))
                         / np.float32(math.log(REL_MAX_DIST / max_exact))
                         * np.float32(nb - max_exact)).astype(np.int32)
    large = np.minimum(large, nb - 1)
    return ret + np.where(n < max_exact, n, large)


def _a_bucket_index():
    qi = np.arange(A_QROWS, dtype=np.int32)[:, None]
    kj = np.arange(A_KWIN, dtype=np.int32)[None, :]
    out = []
    for _, dil in DIL_PAIRS:
        for off in range(A_NOFF):
            rel = kj - qi - A_RADIUS * off
            out.append(np.where(np.abs(rel) <= A_RADIUS, _t5_bucket(rel * dil), -1))
    return np.stack(out).astype(np.int32)


def _a_bias_kernel(tab_ref, bucket_ref, o_ref):
    g = pl.program_id(0) // A_NOFF
    bk = bucket_ref[...]
    for hh in range(A_HEADS_PER_GROUP):
        acc = jnp.full(bk.shape, NEG, F32)
        for b in range(REL_BUCKETS):
            acc = jnp.where(bk == b, tab_ref[b, g * A_HEADS_PER_GROUP + hh], acc)
        o_ref[hh] = acc


def _a_bias(rel_bias):
    n = len(DIL_PAIRS) * A_NOFF
    return pl.pallas_call(
        _a_bias_kernel,
        grid=(n,),
        in_specs=[pl.BlockSpec(memory_space=pltpu.SMEM),
                  pl.BlockSpec((None, A_QROWS, A_KWIN), lambda i: (i, 0, 0))],
        out_specs=pl.BlockSpec((None, A_HEADS_PER_GROUP, A_QROWS, A_KWIN), lambda i: (i, 0, 0, 0)),
        out_shape=jax.ShapeDtypeStruct((n, A_HEADS_PER_GROUP, A_QROWS, A_KWIN), F32),
        compiler_params=_cparams("arbitrary"),
        name="a_bias",
    )(rel_bias, _a_bucket_index())


def _mixer_a_kernel(q_ref, k_ref, v_ref, bias_ref, oa_ref, stage, qstage, o_acc, lse_acc, *residue_kv, seq):
    t_rows = q_ref.shape[0]
    ti = pl.program_id(1)
    n_groups = len(DIL_PAIRS)

    def softmax(s):
        m = jnp.max(s, axis=-1, keepdims=True)
        p = jnp.exp(s - m)
        l = jnp.sum(p, axis=-1, keepdims=True)
        return p.astype(BF16), l, m + jnp.log(l)

    def run_group(gi, dil, kres, vres):
        sub_len = seq // dil
        lq = t_rows // dil
        gcols = lambda hh: slice(gi * A_OUT_W + hh * HEAD_DIM, gi * A_OUT_W + (hh + 1) * HEAD_DIM)
        if dil > 1:
            @pl.when(ti == 0)
            def _():
                for src, dst in ((k_ref, kres), (v_ref, vres)):
                    for hh in range(A_HEADS_PER_GROUP):
                        stage[...] = src[:, gcols(hh)].astype(F32)
                        for r in range(dil):
                            dst[r, :, hh * HEAD_DIM:(hh + 1) * HEAD_DIM] = (
                                stage[pl.ds(r, sub_len, stride=dil), :].astype(BF16))

            for hh in range(A_HEADS_PER_GROUP):
                qstage[hh] = q_ref[:, gcols(hh)].astype(F32)

        def scores(r, i, hh):
            q0 = ti * lq + i * A_QROWS
            ks = jnp.clip(q0 - A_RADIUS, 0, sub_len - A_KWIN)
            off = lax.shift_right_logical(q0 - ks, int(math.log2(A_RADIUS)))
            ks = pl.multiple_of(ks, A_RADIUS)
            if dil > 1:
                cols = slice(hh * HEAD_DIM, (hh + 1) * HEAD_DIM)
                rows = pl.ds(i * A_QROWS * dil + r, A_QROWS, stride=dil)
                q = qstage[hh, rows, :].astype(BF16)
                k = kres[r, pl.ds(ks, A_KWIN), cols]
                v = vres[r, pl.ds(ks, A_KWIN), cols]
            else:
                rows = pl.ds(i * A_QROWS, A_QROWS)
                q = q_ref[rows, gcols(hh)]
                k = k_ref[pl.ds(ks, A_KWIN), gcols(hh)]
                v = v_ref[pl.ds(ks, A_KWIN), gcols(hh)]
            s = lax.dot_general(q, k, _NT, preferred_element_type=F32) * SCALE + bias_ref[gi * A_NOFF + off, hh]
            return rows, s, v

        def fold(hh, rows, o, lse):
            lse = jnp.broadcast_to(lse, (A_QROWS, HEAD_DIM))
            if gi > 0:
                prev_o, prev_lse = o_acc[hh, rows, :], lse_acc[hh, rows, :]
                m = jnp.maximum(prev_lse, lse)
                w_prev, w_new = jnp.exp(prev_lse - m), jnp.exp(lse - m)
                den = w_prev + w_new
                o = (w_prev * prev_o + w_new * o) / den
                lse = m + jnp.log(den)
            o_acc[hh, rows, :] = o
            if gi + 1 < n_groups:
                lse_acc[hh, rows, :] = lse

        items = [(r, i, hh) for r in range(dil) for i in range(lq // A_QROWS) for hh in range(A_HEADS_PER_GROUP)]
        for b0 in range(0, len(items), A_BATCH):
            batch = items[b0:b0 + A_BATCH]
            staged = [scores(*it) for it in batch]
            probs = [softmax(s) for _, s, _ in staged]
            for (_, _, hh), (rows, _, v), (p, l, lse) in zip(batch, staged, probs):
                fold(hh, rows, jnp.dot(p, v, preferred_element_type=F32) / l, lse)

    strided = [gi for gi, (_, dil) in enumerate(DIL_PAIRS) if dil > 1]
    for gi, (_, dil) in enumerate(DIL_PAIRS):
        kres, vres = (residue_kv[2 * strided.index(gi):2 * strided.index(gi) + 2] if dil > 1 else (None, None))
        run_group(gi, dil, kres, vres)
    for hh in range(A_HEADS_PER_GROUP):
        oa_ref[:, hh * HEAD_DIM:(hh + 1) * HEAD_DIM] = o_acc[hh].astype(oa_ref.dtype)


def _mixer_a(qkn, va, bias_a, t_rows=2048):
    b, seq, _ = qkn.shape
    head_buf = lambda rows: pltpu.VMEM((A_HEADS_PER_GROUP, rows, HEAD_DIM), F32)
    scratch = [pltpu.VMEM((seq, HEAD_DIM), F32), head_buf(t_rows), head_buf(t_rows), head_buf(t_rows)]
    for _, dil in DIL_PAIRS:
        if dil > 1:
            scratch += [pltpu.VMEM((dil, seq // dil, A_OUT_W), BF16)] * 2
    return pl.pallas_call(
        functools.partial(_mixer_a_kernel, seq=seq),
        grid=(b, seq // t_rows),
        in_specs=[pl.BlockSpec((None, t_rows, A_W), lambda bi, ti: (bi, ti, 0)),
                  pl.BlockSpec((None, seq, A_W), lambda bi, ti: (bi, 0, 1)),
                  pl.BlockSpec((None, seq, A_W), lambda bi, ti: (bi, 0, 0)),
                  pl.BlockSpec(bias_a.shape, lambda bi, ti: (0, 0, 0, 0), pipeline_mode=pl.Buffered(1))],
        out_specs=pl.BlockSpec((None, t_rows, A_OUT_W), lambda bi, ti: (bi, ti, 0)),
        out_shape=jax.ShapeDtypeStruct((b, seq, A_OUT_W), BF16),
        scratch_shapes=scratch,
        compiler_params=_cparams("parallel", "arbitrary", vmem_mb=62),
        name="mixer_a",
    )(qkn, qkn, va, bias_a)


def _mixer_b_kernel(qt_ref, k_ref, vt_ref, *rest, tk, group, w, n_cast):
    o_ref, s_scr = rest[n_cast], rest[-1]
    _run_cast_jobs(rest[:n_cast], rest[n_cast + 1:-1])
    tq = qt_ref.shape[1]
    seq = k_ref.shape[0]
    n_chunks = seq // tk
    units = [(i, j) for i in range(group) for j in range(tq // w)]

    def pass_a(u, ci, m):
        i, j = units[u]
        qt = qt_ref[i * HEAD_DIM:(i + 1) * HEAD_DIM, j * w:(j + 1) * w]
        st = jnp.dot(k_ref[ci * tk:(ci + 1) * tk, :], qt, preferred_element_type=F32)
        s_scr[u % 2, ci * tk:(ci + 1) * tk, :] = st
        return jnp.maximum(m, jnp.max(st, axis=0, keepdims=True))

    def pass_b(u, ci, m, l, acc):
        pt = jnp.exp2(s_scr[u % 2, ci * tk:(ci + 1) * tk, :] - m)
        l = l + jnp.sum(pt, axis=0, keepdims=True)
        acc = acc + jnp.dot(vt_ref[:, ci * tk:(ci + 1) * tk], pt.astype(BF16), preferred_element_type=F32)
        return l, acc

    m_prev = None
    for s in range(len(units) + 1):
        m_cur = jnp.full((1, w), NEG, F32)
        l = jnp.zeros((1, w), F32)
        acc = jnp.zeros((HEAD_DIM, w), F32)
        for ci in range(n_chunks):
            if s < len(units):
                m_cur = pass_a(s, ci, m_cur)
            if s > 0:
                l, acc = pass_b(s - 1, ci, m_prev, l, acc)
        if s > 0:
            i, j = units[s - 1]
            o_ref[j * w:(j + 1) * w, i * HEAD_DIM:(i + 1) * HEAD_DIM] = (acc / l).T.astype(o_ref.dtype)
        m_prev = m_cur


def _mixer_b(qbt, kb, vbt, cast_jobs, tq=512, tk=512, w=256):
    b, seq, _ = kb.shape
    group = B_Q_HEADS // B_KV_HEADS
    gw = group * HEAD_DIM
    nq = seq // tq
    n_steps = b * B_KV_HEADS * nq
    c_in, c_out, c_shapes = _cast_jobs(cast_jobs, n_steps, lambda bi, kv, qi: (bi * B_KV_HEADS + kv) * nq + qi)
    return pl.pallas_call(
        functools.partial(_mixer_b_kernel, tk=tk, group=group, w=w, n_cast=len(cast_jobs)),
        grid=(b, B_KV_HEADS, nq),
        scratch_shapes=[pltpu.VMEM((2, seq, w), F32)],
        in_specs=[pl.BlockSpec((None, gw, tq), lambda bi, kv, qi: (bi, kv, qi)),
                  pl.BlockSpec((None, seq, HEAD_DIM), lambda bi, kv, qi: (bi, 0, kv)),
                  pl.BlockSpec((None, HEAD_DIM, seq), lambda bi, kv, qi: (bi, kv, 0))] + c_in,
        out_specs=[pl.BlockSpec((None, tq, gw), lambda bi, kv, qi: (bi, qi, kv))] + c_out,
        out_shape=[jax.ShapeDtypeStruct((b, seq, B_QW), BF16)] + c_shapes,
        compiler_params=_cparams("parallel", "parallel", "arbitrary"),
        name="mixer_b",
    )(qbt, kb, vbt, *[src for src, _, _ in cast_jobs])


def _mixer_c_kernel(q_ref, k_ref, v_ref, o_ref):
    for hh in range(C_HEADS):
        sl = slice(hh * HEAD_DIM, (hh + 1) * HEAD_DIM)
        s = lax.dot_general(q_ref[:, sl], k_ref[:, sl], _NT, preferred_element_type=F32) * SCALE
        m = jnp.max(s, axis=-1, keepdims=True)
        p = jnp.exp(s - m)
        l = jnp.sum(p, axis=-1, keepdims=True)
        o = jnp.dot(p.astype(BF16), v_ref[:, sl], preferred_element_type=F32) / l
        o_ref[:, sl] = o.astype(o_ref.dtype)


def _mixer_c(qkn, kc, vc, tq=1024):
    b, seq, _ = qkn.shape
    n_mem = kc.shape[1]
    qc_blk = (2 * A_W) // C_W
    return pl.pallas_call(
        _mixer_c_kernel,
        grid=(b, seq // tq),
        in_specs=[pl.BlockSpec((None, tq, C_W), lambda bi, qi: (bi, qi, qc_blk)),
                  pl.BlockSpec((None, n_mem, C_W), lambda bi, qi: (bi, 0, 0)),
                  pl.BlockSpec((None, n_mem, C_W), lambda bi, qi: (bi, 0, 0))],
        out_specs=pl.BlockSpec((None, tq, C_W), lambda bi, qi: (bi, qi, 0)),
        out_shape=jax.ShapeDtypeStruct((b, seq, C_W), BF16),
        compiler_params=_cparams("parallel", "arbitrary"),
        name="mixer_c",
    )(qkn, kc, vc)


def _merge_kernel(h_ref, wg0_ref, wg1_ref, wg2_ref, oa_ref, ob_ref, oc_ref, wa_ref, wb_ref, wc_ref, out_ref):
    h = h_ref[...]
    ga = jax.nn.sigmoid(jnp.dot(h, wg0_ref[...], preferred_element_type=F32))
    merged = ga * jnp.dot(oa_ref[...], wa_ref[...], preferred_element_type=F32)
    gb = jax.nn.sigmoid(jnp.dot(h, wg1_ref[...], preferred_element_type=F32))
    merged += gb * jnp.dot(ob_ref[...], wb_ref[...], preferred_element_type=F32)
    gc = jax.nn.sigmoid(jnp.dot(h, wg2_ref[...], preferred_element_type=F32))
    merged += gc * jnp.dot(oc_ref[...], wc_ref[...], preferred_element_type=F32)
    out_ref[...] = merged.astype(out_ref.dtype)


def _merge(h, w_gates, oa, ob, oc, wa, wb, wc, tm=1024, tn=512):
    m, d = h.shape
    nj = d // tn
    row = lambda w: pl.BlockSpec((tm, w), lambda i, j: (i, 0))
    col = lambda k: pl.BlockSpec((k, tn), lambda i, j: (0, j))
    return pl.pallas_call(
        _merge_kernel,
        grid=(m // tm, nj),
        in_specs=[row(d), col(d), col(d), col(d), row(A_OUT_W), row(B_QW), row(C_W),
                  col(A_OUT_W), col(B_QW), col(C_W)],
        out_specs=pl.BlockSpec((tm, tn), lambda i, j: (i, j)),
        out_shape=jax.ShapeDtypeStruct((m, d), BF16),
        compiler_params=_cparams("parallel", "arbitrary", vmem_mb=48),
        name="merge",
    )(h, *w_gates, oa, ob, oc, wa, wb, wc)


def _out_proj_kernel(mg_ref, w_ref, x_ref, g_ref, x1_ref, h2_ref, *, r_sub):
    def matmul(c):
        return jnp.dot(mg_ref[c * r_sub:(c + 1) * r_sub, :], w_ref[...], preferred_element_type=F32)

    def epilogue(c, acc):
        rows = slice(c * r_sub, (c + 1) * r_sub)
        x1 = x_ref[rows, :] + acc
        x1_ref[rows, :] = x1
        ms = jnp.mean(x1 * x1, axis=-1, keepdims=True)
        h2_ref[rows, :] = (x1 * lax.rsqrt(ms + EPS) * g_ref[...]).astype(h2_ref.dtype)

    _software_pipeline(mg_ref.shape[0] // r_sub, matmul, epilogue)


def _out_proj(merged, w_o, x2d, g_ffn, tm=512, r_sub=512):
    m, d = x2d.shape
    return pl.pallas_call(
        functools.partial(_out_proj_kernel, r_sub=r_sub),
        grid=(m // tm,),
        in_specs=[pl.BlockSpec((tm, d), lambda i: (i, 0)),
                  pl.BlockSpec((d, d), lambda i: (0, 0)),
                  pl.BlockSpec((tm, d), lambda i: (i, 0)),
                  pl.BlockSpec((1, d), lambda i: (0, 0))],
        out_specs=[pl.BlockSpec((tm, d), lambda i: (i, 0))] * 2,
        out_shape=[jax.ShapeDtypeStruct((m, d), F32), jax.ShapeDtypeStruct((m, d), BF16)],
        compiler_params=_cparams("parallel"),
        name="out_proj",
    )(merged, w_o, x2d, g_ffn.reshape(1, d))


def _ffn_kernel(h_ref, wa_ref, wb_ref, wo_ref, x1_ref, out_ref, *, n_slabs):
    f = pl.program_id(1)
    slab_w = x1_ref.shape[1]

    @pl.when(f == 0)
    def _():
        out_ref[...] = jnp.zeros_like(out_ref)

    for s in range(n_slabs):
        @pl.when(f == s)
        def _():
            out_ref[:, s * slab_w:(s + 1) * slab_w] += x1_ref[...]

    h = h_ref[...]
    a = jnp.dot(h, wa_ref[...], preferred_element_type=F32)
    b = jnp.dot(h, wb_ref[...], preferred_element_type=F32)
    act = (a * jax.nn.sigmoid(a) * b).astype(BF16)
    out_ref[...] += jnp.dot(act, wo_ref[...], preferred_element_type=F32)


def _ffn(h2, w_a, w_b, w_out, x1, tm=1024, tf=512, slab_w=256):
    m, d = h2.shape
    d_ff = w_out.shape[0]
    nf = d_ff // tf
    n_slabs = d // slab_w
    assert n_slabs <= nf
    return pl.pallas_call(
        functools.partial(_ffn_kernel, n_slabs=n_slabs),
        grid=(m // tm, nf),
        in_specs=[pl.BlockSpec((tm, d), lambda i, f: (i, 0)),
                  pl.BlockSpec((d, tf), lambda i, f: (0, f)),
                  pl.BlockSpec((d, tf), lambda i, f: (0, f)),
                  pl.BlockSpec((tf, d), lambda i, f: (f, 0)),
                  pl.BlockSpec((tm, slab_w), lambda i, f: (i, jnp.minimum(f, n_slabs - 1)))],
        out_specs=pl.BlockSpec((tm, d), lambda i, f: (i, 0)),
        out_shape=jax.ShapeDtypeStruct((m, d), F32),
        compiler_params=_cparams("parallel", "arbitrary", vmem_mb=48),
        name="ffn",
    )(h2, w_a, w_b, w_out, x1)


def _deinterleave_cols(w, heads):
    d = w.shape[0]
    return w.reshape(d, heads, HEAD_DIM // 2, 2).transpose(0, 1, 3, 2).reshape(d, heads * HEAD_DIM)


def _rope_tables(seq):
    rows = seq // GRID_W
    r = np.repeat(np.arange(rows), GRID_W).astype(np.float64)
    c = np.tile(np.arange(GRID_W), rows).astype(np.float64)
    nf = HEAD_DIM // 4
    inv = ROPE_THETA ** (-np.arange(nf, dtype=np.float64) / nf)
    ang = np.concatenate([r[:, None] * inv, c[:, None] * inv], axis=-1)
    cos, sin = np.cos(ang).T, np.sin(ang).T
    return (np.concatenate([cos, cos], axis=0).astype(np.float32),
            np.concatenate([-sin, sin], axis=0).astype(np.float32))


def _layer(x2d, mem2d, bias_a, cos_t, sin_t, b, seq, g_mix, w_in, g_qa, g_ka, g_qb, g_kb, g_mem, w_mem_kv,
           g_qc, g_kc, w_br_a, w_br_b, w_br_c, w_o, g_ffn, w_ffn_in, w_ffn_out):
    d = x2d.shape[1]
    o_qa, o_ka, o_va, o_qb, o_kb, o_vb, o_qc, o_gt = np.cumsum(
        (0, A_W, A_W, A_W, B_QW, B_KVW, B_KVW, C_W))
    d_ff = w_ffn_out.shape[0]
    w_all = _cast_cols(w_in, int(o_gt))
    seg = lambda lo, hi: w_all[:, lo:hi]
    wt_b = jnp.concatenate([_deinterleave_cols(seg(o_qb, o_kb), B_Q_HEADS),
                            _deinterleave_cols(seg(o_kb, o_vb), B_KV_HEADS),
                            seg(o_vb, o_qc)], axis=1).T
    g_norm = jnp.concatenate([jnp.tile(g_qa, A_HEADS), jnp.tile(g_ka, A_HEADS), jnp.tile(g_qc, C_HEADS)])
    deint = lambda g: g.reshape(HEAD_DIM // 2, 2).T.reshape(HEAD_DIM)

    n_qk_tiles = (2 * A_W) // C_W
    qkn, h = _proj_norm(x2d, g_mix, w_all, g_norm,
                        lambda j: jnp.where(j < n_qk_tiles, j, int(o_qc) // C_W), tn=C_W)
    qkn = qkn.reshape(b, seq, -1)
    va = _proj_plain(h, w_all, A_W, int(o_va) // A_W).reshape(b, seq, -1)
    late_weights = [(w_ffn_out, 0, d), (w_o, 0, d), (w_br_a, 0, d), (w_br_b, 0, d), (w_br_c, 0, d)]
    qbt, kb, vbt, w_out_b, w_o_b, wa_b, wb_b, wc_b = _proj_t(
        h, wt_b, deint(g_qb) * (SCALE * LOG2E), deint(g_kb), cos_t, sin_t, b, seq, late_weights)
    kb = kb.reshape(b, seq, B_KVW)

    oa = _mixer_a(qkn, va, bias_a).reshape(b * seq, A_OUT_W)
    gate_jobs = [(w_in, int(o_gt) // d + br, d) for br in range(N_BRANCH)]
    ob, w_ffa, w_ffb, *w_gates = _mixer_b(qbt, kb, vbt, [(w_ffn_in, 0, d_ff), (w_ffn_in, 1, d_ff)] + gate_jobs)
    ob = ob.reshape(b * seq, B_QW)

    kc, vc = _mem_kv(mem2d, g_mem, w_mem_kv.astype(BF16), g_kc)
    n_mem = mem2d.shape[0] // b
    oc = _mixer_c(qkn, kc.reshape(b, n_mem, C_W), vc.reshape(b, n_mem, C_W)).reshape(b * seq, C_W)

    merged = _merge(h, w_gates, oa, ob, oc, wa_b, wb_b, wc_b)
    x1, h2 = _out_proj(merged, w_o_b, x2d, g_ffn)
    return _ffn(h2, w_ffa, w_ffb, w_out_b, x1)


def kernel(x, mem, rel_bias, g_mix, w_in, g_qa, g_ka, g_qb, g_kb, g_mem, w_mem_kv, g_qc, g_kc,
           w_br_a, w_br_b, w_br_c, w_o, g_ffn, w_ffn_in, w_ffn_out):
    b, seq, d = x.shape
    depth = w_in.shape[0]
    cos_t, sin_t = _rope_tables(seq)
    bias_a = _a_bias(rel_bias)
    x2d = x.reshape(b * seq, d)
    mem2d = mem.reshape(-1, d)
    for layer in range(depth):
        x2d = _layer(x2d, mem2d, bias_a, cos_t, sin_t, b, seq,
                     g_mix[layer], w_in[layer], g_qa[layer], g_ka[layer], g_qb[layer], g_kb[layer],
                     g_mem[layer], w_mem_kv[layer], g_qc[layer], g_kc[layer],
                     w_br_a[layer], w_br_b[layer], w_br_c[layer], w_o[layer], g_ffn[layer],
                     w_ffn_in[layer], w_ffn_out[layer])
    return x2d.reshape(b, seq, d)
```

```python
import functools
import math

import numpy as np
import jax
import jax.numpy as jnp
from jax import lax
from jax.experimental import pallas as pl
from jax.experimental.pallas import tpu as pltpu

HEAD_DIM = 128
GRID_W = 64
DIL_PAIRS = ((128, 1), (512, 4), (2048, 16))
A_HEADS_PER_GROUP = 2
A_HEADS = A_HEADS_PER_GROUP * len(DIL_PAIRS)
B_Q_HEADS = 6
B_KV_HEADS = 2
ROPE_THETA = 10000.0
C_HEADS = 4
N_BRANCH = 3
REL_BUCKETS = 32
REL_MAX_DIST = 1024
EPS = 1e-6
NEG = -1e30

A_W = A_HEADS * HEAD_DIM
A_OUT_W = A_HEADS_PER_GROUP * HEAD_DIM
B_QW = B_Q_HEADS * HEAD_DIM
B_KVW = B_KV_HEADS * HEAD_DIM
C_W = C_HEADS * HEAD_DIM

SCALE = 1.0 / math.sqrt(HEAD_DIM)
LOG2E = math.log2(math.e)

A_QROWS = 128
A_KWIN = 256
A_RADIUS = 64
A_NOFF = 3
A_BATCH = 8

BF16 = jnp.bfloat16
F32 = jnp.float32
BF16_TILE_ROWS = 16
F32_TILE_ROWS = 8

_NT = (((1,), (1,)), ((), ()))


def _software_pipeline(n_chunks, matmul, epilogue):
    acc = matmul(0)
    for c in range(n_chunks):
        nxt = matmul(c + 1) if c + 1 < n_chunks else None
        epilogue(c, acc)
        acc = nxt


def _cast_jobs(jobs, n_steps, step_index):
    in_specs, out_specs, out_shapes = [], [], []
    for src, col_block, width in jobs:
        rows = src.shape[0]
        rt = rows // n_steps
        assert rt * n_steps == rows and rt % BF16_TILE_ROWS == 0 and src.shape[1] % width == 0
        in_specs.append(pl.BlockSpec((rt, width), lambda *g, cb=col_block: (step_index(*g), cb)))
        out_specs.append(pl.BlockSpec((rt, width), lambda *g: (step_index(*g), 0)))
        out_shapes.append(jax.ShapeDtypeStruct((rows, width), BF16))
    return in_specs, out_specs, out_shapes


def _run_cast_jobs(src_refs, dst_refs):
    for src, dst in zip(src_refs, dst_refs):
        dst[...] = src[...].astype(dst.dtype)


def _cast_kernel(src_ref, dst_ref):
    dst_ref[...] = src_ref[...].astype(dst_ref.dtype)


def _cast_cols(w, n_cols, row_tile=256):
    rows = w.shape[0]
    assert rows % row_tile == 0
    return pl.pallas_call(
        _cast_kernel,
        grid=(rows // row_tile,),
        in_specs=[pl.BlockSpec((row_tile, n_cols), lambda i: (i, 0))],
        out_specs=pl.BlockSpec((row_tile, n_cols), lambda i: (i, 0)),
        out_shape=jax.ShapeDtypeStruct((rows, n_cols), BF16),
        compiler_params=_cparams("parallel"),
        name="cast_head_weights",
    )(w)


def _cparams(*sem, vmem_mb=None):
    limit = None if vmem_mb is None else vmem_mb * 1024 * 1024
    return pltpu.CompilerParams(dimension_semantics=sem, vmem_limit_bytes=limit)


def _proj_norm_kernel(x_hbm, gm_ref, w_ref, g_ref, o_ref, h_ref, xbuf, sem, *, r_sub, n_j):
    i, j = pl.program_id(0), pl.program_id(1)
    n_i = pl.num_programs(0)
    tm = h_ref.shape[0]
    piece = tm // n_j

    def x_copy(tile, p):
        slot = tile % 2
        rows = pl.ds(tile * tm + p * piece, piece)
        return pltpu.make_async_copy(x_hbm.at[rows, :], xbuf.at[slot, pl.ds(p * piece, piece), :], sem.at[slot, p])

    @pl.when((i == 0) & (j == 0))
    def _():
        for p in range(n_j):
            x_copy(0, p).start()

    @pl.when(i + 1 < n_i)
    def _():
        for p in range(n_j):
            @pl.when(j == p)
            def _():
                x_copy(i + 1, p).start()

    @pl.when(j == 0)
    def _():
        for p in range(n_j):
            x_copy(i, p).wait()
        x = xbuf[i % 2]
        ms = jnp.mean(x * x, axis=-1, keepdims=True)
        h_ref[...] = (x * lax.rsqrt(ms + EPS) * gm_ref[...]).astype(h_ref.dtype)

    def matmul(c):
        return jnp.dot(h_ref[c * r_sub:(c + 1) * r_sub, :], w_ref[...], preferred_element_type=F32)

    def epilogue(c, acc):
        for hh in range(o_ref.shape[1] // HEAD_DIM):
            sl = slice(hh * HEAD_DIM, (hh + 1) * HEAD_DIM)
            y = acc[:, sl]
            ms = jnp.mean(y * y, axis=-1, keepdims=True)
            o_ref[c * r_sub:(c + 1) * r_sub, sl] = (y * lax.rsqrt(ms + EPS) * g_ref[:, sl]).astype(o_ref.dtype)

    _software_pipeline(h_ref.shape[0] // r_sub, matmul, epilogue)


def _proj_norm(x2d, g_mix, w, gains, col_block, tm=1024, tn=512, r_sub=256):
    m, d = x2d.shape
    n = gains.shape[0]
    n_j = n // tn
    assert n % tn == 0 and m % tm == 0 and tm % (F32_TILE_ROWS * n_j) == 0
    return pl.pallas_call(
        functools.partial(_proj_norm_kernel, r_sub=r_sub, n_j=n_j),
        grid=(m // tm, n_j),
        in_specs=[pl.BlockSpec(memory_space=pl.ANY),
                  pl.BlockSpec((1, d), lambda i, j: (0, 0)),
                  pl.BlockSpec((d, tn), lambda i, j: (0, col_block(j))),
                  pl.BlockSpec((1, tn), lambda i, j: (0, j))],
        out_specs=[pl.BlockSpec((tm, tn), lambda i, j: (i, j)),
                   pl.BlockSpec((tm, d), lambda i, j: (i, 0))],
        out_shape=[jax.ShapeDtypeStruct((m, n), BF16), jax.ShapeDtypeStruct((m, d), BF16)],
        scratch_shapes=[pltpu.VMEM((2, tm, d), F32), pltpu.SemaphoreType.DMA((2, n_j))],
        compiler_params=_cparams("arbitrary", "arbitrary", vmem_mb=48),
        name="proj_norm",
    )(x2d, g_mix.reshape(1, d), w, gains.reshape(1, n))


def _proj_plain_kernel(h_ref, w_ref, o_ref):
    o_ref[...] = jnp.dot(h_ref[...], w_ref[...], preferred_element_type=F32).astype(o_ref.dtype)


def _proj_plain(h, w, n, col_block, tm=2048):
    m, d = h.shape
    assert m % tm == 0
    return pl.pallas_call(
        _proj_plain_kernel,
        grid=(m // tm,),
        in_specs=[pl.BlockSpec((tm, d), lambda i: (i, 0)),
                  pl.BlockSpec((d, n), lambda i: (0, col_block))],
        out_specs=pl.BlockSpec((tm, n), lambda i: (i, 0)),
        out_shape=jax.ShapeDtypeStruct((m, n), BF16),
        compiler_params=_cparams("parallel"),
        name="proj_plain",
    )(h, w)


def _proj_t_kernel(w_ref, h_ref, gq_ref, gk_ref, cos_ref, sin_ref, *rest, t_sub, n_cast):
    q_ref, k_ref, v_ref = rest[n_cast:n_cast + 3]
    _run_cast_jobs(rest[:n_cast], rest[n_cast + 3:])
    n_q = q_ref.shape[0] // HEAD_DIM
    n_k = k_ref.shape[1] // HEAD_DIM
    n_v = v_ref.shape[0] // HEAD_DIM
    half = HEAD_DIM // 2

    def norm_rope(y, g_ref, tok):
        ms = jnp.mean(y * y, axis=0, keepdims=True)
        y = y * lax.rsqrt(ms + EPS) * g_ref[...]
        partner = jnp.concatenate([y[half:], y[:half]], axis=0)
        return y * cos_ref[:, tok] + partner * sin_ref[:, tok]

    def matmul(c):
        return jnp.dot(h_ref[c * t_sub:(c + 1) * t_sub, :], w_ref[...], preferred_element_type=F32)

    def epilogue(c, y):
        tok = slice(c * t_sub, (c + 1) * t_sub)
        head = lambda hh: y[:, hh * HEAD_DIM:(hh + 1) * HEAD_DIM].T
        for hh in range(n_q):
            q_ref[hh * HEAD_DIM:(hh + 1) * HEAD_DIM, tok] = norm_rope(head(hh), gq_ref, tok).astype(q_ref.dtype)
        for hh in range(n_k):
            kt = norm_rope(head(n_q + hh), gk_ref, tok)
            k_ref[tok, hh * HEAD_DIM:(hh + 1) * HEAD_DIM] = kt.T.astype(k_ref.dtype)
        for hh in range(n_v):
            v_ref[hh * HEAD_DIM:(hh + 1) * HEAD_DIM, tok] = head(n_q + n_k + hh).astype(v_ref.dtype)

    _software_pipeline(h_ref.shape[0] // t_sub, matmul, epilogue)


def _proj_t(h, w, gq_col, gk_col, cos_tt, sin_tt, b, seq, cast_jobs, tm=1024, t_sub=128):
    m, d = h.shape
    n = w.shape[1]
    sb = seq // tm
    col = lambda g: jnp.broadcast_to(g[:, None], (HEAD_DIM, t_sub))
    lane_tile = lambda rows: pl.BlockSpec((None, rows, tm), lambda i: (i // sb, 0, i % sb))
    c_in, c_out, c_shapes = _cast_jobs(cast_jobs, m // tm, lambda i: i)
    return pl.pallas_call(
        functools.partial(_proj_t_kernel, t_sub=t_sub, n_cast=len(cast_jobs)),
        grid=(m // tm,),
        in_specs=[pl.BlockSpec((d, n), lambda i: (0, 0)),
                  pl.BlockSpec((tm, d), lambda i: (i, 0)),
                  pl.BlockSpec((HEAD_DIM, t_sub), lambda i: (0, 0)),
                  pl.BlockSpec((HEAD_DIM, t_sub), lambda i: (0, 0)),
                  pl.BlockSpec((HEAD_DIM, tm), lambda i: (0, i % sb)),
                  pl.BlockSpec((HEAD_DIM, tm), lambda i: (0, i % sb))] + c_in,
        out_specs=[lane_tile(B_QW),
                   pl.BlockSpec((tm, B_KVW), lambda i: (i, 0)),
                   lane_tile(B_KVW)] + c_out,
        out_shape=[jax.ShapeDtypeStruct((b, B_QW, seq), BF16),
                   jax.ShapeDtypeStruct((m, B_KVW), BF16),
                   jax.ShapeDtypeStruct((b, B_KVW, seq), BF16)] + c_shapes,
        compiler_params=_cparams("parallel", vmem_mb=48),
        name="proj_t",
    )(w, h, col(gq_col), col(gk_col), cos_tt, sin_tt, *[src for src, _, _ in cast_jobs])


def _mem_kv_kernel(mem_ref, gm_ref, w_ref, gk_ref, k_ref, v_ref):
    x = mem_ref[...]
    ms = jnp.mean(x * x, axis=-1, keepdims=True)
    hm = (x * lax.rsqrt(ms + EPS) * gm_ref[...]).astype(BF16)
    kv = jnp.dot(hm, w_ref[...], preferred_element_type=F32)
    for hh in range(C_HEADS):
        sl = slice(hh * HEAD_DIM, (hh + 1) * HEAD_DIM)
        y = kv[:, sl]
        ms = jnp.mean(y * y, axis=-1, keepdims=True)
        k_ref[:, sl] = (y * lax.rsqrt(ms + EPS) * gk_ref[...]).astype(BF16)
    v_ref[...] = kv[:, C_W:].astype(BF16)


def _mem_kv(mem2d, g_mem, w_kv, g_kc, tm=256):
    m, d = mem2d.shape
    return pl.pallas_call(
        _mem_kv_kernel,
        grid=(m // tm,),
        in_specs=[pl.BlockSpec((tm, d), lambda i: (i, 0)),
                  pl.BlockSpec((1, d), lambda i: (0, 0)),
                  pl.BlockSpec((d, 2 * C_W), lambda i: (0, 0)),
                  pl.BlockSpec((1, HEAD_DIM), lambda i: (0, 0))],
        out_specs=[pl.BlockSpec((tm, C_W), lambda i: (i, 0))] * 2,
        out_shape=[jax.ShapeDtypeStruct((m, C_W), BF16)] * 2,
        compiler_params=_cparams("parallel"),
        name="mem_kv",
    )(mem2d, g_mem.reshape(1, d), w_kv, g_kc.reshape(1, HEAD_DIM))


def _t5_bucket(rel):
    nb = REL_BUCKETS // 2
    ret = np.where(rel > 0, nb, 0)
    n = np.abs(rel)
    max_exact = nb // 2
    large = max_exact + (np.log(np.maximum(n, 1).astype(np.float32) / np.float32(max_exact))
                         / np.float32(math.log(REL_MAX_DIST / max_exact))
                         * np.float32(nb - max_exact)).astype(np.int32)
    large = np.minimum(large, nb - 1)
    return ret + np.where(n < max_exact, n, large)


def _a_bucket_index():
    qi = np.arange(A_QROWS, dtype=np.int32)[:, None]
    kj = np.arange(A_KWIN, dtype=np.int32)[None, :]
    out = []
    for _, dil in DIL_PAIRS:
        for off in range(A_NOFF):
            rel = kj - qi - A_RADIUS * off
            out.append(np.where(np.abs(rel) <= A_RADIUS, _t5_bucket(rel * dil), -1))
    return np.stack(out).astype(np.int32)


def _a_bias_kernel(tab_ref, bucket_ref, o_ref):
    g = pl.program_id(0) // A_NOFF
    bk = bucket_ref[...]
    for hh in range(A_HEADS_PER_GROUP):
        acc = jnp.full(bk.shape, NEG, F32)
        for b in range(REL_BUCKETS):
            acc = jnp.where(bk == b, tab_ref[b, g * A_HEADS_PER_GROUP + hh], acc)
        o_ref[hh] = acc


def _a_bias(rel_bias):
    n = len(DIL_PAIRS) * A_NOFF
    return pl.pallas_call(
        _a_bias_kernel,
        grid=(n,),
        in_specs=[pl.BlockSpec(memory_space=pltpu.SMEM),
                  pl.BlockSpec((None, A_QROWS, A_KWIN), lambda i: (i, 0, 0))],
        out_specs=pl.BlockSpec((None, A_HEADS_PER_GROUP, A_QROWS, A_KWIN), lambda i: (i, 0, 0, 0)),
        out_shape=jax.ShapeDtypeStruct((n, A_HEADS_PER_GROUP, A_QROWS, A_KWIN), F32),
        compiler_params=_cparams("arbitrary"),
        name="a_bias",
    )(rel_bias, _a_bucket_index())


def _mixer_a_kernel(q_ref, k_ref, v_ref, bias_ref, oa_ref, stage, qstage, o_acc, lse_acc, *residue_kv, seq):
    t_rows = q_ref.shape[0]
    ti = pl.program_id(1)
    n_groups = len(DIL_PAIRS)

    def softmax(s):
        m = jnp.max(s, axis=-1, keepdims=True)
        p = jnp.exp(s - m)
        l = jnp.sum(p, axis=-1, keepdims=True)
        return p.astype(BF16), l, m + jnp.log(l)

    def run_group(gi, dil, kres, vres):
        sub_len = seq // dil
        lq = t_rows // dil
        gcols = lambda hh: slice(gi * A_OUT_W + hh * HEAD_DIM, gi * A_OUT_W + (hh + 1) * HEAD_DIM)
        if dil > 1:
            @pl.when(ti == 0)
            def _():
                for src, dst in ((k_ref, kres), (v_ref, vres)):
                    for hh in range(A_HEADS_PER_GROUP):
                        stage[...] = src[:, gcols(hh)].astype(F32)
                        for r in range(dil):
                            dst[r, :, hh * HEAD_DIM:(hh + 1) * HEAD_DIM] = (
                                stage[pl.ds(r, sub_len, stride=dil), :].astype(BF16))

            for hh in range(A_HEADS_PER_GROUP):
                qstage[hh] = q_ref[:, gcols(hh)].astype(F32)

        def scores(r, i, hh):
            q0 = ti * lq + i * A_QROWS
            ks = jnp.clip(q0 - A_RADIUS, 0, sub_len - A_KWIN)
            off = lax.shift_right_logical(q0 - ks, int(math.log2(A_RADIUS)))
            ks = pl.multiple_of(ks, A_RADIUS)
            if dil > 1:
                cols = slice(hh * HEAD_DIM, (hh + 1) * HEAD_DIM)
                rows = pl.ds(i * A_QROWS * dil + r, A_QROWS, stride=dil)
                q = qstage[hh, rows, :].astype(BF16)
                k = kres[r, pl.ds(ks, A_KWIN), cols]
                v = vres[r, pl.ds(ks, A_KWIN), cols]
            else:
                rows = pl.ds(i * A_QROWS, A_QROWS)
                q = q_ref[rows, gcols(hh)]
                k = k_ref[pl.ds(ks, A_KWIN), gcols(hh)]
                v = v_ref[pl.ds(ks, A_KWIN), gcols(hh)]
            s = lax.dot_general(q, k, _NT, preferred_element_type=F32) * SCALE + bias_ref[gi * A_NOFF + off, hh]
            return rows, s, v

        def fold(hh, rows, o, lse):
            lse = jnp.broadcast_to(lse, (A_QROWS, HEAD_DIM))
            if gi > 0:
                prev_o, prev_lse = o_acc[hh, rows, :], lse_acc[hh, rows, :]
                m = jnp.maximum(prev_lse, lse)
                w_prev, w_new = jnp.exp(prev_lse - m), jnp.exp(lse - m)
                den = w_prev + w_new
                o = (w_prev * prev_o + w_new * o) / den
                lse = m + jnp.log(den)
            o_acc[hh, rows, :] = o
            if gi + 1 < n_groups:
                lse_acc[hh, rows, :] = lse

        items = [(r, i, hh) for r in range(dil) for i in range(lq // A_QROWS) for hh in range(A_HEADS_PER_GROUP)]
        for b0 in range(0, len(items), A_BATCH):
            batch = items[b0:b0 + A_BATCH]
            staged = [scores(*it) for it in batch]
            probs = [softmax(s) for _, s, _ in staged]
            for (_, _, hh), (rows, _, v), (p, l, lse) in zip(batch, staged, probs):
                fold(hh, rows, jnp.dot(p, v, preferred_element_type=F32) / l, lse)

    strided = [gi for gi, (_, dil) in enumerate(DIL_PAIRS) if dil > 1]
    for gi, (_, dil) in enumerate(DIL_PAIRS):
        kres, vres = (residue_kv[2 * strided.index(gi):2 * strided.index(gi) + 2] if dil > 1 else (None, None))
        run_group(gi, dil, kres, vres)
    for hh in range(A_HEADS_PER_GROUP):
        oa_ref[:, hh * HEAD_DIM:(hh + 1) * HEAD_DIM] = o_acc[hh].astype(oa_ref.dtype)


def _mixer_a(qkn, va, bias_a, t_rows=2048):
    b, seq, _ = qkn.shape
    head_buf = lambda rows: pltpu.VMEM((A_HEADS_PER_GROUP, rows, HEAD_DIM), F32)
    scratch = [pltpu.VMEM((seq, HEAD_DIM), F32), head_buf(t_rows), head_buf(t_rows), head_buf(t_rows)]
    for _, dil in DIL_PAIRS:
        if dil > 1:
            scratch += [pltpu.VMEM((dil, seq // dil, A_OUT_W), BF16)] * 2
    return pl.pallas_call(
        functools.partial(_mixer_a_kernel, seq=seq),
        grid=(b, seq // t_rows),
        in_specs=[pl.BlockSpec((None, t_rows, A_W), lambda bi, ti: (bi, ti, 0)),
                  pl.BlockSpec((None, seq, A_W), lambda bi, ti: (bi, 0, 1)),
                  pl.BlockSpec((None, seq, A_W), lambda bi, ti: (bi, 0, 0)),
                  pl.BlockSpec(bias_a.shape, lambda bi, ti: (0, 0, 0, 0), pipeline_mode=pl.Buffered(1))],
        out_specs=pl.BlockSpec((None, t_rows, A_OUT_W), lambda bi, ti: (bi, ti, 0)),
        out_shape=jax.ShapeDtypeStruct((b, seq, A_OUT_W), BF16),
        scratch_shapes=scratch,
        compiler_params=_cparams("parallel", "arbitrary", vmem_mb=62),
        name="mixer_a",
    )(qkn, qkn, va, bias_a)


def _mixer_b_kernel(qt_ref, k_ref, vt_ref, *rest, tk, group, w, n_cast):
    o_ref, s_scr = rest[n_cast], rest[-1]
    _run_cast_jobs(rest[:n_cast], rest[n_cast + 1:-1])
    tq = qt_ref.shape[1]
    seq = k_ref.shape[0]
    n_chunks = seq // tk
    units = [(i, j) for i in range(group) for j in range(tq // w)]

    def pass_a(u, ci, m):
        i, j = units[u]
        qt = qt_ref[i * HEAD_DIM:(i + 1) * HEAD_DIM, j * w:(j + 1) * w]
        st = jnp.dot(k_ref[ci * tk:(ci + 1) * tk, :], qt, preferred_element_type=F32)
        s_scr[u % 2, ci * tk:(ci + 1) * tk, :] = st
        return jnp.maximum(m, jnp.max(st, axis=0, keepdims=True))

    def pass_b(u, ci, m, l, acc):
        pt = jnp.exp2(s_scr[u % 2, ci * tk:(ci + 1) * tk, :] - m)
        l = l + jnp.sum(pt, axis=0, keepdims=True)
        acc = acc + jnp.dot(vt_ref[:, ci * tk:(ci + 1) * tk], pt.astype(BF16), preferred_element_type=F32)
        return l, acc

    m_prev = None
    for s in range(len(units) + 1):
        m_cur = jnp.full((1, w), NEG, F32)
        l = jnp.zeros((1, w), F32)
        acc = jnp.zeros((HEAD_DIM, w), F32)
        for ci in range(n_chunks):
            if s < len(units):
                m_cur = pass_a(s, ci, m_cur)
            if s > 0:
                l, acc = pass_b(s - 1, ci, m_prev, l, acc)
        if s > 0:
            i, j = units[s - 1]
            o_ref[j * w:(j + 1) * w, i * HEAD_DIM:(i + 1) * HEAD_DIM] = (acc / l).T.astype(o_ref.dtype)
        m_prev = m_cur


def _mixer_b(qbt, kb, vbt, cast_jobs, tq=512, tk=512, w=256):
    b, seq, _ = kb.shape
    group = B_Q_HEADS // B_KV_HEADS
    gw = group * HEAD_DIM
    nq = seq // tq
    n_steps = b * B_KV_HEADS * nq
    c_in, c_out, c_shapes = _cast_jobs(cast_jobs, n_steps, lambda bi, kv, qi: (bi * B_KV_HEADS + kv) * nq + qi)
    return pl.pallas_call(
        functools.partial(_mixer_b_kernel, tk=tk, group=group, w=w, n_cast=len(cast_jobs)),
        grid=(b, B_KV_HEADS, nq),
        scratch_shapes=[pltpu.VMEM((2, seq, w), F32)],
        in_specs=[pl.BlockSpec((None, gw, tq), lambda bi, kv, qi: (bi, kv, qi)),
                  pl.BlockSpec((None, seq, HEAD_DIM), lambda bi, kv, qi: (bi, 0, kv)),
                  pl.BlockSpec((None, HEAD_DIM, seq), lambda bi, kv, qi: (bi, kv, 0))] + c_in,
        out_specs=[pl.BlockSpec((None, tq, gw), lambda bi, kv, qi: (bi, qi, kv))] + c_out,
        out_shape=[jax.ShapeDtypeStruct((b, seq, B_QW), BF16)] + c_shapes,
        compiler_params=_cparams("parallel", "parallel", "arbitrary"),
        name="mixer_b",
    )(qbt, kb, vbt, *[src for src, _, _ in cast_jobs])


def _mixer_c_kernel(q_ref, k_ref, v_ref, o_ref):
    for hh in range(C_HEADS):
        sl = slice(hh * HEAD_DIM, (hh + 1) * HEAD_DIM)
        s = lax.dot_general(q_ref[:, sl], k_ref[:, sl], _NT, preferred_element_type=F32) * SCALE
        m = jnp.max(s, axis=-1, keepdims=True)
        p = jnp.exp(s - m)
        l = jnp.sum(p, axis=-1, keepdims=True)
        o = jnp.dot(p.astype(BF16), v_ref[:, sl], preferred_element_type=F32) / l
        o_ref[:, sl] = o.astype(o_ref.dtype)


def _mixer_c(qkn, kc, vc, tq=1024):
    b, seq, _ = qkn.shape
    n_mem = kc.shape[1]
    qc_blk = (2 * A_W) // C_W
    return pl.pallas_call(
        _mixer_c_kernel,
        grid=(b, seq // tq),
        in_specs=[pl.BlockSpec((None, tq, C_W), lambda bi, qi: (bi, qi, qc_blk)),
                  pl.BlockSpec((None, n_mem, C_W), lambda bi, qi: (bi, 0, 0)),
                  pl.BlockSpec((None, n_mem, C_W), lambda bi, qi: (bi, 0, 0))],
        out_specs=pl.BlockSpec((None, tq, C_W), lambda bi, qi: (bi, qi, 0)),
        out_shape=jax.ShapeDtypeStruct((b, seq, C_W), BF16),
        compiler_params=_cparams("parallel", "arbitrary"),
        name="mixer_c",
    )(qkn, kc, vc)


def _merge_kernel(h_ref, wg0_ref, wg1_ref, wg2_ref, oa_ref, ob_ref, oc_ref, wa_ref, wb_ref, wc_ref, out_ref):
    h = h_ref[...]
    ga = jax.nn.sigmoid(jnp.dot(h, wg0_ref[...], preferred_element_type=F32))
    merged = ga * jnp.dot(oa_ref[...], wa_ref[...], preferred_element_type=F32)
    gb = jax.nn.sigmoid(jnp.dot(h, wg1_ref[...], preferred_element_type=F32))
    merged += gb * jnp.dot(ob_ref[...], wb_ref[...], preferred_element_type=F32)
    gc = jax.nn.sigmoid(jnp.dot(h, wg2_ref[...], preferred_element_type=F32))
    merged += gc * jnp.dot(oc_ref[...], wc_ref[...], preferred_element_type=F32)
    out_ref[...] = merged.astype(out_ref.dtype)


def _merge(h, w_gates, oa, ob, oc, wa, wb, wc, tm=1024, tn=512):
    m, d = h.shape
    nj = d // tn
    row = lambda w: pl.BlockSpec((tm, w), lambda i, j: (i, 0))
    col = lambda k: pl.BlockSpec((k, tn), lambda i, j: (0, j))
    return pl.pallas_call(
        _merge_kernel,
        grid=(m // tm, nj),
        in_specs=[row(d), col(d), col(d), col(d), row(A_OUT_W), row(B_QW), row(C_W),
                  col(A_OUT_W), col(B_QW), col(C_W)],
        out_specs=pl.BlockSpec((tm, tn), lambda i, j: (i, j)),
        out_shape=jax.ShapeDtypeStruct((m, d), BF16),
        compiler_params=_cparams("parallel", "arbitrary", vmem_mb=48),
        name="merge",
    )(h, *w_gates, oa, ob, oc, wa, wb, wc)


def _out_proj_kernel(mg_ref, w_ref, x_ref, g_ref, x1_ref, h2_ref, *, r_sub):
    def matmul(c):
        return jnp.dot(mg_ref[c * r_sub:(c + 1) * r_sub, :], w_ref[...], preferred_element_type=F32)

    def epilogue(c, acc):
        rows = slice(c * r_sub, (c + 1) * r_sub)
        x1 = x_ref[rows, :] + acc
        x1_ref[rows, :] = x1
        ms = jnp.mean(x1 * x1, axis=-1, keepdims=True)
        h2_ref[rows, :] = (x1 * lax.rsqrt(ms + EPS) * g_ref[...]).astype(h2_ref.dtype)

    _software_pipeline(mg_ref.shape[0] // r_sub, matmul, epilogue)


def _out_proj(merged, w_o, x2d, g_ffn, tm=512, r_sub=512):
    m, d = x2d.shape
    return pl.pallas_call(
        functools.partial(_out_proj_kernel, r_sub=r_sub),
        grid=(m // tm,),
        in_specs=[pl.BlockSpec((tm, d), lambda i: (i, 0)),
                  pl.BlockSpec((d, d), lambda i: (0, 0)),
                  pl.BlockSpec((tm, d), lambda i: (i, 0)),
                  pl.BlockSpec((1, d), lambda i: (0, 0))],
        out_specs=[pl.BlockSpec((tm, d), lambda i: (i, 0))] * 2,
        out_shape=[jax.ShapeDtypeStruct((m, d), F32), jax.ShapeDtypeStruct((m, d), BF16)],
        compiler_params=_cparams("parallel"),
        name="out_proj",
    )(merged, w_o, x2d, g_ffn.reshape(1, d))


def _ffn_kernel(h_ref, wa_ref, wb_ref, wo_ref, x1_ref, out_ref, *, n_slabs):
    f = pl.program_id(1)
    slab_w = x1_ref.shape[1]

    @pl.when(f == 0)
    def _():
        out_ref[...] = jnp.zeros_like(out_ref)

    for s in range(n_slabs):
        @pl.when(f == s)
        def _():
            out_ref[:, s * slab_w:(s + 1) * slab_w] += x1_ref[...]

    h = h_ref[...]
    a = jnp.dot(h, wa_ref[...], preferred_element_type=F32)
    b = jnp.dot(h, wb_ref[...], preferred_element_type=F32)
    act = (a * jax.nn.sigmoid(a) * b).astype(BF16)
    out_ref[...] += jnp.dot(act, wo_ref[...], preferred_element_type=F32)


def _ffn(h2, w_a, w_b, w_out, x1, tm=1024, tf=512, slab_w=256):
    m, d = h2.shape
    d_ff = w_out.shape[0]
    nf = d_ff // tf
    n_slabs = d // slab_w
    assert n_slabs <= nf
    return pl.pallas_call(
        functools.partial(_ffn_kernel, n_slabs=n_slabs),
        grid=(m // tm, nf),
        in_specs=[pl.BlockSpec((tm, d), lambda i, f: (i, 0)),
                  pl.BlockSpec((d, tf), lambda i, f: (0, f)),
                  pl.BlockSpec((d, tf), lambda i, f: (0, f)),
                  pl.BlockSpec((tf, d), lambda i, f: (f, 0)),
                  pl.BlockSpec((tm, slab_w), lambda i, f: (i, jnp.minimum(f, n_slabs - 1)))],
        out_specs=pl.BlockSpec((tm, d), lambda i, f: (i, 0)),
        out_shape=jax.ShapeDtypeStruct((m, d), F32),
        compiler_params=_cparams("parallel", "arbitrary", vmem_mb=48),
        name="ffn",
    )(h2, w_a, w_b, w_out, x1)


def _deinterleave_cols(w, heads):
    d = w.shape[0]
    return w.reshape(d, heads, HEAD_DIM // 2, 2).transpose(0, 1, 3, 2).reshape(d, heads * HEAD_DIM)


def _rope_tables(seq):
    rows = seq // GRID_W
    r = np.repeat(np.arange(rows), GRID_W).astype(np.float64)
    c = np.tile(np.arange(GRID_W), rows).astype(np.float64)
    nf = HEAD_DIM // 4
    inv = ROPE_THETA ** (-np.arange(nf, dtype=np.float64) / nf)
    ang = np.concatenate([r[:, None] * inv, c[:, None] * inv], axis=-1)
    cos, sin = np.cos(ang).T, np.sin(ang).T
    return (np.concatenate([cos, cos], axis=0).astype(np.float32),
            np.concatenate([-sin, sin], axis=0).astype(np.float32))


def _layer(x2d, mem2d, bias_a, cos_t, sin_t, b, seq, g_mix, w_in, g_qa, g_ka, g_qb, g_kb, g_mem, w_mem_kv,
           g_qc, g_kc, w_br_a, w_br_b, w_br_c, w_o, g_ffn, w_ffn_in, w_ffn_out):
    d = x2d.shape[1]
    o_qa, o_ka, o_va, o_qb, o_kb, o_vb, o_qc, o_gt = np.cumsum(
        (0, A_W, A_W, A_W, B_QW, B_KVW, B_KVW, C_W))
    d_ff = w_ffn_out.shape[0]
    w_all = _cast_cols(w_in, int(o_gt))
    seg = lambda lo, hi: w_all[:, lo:hi]
    w_b = jnp.concatenate([_deinterleave_cols(seg(o_qb, o_kb), B_Q_HEADS),
                           _deinterleave_cols(seg(o_kb, o_vb), B_KV_HEADS),
                           seg(o_vb, o_qc)], axis=1)
    g_norm = jnp.concatenate([jnp.tile(g_qa, A_HEADS), jnp.tile(g_ka, A_HEADS), jnp.tile(g_qc, C_HEADS)])
    deint = lambda g: g.reshape(HEAD_DIM // 2, 2).T.reshape(HEAD_DIM)

    n_qk_tiles = (2 * A_W) // C_W
    qkn, h = _proj_norm(x2d, g_mix, w_all, g_norm,
                        lambda j: jnp.where(j < n_qk_tiles, j, int(o_qc) // C_W), tn=C_W)
    qkn = qkn.reshape(b, seq, -1)
    va = _proj_plain(h, w_all, A_W, int(o_va) // A_W).reshape(b, seq, -1)
    late_weights = [(w_ffn_out, 0, d), (w_o, 0, d), (w_br_a, 0, d), (w_br_b, 0, d), (w_br_c, 0, d)]
    qbt, kb, vbt, w_out_b, w_o_b, wa_b, wb_b, wc_b = _proj_t(
        h, w_b, deint(g_qb) * (SCALE * LOG2E), deint(g_kb), cos_t, sin_t, b, seq, late_weights)
    kb = kb.reshape(b, seq, B_KVW)

    oa = _mixer_a(qkn, va, bias_a).reshape(b * seq, A_OUT_W)
    gate_jobs = [(w_in, int(o_gt) // d + br, d) for br in range(N_BRANCH)]
    ob, w_ffa, w_ffb, *w_gates = _mixer_b(qbt, kb, vbt, [(w_ffn_in, 0, d_ff), (w_ffn_in, 1, d_ff)] + gate_jobs)
    ob = ob.reshape(b * seq, B_QW)

    kc, vc = _mem_kv(mem2d, g_mem, w_mem_kv.astype(BF16), g_kc)
    n_mem = mem2d.shape[0] // b
    oc = _mixer_c(qkn, kc.reshape(b, n_mem, C_W), vc.reshape(b, n_mem, C_W)).reshape(b * seq, C_W)

    merged = _merge(h, w_gates, oa, ob, oc, wa_b, wb_b, wc_b)
    x1, h2 = _out_proj(merged, w_o_b, x2d, g_ffn)
    return _ffn(h2, w_ffa, w_ffb, w_out_b, x1)


def kernel(x, mem, rel_bias, g_mix, w_in, g_qa, g_ka, g_qb, g_kb, g_mem, w_mem_kv, g_qc, g_kc,
           w_br_a, w_br_b, w_br_c, w_o, g_ffn, w_ffn_in, w_ffn_out):
    b, seq, d = x.shape
    depth = w_in.shape[0]
    cos_t, sin_t = _rope_tables(seq)
    bias_a = _a_bias(rel_bias)
    x2d = x.reshape(b * seq, d)
    mem2d = mem.reshape(-1, d)
    for layer in range(depth):
        x2d = _layer(x2d, mem2d, bias_a, cos_t, sin_t, b, seq,
                     g_mix[layer], w_in[layer], g_qa[layer], g_ka[layer], g_qb[layer], g_kb[layer],
                     g_mem[layer], w_mem_kv[layer], g_qc[layer], g_kc[layer],
                     w_br_a[layer], w_br_b[layer], w_br_c[layer], w_o[layer], g_ffn[layer],
                     w_ffn_in[layer], w_ffn_out[layer])
    return x2d.reshape(b, seq, d)
```

```python
import functools
import math

import numpy as np
import jax
import jax.numpy as jnp
from jax import lax
from jax.experimental import pallas as pl
from jax.experimental.pallas import tpu as pltpu

HEAD_DIM = 128
GRID_W = 64
DIL_PAIRS = ((128, 1), (512, 4), (2048, 16))
A_HEADS_PER_GROUP = 2
A_HEADS = A_HEADS_PER_GROUP * len(DIL_PAIRS)
B_Q_HEADS = 6
B_KV_HEADS = 2
ROPE_THETA = 10000.0
C_HEADS = 4
N_BRANCH = 3
REL_BUCKETS = 32
REL_MAX_DIST = 1024
EPS = 1e-6
NEG = -1e30

A_W = A_HEADS * HEAD_DIM
A_OUT_W = A_HEADS_PER_GROUP * HEAD_DIM
B_QW = B_Q_HEADS * HEAD_DIM
B_KVW = B_KV_HEADS * HEAD_DIM
C_W = C_HEADS * HEAD_DIM

SCALE = 1.0 / math.sqrt(HEAD_DIM)
LOG2E = math.log2(math.e)

A_QROWS = 128
A_KWIN = 256
A_RADIUS = 64
A_NOFF = 3
A_BATCH = 8

BF16 = jnp.bfloat16
F32 = jnp.float32
BF16_TILE_ROWS = 16
F32_TILE_ROWS = 8

_NT = (((1,), (1,)), ((), ()))


def _software_pipeline(n_chunks, matmul, epilogue):
    acc = matmul(0)
    for c in range(n_chunks):
        nxt = matmul(c + 1) if c + 1 < n_chunks else None
        epilogue(c, acc)
        acc = nxt


def _cast_jobs(jobs, n_steps, step_index):
    in_specs, out_specs, out_shapes = [], [], []
    for src, col_block, width in jobs:
        rows = src.shape[0]
        rt = rows // n_steps
        assert rt * n_steps == rows and rt % BF16_TILE_ROWS == 0 and src.shape[1] % width == 0
        in_specs.append(pl.BlockSpec((rt, width), lambda *g, cb=col_block: (step_index(*g), cb)))
        out_specs.append(pl.BlockSpec((rt, width), lambda *g: (step_index(*g), 0)))
        out_shapes.append(jax.ShapeDtypeStruct((rows, width), BF16))
    return in_specs, out_specs, out_shapes


def _run_cast_jobs(src_refs, dst_refs):
    for src, dst in zip(src_refs, dst_refs):
        dst[...] = src[...].astype(dst.dtype)


def _cast_kernel(src_ref, dst_ref):
    dst_ref[...] = src_ref[...].astype(dst_ref.dtype)


def _cast_cols(w, n_cols, row_tile=256):
    rows = w.shape[0]
    assert rows % row_tile == 0
    return pl.pallas_call(
        _cast_kernel,
        grid=(rows // row_tile,),
        in_specs=[pl.BlockSpec((row_tile, n_cols), lambda i: (i, 0))],
        out_specs=pl.BlockSpec((row_tile, n_cols), lambda i: (i, 0)),
        out_shape=jax.ShapeDtypeStruct((rows, n_cols), BF16),
        compiler_params=_cparams("parallel"),
        name="cast_head_weights",
    )(w)


def _cparams(*sem, vmem_mb=None):
    limit = None if vmem_mb is None else vmem_mb * 1024 * 1024
    return pltpu.CompilerParams(dimension_semantics=sem, vmem_limit_bytes=limit)


def _proj_norm_kernel(x_hbm, gm_ref, w_ref, g_ref, o_ref, h_ref, xbuf, sem, *, r_sub, n_j):
    i, j = pl.program_id(0), pl.program_id(1)
    n_i = pl.num_programs(0)
    tm = h_ref.shape[0]
    piece = tm // n_j

    def x_copy(tile, p):
        slot = tile % 2
        rows = pl.ds(tile * tm + p * piece, piece)
        return pltpu.make_async_copy(x_hbm.at[rows, :], xbuf.at[slot, pl.ds(p * piece, piece), :], sem.at[slot, p])

    @pl.when((i == 0) & (j == 0))
    def _():
        for p in range(n_j):
            x_copy(0, p).start()

    @pl.when(i + 1 < n_i)
    def _():
        for p in range(n_j):
            @pl.when(j == p)
            def _():
                x_copy(i + 1, p).start()

    @pl.when(j == 0)
    def _():
        for p in range(n_j):
            x_copy(i, p).wait()
        x = xbuf[i % 2]
        ms = jnp.mean(x * x, axis=-1, keepdims=True)
        h_ref[...] = (x * lax.rsqrt(ms + EPS) * gm_ref[...]).astype(h_ref.dtype)

    def matmul(c):
        return jnp.dot(h_ref[c * r_sub:(c + 1) * r_sub, :], w_ref[...], preferred_element_type=F32)

    def epilogue(c, acc):
        for hh in range(o_ref.shape[1] // HEAD_DIM):
            sl = slice(hh * HEAD_DIM, (hh + 1) * HEAD_DIM)
            y = acc[:, sl]
            ms = jnp.mean(y * y, axis=-1, keepdims=True)
            o_ref[c * r_sub:(c + 1) * r_sub, sl] = (y * lax.rsqrt(ms + EPS) * g_ref[:, sl]).astype(o_ref.dtype)

    _software_pipeline(h_ref.shape[0] // r_sub, matmul, epilogue)


def _proj_norm(x2d, g_mix, w, gains, col_block, tm=1024, tn=512, r_sub=256):
    m, d = x2d.shape
    n = gains.shape[0]
    n_j = n // tn
    assert n % tn == 0 and m % tm == 0 and tm % (F32_TILE_ROWS * n_j) == 0
    return pl.pallas_call(
        functools.partial(_proj_norm_kernel, r_sub=r_sub, n_j=n_j),
        grid=(m // tm, n_j),
        in_specs=[pl.BlockSpec(memory_space=pl.ANY),
                  pl.BlockSpec((1, d), lambda i, j: (0, 0)),
                  pl.BlockSpec((d, tn), lambda i, j: (0, col_block(j))),
                  pl.BlockSpec((1, tn), lambda i, j: (0, j))],
        out_specs=[pl.BlockSpec((tm, tn), lambda i, j: (i, j)),
                   pl.BlockSpec((tm, d), lambda i, j: (i, 0))],
        out_shape=[jax.ShapeDtypeStruct((m, n), BF16), jax.ShapeDtypeStruct((m, d), BF16)],
        scratch_shapes=[pltpu.VMEM((2, tm, d), F32), pltpu.SemaphoreType.DMA((2, n_j))],
        compiler_params=_cparams("arbitrary", "arbitrary", vmem_mb=48),
        name="proj_norm",
    )(x2d, g_mix.reshape(1, d), w, gains.reshape(1, n))


def _proj_t_kernel(w_ref, h_ref, gq_ref, gk_ref, cos_ref, sin_ref, *rest, t_sub, n_cast):
    q_ref, k_ref, v_ref, va_ref = rest[n_cast:n_cast + 4]
    _run_cast_jobs(rest[:n_cast], rest[n_cast + 4:])
    n_q = q_ref.shape[0] // HEAD_DIM
    n_k = k_ref.shape[1] // HEAD_DIM
    n_v = v_ref.shape[0] // HEAD_DIM
    half = HEAD_DIM // 2

    def norm_rope(y, g_ref, tok):
        ms = jnp.mean(y * y, axis=0, keepdims=True)
        y = y * lax.rsqrt(ms + EPS) * g_ref[...]
        partner = jnp.concatenate([y[half:], y[:half]], axis=0)
        return y * cos_ref[:, tok] + partner * sin_ref[:, tok]

    def matmul(c):
        return jnp.dot(h_ref[c * t_sub:(c + 1) * t_sub, :], w_ref[...], preferred_element_type=F32)

    def epilogue(c, y):
        tok = slice(c * t_sub, (c + 1) * t_sub)
        head = lambda hh: y[:, hh * HEAD_DIM:(hh + 1) * HEAD_DIM].T
        for hh in range(n_q):
            q_ref[hh * HEAD_DIM:(hh + 1) * HEAD_DIM, tok] = norm_rope(head(hh), gq_ref, tok).astype(q_ref.dtype)
        for hh in range(n_k):
            kt = norm_rope(head(n_q + hh), gk_ref, tok)
            k_ref[tok, hh * HEAD_DIM:(hh + 1) * HEAD_DIM] = kt.T.astype(k_ref.dtype)
        for hh in range(n_v):
            v_ref[hh * HEAD_DIM:(hh + 1) * HEAD_DIM, tok] = head(n_q + n_k + hh).astype(v_ref.dtype)
        va_ref[tok, :] = y[:, (n_q + n_k + n_v) * HEAD_DIM:].astype(va_ref.dtype)

    _software_pipeline(h_ref.shape[0] // t_sub, matmul, epilogue)


def _proj_t(h, w, gq_col, gk_col, cos_tt, sin_tt, b, seq, cast_jobs, tm=1024, t_sub=128):
    m, d = h.shape
    n = w.shape[1]
    assert n == B_QW + 2 * B_KVW + A_W
    sb = seq // tm
    col = lambda g: jnp.broadcast_to(g[:, None], (HEAD_DIM, t_sub))
    lane_tile = lambda rows: pl.BlockSpec((None, rows, tm), lambda i: (i // sb, 0, i % sb))
    c_in, c_out, c_shapes = _cast_jobs(cast_jobs, m // tm, lambda i: i)
    return pl.pallas_call(
        functools.partial(_proj_t_kernel, t_sub=t_sub, n_cast=len(cast_jobs)),
        grid=(m // tm,),
        in_specs=[pl.BlockSpec((d, n), lambda i: (0, 0)),
                  pl.BlockSpec((tm, d), lambda i: (i, 0)),
                  pl.BlockSpec((HEAD_DIM, t_sub), lambda i: (0, 0)),
                  pl.BlockSpec((HEAD_DIM, t_sub), lambda i: (0, 0)),
                  pl.BlockSpec((HEAD_DIM, tm), lambda i: (0, i % sb)),
                  pl.BlockSpec((HEAD_DIM, tm), lambda i: (0, i % sb))] + c_in,
        out_specs=[lane_tile(B_QW),
                   pl.BlockSpec((tm, B_KVW), lambda i: (i, 0)),
                   lane_tile(B_KVW),
                   pl.BlockSpec((tm, A_W), lambda i: (i, 0))] + c_out,
        out_shape=[jax.ShapeDtypeStruct((b, B_QW, seq), BF16),
                   jax.ShapeDtypeStruct((m, B_KVW), BF16),
                   jax.ShapeDtypeStruct((b, B_KVW, seq), BF16),
                   jax.ShapeDtypeStruct((m, A_W), BF16)] + c_shapes,
        compiler_params=_cparams("parallel", vmem_mb=56),
        name="proj_t",
    )(w, h, col(gq_col), col(gk_col), cos_tt, sin_tt, *[src for src, _, _ in cast_jobs])


def _mem_kv_kernel(mem_ref, gm_ref, w_ref, gk_ref, k_ref, v_ref):
    x = mem_ref[...]
    ms = jnp.mean(x * x, axis=-1, keepdims=True)
    hm = (x * lax.rsqrt(ms + EPS) * gm_ref[...]).astype(BF16)
    kv = jnp.dot(hm, w_ref[...], preferred_element_type=F32)
    for hh in range(C_HEADS):
        sl = slice(hh * HEAD_DIM, (hh + 1) * HEAD_DIM)
        y = kv[:, sl]
        ms = jnp.mean(y * y, axis=-1, keepdims=True)
        k_ref[:, sl] = (y * lax.rsqrt(ms + EPS) * gk_ref[...]).astype(BF16)
    v_ref[...] = kv[:, C_W:].astype(BF16)


def _mem_kv(mem2d, g_mem, w_kv, g_kc, tm=256):
    m, d = mem2d.shape
    return pl.pallas_call(
        _mem_kv_kernel,
        grid=(m // tm,),
        in_specs=[pl.BlockSpec((tm, d), lambda i: (i, 0)),
                  pl.BlockSpec((1, d), lambda i: (0, 0)),
                  pl.BlockSpec((d, 2 * C_W), lambda i: (0, 0)),
                  pl.BlockSpec((1, HEAD_DIM), lambda i: (0, 0))],
        out_specs=[pl.BlockSpec((tm, C_W), lambda i: (i, 0))] * 2,
        out_shape=[jax.ShapeDtypeStruct((m, C_W), BF16)] * 2,
        compiler_params=_cparams("parallel"),
        name="mem_kv",
    )(mem2d, g_mem.reshape(1, d), w_kv, g_kc.reshape(1, HEAD_DIM))


def _t5_bucket(rel):
    nb = REL_BUCKETS // 2
    ret = np.where(rel > 0, nb, 0)
    n = np.abs(rel)
    max_exact = nb // 2
    large = max_exact + (np.log(np.maximum(n, 1).astype(np.float32) / np.float32(max_exact))
                         / np.float32(math.log(REL_MAX_DIST / max_exact))
                         * np.float32(nb - max_exact)).astype(np.int32)
    large = np.minimum(large, nb - 1)
    return ret + np.where(n < max_exact, n, large)


def _a_bucket_index():
    qi = np.arange(A_QROWS, dtype=np.int32)[:, None]
    kj = np.arange(A_KWIN, dtype=np.int32)[None, :]
    out = []
    for _, dil in DIL_PAIRS:
        for off in range(A_NOFF):
            rel = kj - qi - A_RADIUS * off
            out.append(np.where(np.abs(rel) <= A_RADIUS, _t5_bucket(rel * dil), -1))
    return np.stack(out).astype(np.int32)


def _a_bias_kernel(tab_ref, bucket_ref, o_ref):
    g = pl.program_id(0) // A_NOFF
    bk = bucket_ref[...]
    for hh in range(A_HEADS_PER_GROUP):
        acc = jnp.full(bk.shape, NEG, F32)
        for b in range(REL_BUCKETS):
            acc = jnp.where(bk == b, tab_ref[b, g * A_HEADS_PER_GROUP + hh], acc)
        o_ref[hh] = acc


def _a_bias(rel_bias):
    n = len(DIL_PAIRS) * A_NOFF
    return pl.pallas_call(
        _a_bias_kernel,
        grid=(n,),
        in_specs=[pl.BlockSpec(memory_space=pltpu.SMEM),
                  pl.BlockSpec((None, A_QROWS, A_KWIN), lambda i: (i, 0, 0))],
        out_specs=pl.BlockSpec((None, A_HEADS_PER_GROUP, A_QROWS, A_KWIN), lambda i: (i, 0, 0, 0)),
        out_shape=jax.ShapeDtypeStruct((n, A_HEADS_PER_GROUP, A_QROWS, A_KWIN), F32),
        compiler_params=_cparams("arbitrary"),
        name="a_bias",
    )(rel_bias, _a_bucket_index())


def _mixer_a_kernel(q_ref, k_ref, v_ref, bias_ref, oa_ref, stage, qstage, o_acc, lse_acc, *residue_kv, seq):
    t_rows = q_ref.shape[0]
    ti = pl.program_id(1)
    n_groups = len(DIL_PAIRS)

    def softmax(s):
        m = jnp.max(s, axis=-1, keepdims=True)
        p = jnp.exp(s - m)
        l = jnp.sum(p, axis=-1, keepdims=True)
        return p.astype(BF16), l, m + jnp.log(l)

    def run_group(gi, dil, kres, vres):
        sub_len = seq // dil
        lq = t_rows // dil
        gcols = lambda hh: slice(gi * A_OUT_W + hh * HEAD_DIM, gi * A_OUT_W + (hh + 1) * HEAD_DIM)
        if dil > 1:
            @pl.when(ti == 0)
            def _():
                for src, dst in ((k_ref, kres), (v_ref, vres)):
                    for hh in range(A_HEADS_PER_GROUP):
                        stage[...] = src[:, gcols(hh)].astype(F32)
                        for r in range(dil):
                            dst[r, :, hh * HEAD_DIM:(hh + 1) * HEAD_DIM] = (
                                stage[pl.ds(r, sub_len, stride=dil), :].astype(BF16))

            for hh in range(A_HEADS_PER_GROUP):
                qstage[hh] = q_ref[:, gcols(hh)].astype(F32)

        def scores(r, i, hh):
            q0 = ti * lq + i * A_QROWS
            ks = jnp.clip(q0 - A_RADIUS, 0, sub_len - A_KWIN)
            off = lax.shift_right_logical(q0 - ks, int(math.log2(A_RADIUS)))
            ks = pl.multiple_of(ks, A_RADIUS)
            if dil > 1:
                cols = slice(hh * HEAD_DIM, (hh + 1) * HEAD_DIM)
                rows = pl.ds(i * A_QROWS * dil + r, A_QROWS, stride=dil)
                q = qstage[hh, rows, :].astype(BF16)
                k = kres[r, pl.ds(ks, A_KWIN), cols]
                v = vres[r, pl.ds(ks, A_KWIN), cols]
            else:
                rows = pl.ds(i * A_QROWS, A_QROWS)
                q = q_ref[rows, gcols(hh)]
                k = k_ref[pl.ds(ks, A_KWIN), gcols(hh)]
                v = v_ref[pl.ds(ks, A_KWIN), gcols(hh)]
            s = lax.dot_general(q, k, _NT, preferred_element_type=F32) * SCALE + bias_ref[gi * A_NOFF + off, hh]
            return rows, s, v

        def fold(hh, rows, o, lse):
            lse = jnp.broadcast_to(lse, (A_QROWS, HEAD_DIM))
            if gi > 0:
                prev_o, prev_lse = o_acc[hh, rows, :], lse_acc[hh, rows, :]
                m = jnp.maximum(prev_lse, lse)
                w_prev, w_new = jnp.exp(prev_lse - m), jnp.exp(lse - m)
                den = w_prev + w_new
                o = (w_prev * prev_o + w_new * o) / den
                lse = m + jnp.log(den)
            o_acc[hh, rows, :] = o
            if gi + 1 < n_groups:
                lse_acc[hh, rows, :] = lse

        items = [(r, i, hh) for r in range(dil) for i in range(lq // A_QROWS) for hh in range(A_HEADS_PER_GROUP)]
        for b0 in range(0, len(items), A_BATCH):
            batch = items[b0:b0 + A_BATCH]
            staged = [scores(*it) for it in batch]
            probs = [softmax(s) for _, s, _ in staged]
            for (_, _, hh), (rows, _, v), (p, l, lse) in zip(batch, staged, probs):
                fold(hh, rows, jnp.dot(p, v, preferred_element_type=F32) / l, lse)

    strided = [gi for gi, (_, dil) in enumerate(DIL_PAIRS) if dil > 1]
    for gi, (_, dil) in enumerate(DIL_PAIRS):
        kres, vres = (residue_kv[2 * strided.index(gi):2 * strided.index(gi) + 2] if dil > 1 else (None, None))
        run_group(gi, dil, kres, vres)
    for hh in range(A_HEADS_PER_GROUP):
        oa_ref[:, hh * HEAD_DIM:(hh + 1) * HEAD_DIM] = o_acc[hh].astype(oa_ref.dtype)


def _mixer_a(qkn, va, bias_a, t_rows=2048):
    b, seq, _ = qkn.shape
    head_buf = lambda rows: pltpu.VMEM((A_HEADS_PER_GROUP, rows, HEAD_DIM), F32)
    scratch = [pltpu.VMEM((seq, HEAD_DIM), F32), head_buf(t_rows), head_buf(t_rows), head_buf(t_rows)]
    for _, dil in DIL_PAIRS:
        if dil > 1:
            scratch += [pltpu.VMEM((dil, seq // dil, A_OUT_W), BF16)] * 2
    return pl.pallas_call(
        functools.partial(_mixer_a_kernel, seq=seq),
        grid=(b, seq // t_rows),
        in_specs=[pl.BlockSpec((None, t_rows, A_W), lambda bi, ti: (bi, ti, 0)),
                  pl.BlockSpec((None, seq, A_W), lambda bi, ti: (bi, 0, 1)),
                  pl.BlockSpec((None, seq, A_W), lambda bi, ti: (bi, 0, 0)),
                  pl.BlockSpec(bias_a.shape, lambda bi, ti: (0, 0, 0, 0), pipeline_mode=pl.Buffered(1))],
        out_specs=pl.BlockSpec((None, t_rows, A_OUT_W), lambda bi, ti: (bi, ti, 0)),
        out_shape=jax.ShapeDtypeStruct((b, seq, A_OUT_W), BF16),
        scratch_shapes=scratch,
        compiler_params=_cparams("parallel", "arbitrary", vmem_mb=62),
        name="mixer_a",
    )(qkn, qkn, va, bias_a)


def _mixer_b_kernel(qt_ref, k_ref, vt_ref, *rest, tk, group, w, n_cast):
    o_ref, s_scr = rest[n_cast], rest[-1]
    _run_cast_jobs(rest[:n_cast], rest[n_cast + 1:-1])
    tq = qt_ref.shape[1]
    seq = k_ref.shape[0]
    n_chunks = seq // tk
    units = [(i, j) for i in range(group) for j in range(tq // w)]

    def pass_a(u, ci, m):
        i, j = units[u]
        qt = qt_ref[i * HEAD_DIM:(i + 1) * HEAD_DIM, j * w:(j + 1) * w]
        st = jnp.dot(k_ref[ci * tk:(ci + 1) * tk, :], qt, preferred_element_type=F32)
        s_scr[u % 2, ci * tk:(ci + 1) * tk, :] = st
        return jnp.maximum(m, jnp.max(st, axis=0, keepdims=True))

    def pass_b(u, ci, m, l, acc):
        pt = jnp.exp2(s_scr[u % 2, ci * tk:(ci + 1) * tk, :] - m)
        l = l + jnp.sum(pt, axis=0, keepdims=True)
        acc = acc + jnp.dot(vt_ref[:, ci * tk:(ci + 1) * tk], pt.astype(BF16), preferred_element_type=F32)
        return l, acc

    m_prev = None
    for s in range(len(units) + 1):
        m_cur = jnp.full((1, w), NEG, F32)
        l = jnp.zeros((1, w), F32)
        acc = jnp.zeros((HEAD_DIM, w), F32)
        for ci in range(n_chunks):
            if s < len(units):
                m_cur = pass_a(s, ci, m_cur)
            if s > 0:
                l, acc = pass_b(s - 1, ci, m_prev, l, acc)
        if s > 0:
            i, j = units[s - 1]
            o_ref[j * w:(j + 1) * w, i * HEAD_DIM:(i + 1) * HEAD_DIM] = (acc / l).T.astype(o_ref.dtype)
        m_prev = m_cur


def _mixer_b(qbt, kb, vbt, cast_jobs, tq=512, tk=512, w=256):
    b, seq, _ = kb.shape
    group = B_Q_HEADS // B_KV_HEADS
    gw = group * HEAD_DIM
    nq = seq // tq
    n_steps = b * B_KV_HEADS * nq
    c_in, c_out, c_shapes = _cast_jobs(cast_jobs, n_steps, lambda bi, kv, qi: (bi * B_KV_HEADS + kv) * nq + qi)
    return pl.pallas_call(
        functools.partial(_mixer_b_kernel, tk=tk, group=group, w=w, n_cast=len(cast_jobs)),
        grid=(b, B_KV_HEADS, nq),
        scratch_shapes=[pltpu.VMEM((2, seq, w), F32)],
        in_specs=[pl.BlockSpec((None, gw, tq), lambda bi, kv, qi: (bi, kv, qi)),
                  pl.BlockSpec((None, seq, HEAD_DIM), lambda bi, kv, qi: (bi, 0, kv)),
                  pl.BlockSpec((None, HEAD_DIM, seq), lambda bi, kv, qi: (bi, kv, 0))] + c_in,
        out_specs=[pl.BlockSpec((None, tq, gw), lambda bi, kv, qi: (bi, qi, kv))] + c_out,
        out_shape=[jax.ShapeDtypeStruct((b, seq, B_QW), BF16)] + c_shapes,
        compiler_params=_cparams("parallel", "parallel", "arbitrary"),
        name="mixer_b",
    )(qbt, kb, vbt, *[src for src, _, _ in cast_jobs])


def _mixer_c_kernel(q_ref, k_ref, v_ref, o_ref):
    for hh in range(C_HEADS):
        sl = slice(hh * HEAD_DIM, (hh + 1) * HEAD_DIM)
        s = lax.dot_general(q_ref[:, sl], k_ref[:, sl], _NT, preferred_element_type=F32) * SCALE
        m = jnp.max(s, axis=-1, keepdims=True)
        p = jnp.exp(s - m)
        l = jnp.sum(p, axis=-1, keepdims=True)
        o = jnp.dot(p.astype(BF16), v_ref[:, sl], preferred_element_type=F32) / l
        o_ref[:, sl] = o.astype(o_ref.dtype)


def _mixer_c(qkn, kc, vc, tq=1024):
    b, seq, _ = qkn.shape
    n_mem = kc.shape[1]
    qc_blk = (2 * A_W) // C_W
    return pl.pallas_call(
        _mixer_c_kernel,
        grid=(b, seq // tq),
        in_specs=[pl.BlockSpec((None, tq, C_W), lambda bi, qi: (bi, qi, qc_blk)),
                  pl.BlockSpec((None, n_mem, C_W), lambda bi, qi: (bi, 0, 0)),
                  pl.BlockSpec((None, n_mem, C_W), lambda bi, qi: (bi, 0, 0))],
        out_specs=pl.BlockSpec((None, tq, C_W), lambda bi, qi: (bi, qi, 0)),
        out_shape=jax.ShapeDtypeStruct((b, seq, C_W), BF16),
        compiler_params=_cparams("parallel", "arbitrary"),
        name="mixer_c",
    )(qkn, kc, vc)


def _merge_kernel(h_ref, wg0_ref, wg1_ref, wg2_ref, oa_ref, ob_ref, oc_ref, wa_ref, wb_ref, wc_ref, out_ref):
    h = h_ref[...]
    ga = jax.nn.sigmoid(jnp.dot(h, wg0_ref[...], preferred_element_type=F32))
    merged = ga * jnp.dot(oa_ref[...], wa_ref[...], preferred_element_type=F32)
    gb = jax.nn.sigmoid(jnp.dot(h, wg1_ref[...], preferred_element_type=F32))
    merged += gb * jnp.dot(ob_ref[...], wb_ref[...], preferred_element_type=F32)
    gc = jax.nn.sigmoid(jnp.dot(h, wg2_ref[...], preferred_element_type=F32))
    merged += gc * jnp.dot(oc_ref[...], wc_ref[...], preferred_element_type=F32)
    out_ref[...] = merged.astype(out_ref.dtype)


def _merge(h, w_gates, oa, ob, oc, wa, wb, wc, tm=1024, tn=512):
    m, d = h.shape
    nj = d // tn
    row = lambda w: pl.BlockSpec((tm, w), lambda i, j: (i, 0))
    col = lambda k: pl.BlockSpec((k, tn), lambda i, j: (0, j))
    return pl.pallas_call(
        _merge_kernel,
        grid=(m // tm, nj),
        in_specs=[row(d), col(d), col(d), col(d), row(A_OUT_W), row(B_QW), row(C_W),
                  col(A_OUT_W), col(B_QW), col(C_W)],
        out_specs=pl.BlockSpec((tm, tn), lambda i, j: (i, j)),
        out_shape=jax.ShapeDtypeStruct((m, d), BF16),
        compiler_params=_cparams("parallel", "arbitrary", vmem_mb=48),
        name="merge",
    )(h, *w_gates, oa, ob, oc, wa, wb, wc)


def _out_proj_kernel(mg_ref, w_ref, x_ref, g_ref, x1_ref, h2_ref, *, r_sub):
    def matmul(c):
        return jnp.dot(mg_ref[c * r_sub:(c + 1) * r_sub, :], w_ref[...], preferred_element_type=F32)

    def epilogue(c, acc):
        rows = slice(c * r_sub, (c + 1) * r_sub)
        x1 = x_ref[rows, :] + acc
        x1_ref[rows, :] = x1
        ms = jnp.mean(x1 * x1, axis=-1, keepdims=True)
        h2_ref[rows, :] = (x1 * lax.rsqrt(ms + EPS) * g_ref[...]).astype(h2_ref.dtype)

    _software_pipeline(mg_ref.shape[0] // r_sub, matmul, epilogue)


def _out_proj(merged, w_o, x2d, g_ffn, tm=512, r_sub=512):
    m, d = x2d.shape
    return pl.pallas_call(
        functools.partial(_out_proj_kernel, r_sub=r_sub),
        grid=(m // tm,),
        in_specs=[pl.BlockSpec((tm, d), lambda i: (i, 0)),
                  pl.BlockSpec((d, d), lambda i: (0, 0)),
                  pl.BlockSpec((tm, d), lambda i: (i, 0)),
                  pl.BlockSpec((1, d), lambda i: (0, 0))],
        out_specs=[pl.BlockSpec((tm, d), lambda i: (i, 0))] * 2,
        out_shape=[jax.ShapeDtypeStruct((m, d), F32), jax.ShapeDtypeStruct((m, d), BF16)],
        compiler_params=_cparams("parallel"),
        name="out_proj",
    )(merged, w_o, x2d, g_ffn.reshape(1, d))


def _ffn_kernel(h_ref, wa_ref, wb_ref, wo_ref, x1_ref, out_ref, *, n_slabs):
    f = pl.program_id(1)
    slab_w = x1_ref.shape[1]

    @pl.when(f == 0)
    def _():
        out_ref[...] = jnp.zeros_like(out_ref)

    for s in range(n_slabs):
        @pl.when(f == s)
        def _():
            out_ref[:, s * slab_w:(s + 1) * slab_w] += x1_ref[...]

    h = h_ref[...]
    a = jnp.dot(h, wa_ref[...], preferred_element_type=F32)
    b = jnp.dot(h, wb_ref[...], preferred_element_type=F32)
    act = (a * jax.nn.sigmoid(a) * b).astype(BF16)
    out_ref[...] += jnp.dot(act, wo_ref[...], preferred_element_type=F32)


def _ffn(h2, w_a, w_b, w_out, x1, tm=1024, tf=512, slab_w=256):
    m, d = h2.shape
    d_ff = w_out.shape[0]
    nf = d_ff // tf
    n_slabs = d // slab_w
    assert n_slabs <= nf
    return pl.pallas_call(
        functools.partial(_ffn_kernel, n_slabs=n_slabs),
        grid=(m // tm, nf),
        in_specs=[pl.BlockSpec((tm, d), lambda i, f: (i, 0)),
                  pl.BlockSpec((d, tf), lambda i, f: (0, f)),
                  pl.BlockSpec((d, tf), lambda i, f: (0, f)),
                  pl.BlockSpec((tf, d), lambda i, f: (f, 0)),
                  pl.BlockSpec((tm, slab_w), lambda i, f: (i, jnp.minimum(f, n_slabs - 1)))],
        out_specs=pl.BlockSpec((tm, d), lambda i, f: (i, 0)),
        out_shape=jax.ShapeDtypeStruct((m, d), F32),
        compiler_params=_cparams("parallel", "arbitrary", vmem_mb=48),
        name="ffn",
    )(h2, w_a, w_b, w_out, x1)


def _deinterleave_cols(w, heads):
    d = w.shape[0]
    return w.reshape(d, heads, HEAD_DIM // 2, 2).transpose(0, 1, 3, 2).reshape(d, heads * HEAD_DIM)


def _rope_tables(seq):
    rows = seq // GRID_W
    r = np.repeat(np.arange(rows), GRID_W).astype(np.float64)
    c = np.tile(np.arange(GRID_W), rows).astype(np.float64)
    nf = HEAD_DIM // 4
    inv = ROPE_THETA ** (-np.arange(nf, dtype=np.float64) / nf)
    ang = np.concatenate([r[:, None] * inv, c[:, None] * inv], axis=-1)
    cos, sin = np.cos(ang).T, np.sin(ang).T
    return (np.concatenate([cos, cos], axis=0).astype(np.float32),
            np.concatenate([-sin, sin], axis=0).astype(np.float32))


def _layer(x2d, mem2d, bias_a, cos_t, sin_t, b, seq, g_mix, w_in, g_qa, g_ka, g_qb, g_kb, g_mem, w_mem_kv,
           g_qc, g_kc, w_br_a, w_br_b, w_br_c, w_o, g_ffn, w_ffn_in, w_ffn_out):
    d = x2d.shape[1]
    o_qa, o_ka, o_va, o_qb, o_kb, o_vb, o_qc, o_gt = np.cumsum(
        (0, A_W, A_W, A_W, B_QW, B_KVW, B_KVW, C_W))
    d_ff = w_ffn_out.shape[0]
    w_all = _cast_cols(w_in, int(o_gt))
    seg = lambda lo, hi: w_all[:, lo:hi]
    w_b = jnp.concatenate([_deinterleave_cols(seg(o_qb, o_kb), B_Q_HEADS),
                           _deinterleave_cols(seg(o_kb, o_vb), B_KV_HEADS),
                           seg(o_vb, o_qc), seg(o_va, o_qb)], axis=1)
    g_norm = jnp.concatenate([jnp.tile(g_qa, A_HEADS), jnp.tile(g_ka, A_HEADS), jnp.tile(g_qc, C_HEADS)])
    deint = lambda g: g.reshape(HEAD_DIM // 2, 2).T.reshape(HEAD_DIM)

    n_qk_tiles = (2 * A_W) // C_W
    qkn, h = _proj_norm(x2d, g_mix, w_all, g_norm,
                        lambda j: jnp.where(j < n_qk_tiles, j, int(o_qc) // C_W), tn=C_W)
    qkn = qkn.reshape(b, seq, -1)
    late_weights = [(w_ffn_out, 0, d), (w_o, 0, d), (w_br_a, 0, d), (w_br_b, 0, d), (w_br_c, 0, d)]
    qbt, kb, vbt, va, w_out_b, w_o_b, wa_b, wb_b, wc_b = _proj_t(
        h, w_b, deint(g_qb) * (SCALE * LOG2E), deint(g_kb), cos_t, sin_t, b, seq, late_weights)
    kb = kb.reshape(b, seq, B_KVW)
    va = va.reshape(b, seq, A_W)

    oa = _mixer_a(qkn, va, bias_a).reshape(b * seq, A_OUT_W)
    gate_jobs = [(w_in, int(o_gt) // d + br, d) for br in range(N_BRANCH)]
    ob, w_ffa, w_ffb, *w_gates = _mixer_b(qbt, kb, vbt, [(w_ffn_in, 0, d_ff), (w_ffn_in, 1, d_ff)] + gate_jobs)
    ob = ob.reshape(b * seq, B_QW)

    kc, vc = _mem_kv(mem2d, g_mem, w_mem_kv.astype(BF16), g_kc)
    n_mem = mem2d.shape[0] // b
    oc = _mixer_c(qkn, kc.reshape(b, n_mem, C_W), vc.reshape(b, n_mem, C_W)).reshape(b * seq, C_W)

    merged = _merge(h, w_gates, oa, ob, oc, wa_b, wb_b, wc_b)
    x1, h2 = _out_proj(merged, w_o_b, x2d, g_ffn)
    return _ffn(h2, w_ffa, w_ffb, w_out_b, x1)


def kernel(x, mem, rel_bias, g_mix, w_in, g_qa, g_ka, g_qb, g_kb, g_mem, w_mem_kv, g_qc, g_kc,
           w_br_a, w_br_b, w_br_c, w_o, g_ffn, w_ffn_in, w_ffn_out):
    b, seq, d = x.shape
    depth = w_in.shape[0]
    cos_t, sin_t = _rope_tables(seq)
    bias_a = _a_bias(rel_bias)
    x2d = x.reshape(b * seq, d)
    mem2d = mem.reshape(-1, d)
    for layer in range(depth):
        x2d = _layer(x2d, mem2d, bias_a, cos_t, sin_t, b, seq,
                     g_mix[layer], w_in[layer], g_qa[layer], g_ka[layer], g_qb[layer], g_kb[layer],
                     g_mem[layer], w_mem_kv[layer], g_qc[layer], g_kc[layer],
                     w_br_a[layer], w_br_b[layer], w_br_c[layer], w_o[layer], g_ffn[layer],
                     w_ffn_in[layer], w_ffn_out[layer])
    return x2d.reshape(b, seq, d)
```

```python
import functools
import math

import numpy as np
import jax
import jax.numpy as jnp
from jax import lax
from jax.experimental import pallas as pl
from jax.experimental.pallas import tpu as pltpu

HEAD_DIM = 128
GRID_W = 64
DIL_PAIRS = ((128, 1), (512, 4), (2048, 16))
A_HEADS_PER_GROUP = 2
A_HEADS = A_HEADS_PER_GROUP * len(DIL_PAIRS)
B_Q_HEADS = 6
B_KV_HEADS = 2
ROPE_THETA = 10000.0
C_HEADS = 4
N_BRANCH = 3
REL_BUCKETS = 32
REL_MAX_DIST = 1024
EPS = 1e-6
NEG = -1e30

A_W = A_HEADS * HEAD_DIM
A_OUT_W = A_HEADS_PER_GROUP * HEAD_DIM
B_QW = B_Q_HEADS * HEAD_DIM
B_KVW = B_KV_HEADS * HEAD_DIM
C_W = C_HEADS * HEAD_DIM

SCALE = 1.0 / math.sqrt(HEAD_DIM)
LOG2E = math.log2(math.e)

A_QROWS = 128
A_KWIN = 256
A_RADIUS = 64
A_NOFF = 3
A_BATCH = 8

BF16 = jnp.bfloat16
F32 = jnp.float32
BF16_TILE_ROWS = 16
F32_TILE_ROWS = 8

_NT = (((1,), (1,)), ((), ()))


def _software_pipeline(n_chunks, matmul, epilogue):
    acc = matmul(0)
    for c in range(n_chunks):
        nxt = matmul(c + 1) if c + 1 < n_chunks else None
        epilogue(c, acc)
        acc = nxt


def _cast_jobs(jobs, n_steps, step_index):
    in_specs, out_specs, out_shapes = [], [], []
    for src, col_block, width in jobs:
        rows = src.shape[0]
        rt = rows // n_steps
        assert rt * n_steps == rows and rt % BF16_TILE_ROWS == 0 and src.shape[1] % width == 0
        in_specs.append(pl.BlockSpec((rt, width), lambda *g, cb=col_block: (step_index(*g), cb)))
        out_specs.append(pl.BlockSpec((rt, width), lambda *g: (step_index(*g), 0)))
        out_shapes.append(jax.ShapeDtypeStruct((rows, width), BF16))
    return in_specs, out_specs, out_shapes


def _run_cast_jobs(src_refs, dst_refs):
    for src, dst in zip(src_refs, dst_refs):
        dst[...] = src[...].astype(dst.dtype)


def _cast_kernel(src_ref, dst_ref):
    dst_ref[...] = src_ref[...].astype(dst_ref.dtype)


def _cast_cols(w, n_cols, row_tile=256):
    rows = w.shape[0]
    assert rows % row_tile == 0
    return pl.pallas_call(
        _cast_kernel,
        grid=(rows // row_tile,),
        in_specs=[pl.BlockSpec((row_tile, n_cols), lambda i: (i, 0))],
        out_specs=pl.BlockSpec((row_tile, n_cols), lambda i: (i, 0)),
        out_shape=jax.ShapeDtypeStruct((rows, n_cols), BF16),
        compiler_params=_cparams("parallel"),
        name="cast_head_weights",
    )(w)


def _cparams(*sem, vmem_mb=None):
    limit = None if vmem_mb is None else vmem_mb * 1024 * 1024
    return pltpu.CompilerParams(dimension_semantics=sem, vmem_limit_bytes=limit)


def _proj_norm_kernel(x_hbm, gm_ref, w_ref, g_ref, o_ref, h_ref, xbuf, sem, *, r_sub, n_j):
    i, j = pl.program_id(0), pl.program_id(1)
    n_i = pl.num_programs(0)
    tm = h_ref.shape[0]
    piece = tm // n_j

    def x_copy(tile, p):
        slot = tile % 2
        rows = pl.ds(tile * tm + p * piece, piece)
        return pltpu.make_async_copy(x_hbm.at[rows, :], xbuf.at[slot, pl.ds(p * piece, piece), :], sem.at[slot, p])

    @pl.when((i == 0) & (j == 0))
    def _():
        for p in range(n_j):
            x_copy(0, p).start()

    @pl.when(i + 1 < n_i)
    def _():
        for p in range(n_j):
            @pl.when(j == p)
            def _():
                x_copy(i + 1, p).start()

    @pl.when(j == 0)
    def _():
        for p in range(n_j):
            x_copy(i, p).wait()
        x = xbuf[i % 2]
        ms = jnp.mean(x * x, axis=-1, keepdims=True)
        h_ref[...] = (x * lax.rsqrt(ms + EPS) * gm_ref[...]).astype(h_ref.dtype)

    def matmul(c):
        return jnp.dot(h_ref[c * r_sub:(c + 1) * r_sub, :], w_ref[...], preferred_element_type=F32)

    def epilogue(c, acc):
        for hh in range(o_ref.shape[1] // HEAD_DIM):
            sl = slice(hh * HEAD_DIM, (hh + 1) * HEAD_DIM)
            y = acc[:, sl]
            ms = jnp.mean(y * y, axis=-1, keepdims=True)
            o_ref[c * r_sub:(c + 1) * r_sub, sl] = (y * lax.rsqrt(ms + EPS) * g_ref[:, sl]).astype(o_ref.dtype)

    _software_pipeline(h_ref.shape[0] // r_sub, matmul, epilogue)


def _proj_norm(x2d, g_mix, w, gains, col_block, tm=1024, tn=512, r_sub=256):
    m, d = x2d.shape
    n = gains.shape[0]
    n_j = n // tn
    assert n % tn == 0 and m % tm == 0 and tm % (F32_TILE_ROWS * n_j) == 0
    return pl.pallas_call(
        functools.partial(_proj_norm_kernel, r_sub=r_sub, n_j=n_j),
        grid=(m // tm, n_j),
        in_specs=[pl.BlockSpec(memory_space=pl.ANY),
                  pl.BlockSpec((1, d), lambda i, j: (0, 0)),
                  pl.BlockSpec((d, tn), lambda i, j: (0, col_block(j))),
                  pl.BlockSpec((1, tn), lambda i, j: (0, j))],
        out_specs=[pl.BlockSpec((tm, tn), lambda i, j: (i, j)),
                   pl.BlockSpec((tm, d), lambda i, j: (i, 0))],
        out_shape=[jax.ShapeDtypeStruct((m, n), BF16), jax.ShapeDtypeStruct((m, d), BF16)],
        scratch_shapes=[pltpu.VMEM((2, tm, d), F32), pltpu.SemaphoreType.DMA((2, n_j))],
        compiler_params=_cparams("arbitrary", "arbitrary", vmem_mb=48),
        name="proj_norm",
    )(x2d, g_mix.reshape(1, d), w, gains.reshape(1, n))


def _proj_t_kernel(w_ref, h_ref, gq_ref, gk_ref, cos_ref, sin_ref, *rest, t_sub, n_cast):
    q_ref, k_ref, v_ref, va_ref = rest[n_cast:n_cast + 4]
    _run_cast_jobs(rest[:n_cast], rest[n_cast + 4:])
    n_q = q_ref.shape[0] // HEAD_DIM
    n_k = k_ref.shape[1] // HEAD_DIM
    n_v = v_ref.shape[0] // HEAD_DIM
    half = HEAD_DIM // 2

    def norm_rope(y, g_ref, tok):
        ms = jnp.mean(y * y, axis=0, keepdims=True)
        y = y * lax.rsqrt(ms + EPS) * g_ref[...]
        partner = jnp.concatenate([y[half:], y[:half]], axis=0)
        return y * cos_ref[:, tok] + partner * sin_ref[:, tok]

    def matmul(c):
        return jnp.dot(h_ref[c * t_sub:(c + 1) * t_sub, :], w_ref[...], preferred_element_type=F32)

    def epilogue(c, y):
        tok = slice(c * t_sub, (c + 1) * t_sub)
        head = lambda hh: y[:, hh * HEAD_DIM:(hh + 1) * HEAD_DIM].T
        for hh in range(n_q):
            q_ref[hh * HEAD_DIM:(hh + 1) * HEAD_DIM, tok] = norm_rope(head(hh), gq_ref, tok).astype(q_ref.dtype)
        for hh in range(n_k):
            kt = norm_rope(head(n_q + hh), gk_ref, tok)
            k_ref[tok, hh * HEAD_DIM:(hh + 1) * HEAD_DIM] = kt.T.astype(k_ref.dtype)
        for hh in range(n_v):
            v_ref[hh * HEAD_DIM:(hh + 1) * HEAD_DIM, tok] = head(n_q + n_k + hh).astype(v_ref.dtype)
        va_ref[tok, :] = y[:, (n_q + n_k + n_v) * HEAD_DIM:].astype(va_ref.dtype)

    _software_pipeline(h_ref.shape[0] // t_sub, matmul, epilogue)


def _proj_t(h, w, gq_col, gk_col, cos_tt, sin_tt, b, seq, cast_jobs, tm=1024, t_sub=128):
    m, d = h.shape
    n = w.shape[1]
    assert n == B_QW + 2 * B_KVW + A_W
    sb = seq // tm
    col = lambda g: jnp.broadcast_to(g[:, None], (HEAD_DIM, t_sub))
    lane_tile = lambda rows: pl.BlockSpec((None, rows, tm), lambda i: (i // sb, 0, i % sb))
    c_in, c_out, c_shapes = _cast_jobs(cast_jobs, m // tm, lambda i: i)
    return pl.pallas_call(
        functools.partial(_proj_t_kernel, t_sub=t_sub, n_cast=len(cast_jobs)),
        grid=(m // tm,),
        in_specs=[pl.BlockSpec((d, n), lambda i: (0, 0)),
                  pl.BlockSpec((tm, d), lambda i: (i, 0)),
                  pl.BlockSpec((HEAD_DIM, t_sub), lambda i: (0, 0)),
                  pl.BlockSpec((HEAD_DIM, t_sub), lambda i: (0, 0)),
                  pl.BlockSpec((HEAD_DIM, tm), lambda i: (0, i % sb)),
                  pl.BlockSpec((HEAD_DIM, tm), lambda i: (0, i % sb))] + c_in,
        out_specs=[lane_tile(B_QW),
                   pl.BlockSpec((tm, B_KVW), lambda i: (i, 0)),
                   lane_tile(B_KVW),
                   pl.BlockSpec((tm, A_W), lambda i: (i, 0))] + c_out,
        out_shape=[jax.ShapeDtypeStruct((b, B_QW, seq), BF16),
                   jax.ShapeDtypeStruct((m, B_KVW), BF16),
                   jax.ShapeDtypeStruct((b, B_KVW, seq), BF16),
                   jax.ShapeDtypeStruct((m, A_W), BF16)] + c_shapes,
        compiler_params=_cparams("parallel", vmem_mb=56),
        name="proj_t",
    )(w, h, col(gq_col), col(gk_col), cos_tt, sin_tt, *[src for src, _, _ in cast_jobs])


def _t5_bucket(rel):
    nb = REL_BUCKETS // 2
    ret = np.where(rel > 0, nb, 0)
    n = np.abs(rel)
    max_exact = nb // 2
    large = max_exact + (np.log(np.maximum(n, 1).astype(np.float32) / np.float32(max_exact))
                         / np.float32(math.log(REL_MAX_DIST / max_exact))
                         * np.float32(nb - max_exact)).astype(np.int32)
    large = np.minimum(large, nb - 1)
    return ret + np.where(n < max_exact, n, large)


def _a_bucket_index():
    qi = np.arange(A_QROWS, dtype=np.int32)[:, None]
    kj = np.arange(A_KWIN, dtype=np.int32)[None, :]
    out = []
    for _, dil in DIL_PAIRS:
        for off in range(A_NOFF):
            rel = kj - qi - A_RADIUS * off
            out.append(np.where(np.abs(rel) <= A_RADIUS, _t5_bucket(rel * dil), -1))
    return np.stack(out).astype(np.int32)


def _a_bias_kernel(tab_ref, bucket_ref, o_ref):
    g = pl.program_id(0)
    for off in range(A_NOFF):
        bk = bucket_ref[off]
        for hh in range(A_HEADS_PER_GROUP):
            acc = jnp.full(bk.shape, NEG, F32)
            for b in range(REL_BUCKETS):
                acc = jnp.where(bk == b, tab_ref[b, g * A_HEADS_PER_GROUP + hh], acc)
            o_ref[off, hh] = acc


def _a_bias(rel_bias):
    n = len(DIL_PAIRS) * A_NOFF
    return pl.pallas_call(
        _a_bias_kernel,
        grid=(len(DIL_PAIRS),),
        in_specs=[pl.BlockSpec(memory_space=pltpu.SMEM),
                  pl.BlockSpec((A_NOFF, A_QROWS, A_KWIN), lambda i: (i, 0, 0))],
        out_specs=pl.BlockSpec((A_NOFF, A_HEADS_PER_GROUP, A_QROWS, A_KWIN), lambda i: (i, 0, 0, 0)),
        out_shape=jax.ShapeDtypeStruct((n, A_HEADS_PER_GROUP, A_QROWS, A_KWIN), F32),
        compiler_params=_cparams("arbitrary"),
        name="a_bias",
    )(rel_bias, _a_bucket_index())


def _mixer_a_kernel(q_ref, k_ref, v_ref, bias_ref, oa_ref, stage, qstage, o_acc, lse_acc, *residue_kv, seq):
    t_rows = q_ref.shape[0]
    ti = pl.program_id(1)
    n_groups = len(DIL_PAIRS)

    def softmax(s):
        m = jnp.max(s, axis=-1, keepdims=True)
        p = jnp.exp(s - m)
        l = jnp.sum(p, axis=-1, keepdims=True)
        return p.astype(BF16), l, m + jnp.log(l)

    def run_group(gi, dil, kres, vres):
        sub_len = seq // dil
        lq = t_rows // dil
        gcols = lambda hh: slice(gi * A_OUT_W + hh * HEAD_DIM, gi * A_OUT_W + (hh + 1) * HEAD_DIM)
        if dil > 1:
            @pl.when(ti == 0)
            def _():
                for src, dst in ((k_ref, kres), (v_ref, vres)):
                    for hh in range(A_HEADS_PER_GROUP):
                        stage[...] = src[:, gcols(hh)].astype(F32)
                        for r in range(dil):
                            dst[r, :, hh * HEAD_DIM:(hh + 1) * HEAD_DIM] = (
                                stage[pl.ds(r, sub_len, stride=dil), :].astype(BF16))

            for hh in range(A_HEADS_PER_GROUP):
                qstage[hh] = q_ref[:, gcols(hh)].astype(F32)

        def scores(r, i, hh):
            q0 = ti * lq + i * A_QROWS
            ks = jnp.clip(q0 - A_RADIUS, 0, sub_len - A_KWIN)
            off = lax.shift_right_logical(q0 - ks, int(math.log2(A_RADIUS)))
            ks = pl.multiple_of(ks, A_RADIUS)
            if dil > 1:
                cols = slice(hh * HEAD_DIM, (hh + 1) * HEAD_DIM)
                rows = pl.ds(i * A_QROWS * dil + r, A_QROWS, stride=dil)
                q = qstage[hh, rows, :].astype(BF16)
                k = kres[r, pl.ds(ks, A_KWIN), cols]
                v = vres[r, pl.ds(ks, A_KWIN), cols]
            else:
                rows = pl.ds(i * A_QROWS, A_QROWS)
                q = q_ref[rows, gcols(hh)]
                k = k_ref[pl.ds(ks, A_KWIN), gcols(hh)]
                v = v_ref[pl.ds(ks, A_KWIN), gcols(hh)]
            s = lax.dot_general(q, k, _NT, preferred_element_type=F32) * SCALE + bias_ref[gi * A_NOFF + off, hh]
            return rows, s, v

        def fold(hh, rows, o, lse):
            lse = jnp.broadcast_to(lse, (A_QROWS, HEAD_DIM))
            if gi > 0:
                prev_o, prev_lse = o_acc[hh, rows, :], lse_acc[hh, rows, :]
                m = jnp.maximum(prev_lse, lse)
                w_prev, w_new = jnp.exp(prev_lse - m), jnp.exp(lse - m)
                den = w_prev + w_new
                o = (w_prev * prev_o + w_new * o) / den
                lse = m + jnp.log(den)
            o_acc[hh, rows, :] = o
            if gi + 1 < n_groups:
                lse_acc[hh, rows, :] = lse

        items = [(r, i, hh) for r in range(dil) for i in range(lq // A_QROWS) for hh in range(A_HEADS_PER_GROUP)]
        for b0 in range(0, len(items), A_BATCH):
            batch = items[b0:b0 + A_BATCH]
            staged = [scores(*it) for it in batch]
            probs = [softmax(s) for _, s, _ in staged]
            for (_, _, hh), (rows, _, v), (p, l, lse) in zip(batch, staged, probs):
                fold(hh, rows, jnp.dot(p, v, preferred_element_type=F32) / l, lse)

    strided = [gi for gi, (_, dil) in enumerate(DIL_PAIRS) if dil > 1]
    for gi, (_, dil) in enumerate(DIL_PAIRS):
        kres, vres = (residue_kv[2 * strided.index(gi):2 * strided.index(gi) + 2] if dil > 1 else (None, None))
        run_group(gi, dil, kres, vres)
    for hh in range(A_HEADS_PER_GROUP):
        oa_ref[:, hh * HEAD_DIM:(hh + 1) * HEAD_DIM] = o_acc[hh].astype(oa_ref.dtype)


def _mixer_a(qkn, va, bias_a, t_rows=2048):
    b, seq, _ = qkn.shape
    head_buf = lambda rows: pltpu.VMEM((A_HEADS_PER_GROUP, rows, HEAD_DIM), F32)
    scratch = [pltpu.VMEM((seq, HEAD_DIM), F32), head_buf(t_rows), head_buf(t_rows), head_buf(t_rows)]
    for _, dil in DIL_PAIRS:
        if dil > 1:
            scratch += [pltpu.VMEM((dil, seq // dil, A_OUT_W), BF16)] * 2
    return pl.pallas_call(
        functools.partial(_mixer_a_kernel, seq=seq),
        grid=(b, seq // t_rows),
        in_specs=[pl.BlockSpec((None, t_rows, A_W), lambda bi, ti: (bi, ti, 0)),
                  pl.BlockSpec((None, seq, A_W), lambda bi, ti: (bi, 0, 1)),
                  pl.BlockSpec((None, seq, A_W), lambda bi, ti: (bi, 0, 0)),
                  pl.BlockSpec(bias_a.shape, lambda bi, ti: (0, 0, 0, 0), pipeline_mode=pl.Buffered(1))],
        out_specs=pl.BlockSpec((None, t_rows, A_OUT_W), lambda bi, ti: (bi, ti, 0)),
        out_shape=jax.ShapeDtypeStruct((b, seq, A_OUT_W), BF16),
        scratch_shapes=scratch,
        compiler_params=_cparams("parallel", "arbitrary", vmem_mb=62),
        name="mixer_a",
    )(qkn, qkn, va, bias_a)


def _mixer_b_kernel(qt_ref, k_ref, vt_ref, *rest, tk, group, w, n_cast):
    o_ref, s_scr = rest[n_cast], rest[-1]
    _run_cast_jobs(rest[:n_cast], rest[n_cast + 1:-1])
    tq = qt_ref.shape[1]
    seq = k_ref.shape[0]
    n_chunks = seq // tk
    units = [(i, j) for i in range(group) for j in range(tq // w)]

    def pass_a(u, ci, m):
        i, j = units[u]
        qt = qt_ref[i * HEAD_DIM:(i + 1) * HEAD_DIM, j * w:(j + 1) * w]
        st = jnp.dot(k_ref[ci * tk:(ci + 1) * tk, :], qt, preferred_element_type=F32)
        s_scr[u % 2, ci * tk:(ci + 1) * tk, :] = st
        return jnp.maximum(m, jnp.max(st, axis=0, keepdims=True))

    def pass_b(u, ci, m, l, acc):
        pt = jnp.exp2(s_scr[u % 2, ci * tk:(ci + 1) * tk, :] - m)
        l = l + jnp.sum(pt, axis=0, keepdims=True)
        acc = acc + jnp.dot(vt_ref[:, ci * tk:(ci + 1) * tk], pt.astype(BF16), preferred_element_type=F32)
        return l, acc

    m_prev = None
    for s in range(len(units) + 1):
        m_cur = jnp.full((1, w), NEG, F32)
        l = jnp.zeros((1, w), F32)
        acc = jnp.zeros((HEAD_DIM, w), F32)
        for ci in range(n_chunks):
            if s < len(units):
                m_cur = pass_a(s, ci, m_cur)
            if s > 0:
                l, acc = pass_b(s - 1, ci, m_prev, l, acc)
        if s > 0:
            i, j = units[s - 1]
            o_ref[j * w:(j + 1) * w, i * HEAD_DIM:(i + 1) * HEAD_DIM] = (acc / l).T.astype(o_ref.dtype)
        m_prev = m_cur


def _mixer_b(qbt, kb, vbt, cast_jobs, tq=512, tk=512, w=256):
    b, seq, _ = kb.shape
    group = B_Q_HEADS // B_KV_HEADS
    gw = group * HEAD_DIM
    nq = seq // tq
    n_steps = b * B_KV_HEADS * nq
    c_in, c_out, c_shapes = _cast_jobs(cast_jobs, n_steps, lambda bi, kv, qi: (bi * B_KV_HEADS + kv) * nq + qi)
    return pl.pallas_call(
        functools.partial(_mixer_b_kernel, tk=tk, group=group, w=w, n_cast=len(cast_jobs)),
        grid=(b, B_KV_HEADS, nq),
        scratch_shapes=[pltpu.VMEM((2, seq, w), F32)],
        in_specs=[pl.BlockSpec((None, gw, tq), lambda bi, kv, qi: (bi, kv, qi)),
                  pl.BlockSpec((None, seq, HEAD_DIM), lambda bi, kv, qi: (bi, 0, kv)),
                  pl.BlockSpec((None, HEAD_DIM, seq), lambda bi, kv, qi: (bi, kv, 0))] + c_in,
        out_specs=[pl.BlockSpec((None, tq, gw), lambda bi, kv, qi: (bi, qi, kv))] + c_out,
        out_shape=[jax.ShapeDtypeStruct((b, seq, B_QW), BF16)] + c_shapes,
        compiler_params=_cparams("parallel", "parallel", "arbitrary"),
        name="mixer_b",
    )(qbt, kb, vbt, *[src for src, _, _ in cast_jobs])


def _mixer_c_kernel(q_ref, mem_ref, gm_ref, w_ref, gk_ref, o_ref, k_scr, v_scr):
    @pl.when(pl.program_id(1) == 0)
    def _():
        x = mem_ref[...]
        ms = jnp.mean(x * x, axis=-1, keepdims=True)
        hm = (x * lax.rsqrt(ms + EPS) * gm_ref[...]).astype(BF16)
        kv = jnp.dot(hm, w_ref[...], preferred_element_type=F32)
        for hh in range(C_HEADS):
            sl = slice(hh * HEAD_DIM, (hh + 1) * HEAD_DIM)
            y = kv[:, sl]
            ms = jnp.mean(y * y, axis=-1, keepdims=True)
            k_scr[:, sl] = (y * lax.rsqrt(ms + EPS) * gk_ref[...]).astype(BF16)
        v_scr[...] = kv[:, C_W:].astype(BF16)

    for hh in range(C_HEADS):
        sl = slice(hh * HEAD_DIM, (hh + 1) * HEAD_DIM)
        s = lax.dot_general(q_ref[:, sl], k_scr[:, sl], _NT, preferred_element_type=F32) * SCALE
        m = jnp.max(s, axis=-1, keepdims=True)
        p = jnp.exp(s - m)
        l = jnp.sum(p, axis=-1, keepdims=True)
        o = jnp.dot(p.astype(BF16), v_scr[:, sl], preferred_element_type=F32) / l
        o_ref[:, sl] = o.astype(o_ref.dtype)


def _mixer_c(qkn, mem, g_mem, w_kv, g_kc, tq=1024):
    b, seq, _ = qkn.shape
    _, n_mem, d = mem.shape
    qc_blk = (2 * A_W) // C_W
    return pl.pallas_call(
        _mixer_c_kernel,
        grid=(b, seq // tq),
        in_specs=[pl.BlockSpec((None, tq, C_W), lambda bi, qi: (bi, qi, qc_blk)),
                  pl.BlockSpec((None, n_mem, d), lambda bi, qi: (bi, 0, 0)),
                  pl.BlockSpec((1, d), lambda bi, qi: (0, 0)),
                  pl.BlockSpec((d, 2 * C_W), lambda bi, qi: (0, 0)),
                  pl.BlockSpec((1, HEAD_DIM), lambda bi, qi: (0, 0))],
        out_specs=pl.BlockSpec((None, tq, C_W), lambda bi, qi: (bi, qi, 0)),
        out_shape=jax.ShapeDtypeStruct((b, seq, C_W), BF16),
        scratch_shapes=[pltpu.VMEM((n_mem, C_W), BF16), pltpu.VMEM((n_mem, C_W), BF16)],
        compiler_params=_cparams("parallel", "arbitrary"),
        name="mixer_c",
    )(qkn, mem, g_mem.reshape(1, d), w_kv, g_kc.reshape(1, HEAD_DIM))


def _merge_kernel(h_ref, wg0_ref, wg1_ref, wg2_ref, oa_ref, ob_ref, oc_ref, wa_ref, wb_ref, wc_ref, out_ref):
    h = h_ref[...]
    ga = jax.nn.sigmoid(jnp.dot(h, wg0_ref[...], preferred_element_type=F32))
    merged = ga * jnp.dot(oa_ref[...], wa_ref[...], preferred_element_type=F32)
    gb = jax.nn.sigmoid(jnp.dot(h, wg1_ref[...], preferred_element_type=F32))
    merged += gb * jnp.dot(ob_ref[...], wb_ref[...], preferred_element_type=F32)
    gc = jax.nn.sigmoid(jnp.dot(h, wg2_ref[...], preferred_element_type=F32))
    merged += gc * jnp.dot(oc_ref[...], wc_ref[...], preferred_element_type=F32)
    out_ref[...] = merged.astype(out_ref.dtype)


def _merge(h, w_gates, oa, ob, oc, wa, wb, wc, tm=1024, tn=512):
    m, d = h.shape
    nj = d // tn
    row = lambda w: pl.BlockSpec((tm, w), lambda i, j: (i, 0))
    col = lambda k: pl.BlockSpec((k, tn), lambda i, j: (0, j))
    return pl.pallas_call(
        _merge_kernel,
        grid=(m // tm, nj),
        in_specs=[row(d), col(d), col(d), col(d), row(A_OUT_W), row(B_QW), row(C_W),
                  col(A_OUT_W), col(B_QW), col(C_W)],
        out_specs=pl.BlockSpec((tm, tn), lambda i, j: (i, j)),
        out_shape=jax.ShapeDtypeStruct((m, d), BF16),
        compiler_params=_cparams("parallel", "arbitrary", vmem_mb=48),
        name="merge",
    )(h, *w_gates, oa, ob, oc, wa, wb, wc)


def _out_proj_kernel(mg_ref, w_ref, x_ref, g_ref, x1_ref, h2_ref, *, r_sub):
    def matmul(c):
        return jnp.dot(mg_ref[c * r_sub:(c + 1) * r_sub, :], w_ref[...], preferred_element_type=F32)

    def epilogue(c, acc):
        rows = slice(c * r_sub, (c + 1) * r_sub)
        x1 = x_ref[rows, :] + acc
        x1_ref[rows, :] = x1
        ms = jnp.mean(x1 * x1, axis=-1, keepdims=True)
        h2_ref[rows, :] = (x1 * lax.rsqrt(ms + EPS) * g_ref[...]).astype(h2_ref.dtype)

    _software_pipeline(mg_ref.shape[0] // r_sub, matmul, epilogue)


def _out_proj(merged, w_o, x2d, g_ffn, tm=512, r_sub=512):
    m, d = x2d.shape
    return pl.pallas_call(
        functools.partial(_out_proj_kernel, r_sub=r_sub),
        grid=(m // tm,),
        in_specs=[pl.BlockSpec((tm, d), lambda i: (i, 0)),
                  pl.BlockSpec((d, d), lambda i: (0, 0)),
                  pl.BlockSpec((tm, d), lambda i: (i, 0)),
                  pl.BlockSpec((1, d), lambda i: (0, 0))],
        out_specs=[pl.BlockSpec((tm, d), lambda i: (i, 0))] * 2,
        out_shape=[jax.ShapeDtypeStruct((m, d), F32), jax.ShapeDtypeStruct((m, d), BF16)],
        compiler_params=_cparams("parallel"),
        name="out_proj",
    )(merged, w_o, x2d, g_ffn.reshape(1, d))


def _ffn_kernel(h_ref, wa_ref, wb_ref, wo_ref, x1_ref, out_ref, *, n_slabs):
    f = pl.program_id(1)
    slab_w = x1_ref.shape[1]

    @pl.when(f == 0)
    def _():
        out_ref[...] = jnp.zeros_like(out_ref)

    for s in range(n_slabs):
        @pl.when(f == s)
        def _():
            out_ref[:, s * slab_w:(s + 1) * slab_w] += x1_ref[...]

    h = h_ref[...]
    a = jnp.dot(h, wa_ref[...], preferred_element_type=F32)
    b = jnp.dot(h, wb_ref[...], preferred_element_type=F32)
    act = (a * jax.nn.sigmoid(a) * b).astype(BF16)
    out_ref[...] += jnp.dot(act, wo_ref[...], preferred_element_type=F32)


def _ffn(h2, w_a, w_b, w_out, x1, tm=1024, tf=512, slab_w=256):
    m, d = h2.shape
    d_ff = w_out.shape[0]
    nf = d_ff // tf
    n_slabs = d // slab_w
    assert n_slabs <= nf
    return pl.pallas_call(
        functools.partial(_ffn_kernel, n_slabs=n_slabs),
        grid=(m // tm, nf),
        in_specs=[pl.BlockSpec((tm, d), lambda i, f: (i, 0)),
                  pl.BlockSpec((d, tf), lambda i, f: (0, f)),
                  pl.BlockSpec((d, tf), lambda i, f: (0, f)),
                  pl.BlockSpec((tf, d), lambda i, f: (f, 0)),
                  pl.BlockSpec((tm, slab_w), lambda i, f: (i, jnp.minimum(f, n_slabs - 1)))],
        out_specs=pl.BlockSpec((tm, d), lambda i, f: (i, 0)),
        out_shape=jax.ShapeDtypeStruct((m, d), F32),
        compiler_params=_cparams("parallel", "arbitrary", vmem_mb=48),
        name="ffn",
    )(h2, w_a, w_b, w_out, x1)


def _deinterleave_cols(w, heads):
    d = w.shape[0]
    return w.reshape(d, heads, HEAD_DIM // 2, 2).transpose(0, 1, 3, 2).reshape(d, heads * HEAD_DIM)


def _rope_tables(seq):
    rows = seq // GRID_W
    r = np.repeat(np.arange(rows), GRID_W).astype(np.float64)
    c = np.tile(np.arange(GRID_W), rows).astype(np.float64)
    nf = HEAD_DIM // 4
    inv = ROPE_THETA ** (-np.arange(nf, dtype=np.float64) / nf)
    ang = np.concatenate([r[:, None] * inv, c[:, None] * inv], axis=-1)
    cos, sin = np.cos(ang).T, np.sin(ang).T
    return (np.concatenate([cos, cos], axis=0).astype(np.float32),
            np.concatenate([-sin, sin], axis=0).astype(np.float32))


def _layer(x2d, mem, bias_a, cos_t, sin_t, b, seq, g_mix, w_in, g_qa, g_ka, g_qb, g_kb, g_mem, w_mem_kv,
           g_qc, g_kc, w_br_a, w_br_b, w_br_c, w_o, g_ffn, w_ffn_in, w_ffn_out):
    d = x2d.shape[1]
    o_qa, o_ka, o_va, o_qb, o_kb, o_vb, o_qc, o_gt = np.cumsum(
        (0, A_W, A_W, A_W, B_QW, B_KVW, B_KVW, C_W))
    d_ff = w_ffn_out.shape[0]
    w_all = _cast_cols(w_in, int(o_gt))
    seg = lambda lo, hi: w_all[:, lo:hi]
    w_b = jnp.concatenate([_deinterleave_cols(seg(o_qb, o_kb), B_Q_HEADS),
                           _deinterleave_cols(seg(o_kb, o_vb), B_KV_HEADS),
                           seg(o_vb, o_qc), seg(o_va, o_qb)], axis=1)
    g_norm = jnp.concatenate([jnp.tile(g_qa, A_HEADS), jnp.tile(g_ka, A_HEADS), jnp.tile(g_qc, C_HEADS)])
    deint = lambda g: g.reshape(HEAD_DIM // 2, 2).T.reshape(HEAD_DIM)

    n_qk_tiles = (2 * A_W) // C_W
    qkn, h = _proj_norm(x2d, g_mix, w_all, g_norm,
                        lambda j: jnp.where(j < n_qk_tiles, j, int(o_qc) // C_W), tn=C_W)
    qkn = qkn.reshape(b, seq, -1)
    late_weights = [(w_ffn_out, 0, d), (w_o, 0, d), (w_br_a, 0, d), (w_br_b, 0, d), (w_br_c, 0, d),
                    (w_mem_kv, 0, 2 * C_W)]
    qbt, kb, vbt, va, w_out_b, w_o_b, wa_b, wb_b, wc_b, w_kv_b = _proj_t(
        h, w_b, deint(g_qb) * (SCALE * LOG2E), deint(g_kb), cos_t, sin_t, b, seq, late_weights)
    kb = kb.reshape(b, seq, B_KVW)
    va = va.reshape(b, seq, A_W)

    oa = _mixer_a(qkn, va, bias_a).reshape(b * seq, A_OUT_W)
    gate_jobs = [(w_in, int(o_gt) // d + br, d) for br in range(N_BRANCH)]
    ob, w_ffa, w_ffb, *w_gates = _mixer_b(qbt, kb, vbt, [(w_ffn_in, 0, d_ff), (w_ffn_in, 1, d_ff)] + gate_jobs)
    ob = ob.reshape(b * seq, B_QW)

    oc = _mixer_c(qkn, mem, g_mem, w_kv_b, g_kc).reshape(b * seq, C_W)

    merged = _merge(h, w_gates, oa, ob, oc, wa_b, wb_b, wc_b)
    x1, h2 = _out_proj(merged, w_o_b, x2d, g_ffn)
    return _ffn(h2, w_ffa, w_ffb, w_out_b, x1)


def kernel(x, mem, rel_bias, g_mix, w_in, g_qa, g_ka, g_qb, g_kb, g_mem, w_mem_kv, g_qc, g_kc,
           w_br_a, w_br_b, w_br_c, w_o, g_ffn, w_ffn_in, w_ffn_out):
    b, seq, d = x.shape
    depth = w_in.shape[0]
    cos_t, sin_t = _rope_tables(seq)
    bias_a = _a_bias(rel_bias)
    x2d = x.reshape(b * seq, d)
    for layer in range(depth):
        x2d = _layer(x2d, mem, bias_a, cos_t, sin_t, b, seq,
                     g_mix[layer], w_in[layer], g_qa[layer], g_ka[layer], g_qb[layer], g_kb[layer],
                     g_mem[layer], w_mem_kv[layer], g_qc[layer], g_kc[layer],
                     w_br_a[layer], w_br_b[layer], w_br_c[layer], w_o[layer], g_ffn[layer],
                     w_ffn_in[layer], w_ffn_out[layer])
    return x2d.reshape(b, seq, d)
```

```python
import functools
import math

import numpy as np
import jax
import jax.numpy as jnp
from jax import lax
from jax.experimental import pallas as pl
from jax.experimental.pallas import tpu as pltpu

HEAD_DIM = 128
GRID_W = 64
DIL_PAIRS = ((128, 1), (512, 4), (2048, 16))
A_HEADS_PER_GROUP = 2
A_HEADS = A_HEADS_PER_GROUP * len(DIL_PAIRS)
B_Q_HEADS = 6
B_KV_HEADS = 2
ROPE_THETA = 10000.0
C_HEADS = 4
N_BRANCH = 3
REL_BUCKETS = 32
REL_MAX_DIST = 1024
EPS = 1e-6
NEG = -1e30

A_W = A_HEADS * HEAD_DIM
A_OUT_W = A_HEADS_PER_GROUP * HEAD_DIM
B_QW = B_Q_HEADS * HEAD_DIM
B_KVW = B_KV_HEADS * HEAD_DIM
C_W = C_HEADS * HEAD_DIM
IN_OFFSETS = tuple(int(v) for v in np.cumsum((0, A_W, A_W, A_W, B_QW, B_KVW, B_KVW, C_W)))

SCALE = 1.0 / math.sqrt(HEAD_DIM)
LOG2E = math.log2(math.e)

A_QROWS = 128
A_KWIN = 256
A_RADIUS = 64
A_NOFF = 3
A_BATCH = 8

BF16 = jnp.bfloat16
F32 = jnp.float32
BF16_TILE_ROWS = 16
F32_TILE_ROWS = 8

_NT = (((1,), (1,)), ((), ()))


def _software_pipeline(n_chunks, matmul, epilogue):
    acc = matmul(0)
    for c in range(n_chunks):
        nxt = matmul(c + 1) if c + 1 < n_chunks else None
        epilogue(c, acc)
        acc = nxt


def _cast_jobs(jobs, n_steps, step_index):
    in_specs, out_specs, out_shapes = [], [], []
    for src, col_block, width in jobs:
        rows = src.shape[0]
        rt = rows // n_steps
        assert rt * n_steps == rows and rt % BF16_TILE_ROWS == 0 and src.shape[1] % width == 0
        in_specs.append(pl.BlockSpec((rt, width), lambda *g, cb=col_block: (step_index(*g), cb)))
        out_specs.append(pl.BlockSpec((rt, width), lambda *g: (step_index(*g), 0)))
        out_shapes.append(jax.ShapeDtypeStruct((rows, width), BF16))
    return in_specs, out_specs, out_shapes


def _run_cast_jobs(src_refs, dst_refs):
    for src, dst in zip(src_refs, dst_refs):
        dst[...] = src[...].astype(dst.dtype)


def _head_weights_kernel(src_ref, perm_ref, wn_ref, wb_ref):
    o_qa, o_ka, o_va, o_qb, o_kb, o_vb, o_qc, o_gt = IN_OFFSETS
    cast = lambda lo, hi: src_ref[:, lo:hi].astype(BF16)
    wn_ref[:, :o_va] = cast(o_qa, o_va)
    wn_ref[:, o_va:] = cast(o_qc, o_gt)
    n_rot = B_QW + B_KVW
    for c0 in range(0, n_rot, HEAD_DIM):
        head = jnp.dot(cast(o_qb + c0, o_qb + c0 + HEAD_DIM), perm_ref[...], preferred_element_type=F32)
        wb_ref[:, c0:c0 + HEAD_DIM] = head.astype(BF16)
    wb_ref[:, n_rot:n_rot + B_KVW] = cast(o_vb, o_qc)
    wb_ref[:, n_rot + B_KVW:] = cast(o_va, o_qb)


def _head_weights(w_in, row_tile=256):
    rows = w_in.shape[0]
    n_head_cols = int(IN_OFFSETS[-1])
    assert rows % row_tile == 0
    half = HEAD_DIM // 2
    perm = np.zeros((HEAD_DIM, HEAD_DIM), np.float32)
    for j in range(HEAD_DIM):
        perm[2 * j if j < half else 2 * (j - half) + 1, j] = 1.0
    return pl.pallas_call(
        _head_weights_kernel,
        grid=(rows // row_tile,),
        in_specs=[pl.BlockSpec((row_tile, n_head_cols), lambda i: (i, 0)),
                  pl.BlockSpec((HEAD_DIM, HEAD_DIM), lambda i: (0, 0))],
        out_specs=[pl.BlockSpec((row_tile, 2 * A_W + C_W), lambda i: (i, 0)),
                   pl.BlockSpec((row_tile, B_QW + 2 * B_KVW + A_W), lambda i: (i, 0))],
        out_shape=[jax.ShapeDtypeStruct((rows, 2 * A_W + C_W), BF16),
                   jax.ShapeDtypeStruct((rows, B_QW + 2 * B_KVW + A_W), BF16)],
        compiler_params=_cparams("parallel"),
        name="head_weights",
    )(w_in, jnp.asarray(perm, BF16))


def _cparams(*sem, vmem_mb=None):
    limit = None if vmem_mb is None else vmem_mb * 1024 * 1024
    return pltpu.CompilerParams(dimension_semantics=sem, vmem_limit_bytes=limit)


def _proj_norm_kernel(x_hbm, gm_ref, w_ref, g_ref, o_ref, h_ref, xbuf, sem, *, r_sub, n_j):
    i, j = pl.program_id(0), pl.program_id(1)
    n_i = pl.num_programs(0)
    tm = h_ref.shape[0]
    piece = tm // n_j

    def x_copy(tile, p):
        slot = tile % 2
        rows = pl.ds(tile * tm + p * piece, piece)
        return pltpu.make_async_copy(x_hbm.at[rows, :], xbuf.at[slot, pl.ds(p * piece, piece), :], sem.at[slot, p])

    @pl.when((i == 0) & (j == 0))
    def _():
        for p in range(n_j):
            x_copy(0, p).start()

    @pl.when(i + 1 < n_i)
    def _():
        for p in range(n_j):
            @pl.when(j == p)
            def _():
                x_copy(i + 1, p).start()

    @pl.when(j == 0)
    def _():
        for p in range(n_j):
            x_copy(i, p).wait()
        x = xbuf[i % 2]
        ms = jnp.mean(x * x, axis=-1, keepdims=True)
        h_ref[...] = (x * lax.rsqrt(ms + EPS) * gm_ref[...]).astype(h_ref.dtype)

    def matmul(c):
        return jnp.dot(h_ref[c * r_sub:(c + 1) * r_sub, :], w_ref[...], preferred_element_type=F32)

    def epilogue(c, acc):
        for hh in range(o_ref.shape[1] // HEAD_DIM):
            sl = slice(hh * HEAD_DIM, (hh + 1) * HEAD_DIM)
            y = acc[:, sl]
            ms = jnp.mean(y * y, axis=-1, keepdims=True)
            o_ref[c * r_sub:(c + 1) * r_sub, sl] = (y * lax.rsqrt(ms + EPS) * g_ref[:, sl]).astype(o_ref.dtype)

    _software_pipeline(h_ref.shape[0] // r_sub, matmul, epilogue)


def _proj_norm(x2d, g_mix, w, gains, col_block, tm=1024, tn=512, r_sub=256):
    m, d = x2d.shape
    n = gains.shape[0]
    n_j = n // tn
    assert n % tn == 0 and m % tm == 0 and tm % (F32_TILE_ROWS * n_j) == 0
    return pl.pallas_call(
        functools.partial(_proj_norm_kernel, r_sub=r_sub, n_j=n_j),
        grid=(m // tm, n_j),
        in_specs=[pl.BlockSpec(memory_space=pl.ANY),
                  pl.BlockSpec((1, d), lambda i, j: (0, 0)),
                  pl.BlockSpec((d, tn), lambda i, j: (0, col_block(j))),
                  pl.BlockSpec((1, tn), lambda i, j: (0, j))],
        out_specs=[pl.BlockSpec((tm, tn), lambda i, j: (i, j)),
                   pl.BlockSpec((tm, d), lambda i, j: (i, 0))],
        out_shape=[jax.ShapeDtypeStruct((m, n), BF16), jax.ShapeDtypeStruct((m, d), BF16)],
        scratch_shapes=[pltpu.VMEM((2, tm, d), F32), pltpu.SemaphoreType.DMA((2, n_j))],
        compiler_params=_cparams("arbitrary", "arbitrary", vmem_mb=48),
        name="proj_norm",
    )(x2d, g_mix.reshape(1, d), w, gains.reshape(1, n))


def _proj_t_kernel(w_ref, h_ref, gq_ref, gk_ref, cos_ref, sin_ref, *rest, t_sub, n_cast):
    q_ref, k_ref, v_ref, va_ref = rest[n_cast:n_cast + 4]
    _run_cast_jobs(rest[:n_cast], rest[n_cast + 4:])
    n_q = q_ref.shape[0] // HEAD_DIM
    n_k = k_ref.shape[1] // HEAD_DIM
    n_v = v_ref.shape[0] // HEAD_DIM
    half = HEAD_DIM // 2

    def norm_rope(y, g_ref, tok):
        ms = jnp.mean(y * y, axis=0, keepdims=True)
        y = y * lax.rsqrt(ms + EPS) * g_ref[...]
        partner = jnp.concatenate([y[half:], y[:half]], axis=0)
        return y * cos_ref[:, tok] + partner * sin_ref[:, tok]

    def matmul(c):
        return jnp.dot(h_ref[c * t_sub:(c + 1) * t_sub, :], w_ref[...], preferred_element_type=F32)

    def epilogue(c, y):
        tok = slice(c * t_sub, (c + 1) * t_sub)
        head = lambda hh: y[:, hh * HEAD_DIM:(hh + 1) * HEAD_DIM].T
        for hh in range(n_q):
            q_ref[hh * HEAD_DIM:(hh + 1) * HEAD_DIM, tok] = norm_rope(head(hh), gq_ref, tok).astype(q_ref.dtype)
        for hh in range(n_k):
            kt = norm_rope(head(n_q + hh), gk_ref, tok)
            k_ref[tok, hh * HEAD_DIM:(hh + 1) * HEAD_DIM] = kt.T.astype(k_ref.dtype)
        for hh in range(n_v):
            v_ref[hh * HEAD_DIM:(hh + 1) * HEAD_DIM, tok] = head(n_q + n_k + hh).astype(v_ref.dtype)
        va_ref[tok, :] = y[:, (n_q + n_k + n_v) * HEAD_DIM:].astype(va_ref.dtype)

    _software_pipeline(h_ref.shape[0] // t_sub, matmul, epilogue)


def _proj_t(h, w, gq_col, gk_col, cos_tt, sin_tt, b, seq, cast_jobs, tm=1024, t_sub=128):
    m, d = h.shape
    n = w.shape[1]
    assert n == B_QW + 2 * B_KVW + A_W
    sb = seq // tm
    col = lambda g: jnp.broadcast_to(g[:, None], (HEAD_DIM, t_sub))
    lane_tile = lambda rows: pl.BlockSpec((None, rows, tm), lambda i: (i // sb, 0, i % sb))
    c_in, c_out, c_shapes = _cast_jobs(cast_jobs, m // tm, lambda i: i)
    return pl.pallas_call(
        functools.partial(_proj_t_kernel, t_sub=t_sub, n_cast=len(cast_jobs)),
        grid=(m // tm,),
        in_specs=[pl.BlockSpec((d, n), lambda i: (0, 0)),
                  pl.BlockSpec((tm, d), lambda i: (i, 0)),
                  pl.BlockSpec((HEAD_DIM, t_sub), lambda i: (0, 0)),
                  pl.BlockSpec((HEAD_DIM, t_sub), lambda i: (0, 0)),
                  pl.BlockSpec((HEAD_DIM, tm), lambda i: (0, i % sb)),
                  pl.BlockSpec((HEAD_DIM, tm), lambda i: (0, i % sb))] + c_in,
        out_specs=[lane_tile(B_QW),
                   pl.BlockSpec((tm, B_KVW), lambda i: (i, 0)),
                   lane_tile(B_KVW),
                   pl.BlockSpec((tm, A_W), lambda i: (i, 0))] + c_out,
        out_shape=[jax.ShapeDtypeStruct((b, B_QW, seq), BF16),
                   jax.ShapeDtypeStruct((m, B_KVW), BF16),
                   jax.ShapeDtypeStruct((b, B_KVW, seq), BF16),
                   jax.ShapeDtypeStruct((m, A_W), BF16)] + c_shapes,
        compiler_params=_cparams("parallel", vmem_mb=56),
        name="proj_t",
    )(w, h, col(gq_col), col(gk_col), cos_tt, sin_tt, *[src for src, _, _ in cast_jobs])


def _t5_bucket(rel):
    nb = REL_BUCKETS // 2
    ret = np.where(rel > 0, nb, 0)
    n = np.abs(rel)
    max_exact = nb // 2
    large = max_exact + (np.log(np.maximum(n, 1).astype(np.float32) / np.float32(max_exact))
                         / np.float32(math.log(REL_MAX_DIST / max_exact))
                         * np.float32(nb - max_exact)).astype(np.int32)
    large = np.minimum(large, nb - 1)
    return ret + np.where(n < max_exact, n, large)


def _a_bucket_index():
    qi = np.arange(A_QROWS, dtype=np.int32)[:, None]
    kj = np.arange(A_KWIN, dtype=np.int32)[None, :]
    out = []
    for _, dil in DIL_PAIRS:
        for off in range(A_NOFF):
            rel = kj - qi - A_RADIUS * off
            out.append(np.where(np.abs(rel) <= A_RADIUS, _t5_bucket(rel * dil), -1))
    return np.stack(out).astype(np.int32)


def _a_bias_kernel(tab_ref, bucket_ref, o_ref):
    g = pl.program_id(0)
    for off in range(A_NOFF):
        bk = bucket_ref[off]
        for hh in range(A_HEADS_PER_GROUP):
            acc = jnp.full(bk.shape, NEG, F32)
            for b in range(REL_BUCKETS):
                acc = jnp.where(bk == b, tab_ref[b, g * A_HEADS_PER_GROUP + hh], acc)
            o_ref[off, hh] = acc


def _a_bias(rel_bias):
    n = len(DIL_PAIRS) * A_NOFF
    return pl.pallas_call(
        _a_bias_kernel,
        grid=(len(DIL_PAIRS),),
        in_specs=[pl.BlockSpec(memory_space=pltpu.SMEM),
                  pl.BlockSpec((A_NOFF, A_QROWS, A_KWIN), lambda i: (i, 0, 0))],
        out_specs=pl.BlockSpec((A_NOFF, A_HEADS_PER_GROUP, A_QROWS, A_KWIN), lambda i: (i, 0, 0, 0)),
        out_shape=jax.ShapeDtypeStruct((n, A_HEADS_PER_GROUP, A_QROWS, A_KWIN), F32),
        compiler_params=_cparams("arbitrary"),
        name="a_bias",
    )(rel_bias, _a_bucket_index())


def _mixer_a_kernel(q_ref, k_ref, v_ref, bias_ref, oa_ref, stage, qstage, o_acc, lse_acc, *residue_kv, seq):
    t_rows = q_ref.shape[0]
    ti = pl.program_id(1)
    n_groups = len(DIL_PAIRS)

    def softmax(s):
        m = jnp.max(s, axis=-1, keepdims=True)
        p = jnp.exp(s - m)
        l = jnp.sum(p, axis=-1, keepdims=True)
        return p.astype(BF16), l, m + jnp.log(l)

    def run_group(gi, dil, kres, vres):
        sub_len = seq // dil
        lq = t_rows // dil
        gcols = lambda hh: slice(gi * A_OUT_W + hh * HEAD_DIM, gi * A_OUT_W + (hh + 1) * HEAD_DIM)
        if dil > 1:
            @pl.when(ti == 0)
            def _():
                for src, dst in ((k_ref, kres), (v_ref, vres)):
                    for hh in range(A_HEADS_PER_GROUP):
                        stage[...] = src[:, gcols(hh)].astype(F32)
                        for r in range(dil):
                            dst[r, :, hh * HEAD_DIM:(hh + 1) * HEAD_DIM] = (
                                stage[pl.ds(r, sub_len, stride=dil), :].astype(BF16))

            for hh in range(A_HEADS_PER_GROUP):
                qstage[hh] = q_ref[:, gcols(hh)].astype(F32)

        def scores(r, i, hh):
            q0 = ti * lq + i * A_QROWS
            ks = jnp.clip(q0 - A_RADIUS, 0, sub_len - A_KWIN)
            off = lax.shift_right_logical(q0 - ks, int(math.log2(A_RADIUS)))
            ks = pl.multiple_of(ks, A_RADIUS)
            if dil > 1:
                cols = slice(hh * HEAD_DIM, (hh + 1) * HEAD_DIM)
                rows = pl.ds(i * A_QROWS * dil + r, A_QROWS, stride=dil)
                q = qstage[hh, rows, :].astype(BF16)
                k = kres[r, pl.ds(ks, A_KWIN), cols]
                v = vres[r, pl.ds(ks, A_KWIN), cols]
            else:
                rows = pl.ds(i * A_QROWS, A_QROWS)
                q = q_ref[rows, gcols(hh)]
                k = k_ref[pl.ds(ks, A_KWIN), gcols(hh)]
                v = v_ref[pl.ds(ks, A_KWIN), gcols(hh)]
            s = lax.dot_general(q, k, _NT, preferred_element_type=F32) * SCALE + bias_ref[gi * A_NOFF + off, hh]
            return rows, s, v

        def fold(hh, rows, o, lse):
            lse = jnp.broadcast_to(lse, (A_QROWS, HEAD_DIM))
            if gi > 0:
                prev_o, prev_lse = o_acc[hh, rows, :], lse_acc[hh, rows, :]
                m = jnp.maximum(prev_lse, lse)
                w_prev, w_new = jnp.exp(prev_lse - m), jnp.exp(lse - m)
                den = w_prev + w_new
                o = (w_prev * prev_o + w_new * o) / den
                lse = m + jnp.log(den)
            o_acc[hh, rows, :] = o
            if gi + 1 < n_groups:
                lse_acc[hh, rows, :] = lse

        items = [(r, i, hh) for r in range(dil) for i in range(lq // A_QROWS) for hh in range(A_HEADS_PER_GROUP)]
        for b0 in range(0, len(items), A_BATCH):
            batch = items[b0:b0 + A_BATCH]
            staged = [scores(*it) for it in batch]
            probs = [softmax(s) for _, s, _ in staged]
            for (_, _, hh), (rows, _, v), (p, l, lse) in zip(batch, staged, probs):
                fold(hh, rows, jnp.dot(p, v, preferred_element_type=F32) / l, lse)

    strided = [gi for gi, (_, dil) in enumerate(DIL_PAIRS) if dil > 1]
    for gi, (_, dil) in enumerate(DIL_PAIRS):
        kres, vres = (residue_kv[2 * strided.index(gi):2 * strided.index(gi) + 2] if dil > 1 else (None, None))
        run_group(gi, dil, kres, vres)
    for hh in range(A_HEADS_PER_GROUP):
        oa_ref[:, hh * HEAD_DIM:(hh + 1) * HEAD_DIM] = o_acc[hh].astype(oa_ref.dtype)


def _mixer_a(qkn, va, bias_a, t_rows=2048):
    b, seq, _ = qkn.shape
    head_buf = lambda rows: pltpu.VMEM((A_HEADS_PER_GROUP, rows, HEAD_DIM), F32)
    scratch = [pltpu.VMEM((seq, HEAD_DIM), F32), head_buf(t_rows), head_buf(t_rows), head_buf(t_rows)]
    for _, dil in DIL_PAIRS:
        if dil > 1:
            scratch += [pltpu.VMEM((dil, seq // dil, A_OUT_W), BF16)] * 2
    return pl.pallas_call(
        functools.partial(_mixer_a_kernel, seq=seq),
        grid=(b, seq // t_rows),
        in_specs=[pl.BlockSpec((None, t_rows, A_W), lambda bi, ti: (bi, ti, 0)),
                  pl.BlockSpec((None, seq, A_W), lambda bi, ti: (bi, 0, 1)),
                  pl.BlockSpec((None, seq, A_W), lambda bi, ti: (bi, 0, 0)),
                  pl.BlockSpec(bias_a.shape, lambda bi, ti: (0, 0, 0, 0), pipeline_mode=pl.Buffered(1))],
        out_specs=pl.BlockSpec((None, t_rows, A_OUT_W), lambda bi, ti: (bi, ti, 0)),
        out_shape=jax.ShapeDtypeStruct((b, seq, A_OUT_W), BF16),
        scratch_shapes=scratch,
        compiler_params=_cparams("parallel", "arbitrary", vmem_mb=62),
        name="mixer_a",
    )(qkn, qkn, va, bias_a)


def _mixer_b_kernel(qt_ref, k_ref, vt_ref, *rest, tk, group, w, n_cast):
    o_ref, s_scr = rest[n_cast], rest[-1]
    _run_cast_jobs(rest[:n_cast], rest[n_cast + 1:-1])
    tq = qt_ref.shape[1]
    seq = k_ref.shape[0]
    n_chunks = seq // tk
    units = [(i, j) for i in range(group) for j in range(tq // w)]

    def pass_a(u, ci, m):
        i, j = units[u]
        qt = qt_ref[i * HEAD_DIM:(i + 1) * HEAD_DIM, j * w:(j + 1) * w]
        st = jnp.dot(k_ref[ci * tk:(ci + 1) * tk, :], qt, preferred_element_type=F32)
        s_scr[u % 2, ci * tk:(ci + 1) * tk, :] = st
        return jnp.maximum(m, jnp.max(st, axis=0, keepdims=True))

    def pass_b(u, ci, m, l, acc):
        pt = jnp.exp2(s_scr[u % 2, ci * tk:(ci + 1) * tk, :] - m)
        l = l + jnp.sum(pt, axis=0, keepdims=True)
        acc = acc + jnp.dot(vt_ref[:, ci * tk:(ci + 1) * tk], pt.astype(BF16), preferred_element_type=F32)
        return l, acc

    m_prev = None
    for s in range(len(units) + 1):
        m_cur = jnp.full((1, w), NEG, F32)
        l = jnp.zeros((1, w), F32)
        acc = jnp.zeros((HEAD_DIM, w), F32)
        for ci in range(n_chunks):
            if s < len(units):
                m_cur = pass_a(s, ci, m_cur)
            if s > 0:
                l, acc = pass_b(s - 1, ci, m_prev, l, acc)
        if s > 0:
            i, j = units[s - 1]
            o_ref[j * w:(j + 1) * w, i * HEAD_DIM:(i + 1) * HEAD_DIM] = (acc / l).T.astype(o_ref.dtype)
        m_prev = m_cur


def _mixer_b(qbt, kb, vbt, cast_jobs, tq=512, tk=512, w=256):
    b, seq, _ = kb.shape
    group = B_Q_HEADS // B_KV_HEADS
    gw = group * HEAD_DIM
    nq = seq // tq
    n_steps = b * B_KV_HEADS * nq
    c_in, c_out, c_shapes = _cast_jobs(cast_jobs, n_steps, lambda bi, kv, qi: (bi * B_KV_HEADS + kv) * nq + qi)
    return pl.pallas_call(
        functools.partial(_mixer_b_kernel, tk=tk, group=group, w=w, n_cast=len(cast_jobs)),
        grid=(b, B_KV_HEADS, nq),
        scratch_shapes=[pltpu.VMEM((2, seq, w), F32)],
        in_specs=[pl.BlockSpec((None, gw, tq), lambda bi, kv, qi: (bi, kv, qi)),
                  pl.BlockSpec((None, seq, HEAD_DIM), lambda bi, kv, qi: (bi, 0, kv)),
                  pl.BlockSpec((None, HEAD_DIM, seq), lambda bi, kv, qi: (bi, kv, 0))] + c_in,
        out_specs=[pl.BlockSpec((None, tq, gw), lambda bi, kv, qi: (bi, qi, kv))] + c_out,
        out_shape=[jax.ShapeDtypeStruct((b, seq, B_QW), BF16)] + c_shapes,
        compiler_params=_cparams("parallel", "parallel", "arbitrary"),
        name="mixer_b",
    )(qbt, kb, vbt, *[src for src, _, _ in cast_jobs])


def _mixer_c_kernel(q_ref, mem_ref, gm_ref, w_ref, gk_ref, o_ref, k_scr, v_scr):
    @pl.when(pl.program_id(1) == 0)
    def _():
        x = mem_ref[...]
        ms = jnp.mean(x * x, axis=-1, keepdims=True)
        hm = (x * lax.rsqrt(ms + EPS) * gm_ref[...]).astype(BF16)
        kv = jnp.dot(hm, w_ref[...], preferred_element_type=F32)
        for hh in range(C_HEADS):
            sl = slice(hh * HEAD_DIM, (hh + 1) * HEAD_DIM)
            y = kv[:, sl]
            ms = jnp.mean(y * y, axis=-1, keepdims=True)
            k_scr[:, sl] = (y * lax.rsqrt(ms + EPS) * gk_ref[...]).astype(BF16)
        v_scr[...] = kv[:, C_W:].astype(BF16)

    for hh in range(C_HEADS):
        sl = slice(hh * HEAD_DIM, (hh + 1) * HEAD_DIM)
        s = lax.dot_general(q_ref[:, sl], k_scr[:, sl], _NT, preferred_element_type=F32) * SCALE
        m = jnp.max(s, axis=-1, keepdims=True)
        p = jnp.exp(s - m)
        l = jnp.sum(p, axis=-1, keepdims=True)
        o = jnp.dot(p.astype(BF16), v_scr[:, sl], preferred_element_type=F32) / l
        o_ref[:, sl] = o.astype(o_ref.dtype)


def _mixer_c(qkn, mem, g_mem, w_kv, g_kc, tq=1024):
    b, seq, _ = qkn.shape
    _, n_mem, d = mem.shape
    qc_blk = (2 * A_W) // C_W
    return pl.pallas_call(
        _mixer_c_kernel,
        grid=(b, seq // tq),
        in_specs=[pl.BlockSpec((None, tq, C_W), lambda bi, qi: (bi, qi, qc_blk)),
                  pl.BlockSpec((None, n_mem, d), lambda bi, qi: (bi, 0, 0)),
                  pl.BlockSpec((1, d), lambda bi, qi: (0, 0)),
                  pl.BlockSpec((d, 2 * C_W), lambda bi, qi: (0, 0)),
                  pl.BlockSpec((1, HEAD_DIM), lambda bi, qi: (0, 0))],
        out_specs=pl.BlockSpec((None, tq, C_W), lambda bi, qi: (bi, qi, 0)),
        out_shape=jax.ShapeDtypeStruct((b, seq, C_W), BF16),
        scratch_shapes=[pltpu.VMEM((n_mem, C_W), BF16), pltpu.VMEM((n_mem, C_W), BF16)],
        compiler_params=_cparams("parallel", "arbitrary"),
        name="mixer_c",
    )(qkn, mem, g_mem.reshape(1, d), w_kv, g_kc.reshape(1, HEAD_DIM))


def _merge_kernel(h_ref, wg0_ref, wg1_ref, wg2_ref, oa_ref, ob_ref, oc_ref, wa_ref, wb_ref, wc_ref, out_ref):
    h = h_ref[...]
    ga = jax.nn.sigmoid(jnp.dot(h, wg0_ref[...], preferred_element_type=F32))
    merged = ga * jnp.dot(oa_ref[...], wa_ref[...], preferred_element_type=F32)
    gb = jax.nn.sigmoid(jnp.dot(h, wg1_ref[...], preferred_element_type=F32))
    merged += gb * jnp.dot(ob_ref[...], wb_ref[...], preferred_element_type=F32)
    gc = jax.nn.sigmoid(jnp.dot(h, wg2_ref[...], preferred_element_type=F32))
    merged += gc * jnp.dot(oc_ref[...], wc_ref[...], preferred_element_type=F32)
    out_ref[...] = merged.astype(out_ref.dtype)


def _merge(h, w_gates, oa, ob, oc, wa, wb, wc, tm=1024, tn=512):
    m, d = h.shape
    nj = d // tn
    row = lambda w: pl.BlockSpec((tm, w), lambda i, j: (i, 0))
    col = lambda k: pl.BlockSpec((k, tn), lambda i, j: (0, j))
    return pl.pallas_call(
        _merge_kernel,
        grid=(m // tm, nj),
        in_specs=[row(d), col(d), col(d), col(d), row(A_OUT_W), row(B_QW), row(C_W),
                  col(A_OUT_W), col(B_QW), col(C_W)],
        out_specs=pl.BlockSpec((tm, tn), lambda i, j: (i, j)),
        out_shape=jax.ShapeDtypeStruct((m, d), BF16),
        compiler_params=_cparams("parallel", "arbitrary", vmem_mb=48),
        name="merge",
    )(h, *w_gates, oa, ob, oc, wa, wb, wc)


def _out_proj_kernel(mg_ref, w_ref, x_ref, g_ref, x1_ref, h2_ref, *, r_sub):
    def matmul(c):
        return jnp.dot(mg_ref[c * r_sub:(c + 1) * r_sub, :], w_ref[...], preferred_element_type=F32)

    def epilogue(c, acc):
        rows = slice(c * r_sub, (c + 1) * r_sub)
        x1 = x_ref[rows, :] + acc
        x1_ref[rows, :] = x1
        ms = jnp.mean(x1 * x1, axis=-1, keepdims=True)
        h2_ref[rows, :] = (x1 * lax.rsqrt(ms + EPS) * g_ref[...]).astype(h2_ref.dtype)

    _software_pipeline(mg_ref.shape[0] // r_sub, matmul, epilogue)


def _out_proj(merged, w_o, x2d, g_ffn, tm=512, r_sub=512):
    m, d = x2d.shape
    return pl.pallas_call(
        functools.partial(_out_proj_kernel, r_sub=r_sub),
        grid=(m // tm,),
        in_specs=[pl.BlockSpec((tm, d), lambda i: (i, 0)),
                  pl.BlockSpec((d, d), lambda i: (0, 0)),
                  pl.BlockSpec((tm, d), lambda i: (i, 0)),
                  pl.BlockSpec((1, d), lambda i: (0, 0))],
        out_specs=[pl.BlockSpec((tm, d), lambda i: (i, 0))] * 2,
        out_shape=[jax.ShapeDtypeStruct((m, d), F32), jax.ShapeDtypeStruct((m, d), BF16)],
        compiler_params=_cparams("parallel"),
        name="out_proj",
    )(merged, w_o, x2d, g_ffn.reshape(1, d))


def _ffn_kernel(h_ref, wa_ref, wb_ref, wo_ref, x1_ref, out_ref, *, n_slabs):
    f = pl.program_id(1)
    slab_w = x1_ref.shape[1]

    @pl.when(f == 0)
    def _():
        out_ref[...] = jnp.zeros_like(out_ref)

    for s in range(n_slabs):
        @pl.when(f == s)
        def _():
            out_ref[:, s * slab_w:(s + 1) * slab_w] += x1_ref[...]

    h = h_ref[...]
    a = jnp.dot(h, wa_ref[...], preferred_element_type=F32)
    b = jnp.dot(h, wb_ref[...], preferred_element_type=F32)
    act = (a * jax.nn.sigmoid(a) * b).astype(BF16)
    out_ref[...] += jnp.dot(act, wo_ref[...], preferred_element_type=F32)


def _ffn(h2, w_a, w_b, w_out, x1, tm=1024, tf=512, slab_w=256):
    m, d = h2.shape
    d_ff = w_out.shape[0]
    nf = d_ff // tf
    n_slabs = d // slab_w
    assert n_slabs <= nf
    return pl.pallas_call(
        functools.partial(_ffn_kernel, n_slabs=n_slabs),
        grid=(m // tm, nf),
        in_specs=[pl.BlockSpec((tm, d), lambda i, f: (i, 0)),
                  pl.BlockSpec((d, tf), lambda i, f: (0, f)),
                  pl.BlockSpec((d, tf), lambda i, f: (0, f)),
                  pl.BlockSpec((tf, d), lambda i, f: (f, 0)),
                  pl.BlockSpec((tm, slab_w), lambda i, f: (i, jnp.minimum(f, n_slabs - 1)))],
        out_specs=pl.BlockSpec((tm, d), lambda i, f: (i, 0)),
        out_shape=jax.ShapeDtypeStruct((m, d), F32),
        compiler_params=_cparams("parallel", "arbitrary", vmem_mb=48),
        name="ffn",
    )(h2, w_a, w_b, w_out, x1)


def _rope_tables(seq):
    rows = seq // GRID_W
    r = np.repeat(np.arange(rows), GRID_W).astype(np.float64)
    c = np.tile(np.arange(GRID_W), rows).astype(np.float64)
    nf = HEAD_DIM // 4
    inv = ROPE_THETA ** (-np.arange(nf, dtype=np.float64) / nf)
    ang = np.concatenate([r[:, None] * inv, c[:, None] * inv], axis=-1)
    cos, sin = np.cos(ang).T, np.sin(ang).T
    return (np.concatenate([cos, cos], axis=0).astype(np.float32),
            np.concatenate([-sin, sin], axis=0).astype(np.float32))


def _layer(x2d, mem, bias_a, cos_t, sin_t, b, seq, g_mix, w_in, g_qa, g_ka, g_qb, g_kb, g_mem, w_mem_kv,
           g_qc, g_kc, w_br_a, w_br_b, w_br_c, w_o, g_ffn, w_ffn_in, w_ffn_out):
    d = x2d.shape[1]
    o_gt = IN_OFFSETS[-1]
    d_ff = w_ffn_out.shape[0]
    w_norm, w_b = _head_weights(w_in)
    g_norm = jnp.concatenate([jnp.tile(g_qa, A_HEADS), jnp.tile(g_ka, A_HEADS), jnp.tile(g_qc, C_HEADS)])
    deint = lambda g: g.reshape(HEAD_DIM // 2, 2).T.reshape(HEAD_DIM)

    qkn, h = _proj_norm(x2d, g_mix, w_norm, g_norm, lambda j: j, tn=C_W)
    qkn = qkn.reshape(b, seq, -1)
    late_weights = [(w_ffn_out, 0, d), (w_o, 0, d), (w_br_a, 0, d), (w_br_b, 0, d), (w_br_c, 0, d),
                    (w_mem_kv, 0, 2 * C_W)]
    qbt, kb, vbt, va, w_out_b, w_o_b, wa_b, wb_b, wc_b, w_kv_b = _proj_t(
        h, w_b, deint(g_qb) * (SCALE * LOG2E), deint(g_kb), cos_t, sin_t, b, seq, late_weights)
    kb = kb.reshape(b, seq, B_KVW)
    va = va.reshape(b, seq, A_W)

    oa = _mixer_a(qkn, va, bias_a).reshape(b * seq, A_OUT_W)
    gate_jobs = [(w_in, int(o_gt) // d + br, d) for br in range(N_BRANCH)]
    ob, w_ffa, w_ffb, *w_gates = _mixer_b(qbt, kb, vbt, [(w_ffn_in, 0, d_ff), (w_ffn_in, 1, d_ff)] + gate_jobs)
    ob = ob.reshape(b * seq, B_QW)

    oc = _mixer_c(qkn, mem, g_mem, w_kv_b, g_kc).reshape(b * seq, C_W)

    merged = _merge(h, w_gates, oa, ob, oc, wa_b, wb_b, wc_b)
    x1, h2 = _out_proj(merged, w_o_b, x2d, g_ffn)
    return _ffn(h2, w_ffa, w_ffb, w_out_b, x1)


def kernel(x, mem, rel_bias, g_mix, w_in, g_qa, g_ka, g_qb, g_kb, g_mem, w_mem_kv, g_qc, g_kc,
           w_br_a, w_br_b, w_br_c, w_o, g_ffn, w_ffn_in, w_ffn_out):
    b, seq, d = x.shape
    depth = w_in.shape[0]
    cos_t, sin_t = _rope_tables(seq)
    bias_a = _a_bias(rel_bias)
    x2d = x.reshape(b * seq, d)
    for layer in range(depth):
        x2d = _layer(x2d, mem, bias_a, cos_t, sin_t, b, seq,
                     g_mix[layer], w_in[layer], g_qa[layer], g_ka[layer], g_qb[layer], g_kb[layer],
                     g_mem[layer], w_mem_kv[layer], g_qc[layer], g_kc[layer],
                     w_br_a[layer], w_br_b[layer], w_br_c[layer], w_o[layer], g_ffn[layer],
                     w_ffn_in[layer], w_ffn_out[layer])
    return x2d.reshape(b, seq, d)
```

```python
import functools
import math

import numpy as np
import jax
import jax.numpy as jnp
from jax import lax
from jax.experimental import pallas as pl
from jax.experimental.pallas import tpu as pltpu

HEAD_DIM = 128
GRID_W = 64
DIL_PAIRS = ((128, 1), (512, 4), (2048, 16))
A_HEADS_PER_GROUP = 2
A_HEADS = A_HEADS_PER_GROUP * len(DIL_PAIRS)
B_Q_HEADS = 6
B_KV_HEADS = 2
ROPE_THETA = 10000.0
C_HEADS = 4
N_BRANCH = 3
REL_BUCKETS = 32
REL_MAX_DIST = 1024
EPS = 1e-6
NEG = -1e30

A_W = A_HEADS * HEAD_DIM
A_OUT_W = A_HEADS_PER_GROUP * HEAD_DIM
B_QW = B_Q_HEADS * HEAD_DIM
B_KVW = B_KV_HEADS * HEAD_DIM
C_W = C_HEADS * HEAD_DIM
IN_OFFSETS = tuple(int(v) for v in np.cumsum((0, A_W, A_W, A_W, B_QW, B_KVW, B_KVW, C_W)))

SCALE = 1.0 / math.sqrt(HEAD_DIM)
LOG2E = math.log2(math.e)

A_QROWS = 128
A_KWIN = 256
A_RADIUS = 64
A_NOFF = 3
A_BATCH = 8

BF16 = jnp.bfloat16
F32 = jnp.float32
BF16_TILE_ROWS = 16
F32_TILE_ROWS = 8

_NT = (((1,), (1,)), ((), ()))


def _software_pipeline(n_chunks, matmul, epilogue):
    acc = matmul(0)
    for c in range(n_chunks):
        nxt = matmul(c + 1) if c + 1 < n_chunks else None
        epilogue(c, acc)
        acc = nxt


def _cast_jobs(jobs, n_steps, step_index):
    in_specs, out_specs, out_shapes = [], [], []
    for src, col_block, width in jobs:
        rows = src.shape[0]
        rt = rows // n_steps
        assert rt * n_steps == rows and rt % BF16_TILE_ROWS == 0 and src.shape[1] % width == 0
        in_specs.append(pl.BlockSpec((rt, width), lambda *g, cb=col_block: (step_index(*g), cb)))
        out_specs.append(pl.BlockSpec((rt, width), lambda *g: (step_index(*g), 0)))
        out_shapes.append(jax.ShapeDtypeStruct((rows, width), BF16))
    return in_specs, out_specs, out_shapes


def _run_cast_jobs(src_refs, dst_refs):
    for src, dst in zip(src_refs, dst_refs):
        dst[...] = src[...].astype(dst.dtype)


def _head_weights_kernel(src_ref, perm_ref, wn_ref, wb_ref):
    o_qa, o_ka, o_va, o_qb, o_kb, o_vb, o_qc, o_gt = IN_OFFSETS
    cast = lambda lo, hi: src_ref[:, lo:hi].astype(BF16)
    wn_ref[:, :o_va] = cast(o_qa, o_va)
    wn_ref[:, o_va:] = cast(o_qc, o_gt)
    n_rot = B_QW + B_KVW
    for c0 in range(0, n_rot, HEAD_DIM):
        head = jnp.dot(cast(o_qb + c0, o_qb + c0 + HEAD_DIM), perm_ref[...], preferred_element_type=F32)
        wb_ref[:, c0:c0 + HEAD_DIM] = head.astype(BF16)
    wb_ref[:, n_rot:n_rot + B_KVW] = cast(o_vb, o_qc)
    wb_ref[:, n_rot + B_KVW:] = cast(o_va, o_qb)


def _head_weights(w_in, row_tile=256):
    rows = w_in.shape[0]
    n_head_cols = int(IN_OFFSETS[-1])
    assert rows % row_tile == 0
    half = HEAD_DIM // 2
    perm = np.zeros((HEAD_DIM, HEAD_DIM), np.float32)
    for j in range(HEAD_DIM):
        perm[2 * j if j < half else 2 * (j - half) + 1, j] = 1.0
    return pl.pallas_call(
        _head_weights_kernel,
        grid=(rows // row_tile,),
        in_specs=[pl.BlockSpec((row_tile, n_head_cols), lambda i: (i, 0)),
                  pl.BlockSpec((HEAD_DIM, HEAD_DIM), lambda i: (0, 0))],
        out_specs=[pl.BlockSpec((row_tile, 2 * A_W + C_W), lambda i: (i, 0)),
                   pl.BlockSpec((row_tile, B_QW + 2 * B_KVW + A_W), lambda i: (i, 0))],
        out_shape=[jax.ShapeDtypeStruct((rows, 2 * A_W + C_W), BF16),
                   jax.ShapeDtypeStruct((rows, B_QW + 2 * B_KVW + A_W), BF16)],
        compiler_params=_cparams("parallel"),
        name="head_weights",
    )(w_in, jnp.asarray(perm, BF16))


def _cparams(*sem, vmem_mb=None):
    limit = None if vmem_mb is None else vmem_mb * 1024 * 1024
    return pltpu.CompilerParams(dimension_semantics=sem, vmem_limit_bytes=limit)


def _proj_norm_kernel(x_hbm, gm_ref, w_ref, g_ref, o_ref, h_ref, xbuf, sem, *, r_sub, n_j):
    i, j = pl.program_id(0), pl.program_id(1)
    n_i = pl.num_programs(0)
    tm = h_ref.shape[0]
    piece = tm // n_j

    def x_copy(tile, p):
        slot = tile % 2
        rows = pl.ds(tile * tm + p * piece, piece)
        return pltpu.make_async_copy(x_hbm.at[rows, :], xbuf.at[slot, pl.ds(p * piece, piece), :], sem.at[slot, p])

    @pl.when((i == 0) & (j == 0))
    def _():
        for p in range(n_j):
            x_copy(0, p).start()

    @pl.when(i + 1 < n_i)
    def _():
        for p in range(n_j):
            @pl.when(j == p)
            def _():
                x_copy(i + 1, p).start()

    @pl.when(j == 0)
    def _():
        for p in range(n_j):
            x_copy(i, p).wait()
        x = xbuf[i % 2]
        ms = jnp.mean(x * x, axis=-1, keepdims=True)
        h_ref[...] = (x * lax.rsqrt(ms + EPS) * gm_ref[...]).astype(h_ref.dtype)

    def matmul(c):
        return jnp.dot(h_ref[c * r_sub:(c + 1) * r_sub, :], w_ref[...], preferred_element_type=F32)

    def epilogue(c, acc):
        for hh in range(o_ref.shape[1] // HEAD_DIM):
            sl = slice(hh * HEAD_DIM, (hh + 1) * HEAD_DIM)
            y = acc[:, sl]
            ms = jnp.mean(y * y, axis=-1, keepdims=True)
            o_ref[c * r_sub:(c + 1) * r_sub, sl] = (y * lax.rsqrt(ms + EPS) * g_ref[:, sl]).astype(o_ref.dtype)

    _software_pipeline(h_ref.shape[0] // r_sub, matmul, epilogue)


def _proj_norm(x2d, g_mix, w, gains, col_block, tm=1024, tn=512, r_sub=256):
    m, d = x2d.shape
    n = gains.shape[0]
    n_j = n // tn
    assert n % tn == 0 and m % tm == 0 and tm % (F32_TILE_ROWS * n_j) == 0
    return pl.pallas_call(
        functools.partial(_proj_norm_kernel, r_sub=r_sub, n_j=n_j),
        grid=(m // tm, n_j),
        in_specs=[pl.BlockSpec(memory_space=pl.ANY),
                  pl.BlockSpec((1, d), lambda i, j: (0, 0)),
                  pl.BlockSpec((d, tn), lambda i, j: (0, col_block(j))),
                  pl.BlockSpec((1, tn), lambda i, j: (0, j))],
        out_specs=[pl.BlockSpec((tm, tn), lambda i, j: (i, j)),
                   pl.BlockSpec((tm, d), lambda i, j: (i, 0))],
        out_shape=[jax.ShapeDtypeStruct((m, n), BF16), jax.ShapeDtypeStruct((m, d), BF16)],
        scratch_shapes=[pltpu.VMEM((2, tm, d), F32), pltpu.SemaphoreType.DMA((2, n_j))],
        compiler_params=_cparams("arbitrary", "arbitrary", vmem_mb=48),
        name="proj_norm",
    )(x2d, g_mix.reshape(1, d), w, gains.reshape(1, n))


def _proj_t_kernel(w_ref, h_ref, gq_ref, gk_ref, cos_ref, sin_ref, *rest, t_sub, n_cast):
    q_ref, k_ref, v_ref, va_ref = rest[n_cast:n_cast + 4]
    _run_cast_jobs(rest[:n_cast], rest[n_cast + 4:])
    n_q = q_ref.shape[0] // HEAD_DIM
    n_k = k_ref.shape[1] // HEAD_DIM
    n_v = v_ref.shape[0] // HEAD_DIM
    half = HEAD_DIM // 2

    def norm_rope(y, g_ref, tok):
        ms = jnp.mean(y * y, axis=0, keepdims=True)
        y = y * lax.rsqrt(ms + EPS) * g_ref[...]
        partner = jnp.concatenate([y[half:], y[:half]], axis=0)
        return y * cos_ref[:, tok] + partner * sin_ref[:, tok]

    def matmul(c):
        return jnp.dot(h_ref[c * t_sub:(c + 1) * t_sub, :], w_ref[...], preferred_element_type=F32)

    def epilogue(c, y):
        tok = slice(c * t_sub, (c + 1) * t_sub)
        head = lambda hh: y[:, hh * HEAD_DIM:(hh + 1) * HEAD_DIM].T
        for hh in range(n_q):
            q_ref[hh * HEAD_DIM:(hh + 1) * HEAD_DIM, tok] = norm_rope(head(hh), gq_ref, tok).astype(q_ref.dtype)
        for hh in range(n_k):
            kt = norm_rope(head(n_q + hh), gk_ref, tok)
            k_ref[tok, hh * HEAD_DIM:(hh + 1) * HEAD_DIM] = kt.T.astype(k_ref.dtype)
        for hh in range(n_v):
            v_ref[hh * HEAD_DIM:(hh + 1) * HEAD_DIM, tok] = head(n_q + n_k + hh).astype(v_ref.dtype)
        va_ref[tok, :] = y[:, (n_q + n_k + n_v) * HEAD_DIM:].astype(va_ref.dtype)

    _software_pipeline(h_ref.shape[0] // t_sub, matmul, epilogue)


def _proj_t(h, w, gq_col, gk_col, cos_tt, sin_tt, b, seq, cast_jobs, tm=1024, t_sub=128):
    m, d = h.shape
    n = w.shape[1]
    assert n == B_QW + 2 * B_KVW + A_W
    sb = seq // tm
    col = lambda g: jnp.broadcast_to(g[:, None], (HEAD_DIM, t_sub))
    lane_tile = lambda rows: pl.BlockSpec((None, rows, tm), lambda i: (i // sb, 0, i % sb))
    c_in, c_out, c_shapes = _cast_jobs(cast_jobs, m // tm, lambda i: i)
    return pl.pallas_call(
        functools.partial(_proj_t_kernel, t_sub=t_sub, n_cast=len(cast_jobs)),
        grid=(m // tm,),
        in_specs=[pl.BlockSpec((d, n), lambda i: (0, 0)),
                  pl.BlockSpec((tm, d), lambda i: (i, 0)),
                  pl.BlockSpec((HEAD_DIM, t_sub), lambda i: (0, 0)),
                  pl.BlockSpec((HEAD_DIM, t_sub), lambda i: (0, 0)),
                  pl.BlockSpec((HEAD_DIM, tm), lambda i: (0, i % sb)),
                  pl.BlockSpec((HEAD_DIM, tm), lambda i: (0, i % sb))] + c_in,
        out_specs=[lane_tile(B_QW),
                   pl.BlockSpec((tm, B_KVW), lambda i: (i, 0)),
                   lane_tile(B_KVW),
                   pl.BlockSpec((tm, A_W), lambda i: (i, 0))] + c_out,
        out_shape=[jax.ShapeDtypeStruct((b, B_QW, seq), BF16),
                   jax.ShapeDtypeStruct((m, B_KVW), BF16),
                   jax.ShapeDtypeStruct((b, B_KVW, seq), BF16),
                   jax.ShapeDtypeStruct((m, A_W), BF16)] + c_shapes,
        compiler_params=_cparams("parallel", vmem_mb=56),
        name="proj_t",
    )(w, h, col(gq_col), col(gk_col), cos_tt, sin_tt, *[src for src, _, _ in cast_jobs])


def _t5_bucket(rel):
    nb = REL_BUCKETS // 2
    ret = np.where(rel > 0, nb, 0)
    n = np.abs(rel)
    max_exact = nb // 2
    large = max_exact + (np.log(np.maximum(n, 1).astype(np.float32) / np.float32(max_exact))
                         / np.float32(math.log(REL_MAX_DIST / max_exact))
                         * np.float32(nb - max_exact)).astype(np.int32)
    large = np.minimum(large, nb - 1)
    return ret + np.where(n < max_exact, n, large)


def _a_bucket_index():
    qi = np.arange(A_QROWS, dtype=np.int32)[:, None]
    kj = np.arange(A_KWIN, dtype=np.int32)[None, :]
    out = []
    for _, dil in DIL_PAIRS:
        for off in range(A_NOFF):
            rel = kj - qi - A_RADIUS * off
            out.append(np.where(np.abs(rel) <= A_RADIUS, _t5_bucket(rel * dil), -1))
    return np.stack(out).astype(np.int32)


def _a_bias_kernel(tab_ref, bucket_ref, o_ref):
    g = pl.program_id(0)
    for off in range(A_NOFF):
        bk = bucket_ref[off]
        for hh in range(A_HEADS_PER_GROUP):
            acc = jnp.full(bk.shape, NEG, F32)
            for b in range(REL_BUCKETS):
                acc = jnp.where(bk == b, tab_ref[b, g * A_HEADS_PER_GROUP + hh], acc)
            o_ref[off, hh] = acc


def _a_bias(rel_bias):
    n = len(DIL_PAIRS) * A_NOFF
    return pl.pallas_call(
        _a_bias_kernel,
        grid=(len(DIL_PAIRS),),
        in_specs=[pl.BlockSpec(memory_space=pltpu.SMEM),
                  pl.BlockSpec((A_NOFF, A_QROWS, A_KWIN), lambda i: (i, 0, 0))],
        out_specs=pl.BlockSpec((A_NOFF, A_HEADS_PER_GROUP, A_QROWS, A_KWIN), lambda i: (i, 0, 0, 0)),
        out_shape=jax.ShapeDtypeStruct((n, A_HEADS_PER_GROUP, A_QROWS, A_KWIN), F32),
        compiler_params=_cparams("arbitrary"),
        name="a_bias",
    )(rel_bias, _a_bucket_index())


def _mixer_a_kernel(q_ref, k_ref, v_ref, bias_ref, oa_ref, stage, qstage, o_acc, lse_acc, *residue_kv, seq):
    t_rows = q_ref.shape[0]
    ti = pl.program_id(1)
    n_groups = len(DIL_PAIRS)

    def softmax(s):
        m = jnp.max(s, axis=-1, keepdims=True)
        p = jnp.exp(s - m)
        l = jnp.sum(p, axis=-1, keepdims=True)
        return p.astype(BF16), l, m + jnp.log(l)

    def run_group(gi, dil, kres, vres):
        sub_len = seq // dil
        lq = t_rows // dil
        gcols = lambda hh: slice(gi * A_OUT_W + hh * HEAD_DIM, gi * A_OUT_W + (hh + 1) * HEAD_DIM)
        if dil > 1:
            @pl.when(ti == 0)
            def _():
                for src, dst in ((k_ref, kres), (v_ref, vres)):
                    for hh in range(A_HEADS_PER_GROUP):
                        stage[...] = src[:, gcols(hh)].astype(F32)
                        for r in range(dil):
                            dst[r, :, hh * HEAD_DIM:(hh + 1) * HEAD_DIM] = (
                                stage[pl.ds(r, sub_len, stride=dil), :].astype(BF16))

            for hh in range(A_HEADS_PER_GROUP):
                qstage[hh] = q_ref[:, gcols(hh)].astype(F32)

        def scores(r, i, hh):
            q0 = ti * lq + i * A_QROWS
            ks = jnp.clip(q0 - A_RADIUS, 0, sub_len - A_KWIN)
            off = lax.shift_right_logical(q0 - ks, int(math.log2(A_RADIUS)))
            ks = pl.multiple_of(ks, A_RADIUS)
            if dil > 1:
                cols = slice(hh * HEAD_DIM, (hh + 1) * HEAD_DIM)
                rows = pl.ds(i * A_QROWS * dil + r, A_QROWS, stride=dil)
                q = qstage[hh, rows, :].astype(BF16)
                k = kres[r, pl.ds(ks, A_KWIN), cols]
                v = vres[r, pl.ds(ks, A_KWIN), cols]
            else:
                rows = pl.ds(i * A_QROWS, A_QROWS)
                q = q_ref[rows, gcols(hh)]
                k = k_ref[pl.ds(ks, A_KWIN), gcols(hh)]
                v = v_ref[pl.ds(ks, A_KWIN), gcols(hh)]
            s = lax.dot_general(q, k, _NT, preferred_element_type=F32) * SCALE + bias_ref[gi * A_NOFF + off, hh]
            return rows, s, v

        def fold(hh, rows, o, lse):
            lse = jnp.broadcast_to(lse, (A_QROWS, HEAD_DIM))
            if gi > 0:
                prev_o, prev_lse = o_acc[hh, rows, :], lse_acc[hh, rows, :]
                m = jnp.maximum(prev_lse, lse)
                w_prev, w_new = jnp.exp(prev_lse - m), jnp.exp(lse - m)
                den = w_prev + w_new
                o = (w_prev * prev_o + w_new * o) / den
                lse = m + jnp.log(den)
            o_acc[hh, rows, :] = o
            if gi + 1 < n_groups:
                lse_acc[hh, rows, :] = lse

        items = [(r, i, hh) for r in range(dil) for i in range(lq // A_QROWS) for hh in range(A_HEADS_PER_GROUP)]
        for b0 in range(0, len(items), A_BATCH):
            batch = items[b0:b0 + A_BATCH]
            staged = [scores(*it) for it in batch]
            probs = [softmax(s) for _, s, _ in staged]
            for (_, _, hh), (rows, _, v), (p, l, lse) in zip(batch, staged, probs):
                fold(hh, rows, jnp.dot(p, v, preferred_element_type=F32) / l, lse)

    strided = [gi for gi, (_, dil) in enumerate(DIL_PAIRS) if dil > 1]
    for gi, (_, dil) in enumerate(DIL_PAIRS):
        kres, vres = (residue_kv[2 * strided.index(gi):2 * strided.index(gi) + 2] if dil > 1 else (None, None))
        run_group(gi, dil, kres, vres)
    for hh in range(A_HEADS_PER_GROUP):
        oa_ref[:, hh * HEAD_DIM:(hh + 1) * HEAD_DIM] = o_acc[hh].astype(oa_ref.dtype)


def _mixer_a(qkn, va, bias_a, t_rows=2048):
    b, seq, _ = qkn.shape
    head_buf = lambda rows: pltpu.VMEM((A_HEADS_PER_GROUP, rows, HEAD_DIM), F32)
    scratch = [pltpu.VMEM((seq, HEAD_DIM), F32), head_buf(t_rows), head_buf(t_rows), head_buf(t_rows)]
    for _, dil in DIL_PAIRS:
        if dil > 1:
            scratch += [pltpu.VMEM((dil, seq // dil, A_OUT_W), BF16)] * 2
    return pl.pallas_call(
        functools.partial(_mixer_a_kernel, seq=seq),
        grid=(b, seq // t_rows),
        in_specs=[pl.BlockSpec((None, t_rows, A_W), lambda bi, ti: (bi, ti, 0)),
                  pl.BlockSpec((None, seq, A_W), lambda bi, ti: (bi, 0, 1)),
                  pl.BlockSpec((None, seq, A_W), lambda bi, ti: (bi, 0, 0)),
                  pl.BlockSpec(bias_a.shape, lambda bi, ti: (0, 0, 0, 0), pipeline_mode=pl.Buffered(1))],
        out_specs=pl.BlockSpec((None, t_rows, A_OUT_W), lambda bi, ti: (bi, ti, 0)),
        out_shape=jax.ShapeDtypeStruct((b, seq, A_OUT_W), BF16),
        scratch_shapes=scratch,
        compiler_params=_cparams("parallel", "arbitrary", vmem_mb=62),
        name="mixer_a",
    )(qkn, qkn, va, bias_a)


def _mixer_b_kernel(qt_ref, k_ref, vt_ref, *rest, tk, group, w, n_cast):
    o_ref, s_scr = rest[n_cast], rest[-1]
    _run_cast_jobs(rest[:n_cast], rest[n_cast + 1:-1])
    tq = qt_ref.shape[1]
    seq = k_ref.shape[0]
    n_chunks = seq // tk
    units = [(i, j) for i in range(group) for j in range(tq // w)]

    def pass_a(u, ci, m):
        i, j = units[u]
        qt = qt_ref[i * HEAD_DIM:(i + 1) * HEAD_DIM, j * w:(j + 1) * w]
        st = jnp.dot(k_ref[ci * tk:(ci + 1) * tk, :], qt, preferred_element_type=F32)
        s_scr[u % 2, ci * tk:(ci + 1) * tk, :] = st
        return jnp.maximum(m, jnp.max(st, axis=0, keepdims=True))

    def pass_b(u, ci, m, l, acc):
        pt = jnp.exp2(s_scr[u % 2, ci * tk:(ci + 1) * tk, :] - m)
        l = l + jnp.sum(pt, axis=0, keepdims=True)
        acc = acc + jnp.dot(vt_ref[:, ci * tk:(ci + 1) * tk], pt.astype(BF16), preferred_element_type=F32)
        return l, acc

    m_prev = None
    for s in range(len(units) + 1):
        m_cur = jnp.full((1, w), NEG, F32)
        l = jnp.zeros((1, w), F32)
        acc = jnp.zeros((HEAD_DIM, w), F32)
        for ci in range(n_chunks):
            if s < len(units):
                m_cur = pass_a(s, ci, m_cur)
            if s > 0:
                l, acc = pass_b(s - 1, ci, m_prev, l, acc)
        if s > 0:
            i, j = units[s - 1]
            o_ref[j * w:(j + 1) * w, i * HEAD_DIM:(i + 1) * HEAD_DIM] = (acc / l).T.astype(o_ref.dtype)
        m_prev = m_cur


def _mixer_b(qbt, kb, vbt, cast_jobs, tq=1024, tk=512, w=256):
    b, seq, _ = kb.shape
    group = B_Q_HEADS // B_KV_HEADS
    gw = group * HEAD_DIM
    nq = seq // tq
    n_steps = b * B_KV_HEADS * nq
    c_in, c_out, c_shapes = _cast_jobs(cast_jobs, n_steps, lambda bi, kv, qi: (bi * B_KV_HEADS + kv) * nq + qi)
    return pl.pallas_call(
        functools.partial(_mixer_b_kernel, tk=tk, group=group, w=w, n_cast=len(cast_jobs)),
        grid=(b, B_KV_HEADS, nq),
        scratch_shapes=[pltpu.VMEM((2, seq, w), F32)],
        in_specs=[pl.BlockSpec((None, gw, tq), lambda bi, kv, qi: (bi, kv, qi)),
                  pl.BlockSpec((None, seq, HEAD_DIM), lambda bi, kv, qi: (bi, 0, kv)),
                  pl.BlockSpec((None, HEAD_DIM, seq), lambda bi, kv, qi: (bi, kv, 0))] + c_in,
        out_specs=[pl.BlockSpec((None, tq, gw), lambda bi, kv, qi: (bi, qi, kv))] + c_out,
        out_shape=[jax.ShapeDtypeStruct((b, seq, B_QW), BF16)] + c_shapes,
        compiler_params=_cparams("parallel", "parallel", "arbitrary"),
        name="mixer_b",
    )(qbt, kb, vbt, *[src for src, _, _ in cast_jobs])


def _mixer_c_kernel(q_ref, mem_ref, gm_ref, w_ref, gk_ref, o_ref, k_scr, v_scr):
    @pl.when(pl.program_id(1) == 0)
    def _():
        x = mem_ref[...]
        ms = jnp.mean(x * x, axis=-1, keepdims=True)
        hm = (x * lax.rsqrt(ms + EPS) * gm_ref[...]).astype(BF16)
        kv = jnp.dot(hm, w_ref[...], preferred_element_type=F32)
        for hh in range(C_HEADS):
            sl = slice(hh * HEAD_DIM, (hh + 1) * HEAD_DIM)
            y = kv[:, sl]
            ms = jnp.mean(y * y, axis=-1, keepdims=True)
            k_scr[:, sl] = (y * lax.rsqrt(ms + EPS) * gk_ref[...]).astype(BF16)
        v_scr[...] = kv[:, C_W:].astype(BF16)

    for hh in range(C_HEADS):
        sl = slice(hh * HEAD_DIM, (hh + 1) * HEAD_DIM)
        s = lax.dot_general(q_ref[:, sl], k_scr[:, sl], _NT, preferred_element_type=F32) * SCALE
        m = jnp.max(s, axis=-1, keepdims=True)
        p = jnp.exp(s - m)
        l = jnp.sum(p, axis=-1, keepdims=True)
        o = jnp.dot(p.astype(BF16), v_scr[:, sl], preferred_element_type=F32) / l
        o_ref[:, sl] = o.astype(o_ref.dtype)


def _mixer_c(qkn, mem, g_mem, w_kv, g_kc, tq=1024):
    b, seq, _ = qkn.shape
    _, n_mem, d = mem.shape
    qc_blk = (2 * A_W) // C_W
    return pl.pallas_call(
        _mixer_c_kernel,
        grid=(b, seq // tq),
        in_specs=[pl.BlockSpec((None, tq, C_W), lambda bi, qi: (bi, qi, qc_blk)),
                  pl.BlockSpec((None, n_mem, d), lambda bi, qi: (bi, 0, 0)),
                  pl.BlockSpec((1, d), lambda bi, qi: (0, 0)),
                  pl.BlockSpec((d, 2 * C_W), lambda bi, qi: (0, 0)),
                  pl.BlockSpec((1, HEAD_DIM), lambda bi, qi: (0, 0))],
        out_specs=pl.BlockSpec((None, tq, C_W), lambda bi, qi: (bi, qi, 0)),
        out_shape=jax.ShapeDtypeStruct((b, seq, C_W), BF16),
        scratch_shapes=[pltpu.VMEM((n_mem, C_W), BF16), pltpu.VMEM((n_mem, C_W), BF16)],
        compiler_params=_cparams("parallel", "arbitrary"),
        name="mixer_c",
    )(qkn, mem, g_mem.reshape(1, d), w_kv, g_kc.reshape(1, HEAD_DIM))


def _merge_kernel(h_ref, wg0_ref, wg1_ref, wg2_ref, oa_ref, ob_ref, oc_ref, wa_ref, wb_ref, wc_ref, out_ref):
    h = h_ref[...]
    ga = jax.nn.sigmoid(jnp.dot(h, wg0_ref[...], preferred_element_type=F32))
    merged = ga * jnp.dot(oa_ref[...], wa_ref[...], preferred_element_type=F32)
    gb = jax.nn.sigmoid(jnp.dot(h, wg1_ref[...], preferred_element_type=F32))
    merged += gb * jnp.dot(ob_ref[...], wb_ref[...], preferred_element_type=F32)
    gc = jax.nn.sigmoid(jnp.dot(h, wg2_ref[...], preferred_element_type=F32))
    merged += gc * jnp.dot(oc_ref[...], wc_ref[...], preferred_element_type=F32)
    out_ref[...] = merged.astype(out_ref.dtype)


def _merge(h, w_gates, oa, ob, oc, wa, wb, wc, tm=1024, tn=512):
    m, d = h.shape
    nj = d // tn
    row = lambda w: pl.BlockSpec((tm, w), lambda i, j: (i, 0))
    col = lambda k: pl.BlockSpec((k, tn), lambda i, j: (0, j))
    return pl.pallas_call(
        _merge_kernel,
        grid=(m // tm, nj),
        in_specs=[row(d), col(d), col(d), col(d), row(A_OUT_W), row(B_QW), row(C_W),
                  col(A_OUT_W), col(B_QW), col(C_W)],
        out_specs=pl.BlockSpec((tm, tn), lambda i, j: (i, j)),
        out_shape=jax.ShapeDtypeStruct((m, d), BF16),
        compiler_params=_cparams("parallel", "arbitrary", vmem_mb=48),
        name="merge",
    )(h, *w_gates, oa, ob, oc, wa, wb, wc)


def _out_proj_kernel(mg_ref, w_ref, x_ref, g_ref, x1_ref, h2_ref, *, r_sub):
    def matmul(c):
        return jnp.dot(mg_ref[c * r_sub:(c + 1) * r_sub, :], w_ref[...], preferred_element_type=F32)

    def epilogue(c, acc):
        rows = slice(c * r_sub, (c + 1) * r_sub)
        x1 = x_ref[rows, :] + acc
        x1_ref[rows, :] = x1
        ms = jnp.mean(x1 * x1, axis=-1, keepdims=True)
        h2_ref[rows, :] = (x1 * lax.rsqrt(ms + EPS) * g_ref[...]).astype(h2_ref.dtype)

    _software_pipeline(mg_ref.shape[0] // r_sub, matmul, epilogue)


def _out_proj(merged, w_o, x2d, g_ffn, tm=512, r_sub=512):
    m, d = x2d.shape
    return pl.pallas_call(
        functools.partial(_out_proj_kernel, r_sub=r_sub),
        grid=(m // tm,),
        in_specs=[pl.BlockSpec((tm, d), lambda i: (i, 0)),
                  pl.BlockSpec((d, d), lambda i: (0, 0)),
                  pl.BlockSpec((tm, d), lambda i: (i, 0)),
                  pl.BlockSpec((1, d), lambda i: (0, 0))],
        out_specs=[pl.BlockSpec((tm, d), lambda i: (i, 0))] * 2,
        out_shape=[jax.ShapeDtypeStruct((m, d), F32), jax.ShapeDtypeStruct((m, d), BF16)],
        compiler_params=_cparams("parallel"),
        name="out_proj",
    )(merged, w_o, x2d, g_ffn.reshape(1, d))


def _ffn_kernel(h_ref, wa_ref, wb_ref, wo_ref, x1_ref, out_ref, *, n_slabs):
    f = pl.program_id(1)
    slab_w = x1_ref.shape[1]

    @pl.when(f == 0)
    def _():
        out_ref[...] = jnp.zeros_like(out_ref)

    for s in range(n_slabs):
        @pl.when(f == s)
        def _():
            out_ref[:, s * slab_w:(s + 1) * slab_w] += x1_ref[...]

    h = h_ref[...]
    a = jnp.dot(h, wa_ref[...], preferred_element_type=F32)
    b = jnp.dot(h, wb_ref[...], preferred_element_type=F32)
    act = (a * jax.nn.sigmoid(a) * b).astype(BF16)
    out_ref[...] += jnp.dot(act, wo_ref[...], preferred_element_type=F32)


def _ffn(h2, w_a, w_b, w_out, x1, tm=1024, tf=512, slab_w=256):
    m, d = h2.shape
    d_ff = w_out.shape[0]
    nf = d_ff // tf
    n_slabs = d // slab_w
    assert n_slabs <= nf
    return pl.pallas_call(
        functools.partial(_ffn_kernel, n_slabs=n_slabs),
        grid=(m // tm, nf),
        in_specs=[pl.BlockSpec((tm, d), lambda i, f: (i, 0)),
                  pl.BlockSpec((d, tf), lambda i, f: (0, f)),
                  pl.BlockSpec((d, tf), lambda i, f: (0, f)),
                  pl.BlockSpec((tf, d), lambda i, f: (f, 0)),
                  pl.BlockSpec((tm, slab_w), lambda i, f: (i, jnp.minimum(f, n_slabs - 1)))],
        out_specs=pl.BlockSpec((tm, d), lambda i, f: (i, 0)),
        out_shape=jax.ShapeDtypeStruct((m, d), F32),
        compiler_params=_cparams("parallel", "arbitrary", vmem_mb=48),
        name="ffn",
    )(h2, w_a, w_b, w_out, x1)


def _rope_tables(seq):
    rows = seq // GRID_W
    r = np.repeat(np.arange(rows), GRID_W).astype(np.float64)
    c = np.tile(np.arange(GRID_W), rows).astype(np.float64)
    nf = HEAD_DIM // 4
    inv = ROPE_THETA ** (-np.arange(nf, dtype=np.float64) / nf)
    ang = np.concatenate([r[:, None] * inv, c[:, None] * inv], axis=-1)
    cos, sin = np.cos(ang).T, np.sin(ang).T
    return (np.concatenate([cos, cos], axis=0).astype(np.float32),
            np.concatenate([-sin, sin], axis=0).astype(np.float32))


def _layer(x2d, mem, bias_a, cos_t, sin_t, b, seq, g_mix, w_in, g_qa, g_ka, g_qb, g_kb, g_mem, w_mem_kv,
           g_qc, g_kc, w_br_a, w_br_b, w_br_c, w_o, g_ffn, w_ffn_in, w_ffn_out):
    d = x2d.shape[1]
    o_gt = IN_OFFSETS[-1]
    d_ff = w_ffn_out.shape[0]
    w_norm, w_b = _head_weights(w_in)
    g_norm = jnp.concatenate([jnp.tile(g_qa, A_HEADS), jnp.tile(g_ka, A_HEADS), jnp.tile(g_qc, C_HEADS)])
    deint = lambda g: g.reshape(HEAD_DIM // 2, 2).T.reshape(HEAD_DIM)

    qkn, h = _proj_norm(x2d, g_mix, w_norm, g_norm, lambda j: j, tn=C_W)
    qkn = qkn.reshape(b, seq, -1)
    late_weights = [(w_ffn_out, 0, d), (w_o, 0, d), (w_br_a, 0, d), (w_br_b, 0, d), (w_br_c, 0, d),
                    (w_mem_kv, 0, 2 * C_W)]
    qbt, kb, vbt, va, w_out_b, w_o_b, wa_b, wb_b, wc_b, w_kv_b = _proj_t(
        h, w_b, deint(g_qb) * (SCALE * LOG2E), deint(g_kb), cos_t, sin_t, b, seq, late_weights)
    kb = kb.reshape(b, seq, B_KVW)
    va = va.reshape(b, seq, A_W)

    oa = _mixer_a(qkn, va, bias_a).reshape(b * seq, A_OUT_W)
    gate_jobs = [(w_in, int(o_gt) // d + br, d) for br in range(N_BRANCH)]
    ob, w_ffa, w_ffb, *w_gates = _mixer_b(qbt, kb, vbt, [(w_ffn_in, 0, d_ff), (w_ffn_in, 1, d_ff)] + gate_jobs)
    ob = ob.reshape(b * seq, B_QW)

    oc = _mixer_c(qkn, mem, g_mem, w_kv_b, g_kc).reshape(b * seq, C_W)

    merged = _merge(h, w_gates, oa, ob, oc, wa_b, wb_b, wc_b)
    x1, h2 = _out_proj(merged, w_o_b, x2d, g_ffn)
    return _ffn(h2, w_ffa, w_ffb, w_out_b, x1)


def kernel(x, mem, rel_bias, g_mix, w_in, g_qa, g_ka, g_qb, g_kb, g_mem, w_mem_kv, g_qc, g_kc,
           w_br_a, w_br_b, w_br_c, w_o, g_ffn, w_ffn_in, w_ffn_out):
    b, seq, d = x.shape
    depth = w_in.shape[0]
    cos_t, sin_t = _rope_tables(seq)
    bias_a = _a_bias(rel_bias)
    x2d = x.reshape(b * seq, d)
    for layer in range(depth):
        x2d = _layer(x2d, mem, bias_a, cos_t, sin_t, b, seq,
                     g_mix[layer], w_in[layer], g_qa[layer], g_ka[layer], g_qb[layer], g_kb[layer],
                     g_mem[layer], w_mem_kv[layer], g_qc[layer], g_kc[layer],
                     w_br_a[layer], w_br_b[layer], w_br_c[layer], w_o[layer], g_ffn[layer],
                     w_ffn_in[layer], w_ffn_out[layer])
    return x2d.reshape(b, seq, d)
```

```python
import functools
import math

import numpy as np
import jax
import jax.numpy as jnp
from jax import lax
from jax.experimental import pallas as pl
from jax.experimental.pallas import tpu as pltpu

HEAD_DIM = 128
GRID_W = 64
DIL_PAIRS = ((128, 1), (512, 4), (2048, 16))
A_HEADS_PER_GROUP = 2
A_HEADS = A_HEADS_PER_GROUP * len(DIL_PAIRS)
B_Q_HEADS = 6
B_KV_HEADS = 2
ROPE_THETA = 10000.0
C_HEADS = 4
N_BRANCH = 3
REL_BUCKETS = 32
REL_MAX_DIST = 1024
EPS = 1e-6
NEG = -1e30

A_W = A_HEADS * HEAD_DIM
A_OUT_W = A_HEADS_PER_GROUP * HEAD_DIM
B_QW = B_Q_HEADS * HEAD_DIM
B_KVW = B_KV_HEADS * HEAD_DIM
C_W = C_HEADS * HEAD_DIM
IN_OFFSETS = tuple(int(v) for v in np.cumsum((0, A_W, A_W, A_W, B_QW, B_KVW, B_KVW, C_W)))

SCALE = 1.0 / math.sqrt(HEAD_DIM)
LOG2E = math.log2(math.e)

A_QROWS = 128
A_KWIN = 256
A_RADIUS = 64
A_NOFF = 3
A_BATCH = 8

BF16 = jnp.bfloat16
F32 = jnp.float32
BF16_TILE_ROWS = 16
F32_TILE_ROWS = 8

_NT = (((1,), (1,)), ((), ()))


def _software_pipeline(n_chunks, matmul, epilogue):
    acc = matmul(0)
    for c in range(n_chunks):
        nxt = matmul(c + 1) if c + 1 < n_chunks else None
        epilogue(c, acc)
        acc = nxt


def _cast_jobs(jobs, n_steps, step_index):
    in_specs, out_specs, out_shapes = [], [], []
    for src, col_block, width in jobs:
        rows = src.shape[0]
        rt = rows // n_steps
        assert rt * n_steps == rows and rt % BF16_TILE_ROWS == 0 and src.shape[1] % width == 0
        in_specs.append(pl.BlockSpec((rt, width), lambda *g, cb=col_block: (step_index(*g), cb)))
        out_specs.append(pl.BlockSpec((rt, width), lambda *g: (step_index(*g), 0)))
        out_shapes.append(jax.ShapeDtypeStruct((rows, width), BF16))
    return in_specs, out_specs, out_shapes


def _run_cast_jobs(src_refs, dst_refs):
    for src, dst in zip(src_refs, dst_refs):
        dst[...] = src[...].astype(dst.dtype)


def _head_weights_kernel(src_ref, perm_ref, wn_ref, wb_ref):
    o_qa, o_ka, o_va, o_qb, o_kb, o_vb, o_qc, o_gt = IN_OFFSETS
    cast = lambda lo, hi: src_ref[:, lo:hi].astype(BF16)
    wn_ref[:, :o_va] = cast(o_qa, o_va)
    wn_ref[:, o_va:] = cast(o_qc, o_gt)
    n_rot = B_QW + B_KVW
    for c0 in range(0, n_rot, HEAD_DIM):
        head = jnp.dot(cast(o_qb + c0, o_qb + c0 + HEAD_DIM), perm_ref[...], preferred_element_type=F32)
        wb_ref[:, c0:c0 + HEAD_DIM] = head.astype(BF16)
    wb_ref[:, n_rot:n_rot + B_KVW] = cast(o_vb, o_qc)
    wb_ref[:, n_rot + B_KVW:] = cast(o_va, o_qb)


def _head_weights(w_in, row_tile=256):
    rows = w_in.shape[0]
    n_head_cols = int(IN_OFFSETS[-1])
    assert rows % row_tile == 0
    half = HEAD_DIM // 2
    perm = np.zeros((HEAD_DIM, HEAD_DIM), np.float32)
    for j in range(HEAD_DIM):
        perm[2 * j if j < half else 2 * (j - half) + 1, j] = 1.0
    return pl.pallas_call(
        _head_weights_kernel,
        grid=(rows // row_tile,),
        in_specs=[pl.BlockSpec((row_tile, n_head_cols), lambda i: (i, 0)),
                  pl.BlockSpec((HEAD_DIM, HEAD_DIM), lambda i: (0, 0))],
        out_specs=[pl.BlockSpec((row_tile, 2 * A_W + C_W), lambda i: (i, 0)),
                   pl.BlockSpec((row_tile, B_QW + 2 * B_KVW + A_W), lambda i: (i, 0))],
        out_shape=[jax.ShapeDtypeStruct((rows, 2 * A_W + C_W), BF16),
                   jax.ShapeDtypeStruct((rows, B_QW + 2 * B_KVW + A_W), BF16)],
        compiler_params=_cparams("parallel"),
        name="head_weights",
    )(w_in, jnp.asarray(perm, BF16))


def _cparams(*sem, vmem_mb=None):
    limit = None if vmem_mb is None else vmem_mb * 1024 * 1024
    return pltpu.CompilerParams(dimension_semantics=sem, vmem_limit_bytes=limit)


def _proj_norm_kernel(x_hbm, gm_ref, w_ref, g_ref, o_ref, h_ref, xbuf, sem, *, r_sub, n_j):
    i, j = pl.program_id(0), pl.program_id(1)
    n_i = pl.num_programs(0)
    tm = h_ref.shape[0]
    piece = tm // n_j

    def x_copy(tile, p):
        slot = tile % 2
        rows = pl.ds(tile * tm + p * piece, piece)
        return pltpu.make_async_copy(x_hbm.at[rows, :], xbuf.at[slot, pl.ds(p * piece, piece), :], sem.at[slot, p])

    @pl.when((i == 0) & (j == 0))
    def _():
        for p in range(n_j):
            x_copy(0, p).start()

    @pl.when(i + 1 < n_i)
    def _():
        for p in range(n_j):
            @pl.when(j == p)
            def _():
                x_copy(i + 1, p).start()

    @pl.when(j == 0)
    def _():
        for p in range(n_j):
            x_copy(i, p).wait()
        x = xbuf[i % 2]
        ms = jnp.mean(x * x, axis=-1, keepdims=True)
        h_ref[...] = (x * lax.rsqrt(ms + EPS) * gm_ref[...]).astype(h_ref.dtype)

    def matmul(c):
        return jnp.dot(h_ref[c * r_sub:(c + 1) * r_sub, :], w_ref[...], preferred_element_type=F32)

    def epilogue(c, acc):
        for hh in range(o_ref.shape[1] // HEAD_DIM):
            sl = slice(hh * HEAD_DIM, (hh + 1) * HEAD_DIM)
            y = acc[:, sl]
            ms = jnp.mean(y * y, axis=-1, keepdims=True)
            o_ref[c * r_sub:(c + 1) * r_sub, sl] = (y * lax.rsqrt(ms + EPS) * g_ref[:, sl]).astype(o_ref.dtype)

    _software_pipeline(h_ref.shape[0] // r_sub, matmul, epilogue)


def _proj_norm(x2d, g_mix, w, gains, col_block, tm=1024, tn=512, r_sub=256):
    m, d = x2d.shape
    n = gains.shape[0]
    n_j = n // tn
    assert n % tn == 0 and m % tm == 0 and tm % (F32_TILE_ROWS * n_j) == 0
    return pl.pallas_call(
        functools.partial(_proj_norm_kernel, r_sub=r_sub, n_j=n_j),
        grid=(m // tm, n_j),
        in_specs=[pl.BlockSpec(memory_space=pl.ANY),
                  pl.BlockSpec((1, d), lambda i, j: (0, 0)),
                  pl.BlockSpec((d, tn), lambda i, j: (0, col_block(j))),
                  pl.BlockSpec((1, tn), lambda i, j: (0, j))],
        out_specs=[pl.BlockSpec((tm, tn), lambda i, j: (i, j)),
                   pl.BlockSpec((tm, d), lambda i, j: (i, 0))],
        out_shape=[jax.ShapeDtypeStruct((m, n), BF16), jax.ShapeDtypeStruct((m, d), BF16)],
        scratch_shapes=[pltpu.VMEM((2, tm, d), F32), pltpu.SemaphoreType.DMA((2, n_j))],
        compiler_params=_cparams("arbitrary", "arbitrary", vmem_mb=48),
        name="proj_norm",
    )(x2d, g_mix.reshape(1, d), w, gains.reshape(1, n))


def _proj_t_kernel(w_ref, h_ref, gq_ref, gk_ref, cos_ref, sin_ref, *rest, t_sub, n_cast):
    q_ref, k_ref, v_ref, va_ref = rest[n_cast:n_cast + 4]
    _run_cast_jobs(rest[:n_cast], rest[n_cast + 4:])
    n_q = q_ref.shape[0] // HEAD_DIM
    n_k = k_ref.shape[1] // HEAD_DIM
    n_v = v_ref.shape[0] // HEAD_DIM
    half = HEAD_DIM // 2

    def norm_rope(y, g_ref, tok):
        ms = jnp.mean(y * y, axis=0, keepdims=True)
        y = y * lax.rsqrt(ms + EPS) * g_ref[...]
        partner = jnp.concatenate([y[half:], y[:half]], axis=0)
        return y * cos_ref[:, tok] + partner * sin_ref[:, tok]

    def matmul(c):
        return jnp.dot(h_ref[c * t_sub:(c + 1) * t_sub, :], w_ref[...], preferred_element_type=F32)

    def epilogue(c, y):
        tok = slice(c * t_sub, (c + 1) * t_sub)
        head = lambda hh: y[:, hh * HEAD_DIM:(hh + 1) * HEAD_DIM].T
        for hh in range(n_q):
            q_ref[hh * HEAD_DIM:(hh + 1) * HEAD_DIM, tok] = norm_rope(head(hh), gq_ref, tok).astype(q_ref.dtype)
        for hh in range(n_k):
            kt = norm_rope(head(n_q + hh), gk_ref, tok)
            k_ref[tok, hh * HEAD_DIM:(hh + 1) * HEAD_DIM] = kt.T.astype(k_ref.dtype)
        for hh in range(n_v):
            v_ref[hh * HEAD_DIM:(hh + 1) * HEAD_DIM, tok] = head(n_q + n_k + hh).astype(v_ref.dtype)
        va_ref[tok, :] = y[:, (n_q + n_k + n_v) * HEAD_DIM:].astype(va_ref.dtype)

    _software_pipeline(h_ref.shape[0] // t_sub, matmul, epilogue)


def _proj_t(h, w, gq_col, gk_col, cos_tt, sin_tt, b, seq, cast_jobs, tm=1024, t_sub=128):
    m, d = h.shape
    n = w.shape[1]
    assert n == B_QW + 2 * B_KVW + A_W
    sb = seq // tm
    col = lambda g: jnp.broadcast_to(g[:, None], (HEAD_DIM, t_sub))
    lane_tile = lambda rows: pl.BlockSpec((None, rows, tm), lambda i: (i // sb, 0, i % sb))
    c_in, c_out, c_shapes = _cast_jobs(cast_jobs, m // tm, lambda i: i)
    return pl.pallas_call(
        functools.partial(_proj_t_kernel, t_sub=t_sub, n_cast=len(cast_jobs)),
        grid=(m // tm,),
        in_specs=[pl.BlockSpec((d, n), lambda i: (0, 0)),
                  pl.BlockSpec((tm, d), lambda i: (i, 0)),
                  pl.BlockSpec((HEAD_DIM, t_sub), lambda i: (0, 0)),
                  pl.BlockSpec((HEAD_DIM, t_sub), lambda i: (0, 0)),
                  pl.BlockSpec((HEAD_DIM, tm), lambda i: (0, i % sb)),
                  pl.BlockSpec((HEAD_DIM, tm), lambda i: (0, i % sb))] + c_in,
        out_specs=[lane_tile(B_QW),
                   pl.BlockSpec((tm, B_KVW), lambda i: (i, 0)),
                   lane_tile(B_KVW),
                   pl.BlockSpec((tm, A_W), lambda i: (i, 0))] + c_out,
        out_shape=[jax.ShapeDtypeStruct((b, B_QW, seq), BF16),
                   jax.ShapeDtypeStruct((m, B_KVW), BF16),
                   jax.ShapeDtypeStruct((b, B_KVW, seq), BF16),
                   jax.ShapeDtypeStruct((m, A_W), BF16)] + c_shapes,
        compiler_params=_cparams("parallel", vmem_mb=56),
        name="proj_t",
    )(w, h, col(gq_col), col(gk_col), cos_tt, sin_tt, *[src for src, _, _ in cast_jobs])


def _t5_bucket(rel):
    nb = REL_BUCKETS // 2
    ret = np.where(rel > 0, nb, 0)
    n = np.abs(rel)
    max_exact = nb // 2
    large = max_exact + (np.log(np.maximum(n, 1).astype(np.float32) / np.float32(max_exact))
                         / np.float32(math.log(REL_MAX_DIST / max_exact))
                         * np.float32(nb - max_exact)).astype(np.int32)
    large = np.minimum(large, nb - 1)
    return ret + np.where(n < max_exact, n, large)


def _a_bucket_index():
    qi = np.arange(A_QROWS, dtype=np.int32)[:, None]
    kj = np.arange(A_KWIN, dtype=np.int32)[None, :]
    out = []
    for _, dil in DIL_PAIRS:
        for off in range(A_NOFF):
            rel = kj - qi - A_RADIUS * off
            out.append(np.where(np.abs(rel) <= A_RADIUS, _t5_bucket(rel * dil), -1))
    return np.stack(out).astype(np.int32)


def _a_bias_kernel(tab_ref, bucket_ref, o_ref):
    g = pl.program_id(0)
    for off in range(A_NOFF):
        bk = bucket_ref[off]
        for hh in range(A_HEADS_PER_GROUP):
            acc = jnp.full(bk.shape, NEG, F32)
            for b in range(REL_BUCKETS):
                acc = jnp.where(bk == b, tab_ref[b, g * A_HEADS_PER_GROUP + hh], acc)
            o_ref[off, hh] = acc


def _a_bias(rel_bias):
    n = len(DIL_PAIRS) * A_NOFF
    return pl.pallas_call(
        _a_bias_kernel,
        grid=(len(DIL_PAIRS),),
        in_specs=[pl.BlockSpec(memory_space=pltpu.SMEM),
                  pl.BlockSpec((A_NOFF, A_QROWS, A_KWIN), lambda i: (i, 0, 0))],
        out_specs=pl.BlockSpec((A_NOFF, A_HEADS_PER_GROUP, A_QROWS, A_KWIN), lambda i: (i, 0, 0, 0)),
        out_shape=jax.ShapeDtypeStruct((n, A_HEADS_PER_GROUP, A_QROWS, A_KWIN), F32),
        compiler_params=_cparams("arbitrary"),
        name="a_bias",
    )(rel_bias, _a_bucket_index())


def _mixer_a_kernel(q_ref, k_ref, v_ref, bias_ref, oa_ref, stage, qstage, o_acc, lse_acc, *residue_kv, seq):
    t_rows = q_ref.shape[0]
    ti = pl.program_id(1)
    n_groups = len(DIL_PAIRS)

    def softmax(s):
        m = jnp.max(s, axis=-1, keepdims=True)
        p = jnp.exp(s - m)
        l = jnp.sum(p, axis=-1, keepdims=True)
        return p.astype(BF16), l, m + jnp.log(l)

    def run_group(gi, dil, kres, vres):
        sub_len = seq // dil
        lq = t_rows // dil
        gcols = lambda hh: slice(gi * A_OUT_W + hh * HEAD_DIM, gi * A_OUT_W + (hh + 1) * HEAD_DIM)
        if dil > 1:
            @pl.when(ti == 0)
            def _():
                for src, dst in ((k_ref, kres), (v_ref, vres)):
                    for hh in range(A_HEADS_PER_GROUP):
                        stage[...] = src[:, gcols(hh)].astype(F32)
                        for r in range(dil):
                            dst[r, :, hh * HEAD_DIM:(hh + 1) * HEAD_DIM] = (
                                stage[pl.ds(r, sub_len, stride=dil), :].astype(BF16))

            for hh in range(A_HEADS_PER_GROUP):
                qstage[hh] = q_ref[:, gcols(hh)].astype(F32)

        def scores(r, i, hh):
            q0 = ti * lq + i * A_QROWS
            ks = jnp.clip(q0 - A_RADIUS, 0, sub_len - A_KWIN)
            off = lax.shift_right_logical(q0 - ks, int(math.log2(A_RADIUS)))
            ks = pl.multiple_of(ks, A_RADIUS)
            if dil > 1:
                cols = slice(hh * HEAD_DIM, (hh + 1) * HEAD_DIM)
                rows = pl.ds(i * A_QROWS * dil + r, A_QROWS, stride=dil)
                q = qstage[hh, rows, :].astype(BF16)
                k = kres[r, pl.ds(ks, A_KWIN), cols]
                v = vres[r, pl.ds(ks, A_KWIN), cols]
            else:
                rows = pl.ds(i * A_QROWS, A_QROWS)
                q = q_ref[rows, gcols(hh)]
                k = k_ref[pl.ds(ks, A_KWIN), gcols(hh)]
                v = v_ref[pl.ds(ks, A_KWIN), gcols(hh)]
            s = lax.dot_general(q, k, _NT, preferred_element_type=F32) * SCALE + bias_ref[gi * A_NOFF + off, hh]
            return rows, s, v

        def fold(hh, rows, o, lse):
            lse = jnp.broadcast_to(lse, (A_QROWS, HEAD_DIM))
            if gi > 0:
                prev_o, prev_lse = o_acc[hh, rows, :], lse_acc[hh, rows, :]
                m = jnp.maximum(prev_lse, lse)
                w_prev, w_new = jnp.exp(prev_lse - m), jnp.exp(lse - m)
                den = w_prev + w_new
                o = (w_prev * prev_o + w_new * o) / den
                lse = m + jnp.log(den)
            o_acc[hh, rows, :] = o
            if gi + 1 < n_groups:
                lse_acc[hh, rows, :] = lse

        items = [(r, i, hh) for r in range(dil) for i in range(lq // A_QROWS) for hh in range(A_HEADS_PER_GROUP)]
        for b0 in range(0, len(items), A_BATCH):
            batch = items[b0:b0 + A_BATCH]
            staged = [scores(*it) for it in batch]
            probs = [softmax(s) for _, s, _ in staged]
            for (_, _, hh), (rows, _, v), (p, l, lse) in zip(batch, staged, probs):
                fold(hh, rows, jnp.dot(p, v, preferred_element_type=F32) / l, lse)

    strided = [gi for gi, (_, dil) in enumerate(DIL_PAIRS) if dil > 1]
    for gi, (_, dil) in enumerate(DIL_PAIRS):
        kres, vres = (residue_kv[2 * strided.index(gi):2 * strided.index(gi) + 2] if dil > 1 else (None, None))
        run_group(gi, dil, kres, vres)
    for hh in range(A_HEADS_PER_GROUP):
        oa_ref[:, hh * HEAD_DIM:(hh + 1) * HEAD_DIM] = o_acc[hh].astype(oa_ref.dtype)


def _mixer_a(qkn, va, bias_a, t_rows=2048):
    b, seq, _ = qkn.shape
    head_buf = lambda rows: pltpu.VMEM((A_HEADS_PER_GROUP, rows, HEAD_DIM), F32)
    scratch = [pltpu.VMEM((seq, HEAD_DIM), F32), head_buf(t_rows), head_buf(t_rows), head_buf(t_rows)]
    for _, dil in DIL_PAIRS:
        if dil > 1:
            scratch += [pltpu.VMEM((dil, seq // dil, A_OUT_W), BF16)] * 2
    return pl.pallas_call(
        functools.partial(_mixer_a_kernel, seq=seq),
        grid=(b, seq // t_rows),
        in_specs=[pl.BlockSpec((None, t_rows, A_W), lambda bi, ti: (bi, ti, 0)),
                  pl.BlockSpec((None, seq, A_W), lambda bi, ti: (bi, 0, 1)),
                  pl.BlockSpec((None, seq, A_W), lambda bi, ti: (bi, 0, 0)),
                  pl.BlockSpec(bias_a.shape, lambda bi, ti: (0, 0, 0, 0), pipeline_mode=pl.Buffered(1))],
        out_specs=pl.BlockSpec((None, t_rows, A_OUT_W), lambda bi, ti: (bi, ti, 0)),
        out_shape=jax.ShapeDtypeStruct((b, seq, A_OUT_W), BF16),
        scratch_shapes=scratch,
        compiler_params=_cparams("parallel", "arbitrary", vmem_mb=62),
        name="mixer_a",
    )(qkn, qkn, va, bias_a)


def _mixer_b_kernel(qt_ref, k_ref, vt_ref, *rest, tk, group, w, n_cast):
    o_ref, s_scr = rest[n_cast], rest[-1]
    _run_cast_jobs(rest[:n_cast], rest[n_cast + 1:-1])
    tq = qt_ref.shape[1]
    seq = k_ref.shape[0]
    n_chunks = seq // tk
    units = [(i, j) for i in range(group) for j in range(tq // w)]

    def pass_a(u, ci, m):
        i, j = units[u]
        qt = qt_ref[i * HEAD_DIM:(i + 1) * HEAD_DIM, j * w:(j + 1) * w]
        st = jnp.dot(k_ref[ci * tk:(ci + 1) * tk, :], qt, preferred_element_type=F32)
        s_scr[u % 2, ci * tk:(ci + 1) * tk, :] = st
        return jnp.maximum(m, jnp.max(st, axis=0, keepdims=True))

    def pass_b(u, ci, m, l, acc):
        pt = jnp.exp2(s_scr[u % 2, ci * tk:(ci + 1) * tk, :] - m)
        l = l + jnp.sum(pt, axis=0, keepdims=True)
        acc = acc + jnp.dot(vt_ref[:, ci * tk:(ci + 1) * tk], pt.astype(BF16), preferred_element_type=F32)
        return l, acc

    m_prev = None
    for s in range(len(units) + 1):
        m_cur = jnp.full((1, w), NEG, F32)
        l = jnp.zeros((1, w), F32)
        acc = jnp.zeros((HEAD_DIM, w), F32)
        for ci in range(n_chunks):
            if s < len(units):
                m_cur = pass_a(s, ci, m_cur)
            if s > 0:
                l, acc = pass_b(s - 1, ci, m_prev, l, acc)
        if s > 0:
            i, j = units[s - 1]
            o_ref[j * w:(j + 1) * w, i * HEAD_DIM:(i + 1) * HEAD_DIM] = (acc / l).T.astype(o_ref.dtype)
        m_prev = m_cur


def _mixer_b(qbt, kb, vbt, cast_jobs, tq=2048, tk=512, w=256):
    b, seq, _ = kb.shape
    group = B_Q_HEADS // B_KV_HEADS
    gw = group * HEAD_DIM
    nq = seq // tq
    n_steps = b * B_KV_HEADS * nq
    c_in, c_out, c_shapes = _cast_jobs(cast_jobs, n_steps, lambda bi, kv, qi: (bi * B_KV_HEADS + kv) * nq + qi)
    return pl.pallas_call(
        functools.partial(_mixer_b_kernel, tk=tk, group=group, w=w, n_cast=len(cast_jobs)),
        grid=(b, B_KV_HEADS, nq),
        scratch_shapes=[pltpu.VMEM((2, seq, w), F32)],
        in_specs=[pl.BlockSpec((None, gw, tq), lambda bi, kv, qi: (bi, kv, qi)),
                  pl.BlockSpec((None, seq, HEAD_DIM), lambda bi, kv, qi: (bi, 0, kv)),
                  pl.BlockSpec((None, HEAD_DIM, seq), lambda bi, kv, qi: (bi, kv, 0))] + c_in,
        out_specs=[pl.BlockSpec((None, tq, gw), lambda bi, kv, qi: (bi, qi, kv))] + c_out,
        out_shape=[jax.ShapeDtypeStruct((b, seq, B_QW), BF16)] + c_shapes,
        compiler_params=_cparams("parallel", "parallel", "arbitrary"),
        name="mixer_b",
    )(qbt, kb, vbt, *[src for src, _, _ in cast_jobs])


def _mixer_c_kernel(q_ref, mem_ref, gm_ref, w_ref, gk_ref, o_ref, k_scr, v_scr):
    @pl.when(pl.program_id(1) == 0)
    def _():
        x = mem_ref[...]
        ms = jnp.mean(x * x, axis=-1, keepdims=True)
        hm = (x * lax.rsqrt(ms + EPS) * gm_ref[...]).astype(BF16)
        kv = jnp.dot(hm, w_ref[...], preferred_element_type=F32)
        for hh in range(C_HEADS):
            sl = slice(hh * HEAD_DIM, (hh + 1) * HEAD_DIM)
            y = kv[:, sl]
            ms = jnp.mean(y * y, axis=-1, keepdims=True)
            k_scr[:, sl] = (y * lax.rsqrt(ms + EPS) * gk_ref[...]).astype(BF16)
        v_scr[...] = kv[:, C_W:].astype(BF16)

    for hh in range(C_HEADS):
        sl = slice(hh * HEAD_DIM, (hh + 1) * HEAD_DIM)
        s = lax.dot_general(q_ref[:, sl], k_scr[:, sl], _NT, preferred_element_type=F32) * SCALE
        m = jnp.max(s, axis=-1, keepdims=True)
        p = jnp.exp(s - m)
        l = jnp.sum(p, axis=-1, keepdims=True)
        o = jnp.dot(p.astype(BF16), v_scr[:, sl], preferred_element_type=F32) / l
        o_ref[:, sl] = o.astype(o_ref.dtype)


def _mixer_c(qkn, mem, g_mem, w_kv, g_kc, tq=1024):
    b, seq, _ = qkn.shape
    _, n_mem, d = mem.shape
    qc_blk = (2 * A_W) // C_W
    return pl.pallas_call(
        _mixer_c_kernel,
        grid=(b, seq // tq),
        in_specs=[pl.BlockSpec((None, tq, C_W), lambda bi, qi: (bi, qi, qc_blk)),
                  pl.BlockSpec((None, n_mem, d), lambda bi, qi: (bi, 0, 0)),
                  pl.BlockSpec((1, d), lambda bi, qi: (0, 0)),
                  pl.BlockSpec((d, 2 * C_W), lambda bi, qi: (0, 0)),
                  pl.BlockSpec((1, HEAD_DIM), lambda bi, qi: (0, 0))],
        out_specs=pl.BlockSpec((None, tq, C_W), lambda bi, qi: (bi, qi, 0)),
        out_shape=jax.ShapeDtypeStruct((b, seq, C_W), BF16),
        scratch_shapes=[pltpu.VMEM((n_mem, C_W), BF16), pltpu.VMEM((n_mem, C_W), BF16)],
        compiler_params=_cparams("parallel", "arbitrary"),
        name="mixer_c",
    )(qkn, mem, g_mem.reshape(1, d), w_kv, g_kc.reshape(1, HEAD_DIM))


def _merge_kernel(h_ref, wg0_ref, wg1_ref, wg2_ref, oa_ref, ob_ref, oc_ref, wa_ref, wb_ref, wc_ref, out_ref):
    h = h_ref[...]
    ga = jax.nn.sigmoid(jnp.dot(h, wg0_ref[...], preferred_element_type=F32))
    merged = ga * jnp.dot(oa_ref[...], wa_ref[...], preferred_element_type=F32)
    gb = jax.nn.sigmoid(jnp.dot(h, wg1_ref[...], preferred_element_type=F32))
    merged += gb * jnp.dot(ob_ref[...], wb_ref[...], preferred_element_type=F32)
    gc = jax.nn.sigmoid(jnp.dot(h, wg2_ref[...], preferred_element_type=F32))
    merged += gc * jnp.dot(oc_ref[...], wc_ref[...], preferred_element_type=F32)
    out_ref[...] = merged.astype(out_ref.dtype)


def _merge(h, w_gates, oa, ob, oc, wa, wb, wc, tm=1024, tn=512):
    m, d = h.shape
    nj = d // tn
    row = lambda w: pl.BlockSpec((tm, w), lambda i, j: (i, 0))
    col = lambda k: pl.BlockSpec((k, tn), lambda i, j: (0, j))
    return pl.pallas_call(
        _merge_kernel,
        grid=(m // tm, nj),
        in_specs=[row(d), col(d), col(d), col(d), row(A_OUT_W), row(B_QW), row(C_W),
                  col(A_OUT_W), col(B_QW), col(C_W)],
        out_specs=pl.BlockSpec((tm, tn), lambda i, j: (i, j)),
        out_shape=jax.ShapeDtypeStruct((m, d), BF16),
        compiler_params=_cparams("parallel", "arbitrary", vmem_mb=48),
        name="merge",
    )(h, *w_gates, oa, ob, oc, wa, wb, wc)


def _out_proj_kernel(mg_ref, w_ref, x_ref, g_ref, x1_ref, h2_ref, *, r_sub):
    def matmul(c):
        return jnp.dot(mg_ref[c * r_sub:(c + 1) * r_sub, :], w_ref[...], preferred_element_type=F32)

    def epilogue(c, acc):
        rows = slice(c * r_sub, (c + 1) * r_sub)
        x1 = x_ref[rows, :] + acc
        x1_ref[rows, :] = x1
        ms = jnp.mean(x1 * x1, axis=-1, keepdims=True)
        h2_ref[rows, :] = (x1 * lax.rsqrt(ms + EPS) * g_ref[...]).astype(h2_ref.dtype)

    _software_pipeline(mg_ref.shape[0] // r_sub, matmul, epilogue)


def _out_proj(merged, w_o, x2d, g_ffn, tm=512, r_sub=512):
    m, d = x2d.shape
    return pl.pallas_call(
        functools.partial(_out_proj_kernel, r_sub=r_sub),
        grid=(m // tm,),
        in_specs=[pl.BlockSpec((tm, d), lambda i: (i, 0)),
                  pl.BlockSpec((d, d), lambda i: (0, 0)),
                  pl.BlockSpec((tm, d), lambda i: (i, 0)),
                  pl.BlockSpec((1, d), lambda i: (0, 0))],
        out_specs=[pl.BlockSpec((tm, d), lambda i: (i, 0))] * 2,
        out_shape=[jax.ShapeDtypeStruct((m, d), F32), jax.ShapeDtypeStruct((m, d), BF16)],
        compiler_params=_cparams("parallel"),
        name="out_proj",
    )(merged, w_o, x2d, g_ffn.reshape(1, d))


def _ffn_kernel(h_ref, wa_ref, wb_ref, wo_ref, x1_ref, out_ref, *, n_slabs):
    f = pl.program_id(1)
    slab_w = x1_ref.shape[1]

    @pl.when(f == 0)
    def _():
        out_ref[...] = jnp.zeros_like(out_ref)

    for s in range(n_slabs):
        @pl.when(f == s)
        def _():
            out_ref[:, s * slab_w:(s + 1) * slab_w] += x1_ref[...]

    h = h_ref[...]
    a = jnp.dot(h, wa_ref[...], preferred_element_type=F32)
    b = jnp.dot(h, wb_ref[...], preferred_element_type=F32)
    act = (a * jax.nn.sigmoid(a) * b).astype(BF16)
    out_ref[...] += jnp.dot(act, wo_ref[...], preferred_element_type=F32)


def _ffn(h2, w_a, w_b, w_out, x1, tm=1024, tf=512, slab_w=256):
    m, d = h2.shape
    d_ff = w_out.shape[0]
    nf = d_ff // tf
    n_slabs = d // slab_w
    assert n_slabs <= nf
    return pl.pallas_call(
        functools.partial(_ffn_kernel, n_slabs=n_slabs),
        grid=(m // tm, nf),
        in_specs=[pl.BlockSpec((tm, d), lambda i, f: (i, 0)),
                  pl.BlockSpec((d, tf), lambda i, f: (0, f)),
                  pl.BlockSpec((d, tf), lambda i, f: (0, f)),
                  pl.BlockSpec((tf, d), lambda i, f: (f, 0)),
                  pl.BlockSpec((tm, slab_w), lambda i, f: (i, jnp.minimum(f, n_slabs - 1)))],
        out_specs=pl.BlockSpec((tm, d), lambda i, f: (i, 0)),
        out_shape=jax.ShapeDtypeStruct((m, d), F32),
        compiler_params=_cparams("parallel", "arbitrary", vmem_mb=48),
        name="ffn",
    )(h2, w_a, w_b, w_out, x1)


def _rope_tables(seq):
    rows = seq // GRID_W
    r = np.repeat(np.arange(rows), GRID_W).astype(np.float64)
    c = np.tile(np.arange(GRID_W), rows).astype(np.float64)
    nf = HEAD_DIM // 4
    inv = ROPE_THETA ** (-np.arange(nf, dtype=np.float64) / nf)
    ang = np.concatenate([r[:, None] * inv, c[:, None] * inv], axis=-1)
    cos, sin = np.cos(ang).T, np.sin(ang).T
    return (np.concatenate([cos, cos], axis=0).astype(np.float32),
            np.concatenate([-sin, sin], axis=0).astype(np.float32))


def _layer(x2d, mem, bias_a, cos_t, sin_t, b, seq, g_mix, w_in, g_qa, g_ka, g_qb, g_kb, g_mem, w_mem_kv,
           g_qc, g_kc, w_br_a, w_br_b, w_br_c, w_o, g_ffn, w_ffn_in, w_ffn_out):
    d = x2d.shape[1]
    o_gt = IN_OFFSETS[-1]
    d_ff = w_ffn_out.shape[0]
    w_norm, w_b = _head_weights(w_in)
    g_norm = jnp.concatenate([jnp.tile(g_qa, A_HEADS), jnp.tile(g_ka, A_HEADS), jnp.tile(g_qc, C_HEADS)])
    deint = lambda g: g.reshape(HEAD_DIM // 2, 2).T.reshape(HEAD_DIM)

    qkn, h = _proj_norm(x2d, g_mix, w_norm, g_norm, lambda j: j, tn=C_W)
    qkn = qkn.reshape(b, seq, -1)
    late_weights = [(w_ffn_out, 0, d), (w_o, 0, d), (w_br_a, 0, d), (w_br_b, 0, d), (w_br_c, 0, d),
                    (w_mem_kv, 0, 2 * C_W)]
    qbt, kb, vbt, va, w_out_b, w_o_b, wa_b, wb_b, wc_b, w_kv_b = _proj_t(
        h, w_b, deint(g_qb) * (SCALE * LOG2E), deint(g_kb), cos_t, sin_t, b, seq, late_weights)
    kb = kb.reshape(b, seq, B_KVW)
    va = va.reshape(b, seq, A_W)

    oa = _mixer_a(qkn, va, bias_a).reshape(b * seq, A_OUT_W)
    gate_jobs = [(w_in, int(o_gt) // d + br, d) for br in range(N_BRANCH)]
    ob, w_ffa, w_ffb, *w_gates = _mixer_b(qbt, kb, vbt, [(w_ffn_in, 0, d_ff), (w_ffn_in, 1, d_ff)] + gate_jobs)
    ob = ob.reshape(b * seq, B_QW)

    oc = _mixer_c(qkn, mem, g_mem, w_kv_b, g_kc).reshape(b * seq, C_W)

    merged = _merge(h, w_gates, oa, ob, oc, wa_b, wb_b, wc_b)
    x1, h2 = _out_proj(merged, w_o_b, x2d, g_ffn)
    return _ffn(h2, w_ffa, w_ffb, w_out_b, x1)


def kernel(x, mem, rel_bias, g_mix, w_in, g_qa, g_ka, g_qb, g_kb, g_mem, w_mem_kv, g_qc, g_kc,
           w_br_a, w_br_b, w_br_c, w_o, g_ffn, w_ffn_in, w_ffn_out):
    b, seq, d = x.shape
    depth = w_in.shape[0]
    cos_t, sin_t = _rope_tables(seq)
    bias_a = _a_bias(rel_bias)
    x2d = x.reshape(b * seq, d)
    for layer in range(depth):
        x2d = _layer(x2d, mem, bias_a, cos_t, sin_t, b, seq,
                     g_mix[layer], w_in[layer], g_qa[layer], g_ka[layer], g_qb[layer], g_kb[layer],
                     g_mem[layer], w_mem_kv[layer], g_qc[layer], g_kc[layer],
                     w_br_a[layer], w_br_b[layer], w_br_c[layer], w_o[layer], g_ffn[layer],
                     w_ffn_in[layer], w_ffn_out[layer])
    return x2d.reshape(b, seq, d)
```

```python
import functools
import math

import numpy as np
import jax
import jax.numpy as jnp
from jax import lax
from jax.experimental import pallas as pl
from jax.experimental.pallas import tpu as pltpu

HEAD_DIM = 128
GRID_W = 64
DIL_PAIRS = ((128, 1), (512, 4), (2048, 16))
A_HEADS_PER_GROUP = 2
A_HEADS = A_HEADS_PER_GROUP * len(DIL_PAIRS)
B_Q_HEADS = 6
B_KV_HEADS = 2
ROPE_THETA = 10000.0
C_HEADS = 4
N_BRANCH = 3
REL_BUCKETS = 32
REL_MAX_DIST = 1024
EPS = 1e-6
NEG = -1e30

A_W = A_HEADS * HEAD_DIM
A_OUT_W = A_HEADS_PER_GROUP * HEAD_DIM
B_QW = B_Q_HEADS * HEAD_DIM
B_KVW = B_KV_HEADS * HEAD_DIM
C_W = C_HEADS * HEAD_DIM
IN_OFFSETS = tuple(int(v) for v in np.cumsum((0, A_W, A_W, A_W, B_QW, B_KVW, B_KVW, C_W)))

SCALE = 1.0 / math.sqrt(HEAD_DIM)
LOG2E = math.log2(math.e)

A_QROWS = 128
A_KWIN = 256
A_RADIUS = 64
A_NOFF = 3
A_BATCH = 8

BF16 = jnp.bfloat16
F32 = jnp.float32
BF16_TILE_ROWS = 16
F32_TILE_ROWS = 8

_NT = (((1,), (1,)), ((), ()))


def _software_pipeline(n_chunks, matmul, epilogue):
    acc = matmul(0)
    for c in range(n_chunks):
        nxt = matmul(c + 1) if c + 1 < n_chunks else None
        epilogue(c, acc)
        acc = nxt


def _cast_jobs(jobs, n_steps, step_index):
    in_specs, out_specs, out_shapes = [], [], []
    for src, col_block, width in jobs:
        rows = src.shape[0]
        rt = rows // n_steps
        assert rt * n_steps == rows and rt % BF16_TILE_ROWS == 0 and src.shape[1] % width == 0
        in_specs.append(pl.BlockSpec((rt, width), lambda *g, cb=col_block: (step_index(*g), cb)))
        out_specs.append(pl.BlockSpec((rt, width), lambda *g: (step_index(*g), 0)))
        out_shapes.append(jax.ShapeDtypeStruct((rows, width), BF16))
    return in_specs, out_specs, out_shapes


def _run_cast_jobs(src_refs, dst_refs):
    for src, dst in zip(src_refs, dst_refs):
        dst[...] = src[...].astype(dst.dtype)


def _head_weights_kernel(src_ref, perm_ref, wn_ref, wb_ref):
    o_qa, o_ka, o_va, o_qb, o_kb, o_vb, o_qc, o_gt = IN_OFFSETS
    cast = lambda lo, hi: src_ref[:, lo:hi].astype(BF16)
    wn_ref[:, :o_va] = cast(o_qa, o_va)
    wn_ref[:, o_va:] = cast(o_qc, o_gt)
    n_rot = B_QW + B_KVW
    for c0 in range(0, n_rot, HEAD_DIM):
        head = jnp.dot(cast(o_qb + c0, o_qb + c0 + HEAD_DIM), perm_ref[...], preferred_element_type=F32)
        wb_ref[:, c0:c0 + HEAD_DIM] = head.astype(BF16)
    wb_ref[:, n_rot:n_rot + B_KVW] = cast(o_vb, o_qc)
    wb_ref[:, n_rot + B_KVW:] = cast(o_va, o_qb)


def _head_weights(w_in, row_tile=256):
    rows = w_in.shape[0]
    n_head_cols = int(IN_OFFSETS[-1])
    assert rows % row_tile == 0
    half = HEAD_DIM // 2
    perm = np.zeros((HEAD_DIM, HEAD_DIM), np.float32)
    for j in range(HEAD_DIM):
        perm[2 * j if j < half else 2 * (j - half) + 1, j] = 1.0
    return pl.pallas_call(
        _head_weights_kernel,
        grid=(rows // row_tile,),
        in_specs=[pl.BlockSpec((row_tile, n_head_cols), lambda i: (i, 0)),
                  pl.BlockSpec((HEAD_DIM, HEAD_DIM), lambda i: (0, 0))],
        out_specs=[pl.BlockSpec((row_tile, 2 * A_W + C_W), lambda i: (i, 0)),
                   pl.BlockSpec((row_tile, B_QW + 2 * B_KVW + A_W), lambda i: (i, 0))],
        out_shape=[jax.ShapeDtypeStruct((rows, 2 * A_W + C_W), BF16),
                   jax.ShapeDtypeStruct((rows, B_QW + 2 * B_KVW + A_W), BF16)],
        compiler_params=_cparams("parallel"),
        name="head_weights",
    )(w_in, jnp.asarray(perm, BF16))


def _cparams(*sem, vmem_mb=None):
    limit = None if vmem_mb is None else vmem_mb * 1024 * 1024
    return pltpu.CompilerParams(dimension_semantics=sem, vmem_limit_bytes=limit)


def _proj_norm_kernel(x_hbm, gm_ref, w_ref, g_ref, o_ref, h_ref, xbuf, sem, *, r_sub, n_j):
    i, j = pl.program_id(0), pl.program_id(1)
    n_i = pl.num_programs(0)
    tm = h_ref.shape[0]
    piece = tm // n_j

    def x_copy(tile, p):
        slot = tile % 2
        rows = pl.ds(tile * tm + p * piece, piece)
        return pltpu.make_async_copy(x_hbm.at[rows, :], xbuf.at[slot, pl.ds(p * piece, piece), :], sem.at[slot, p])

    @pl.when((i == 0) & (j == 0))
    def _():
        for p in range(n_j):
            x_copy(0, p).start()

    @pl.when(i + 1 < n_i)
    def _():
        for p in range(n_j):
            @pl.when(j == p)
            def _():
                x_copy(i + 1, p).start()

    @pl.when(j == 0)
    def _():
        for p in range(n_j):
            x_copy(i, p).wait()
        x = xbuf[i % 2]
        ms = jnp.mean(x * x, axis=-1, keepdims=True)
        h_ref[...] = (x * lax.rsqrt(ms + EPS) * gm_ref[...]).astype(h_ref.dtype)

    def matmul(c):
        return jnp.dot(h_ref[c * r_sub:(c + 1) * r_sub, :], w_ref[...], preferred_element_type=F32)

    def epilogue(c, acc):
        for hh in range(o_ref.shape[1] // HEAD_DIM):
            sl = slice(hh * HEAD_DIM, (hh + 1) * HEAD_DIM)
            y = acc[:, sl]
            ms = jnp.mean(y * y, axis=-1, keepdims=True)
            o_ref[c * r_sub:(c + 1) * r_sub, sl] = (y * lax.rsqrt(ms + EPS) * g_ref[:, sl]).astype(o_ref.dtype)

    _software_pipeline(h_ref.shape[0] // r_sub, matmul, epilogue)


def _proj_norm(x2d, g_mix, w, gains, col_block, tm=1024, tn=512, r_sub=256):
    m, d = x2d.shape
    n = gains.shape[0]
    n_j = n // tn
    assert n % tn == 0 and m % tm == 0 and tm % (F32_TILE_ROWS * n_j) == 0
    return pl.pallas_call(
        functools.partial(_proj_norm_kernel, r_sub=r_sub, n_j=n_j),
        grid=(m // tm, n_j),
        in_specs=[pl.BlockSpec(memory_space=pl.ANY),
                  pl.BlockSpec((1, d), lambda i, j: (0, 0)),
                  pl.BlockSpec((d, tn), lambda i, j: (0, col_block(j))),
                  pl.BlockSpec((1, tn), lambda i, j: (0, j))],
        out_specs=[pl.BlockSpec((tm, tn), lambda i, j: (i, j)),
                   pl.BlockSpec((tm, d), lambda i, j: (i, 0))],
        out_shape=[jax.ShapeDtypeStruct((m, n), BF16), jax.ShapeDtypeStruct((m, d), BF16)],
        scratch_shapes=[pltpu.VMEM((2, tm, d), F32), pltpu.SemaphoreType.DMA((2, n_j))],
        compiler_params=_cparams("arbitrary", "arbitrary", vmem_mb=48),
        name="proj_norm",
    )(x2d, g_mix.reshape(1, d), w, gains.reshape(1, n))


def _proj_t_kernel(w_ref, h_ref, gq_ref, gk_ref, cos_ref, sin_ref, *rest, t_sub, n_cast):
    q_ref, k_ref, v_ref, va_ref = rest[n_cast:n_cast + 4]
    _run_cast_jobs(rest[:n_cast], rest[n_cast + 4:])
    n_q = q_ref.shape[0] // HEAD_DIM
    n_k = k_ref.shape[1] // HEAD_DIM
    n_v = v_ref.shape[0] // HEAD_DIM
    half = HEAD_DIM // 2

    def norm_rope(y, g_ref, tok):
        ms = jnp.mean(y * y, axis=0, keepdims=True)
        y = y * lax.rsqrt(ms + EPS) * g_ref[...]
        partner = jnp.concatenate([y[half:], y[:half]], axis=0)
        return y * cos_ref[:, tok] + partner * sin_ref[:, tok]

    def matmul(c):
        return jnp.dot(h_ref[c * t_sub:(c + 1) * t_sub, :], w_ref[...], preferred_element_type=F32)

    def epilogue(c, y):
        tok = slice(c * t_sub, (c + 1) * t_sub)
        head = lambda hh: y[:, hh * HEAD_DIM:(hh + 1) * HEAD_DIM].T
        for hh in range(n_q):
            q_ref[hh * HEAD_DIM:(hh + 1) * HEAD_DIM, tok] = norm_rope(head(hh), gq_ref, tok).astype(q_ref.dtype)
        for hh in range(n_k):
            kt = norm_rope(head(n_q + hh), gk_ref, tok)
            k_ref[tok, hh * HEAD_DIM:(hh + 1) * HEAD_DIM] = kt.T.astype(k_ref.dtype)
        for hh in range(n_v):
            v_ref[hh * HEAD_DIM:(hh + 1) * HEAD_DIM, tok] = head(n_q + n_k + hh).astype(v_ref.dtype)
        va_ref[tok, :] = y[:, (n_q + n_k + n_v) * HEAD_DIM:].astype(va_ref.dtype)

    _software_pipeline(h_ref.shape[0] // t_sub, matmul, epilogue)


def _proj_t(h, w, gq_col, gk_col, cos_tt, sin_tt, b, seq, cast_jobs, tm=1024, t_sub=128):
    m, d = h.shape
    n = w.shape[1]
    assert n == B_QW + 2 * B_KVW + A_W
    sb = seq // tm
    col = lambda g: jnp.broadcast_to(g[:, None], (HEAD_DIM, t_sub))
    lane_tile = lambda rows: pl.BlockSpec((None, rows, tm), lambda i: (i // sb, 0, i % sb))
    c_in, c_out, c_shapes = _cast_jobs(cast_jobs, m // tm, lambda i: i)
    return pl.pallas_call(
        functools.partial(_proj_t_kernel, t_sub=t_sub, n_cast=len(cast_jobs)),
        grid=(m // tm,),
        in_specs=[pl.BlockSpec((d, n), lambda i: (0, 0)),
                  pl.BlockSpec((tm, d), lambda i: (i, 0)),
                  pl.BlockSpec((HEAD_DIM, t_sub), lambda i: (0, 0)),
                  pl.BlockSpec((HEAD_DIM, t_sub), lambda i: (0, 0)),
                  pl.BlockSpec((HEAD_DIM, tm), lambda i: (0, i % sb)),
                  pl.BlockSpec((HEAD_DIM, tm), lambda i: (0, i % sb))] + c_in,
        out_specs=[lane_tile(B_QW),
                   pl.BlockSpec((tm, B_KVW), lambda i: (i, 0)),
                   lane_tile(B_KVW),
                   pl.BlockSpec((tm, A_W), lambda i: (i, 0))] + c_out,
        out_shape=[jax.ShapeDtypeStruct((b, B_QW, seq), BF16),
                   jax.ShapeDtypeStruct((m, B_KVW), BF16),
                   jax.ShapeDtypeStruct((b, B_KVW, seq), BF16),
                   jax.ShapeDtypeStruct((m, A_W), BF16)] + c_shapes,
        compiler_params=_cparams("parallel", vmem_mb=56),
        name="proj_t",
    )(w, h, col(gq_col), col(gk_col), cos_tt, sin_tt, *[src for src, _, _ in cast_jobs])


def _t5_bucket(rel):
    nb = REL_BUCKETS // 2
    ret = np.where(rel > 0, nb, 0)
    n = np.abs(rel)
    max_exact = nb // 2
    large = max_exact + (np.log(np.maximum(n, 1).astype(np.float32) / np.float32(max_exact))
                         / np.float32(math.log(REL_MAX_DIST / max_exact))
                         * np.float32(nb - max_exact)).astype(np.int32)
    large = np.minimum(large, nb - 1)
    return ret + np.where(n < max_exact, n, large)


def _a_bucket_index():
    qi = np.arange(A_QROWS, dtype=np.int32)[:, None]
    kj = np.arange(A_KWIN, dtype=np.int32)[None, :]
    out = []
    for _, dil in DIL_PAIRS:
        for off in range(A_NOFF):
            rel = kj - qi - A_RADIUS * off
            out.append(np.where(np.abs(rel) <= A_RADIUS, _t5_bucket(rel * dil), -1))
    return np.stack(out).astype(np.int32)


def _a_bias_kernel(tab_ref, bucket_ref, o_ref):
    g = pl.program_id(0)
    for off in range(A_NOFF):
        bk = bucket_ref[off]
        for hh in range(A_HEADS_PER_GROUP):
            acc = jnp.full(bk.shape, NEG, F32)
            for b in range(REL_BUCKETS):
                acc = jnp.where(bk == b, tab_ref[b, g * A_HEADS_PER_GROUP + hh], acc)
            o_ref[off, hh] = acc


def _a_bias(rel_bias):
    n = len(DIL_PAIRS) * A_NOFF
    return pl.pallas_call(
        _a_bias_kernel,
        grid=(len(DIL_PAIRS),),
        in_specs=[pl.BlockSpec(memory_space=pltpu.SMEM),
                  pl.BlockSpec((A_NOFF, A_QROWS, A_KWIN), lambda i: (i, 0, 0))],
        out_specs=pl.BlockSpec((A_NOFF, A_HEADS_PER_GROUP, A_QROWS, A_KWIN), lambda i: (i, 0, 0, 0)),
        out_shape=jax.ShapeDtypeStruct((n, A_HEADS_PER_GROUP, A_QROWS, A_KWIN), F32),
        compiler_params=_cparams("arbitrary"),
        name="a_bias",
    )(rel_bias, _a_bucket_index())


def _mixer_a_kernel(q_ref, k_ref, v_ref, bias_ref, oa_ref, stage, qstage, o_acc, lse_acc, *residue_kv, seq):
    t_rows = q_ref.shape[0]
    ti = pl.program_id(1)
    n_groups = len(DIL_PAIRS)

    def softmax(s):
        m = jnp.max(s, axis=-1, keepdims=True)
        p = jnp.exp(s - m)
        l = jnp.sum(p, axis=-1, keepdims=True)
        return p.astype(BF16), l, m + jnp.log(l)

    def run_group(gi, dil, kres, vres):
        sub_len = seq // dil
        lq = t_rows // dil
        gcols = lambda hh: slice(gi * A_OUT_W + hh * HEAD_DIM, gi * A_OUT_W + (hh + 1) * HEAD_DIM)
        if dil > 1:
            @pl.when(ti == 0)
            def _():
                for src, dst in ((k_ref, kres), (v_ref, vres)):
                    for hh in range(A_HEADS_PER_GROUP):
                        stage[...] = src[:, gcols(hh)].astype(F32)
                        for r in range(dil):
                            dst[r, :, hh * HEAD_DIM:(hh + 1) * HEAD_DIM] = (
                                stage[pl.ds(r, sub_len, stride=dil), :].astype(BF16))

            for hh in range(A_HEADS_PER_GROUP):
                qstage[hh] = q_ref[:, gcols(hh)].astype(F32)

        def scores(r, i, hh):
            q0 = ti * lq + i * A_QROWS
            ks = jnp.clip(q0 - A_RADIUS, 0, sub_len - A_KWIN)
            off = lax.shift_right_logical(q0 - ks, int(math.log2(A_RADIUS)))
            ks = pl.multiple_of(ks, A_RADIUS)
            if dil > 1:
                cols = slice(hh * HEAD_DIM, (hh + 1) * HEAD_DIM)
                rows = pl.ds(i * A_QROWS * dil + r, A_QROWS, stride=dil)
                q = qstage[hh, rows, :].astype(BF16)
                k = kres[r, pl.ds(ks, A_KWIN), cols]
                v = vres[r, pl.ds(ks, A_KWIN), cols]
            else:
                rows = pl.ds(i * A_QROWS, A_QROWS)
                q = q_ref[rows, gcols(hh)]
                k = k_ref[pl.ds(ks, A_KWIN), gcols(hh)]
                v = v_ref[pl.ds(ks, A_KWIN), gcols(hh)]
            s = lax.dot_general(q, k, _NT, preferred_element_type=F32) * SCALE + bias_ref[gi * A_NOFF + off, hh]
            return rows, s, v

        def fold(hh, rows, o, lse):
            lse = jnp.broadcast_to(lse, (A_QROWS, HEAD_DIM))
            if gi > 0:
                prev_o, prev_lse = o_acc[hh, rows, :], lse_acc[hh, rows, :]
                m = jnp.maximum(prev_lse, lse)
                w_prev, w_new = jnp.exp(prev_lse - m), jnp.exp(lse - m)
                den = w_prev + w_new
                o = (w_prev * prev_o + w_new * o) / den
                lse = m + jnp.log(den)
            o_acc[hh, rows, :] = o
            if gi + 1 < n_groups:
                lse_acc[hh, rows, :] = lse

        items = [(r, i, hh) for r in range(dil) for i in range(lq // A_QROWS) for hh in range(A_HEADS_PER_GROUP)]
        for b0 in range(0, len(items), A_BATCH):
            batch = items[b0:b0 + A_BATCH]
            staged = [scores(*it) for it in batch]
            probs = [softmax(s) for _, s, _ in staged]
            for (_, _, hh), (rows, _, v), (p, l, lse) in zip(batch, staged, probs):
                fold(hh, rows, jnp.dot(p, v, preferred_element_type=F32) / l, lse)

    strided = [gi for gi, (_, dil) in enumerate(DIL_PAIRS) if dil > 1]
    for gi, (_, dil) in enumerate(DIL_PAIRS):
        kres, vres = (residue_kv[2 * strided.index(gi):2 * strided.index(gi) + 2] if dil > 1 else (None, None))
        run_group(gi, dil, kres, vres)
    for hh in range(A_HEADS_PER_GROUP):
        oa_ref[:, hh * HEAD_DIM:(hh + 1) * HEAD_DIM] = o_acc[hh].astype(oa_ref.dtype)


def _mixer_a(qkn, va, bias_a, t_rows=2048):
    b, seq, _ = qkn.shape
    head_buf = lambda rows: pltpu.VMEM((A_HEADS_PER_GROUP, rows, HEAD_DIM), F32)
    scratch = [pltpu.VMEM((seq, HEAD_DIM), F32), head_buf(t_rows), head_buf(t_rows), head_buf(t_rows)]
    for _, dil in DIL_PAIRS:
        if dil > 1:
            scratch += [pltpu.VMEM((dil, seq // dil, A_OUT_W), BF16)] * 2
    return pl.pallas_call(
        functools.partial(_mixer_a_kernel, seq=seq),
        grid=(b, seq // t_rows),
        in_specs=[pl.BlockSpec((None, t_rows, A_W), lambda bi, ti: (bi, ti, 0)),
                  pl.BlockSpec((None, seq, A_W), lambda bi, ti: (bi, 0, 1)),
                  pl.BlockSpec((None, seq, A_W), lambda bi, ti: (bi, 0, 0)),
                  pl.BlockSpec(bias_a.shape, lambda bi, ti: (0, 0, 0, 0), pipeline_mode=pl.Buffered(1))],
        out_specs=pl.BlockSpec((None, t_rows, A_OUT_W), lambda bi, ti: (bi, ti, 0)),
        out_shape=jax.ShapeDtypeStruct((b, seq, A_OUT_W), BF16),
        scratch_shapes=scratch,
        compiler_params=_cparams("parallel", "arbitrary", vmem_mb=62),
        name="mixer_a",
    )(qkn, qkn, va, bias_a)


def _mixer_b_kernel(qt_ref, k_ref, vt_ref, *rest, tk, group, w, n_cast):
    o_ref, s_scr = rest[n_cast], rest[-1]
    _run_cast_jobs(rest[:n_cast], rest[n_cast + 1:-1])
    tq = qt_ref.shape[1]
    seq = k_ref.shape[0]
    n_chunks = seq // tk
    units = [(i, j) for i in range(group) for j in range(tq // w)]

    def pass_a(u, ci, m):
        i, j = units[u]
        qt = qt_ref[i * HEAD_DIM:(i + 1) * HEAD_DIM, j * w:(j + 1) * w]
        st = jnp.dot(k_ref[ci * tk:(ci + 1) * tk, :], qt, preferred_element_type=F32)
        s_scr[u % 2, ci * tk:(ci + 1) * tk, :] = st
        return jnp.maximum(m, jnp.max(st, axis=0, keepdims=True))

    def pass_b(u, ci, m, l, acc):
        pt = jnp.exp2(s_scr[u % 2, ci * tk:(ci + 1) * tk, :] - m)
        l = l + jnp.sum(pt, axis=0, keepdims=True)
        acc = acc + jnp.dot(vt_ref[:, ci * tk:(ci + 1) * tk], pt.astype(BF16), preferred_element_type=F32)
        return l, acc

    m_prev = None
    for s in range(len(units) + 1):
        m_cur = jnp.full((1, w), NEG, F32)
        l = jnp.zeros((1, w), F32)
        acc = jnp.zeros((HEAD_DIM, w), F32)
        for ci in range(n_chunks):
            if s < len(units):
                m_cur = pass_a(s, ci, m_cur)
            if s > 0:
                l, acc = pass_b(s - 1, ci, m_prev, l, acc)
        if s > 0:
            i, j = units[s - 1]
            o_ref[j * w:(j + 1) * w, i * HEAD_DIM:(i + 1) * HEAD_DIM] = (acc / l).T.astype(o_ref.dtype)
        m_prev = m_cur


def _mixer_b(qbt, kb, vbt, cast_jobs, tq=2048, tk=512, w=256):
    b, seq, _ = kb.shape
    group = B_Q_HEADS // B_KV_HEADS
    gw = group * HEAD_DIM
    nq = seq // tq
    n_steps = b * B_KV_HEADS * nq
    c_in, c_out, c_shapes = _cast_jobs(cast_jobs, n_steps, lambda bi, kv, qi: (bi * B_KV_HEADS + kv) * nq + qi)
    return pl.pallas_call(
        functools.partial(_mixer_b_kernel, tk=tk, group=group, w=w, n_cast=len(cast_jobs)),
        grid=(b, B_KV_HEADS, nq),
        scratch_shapes=[pltpu.VMEM((2, seq, w), F32)],
        in_specs=[pl.BlockSpec((None, gw, tq), lambda bi, kv, qi: (bi, kv, qi)),
                  pl.BlockSpec((None, seq, HEAD_DIM), lambda bi, kv, qi: (bi, 0, kv)),
                  pl.BlockSpec((None, HEAD_DIM, seq), lambda bi, kv, qi: (bi, kv, 0))] + c_in,
        out_specs=[pl.BlockSpec((None, tq, gw), lambda bi, kv, qi: (bi, qi, kv))] + c_out,
        out_shape=[jax.ShapeDtypeStruct((b, seq, B_QW), BF16)] + c_shapes,
        compiler_params=_cparams("parallel", "parallel", "arbitrary"),
        name="mixer_b",
    )(qbt, kb, vbt, *[src for src, _, _ in cast_jobs])


def _mixer_c_kernel(q_ref, mem_ref, gm_ref, w_ref, gk_ref, o_ref, k_scr, v_scr):
    @pl.when(pl.program_id(1) == 0)
    def _():
        x = mem_ref[...]
        ms = jnp.mean(x * x, axis=-1, keepdims=True)
        hm = (x * lax.rsqrt(ms + EPS) * gm_ref[...]).astype(BF16)
        kv = jnp.dot(hm, w_ref[...], preferred_element_type=F32)
        for hh in range(C_HEADS):
            sl = slice(hh * HEAD_DIM, (hh + 1) * HEAD_DIM)
            y = kv[:, sl]
            ms = jnp.mean(y * y, axis=-1, keepdims=True)
            k_scr[:, sl] = (y * lax.rsqrt(ms + EPS) * gk_ref[...]).astype(BF16)
        v_scr[...] = kv[:, C_W:].astype(BF16)

    for hh in range(C_HEADS):
        sl = slice(hh * HEAD_DIM, (hh + 1) * HEAD_DIM)
        s = lax.dot_general(q_ref[:, sl], k_scr[:, sl], _NT, preferred_element_type=F32) * SCALE
        m = jnp.max(s, axis=-1, keepdims=True)
        p = jnp.exp(s - m)
        l = jnp.sum(p, axis=-1, keepdims=True)
        o = jnp.dot(p.astype(BF16), v_scr[:, sl], preferred_element_type=F32) / l
        o_ref[:, sl] = o.astype(o_ref.dtype)


def _mixer_c(qkn, mem, g_mem, w_kv, g_kc, tq=1024):
    b, seq, _ = qkn.shape
    _, n_mem, d = mem.shape
    qc_blk = (2 * A_W) // C_W
    return pl.pallas_call(
        _mixer_c_kernel,
        grid=(b, seq // tq),
        in_specs=[pl.BlockSpec((None, tq, C_W), lambda bi, qi: (bi, qi, qc_blk)),
                  pl.BlockSpec((None, n_mem, d), lambda bi, qi: (bi, 0, 0)),
                  pl.BlockSpec((1, d), lambda bi, qi: (0, 0)),
                  pl.BlockSpec((d, 2 * C_W), lambda bi, qi: (0, 0)),
                  pl.BlockSpec((1, HEAD_DIM), lambda bi, qi: (0, 0))],
        out_specs=pl.BlockSpec((None, tq, C_W), lambda bi, qi: (bi, qi, 0)),
        out_shape=jax.ShapeDtypeStruct((b, seq, C_W), BF16),
        scratch_shapes=[pltpu.VMEM((n_mem, C_W), BF16), pltpu.VMEM((n_mem, C_W), BF16)],
        compiler_params=_cparams("parallel", "arbitrary"),
        name="mixer_c",
    )(qkn, mem, g_mem.reshape(1, d), w_kv, g_kc.reshape(1, HEAD_DIM))


def _merge_kernel(h_ref, wg0_ref, wg1_ref, wg2_ref, oa_ref, ob_ref, oc_ref, wa_ref, wb_ref, wc_ref, out_ref):
    h = h_ref[...]
    ga = jax.nn.sigmoid(jnp.dot(h, wg0_ref[...], preferred_element_type=F32))
    merged = ga * jnp.dot(oa_ref[...], wa_ref[...], preferred_element_type=F32)
    gb = jax.nn.sigmoid(jnp.dot(h, wg1_ref[...], preferred_element_type=F32))
    merged += gb * jnp.dot(ob_ref[...], wb_ref[...], preferred_element_type=F32)
    gc = jax.nn.sigmoid(jnp.dot(h, wg2_ref[...], preferred_element_type=F32))
    merged += gc * jnp.dot(oc_ref[...], wc_ref[...], preferred_element_type=F32)
    out_ref[...] = merged.astype(out_ref.dtype)


def _merge(h, w_gates, oa, ob, oc, wa, wb, wc, tm=1024, tn=512):
    m, d = h.shape
    nj = d // tn
    row = lambda w: pl.BlockSpec((tm, w), lambda i, j: (i, 0))
    col = lambda k: pl.BlockSpec((k, tn), lambda i, j: (0, j))
    return pl.pallas_call(
        _merge_kernel,
        grid=(m // tm, nj),
        in_specs=[row(d), col(d), col(d), col(d), row(A_OUT_W), row(B_QW), row(C_W),
                  col(A_OUT_W), col(B_QW), col(C_W)],
        out_specs=pl.BlockSpec((tm, tn), lambda i, j: (i, j)),
        out_shape=jax.ShapeDtypeStruct((m, d), BF16),
        compiler_params=_cparams("parallel", "arbitrary", vmem_mb=48),
        name="merge",
    )(h, *w_gates, oa, ob, oc, wa, wb, wc)


def _out_proj_kernel(mg_ref, w_ref, x_ref, g_ref, x1_ref, h2_ref, *, r_sub):
    def matmul(c):
        return jnp.dot(mg_ref[c * r_sub:(c + 1) * r_sub, :], w_ref[...], preferred_element_type=F32)

    def epilogue(c, acc):
        rows = slice(c * r_sub, (c + 1) * r_sub)
        x1 = x_ref[rows, :] + acc
        x1_ref[rows, :] = x1
        ms = jnp.mean(x1 * x1, axis=-1, keepdims=True)
        h2_ref[rows, :] = (x1 * lax.rsqrt(ms + EPS) * g_ref[...]).astype(h2_ref.dtype)

    _software_pipeline(mg_ref.shape[0] // r_sub, matmul, epilogue)


def _out_proj(merged, w_o, x2d, g_ffn, tm=512, r_sub=512):
    m, d = x2d.shape
    return pl.pallas_call(
        functools.partial(_out_proj_kernel, r_sub=r_sub),
        grid=(m // tm,),
        in_specs=[pl.BlockSpec((tm, d), lambda i: (i, 0)),
                  pl.BlockSpec((d, d), lambda i: (0, 0)),
                  pl.BlockSpec((tm, d), lambda i: (i, 0)),
                  pl.BlockSpec((1, d), lambda i: (0, 0))],
        out_specs=[pl.BlockSpec((tm, d), lambda i: (i, 0))] * 2,
        out_shape=[jax.ShapeDtypeStruct((m, d), F32), jax.ShapeDtypeStruct((m, d), BF16)],
        compiler_params=_cparams("parallel"),
        name="out_proj",
    )(merged, w_o, x2d, g_ffn.reshape(1, d))


def _ffn_kernel(h_ref, wa_ref, wb_ref, wo_ref, x1_ref, out_ref, *, n_slabs):
    f = pl.program_id(1)
    slab_w = x1_ref.shape[1]

    @pl.when(f == 0)
    def _():
        out_ref[...] = jnp.zeros_like(out_ref)

    for s in range(n_slabs):
        @pl.when(f == s)
        def _():
            out_ref[:, s * slab_w:(s + 1) * slab_w] += x1_ref[...]

    h = h_ref[...]
    a = jnp.dot(h, wa_ref[...], preferred_element_type=F32)
    b = jnp.dot(h, wb_ref[...], preferred_element_type=F32)
    act = (a * jax.nn.sigmoid(a) * b).astype(BF16)
    out_ref[...] += jnp.dot(act, wo_ref[...], preferred_element_type=F32)


def _ffn(h2, w_a, w_b, w_out, x1, tm=1024, tf=512, slab_w=256):
    m, d = h2.shape
    d_ff = w_out.shape[0]
    nf = d_ff // tf
    n_slabs = d // slab_w
    assert n_slabs <= nf
    return pl.pallas_call(
        functools.partial(_ffn_kernel, n_slabs=n_slabs),
        grid=(m // tm, nf),
        in_specs=[pl.BlockSpec((tm, d), lambda i, f: (i, 0)),
                  pl.BlockSpec((d, tf), lambda i, f: (0, f)),
                  pl.BlockSpec((d, tf), lambda i, f: (0, f)),
                  pl.BlockSpec((tf, d), lambda i, f: (f, 0)),
                  pl.BlockSpec((tm, slab_w), lambda i, f: (i, jnp.minimum(f, n_slabs - 1)))],
        out_specs=pl.BlockSpec((tm, d), lambda i, f: (i, 0)),
        out_shape=jax.ShapeDtypeStruct((m, d), F32),
        compiler_params=_cparams("parallel", "arbitrary", vmem_mb=48),
        name="ffn",
    )(h2, w_a, w_b, w_out, x1)


def _rope_tables(seq):
    rows = seq // GRID_W
    r = np.repeat(np.arange(rows), GRID_W).astype(np.float64)
    c = np.tile(np.arange(GRID_W), rows).astype(np.float64)
    nf = HEAD_DIM // 4
    inv = ROPE_THETA ** (-np.arange(nf, dtype=np.float64) / nf)
    ang = np.concatenate([r[:, None] * inv, c[:, None] * inv], axis=-1)
    cos, sin = np.cos(ang).T, np.sin(ang).T
    return (np.concatenate([cos, cos], axis=0).astype(np.float32),
            np.concatenate([-sin, sin], axis=0).astype(np.float32))


def _layer(x2d, mem, bias_a, cos_t, sin_t, b, seq, g_mix, w_in, g_qa, g_ka, g_qb, g_kb, g_mem, w_mem_kv,
           g_qc, g_kc, w_br_a, w_br_b, w_br_c, w_o, g_ffn, w_ffn_in, w_ffn_out):
    d = x2d.shape[1]
    o_gt = IN_OFFSETS[-1]
    d_ff = w_ffn_out.shape[0]
    w_norm, w_b = _head_weights(w_in)
    g_norm = jnp.concatenate([jnp.tile(g_qa, A_HEADS), jnp.tile(g_ka, A_HEADS), jnp.tile(g_qc, C_HEADS)])
    deint = lambda g: g.reshape(HEAD_DIM // 2, 2).T.reshape(HEAD_DIM)

    qkn, h = _proj_norm(x2d, g_mix, w_norm, g_norm, lambda j: j, tn=2 * C_W)
    qkn = qkn.reshape(b, seq, -1)
    late_weights = [(w_ffn_out, 0, d), (w_o, 0, d), (w_br_a, 0, d), (w_br_b, 0, d), (w_br_c, 0, d),
                    (w_mem_kv, 0, 2 * C_W)]
    qbt, kb, vbt, va, w_out_b, w_o_b, wa_b, wb_b, wc_b, w_kv_b = _proj_t(
        h, w_b, deint(g_qb) * (SCALE * LOG2E), deint(g_kb), cos_t, sin_t, b, seq, late_weights)
    kb = kb.reshape(b, seq, B_KVW)
    va = va.reshape(b, seq, A_W)

    oa = _mixer_a(qkn, va, bias_a).reshape(b * seq, A_OUT_W)
    gate_jobs = [(w_in, int(o_gt) // d + br, d) for br in range(N_BRANCH)]
    ob, w_ffa, w_ffb, *w_gates = _mixer_b(qbt, kb, vbt, [(w_ffn_in, 0, d_ff), (w_ffn_in, 1, d_ff)] + gate_jobs)
    ob = ob.reshape(b * seq, B_QW)

    oc = _mixer_c(qkn, mem, g_mem, w_kv_b, g_kc).reshape(b * seq, C_W)

    merged = _merge(h, w_gates, oa, ob, oc, wa_b, wb_b, wc_b)
    x1, h2 = _out_proj(merged, w_o_b, x2d, g_ffn)
    return _ffn(h2, w_ffa, w_ffb, w_out_b, x1)


def kernel(x, mem, rel_bias, g_mix, w_in, g_qa, g_ka, g_qb, g_kb, g_mem, w_mem_kv, g_qc, g_kc,
           w_br_a, w_br_b, w_br_c, w_o, g_ffn, w_ffn_in, w_ffn_out):
    b, seq, d = x.shape
    depth = w_in.shape[0]
    cos_t, sin_t = _rope_tables(seq)
    bias_a = _a_bias(rel_bias)
    x2d = x.reshape(b * seq, d)
    for layer in range(depth):
        x2d = _layer(x2d, mem, bias_a, cos_t, sin_t, b, seq,
                     g_mix[layer], w_in[layer], g_qa[layer], g_ka[layer], g_qb[layer], g_kb[layer],
                     g_mem[layer], w_mem_kv[layer], g_qc[layer], g_kc[layer],
                     w_br_a[layer], w_br_b[layer], w_br_c[layer], w_o[layer], g_ffn[layer],
                     w_ffn_in[layer], w_ffn_out[layer])
    return x2d.reshape(b, seq, d)
```

```python
import functools
import math

import numpy as np
import jax
import jax.numpy as jnp
from jax import lax
from jax.experimental import pallas as pl
from jax.experimental.pallas import tpu as pltpu

HEAD_DIM = 128
GRID_W = 64
DIL_PAIRS = ((128, 1), (512, 4), (2048, 16))
A_HEADS_PER_GROUP = 2
A_HEADS = A_HEADS_PER_GROUP * len(DIL_PAIRS)
B_Q_HEADS = 6
B_KV_HEADS = 2
ROPE_THETA = 10000.0
C_HEADS = 4
N_BRANCH = 3
REL_BUCKETS = 32
REL_MAX_DIST = 1024
EPS = 1e-6
NEG = -1e30

A_W = A_HEADS * HEAD_DIM
A_OUT_W = A_HEADS_PER_GROUP * HEAD_DIM
B_QW = B_Q_HEADS * HEAD_DIM
B_KVW = B_KV_HEADS * HEAD_DIM
C_W = C_HEADS * HEAD_DIM
IN_OFFSETS = tuple(int(v) for v in np.cumsum((0, A_W, A_W, A_W, B_QW, B_KVW, B_KVW, C_W)))

SCALE = 1.0 / math.sqrt(HEAD_DIM)
LOG2E = math.log2(math.e)

A_QROWS = 128
A_KWIN = 256
A_RADIUS = 64
A_NOFF = 3
A_BATCH = 8

BF16 = jnp.bfloat16
F32 = jnp.float32
BF16_TILE_ROWS = 16
F32_TILE_ROWS = 8

_NT = (((1,), (1,)), ((), ()))


def _software_pipeline(n_chunks, matmul, epilogue):
    acc = matmul(0)
    for c in range(n_chunks):
        nxt = matmul(c + 1) if c + 1 < n_chunks else None
        epilogue(c, acc)
        acc = nxt


def _cast_jobs(jobs, n_steps, step_index):
    in_specs, out_specs, out_shapes = [], [], []
    for src, col_block, width in jobs:
        rows = src.shape[0]
        rt = rows // n_steps
        assert rt * n_steps == rows and rt % BF16_TILE_ROWS == 0 and src.shape[1] % width == 0
        in_specs.append(pl.BlockSpec((rt, width), lambda *g, cb=col_block: (step_index(*g), cb)))
        out_specs.append(pl.BlockSpec((rt, width), lambda *g: (step_index(*g), 0)))
        out_shapes.append(jax.ShapeDtypeStruct((rows, width), BF16))
    return in_specs, out_specs, out_shapes


def _run_cast_jobs(src_refs, dst_refs):
    for src, dst in zip(src_refs, dst_refs):
        dst[...] = src[...].astype(dst.dtype)


def _head_weights_kernel(src_ref, perm_ref, wn_ref, wb_ref):
    o_qa, o_ka, o_va, o_qb, o_kb, o_vb, o_qc, o_gt = IN_OFFSETS
    cast = lambda lo, hi: src_ref[:, lo:hi].astype(BF16)
    wn_ref[:, :o_va] = cast(o_qa, o_va)
    wn_ref[:, o_va:] = cast(o_qc, o_gt)
    n_rot = B_QW + B_KVW
    for c0 in range(0, n_rot, HEAD_DIM):
        head = jnp.dot(cast(o_qb + c0, o_qb + c0 + HEAD_DIM), perm_ref[...], preferred_element_type=F32)
        wb_ref[:, c0:c0 + HEAD_DIM] = head.astype(BF16)
    wb_ref[:, n_rot:n_rot + B_KVW] = cast(o_vb, o_qc)
    wb_ref[:, n_rot + B_KVW:] = cast(o_va, o_qb)


def _head_weights(w_in, row_tile=256):
    rows = w_in.shape[0]
    n_head_cols = int(IN_OFFSETS[-1])
    assert rows % row_tile == 0
    half = HEAD_DIM // 2
    perm = np.zeros((HEAD_DIM, HEAD_DIM), np.float32)
    for j in range(HEAD_DIM):
        perm[2 * j if j < half else 2 * (j - half) + 1, j] = 1.0
    return pl.pallas_call(
        _head_weights_kernel,
        grid=(rows // row_tile,),
        in_specs=[pl.BlockSpec((row_tile, n_head_cols), lambda i: (i, 0)),
                  pl.BlockSpec((HEAD_DIM, HEAD_DIM), lambda i: (0, 0))],
        out_specs=[pl.BlockSpec((row_tile, 2 * A_W + C_W), lambda i: (i, 0)),
                   pl.BlockSpec((row_tile, B_QW + 2 * B_KVW + A_W), lambda i: (i, 0))],
        out_shape=[jax.ShapeDtypeStruct((rows, 2 * A_W + C_W), BF16),
                   jax.ShapeDtypeStruct((rows, B_QW + 2 * B_KVW + A_W), BF16)],
        compiler_params=_cparams("parallel"),
        name="head_weights",
    )(w_in, jnp.asarray(perm, BF16))


def _cparams(*sem, vmem_mb=None):
    limit = None if vmem_mb is None else vmem_mb * 1024 * 1024
    return pltpu.CompilerParams(dimension_semantics=sem, vmem_limit_bytes=limit)


def _proj_norm_kernel(x_hbm, gm_ref, w_ref, g_ref, o_ref, h_ref, xbuf, sem, *, r_sub, n_j):
    i, j = pl.program_id(0), pl.program_id(1)
    n_i = pl.num_programs(0)
    tm = h_ref.shape[0]
    piece = tm // n_j

    def x_copy(tile, p):
        slot = tile % 2
        rows = pl.ds(tile * tm + p * piece, piece)
        return pltpu.make_async_copy(x_hbm.at[rows, :], xbuf.at[slot, pl.ds(p * piece, piece), :], sem.at[slot, p])

    @pl.when((i == 0) & (j == 0))
    def _():
        for p in range(n_j):
            x_copy(0, p).start()

    @pl.when(i + 1 < n_i)
    def _():
        for p in range(n_j):
            @pl.when(j == p)
            def _():
                x_copy(i + 1, p).start()

    @pl.when(j == 0)
    def _():
        for p in range(n_j):
            x_copy(i, p).wait()
        x = xbuf[i % 2]
        ms = jnp.mean(x * x, axis=-1, keepdims=True)
        h_ref[...] = (x * lax.rsqrt(ms + EPS) * gm_ref[...]).astype(h_ref.dtype)

    def matmul(c):
        return jnp.dot(h_ref[c * r_sub:(c + 1) * r_sub, :], w_ref[...], preferred_element_type=F32)

    def epilogue(c, acc):
        for hh in range(o_ref.shape[1] // HEAD_DIM):
            sl = slice(hh * HEAD_DIM, (hh + 1) * HEAD_DIM)
            y = acc[:, sl]
            ms = jnp.mean(y * y, axis=-1, keepdims=True)
            o_ref[c * r_sub:(c + 1) * r_sub, sl] = (y * lax.rsqrt(ms + EPS) * g_ref[:, sl]).astype(o_ref.dtype)

    _software_pipeline(h_ref.shape[0] // r_sub, matmul, epilogue)


def _proj_norm(x2d, g_mix, w, gains, col_block, tm=1024, tn=512, r_sub=256):
    m, d = x2d.shape
    n = gains.shape[0]
    n_j = n // tn
    assert n % tn == 0 and m % tm == 0 and tm % (F32_TILE_ROWS * n_j) == 0
    return pl.pallas_call(
        functools.partial(_proj_norm_kernel, r_sub=r_sub, n_j=n_j),
        grid=(m // tm, n_j),
        in_specs=[pl.BlockSpec(memory_space=pl.ANY),
                  pl.BlockSpec((1, d), lambda i, j: (0, 0)),
                  pl.BlockSpec((d, tn), lambda i, j: (0, col_block(j))),
                  pl.BlockSpec((1, tn), lambda i, j: (0, j))],
        out_specs=[pl.BlockSpec((tm, tn), lambda i, j: (i, j)),
                   pl.BlockSpec((tm, d), lambda i, j: (i, 0))],
        out_shape=[jax.ShapeDtypeStruct((m, n), BF16), jax.ShapeDtypeStruct((m, d), BF16)],
        scratch_shapes=[pltpu.VMEM((2, tm, d), F32), pltpu.SemaphoreType.DMA((2, n_j))],
        compiler_params=_cparams("arbitrary", "arbitrary", vmem_mb=48),
        name="proj_norm",
    )(x2d, g_mix.reshape(1, d), w, gains.reshape(1, n))


def _proj_t_kernel(w_ref, h_ref, gq_ref, gk_ref, cos_ref, sin_ref, *rest, t_sub, n_cast):
    q_ref, k_ref, v_ref, va_ref = rest[n_cast:n_cast + 4]
    _run_cast_jobs(rest[:n_cast], rest[n_cast + 4:])
    n_q = q_ref.shape[0] // HEAD_DIM
    n_k = k_ref.shape[1] // HEAD_DIM
    n_v = v_ref.shape[0] // HEAD_DIM
    half = HEAD_DIM // 2

    def norm_rope(y, g_ref, tok):
        ms = jnp.mean(y * y, axis=0, keepdims=True)
        y = y * lax.rsqrt(ms + EPS) * g_ref[...]
        partner = jnp.concatenate([y[half:], y[:half]], axis=0)
        return y * cos_ref[:, tok] + partner * sin_ref[:, tok]

    def matmul(c):
        return jnp.dot(h_ref[c * t_sub:(c + 1) * t_sub, :], w_ref[...], preferred_element_type=F32)

    def epilogue(c, y):
        tok = slice(c * t_sub, (c + 1) * t_sub)
        head = lambda hh: y[:, hh * HEAD_DIM:(hh + 1) * HEAD_DIM].T
        for hh in range(n_q):
            q_ref[hh * HEAD_DIM:(hh + 1) * HEAD_DIM, tok] = norm_rope(head(hh), gq_ref, tok).astype(q_ref.dtype)
        for hh in range(n_k):
            kt = norm_rope(head(n_q + hh), gk_ref, tok)
            k_ref[tok, hh * HEAD_DIM:(hh + 1) * HEAD_DIM] = kt.T.astype(k_ref.dtype)
        for hh in range(n_v):
            v_ref[hh * HEAD_DIM:(hh + 1) * HEAD_DIM, tok] = head(n_q + n_k + hh).astype(v_ref.dtype)
        va_ref[tok, :] = y[:, (n_q + n_k + n_v) * HEAD_DIM:].astype(va_ref.dtype)

    _software_pipeline(h_ref.shape[0] // t_sub, matmul, epilogue)


def _proj_t(h, w, gq_col, gk_col, cos_tt, sin_tt, b, seq, cast_jobs, tm=1024, t_sub=128):
    m, d = h.shape
    n = w.shape[1]
    assert n == B_QW + 2 * B_KVW + A_W
    sb = seq // tm
    col = lambda g: jnp.broadcast_to(g[:, None], (HEAD_DIM, t_sub))
    lane_tile = lambda rows: pl.BlockSpec((None, rows, tm), lambda i: (i // sb, 0, i % sb))
    c_in, c_out, c_shapes = _cast_jobs(cast_jobs, m // tm, lambda i: i)
    return pl.pallas_call(
        functools.partial(_proj_t_kernel, t_sub=t_sub, n_cast=len(cast_jobs)),
        grid=(m // tm,),
        in_specs=[pl.BlockSpec((d, n), lambda i: (0, 0)),
                  pl.BlockSpec((tm, d), lambda i: (i, 0)),
                  pl.BlockSpec((HEAD_DIM, t_sub), lambda i: (0, 0)),
                  pl.BlockSpec((HEAD_DIM, t_sub), lambda i: (0, 0)),
                  pl.BlockSpec((HEAD_DIM, tm), lambda i: (0, i % sb)),
                  pl.BlockSpec((HEAD_DIM, tm), lambda i: (0, i % sb))] + c_in,
        out_specs=[lane_tile(B_QW),
                   pl.BlockSpec((tm, B_KVW), lambda i: (i, 0)),
                   lane_tile(B_KVW),
                   pl.BlockSpec((tm, A_W), lambda i: (i, 0))] + c_out,
        out_shape=[jax.ShapeDtypeStruct((b, B_QW, seq), BF16),
                   jax.ShapeDtypeStruct((m, B_KVW), BF16),
                   jax.ShapeDtypeStruct((b, B_KVW, seq), BF16),
                   jax.ShapeDtypeStruct((m, A_W), BF16)] + c_shapes,
        compiler_params=_cparams("parallel", vmem_mb=56),
        name="proj_t",
    )(w, h, col(gq_col), col(gk_col), cos_tt, sin_tt, *[src for src, _, _ in cast_jobs])


def _t5_bucket(rel):
    nb = REL_BUCKETS // 2
    ret = np.where(rel > 0, nb, 0)
    n = np.abs(rel)
    max_exact = nb // 2
    large = max_exact + (np.log(np.maximum(n, 1).astype(np.float32) / np.float32(max_exact))
                         / np.float32(math.log(REL_MAX_DIST / max_exact))
                         * np.float32(nb - max_exact)).astype(np.int32)
    large = np.minimum(large, nb - 1)
    return ret + np.where(n < max_exact, n, large)


def _a_bucket_index():
    qi = np.arange(A_QROWS, dtype=np.int32)[:, None]
    kj = np.arange(A_KWIN, dtype=np.int32)[None, :]
    out = []
    for _, dil in DIL_PAIRS:
        for off in range(A_NOFF):
            rel = kj - qi - A_RADIUS * off
            out.append(np.where(np.abs(rel) <= A_RADIUS, _t5_bucket(rel * dil), -1))
    return np.stack(out).astype(np.int32)


def _a_bias_kernel(tab_ref, bucket_ref, o_ref):
    g = pl.program_id(0)
    for off in range(A_NOFF):
        bk = bucket_ref[off]
        for hh in range(A_HEADS_PER_GROUP):
            acc = jnp.full(bk.shape, NEG, F32)
            for b in range(REL_BUCKETS):
                acc = jnp.where(bk == b, tab_ref[b, g * A_HEADS_PER_GROUP + hh], acc)
            o_ref[off, hh] = acc


def _a_bias(rel_bias):
    n = len(DIL_PAIRS) * A_NOFF
    return pl.pallas_call(
        _a_bias_kernel,
        grid=(len(DIL_PAIRS),),
        in_specs=[pl.BlockSpec(memory_space=pltpu.SMEM),
                  pl.BlockSpec((A_NOFF, A_QROWS, A_KWIN), lambda i: (i, 0, 0))],
        out_specs=pl.BlockSpec((A_NOFF, A_HEADS_PER_GROUP, A_QROWS, A_KWIN), lambda i: (i, 0, 0, 0)),
        out_shape=jax.ShapeDtypeStruct((n, A_HEADS_PER_GROUP, A_QROWS, A_KWIN), F32),
        compiler_params=_cparams("arbitrary"),
        name="a_bias",
    )(rel_bias, _a_bucket_index())


def _mixer_a_kernel(q_ref, k_ref, v_ref, bias_ref, oa_ref, stage, qstage, o_acc, lse_acc, *residue_kv, seq):
    t_rows = q_ref.shape[0]
    ti = pl.program_id(1)
    n_groups = len(DIL_PAIRS)

    def softmax(s):
        m = jnp.max(s, axis=-1, keepdims=True)
        p = jnp.exp(s - m)
        l = jnp.sum(p, axis=-1, keepdims=True)
        return p.astype(BF16), l, m + jnp.log(l)

    def run_group(gi, dil, kres, vres):
        sub_len = seq // dil
        lq = t_rows // dil
        gcols = lambda hh: slice(gi * A_OUT_W + hh * HEAD_DIM, gi * A_OUT_W + (hh + 1) * HEAD_DIM)
        if dil > 1:
            @pl.when(ti == 0)
            def _():
                for src, dst in ((k_ref, kres), (v_ref, vres)):
                    for hh in range(A_HEADS_PER_GROUP):
                        stage[...] = src[:, gcols(hh)].astype(F32)
                        for r in range(dil):
                            dst[r, :, hh * HEAD_DIM:(hh + 1) * HEAD_DIM] = (
                                stage[pl.ds(r, sub_len, stride=dil), :].astype(BF16))

            for hh in range(A_HEADS_PER_GROUP):
                qstage[hh] = q_ref[:, gcols(hh)].astype(F32)

        def scores(r, i, hh):
            q0 = ti * lq + i * A_QROWS
            ks = jnp.clip(q0 - A_RADIUS, 0, sub_len - A_KWIN)
            off = lax.shift_right_logical(q0 - ks, int(math.log2(A_RADIUS)))
            ks = pl.multiple_of(ks, A_RADIUS)
            if dil > 1:
                cols = slice(hh * HEAD_DIM, (hh + 1) * HEAD_DIM)
                rows = pl.ds(i * A_QROWS * dil + r, A_QROWS, stride=dil)
                q = qstage[hh, rows, :].astype(BF16)
                k = kres[r, pl.ds(ks, A_KWIN), cols]
                v = vres[r, pl.ds(ks, A_KWIN), cols]
            else:
                rows = pl.ds(i * A_QROWS, A_QROWS)
                q = q_ref[rows, gcols(hh)]
                k = k_ref[pl.ds(ks, A_KWIN), gcols(hh)]
                v = v_ref[pl.ds(ks, A_KWIN), gcols(hh)]
            s = lax.dot_general(q, k, _NT, preferred_element_type=F32) * SCALE + bias_ref[gi * A_NOFF + off, hh]
            return rows, s, v

        def fold(hh, rows, o, lse):
            lse = jnp.broadcast_to(lse, (A_QROWS, HEAD_DIM))
            if gi > 0:
                prev_o, prev_lse = o_acc[hh, rows, :], lse_acc[hh, rows, :]
                m = jnp.maximum(prev_lse, lse)
                w_prev, w_new = jnp.exp(prev_lse - m), jnp.exp(lse - m)
                den = w_prev + w_new
                o = (w_prev * prev_o + w_new * o) / den
                lse = m + jnp.log(den)
            o_acc[hh, rows, :] = o
            if gi + 1 < n_groups:
                lse_acc[hh, rows, :] = lse

        items = [(r, i, hh) for r in range(dil) for i in range(lq // A_QROWS) for hh in range(A_HEADS_PER_GROUP)]
        for b0 in range(0, len(items), A_BATCH):
            batch = items[b0:b0 + A_BATCH]
            staged = [scores(*it) for it in batch]
            probs = [softmax(s) for _, s, _ in staged]
            for (_, _, hh), (rows, _, v), (p, l, lse) in zip(batch, staged, probs):
                fold(hh, rows, jnp.dot(p, v, preferred_element_type=F32) / l, lse)

    strided = [gi for gi, (_, dil) in enumerate(DIL_PAIRS) if dil > 1]
    for gi, (_, dil) in enumerate(DIL_PAIRS):
        kres, vres = (residue_kv[2 * strided.index(gi):2 * strided.index(gi) + 2] if dil > 1 else (None, None))
        run_group(gi, dil, kres, vres)
    for hh in range(A_HEADS_PER_GROUP):
        oa_ref[:, hh * HEAD_DIM:(hh + 1) * HEAD_DIM] = o_acc[hh].astype(oa_ref.dtype)


def _mixer_a(qkn, va, bias_a, t_rows=2048):
    b, seq, _ = qkn.shape
    head_buf = lambda rows: pltpu.VMEM((A_HEADS_PER_GROUP, rows, HEAD_DIM), F32)
    scratch = [pltpu.VMEM((seq, HEAD_DIM), F32), head_buf(t_rows), head_buf(t_rows), head_buf(t_rows)]
    for _, dil in DIL_PAIRS:
        if dil > 1:
            scratch += [pltpu.VMEM((dil, seq // dil, A_OUT_W), BF16)] * 2
    return pl.pallas_call(
        functools.partial(_mixer_a_kernel, seq=seq),
        grid=(b, seq // t_rows),
        in_specs=[pl.BlockSpec((None, t_rows, A_W), lambda bi, ti: (bi, ti, 0)),
                  pl.BlockSpec((None, seq, A_W), lambda bi, ti: (bi, 0, 1)),
                  pl.BlockSpec((None, seq, A_W), lambda bi, ti: (bi, 0, 0)),
                  pl.BlockSpec(bias_a.shape, lambda bi, ti: (0, 0, 0, 0), pipeline_mode=pl.Buffered(1))],
        out_specs=pl.BlockSpec((None, t_rows, A_OUT_W), lambda bi, ti: (bi, ti, 0)),
        out_shape=jax.ShapeDtypeStruct((b, seq, A_OUT_W), BF16),
        scratch_shapes=scratch,
        compiler_params=_cparams("parallel", "arbitrary", vmem_mb=62),
        name="mixer_a",
    )(qkn, qkn, va, bias_a)


def _mixer_b_kernel(qt_ref, k_ref, vt_ref, *rest, tk, group, w, n_cast):
    o_ref, s_scr = rest[n_cast], rest[-1]
    _run_cast_jobs(rest[:n_cast], rest[n_cast + 1:-1])
    tq = qt_ref.shape[1]
    seq = k_ref.shape[0]
    n_chunks = seq // tk
    units = [(i, j) for i in range(group) for j in range(tq // w)]

    def pass_a(u, ci, m):
        i, j = units[u]
        qt = qt_ref[i * HEAD_DIM:(i + 1) * HEAD_DIM, j * w:(j + 1) * w]
        st = jnp.dot(k_ref[ci * tk:(ci + 1) * tk, :], qt, preferred_element_type=F32)
        s_scr[u % 2, ci * tk:(ci + 1) * tk, :] = st
        return jnp.maximum(m, jnp.max(st, axis=0, keepdims=True))

    def pass_b(u, ci, m, l, acc):
        pt = jnp.exp2(s_scr[u % 2, ci * tk:(ci + 1) * tk, :] - m)
        l = l + jnp.sum(pt, axis=0, keepdims=True)
        acc = acc + jnp.dot(vt_ref[:, ci * tk:(ci + 1) * tk], pt.astype(BF16), preferred_element_type=F32)
        return l, acc

    m_prev = None
    for s in range(len(units) + 1):
        m_cur = jnp.full((1, w), NEG, F32)
        l = jnp.zeros((1, w), F32)
        acc = jnp.zeros((HEAD_DIM, w), F32)
        for ci in range(n_chunks):
            if s < len(units):
                m_cur = pass_a(s, ci, m_cur)
            if s > 0:
                l, acc = pass_b(s - 1, ci, m_prev, l, acc)
        if s > 0:
            i, j = units[s - 1]
            o_ref[j * w:(j + 1) * w, i * HEAD_DIM:(i + 1) * HEAD_DIM] = (acc / l).T.astype(o_ref.dtype)
        m_prev = m_cur


def _mixer_b(qbt, kb, vbt, cast_jobs, tq=2048, tk=512, w=256):
    b, seq, _ = kb.shape
    group = B_Q_HEADS // B_KV_HEADS
    gw = group * HEAD_DIM
    nq = seq // tq
    n_steps = b * B_KV_HEADS * nq
    c_in, c_out, c_shapes = _cast_jobs(cast_jobs, n_steps, lambda bi, kv, qi: (bi * B_KV_HEADS + kv) * nq + qi)
    return pl.pallas_call(
        functools.partial(_mixer_b_kernel, tk=tk, group=group, w=w, n_cast=len(cast_jobs)),
        grid=(b, B_KV_HEADS, nq),
        scratch_shapes=[pltpu.VMEM((2, seq, w), F32)],
        in_specs=[pl.BlockSpec((None, gw, tq), lambda bi, kv, qi: (bi, kv, qi)),
                  pl.BlockSpec((None, seq, HEAD_DIM), lambda bi, kv, qi: (bi, 0, kv)),
                  pl.BlockSpec((None, HEAD_DIM, seq), lambda bi, kv, qi: (bi, kv, 0))] + c_in,
        out_specs=[pl.BlockSpec((None, tq, gw), lambda bi, kv, qi: (bi, qi, kv))] + c_out,
        out_shape=[jax.ShapeDtypeStruct((b, seq, B_QW), BF16)] + c_shapes,
        compiler_params=_cparams("parallel", "parallel", "arbitrary"),
        name="mixer_b",
    )(qbt, kb, vbt, *[src for src, _, _ in cast_jobs])


def _mixer_c_kernel(q_ref, mem_ref, gm_ref, w_ref, gk_ref, o_ref, k_scr, v_scr):
    @pl.when(pl.program_id(1) == 0)
    def _():
        x = mem_ref[...]
        ms = jnp.mean(x * x, axis=-1, keepdims=True)
        hm = (x * lax.rsqrt(ms + EPS) * gm_ref[...]).astype(BF16)
        kv = jnp.dot(hm, w_ref[...], preferred_element_type=F32)
        for hh in range(C_HEADS):
            sl = slice(hh * HEAD_DIM, (hh + 1) * HEAD_DIM)
            y = kv[:, sl]
            ms = jnp.mean(y * y, axis=-1, keepdims=True)
            k_scr[:, sl] = (y * lax.rsqrt(ms + EPS) * gk_ref[...]).astype(BF16)
        v_scr[...] = kv[:, C_W:].astype(BF16)

    for hh in range(C_HEADS):
        sl = slice(hh * HEAD_DIM, (hh + 1) * HEAD_DIM)
        s = lax.dot_general(q_ref[:, sl], k_scr[:, sl], _NT, preferred_element_type=F32) * SCALE
        m = jnp.max(s, axis=-1, keepdims=True)
        p = jnp.exp(s - m)
        l = jnp.sum(p, axis=-1, keepdims=True)
        o = jnp.dot(p.astype(BF16), v_scr[:, sl], preferred_element_type=F32) / l
        o_ref[:, sl] = o.astype(o_ref.dtype)


def _mixer_c(qkn, mem, g_mem, w_kv, g_kc, tq=1024):
    b, seq, _ = qkn.shape
    _, n_mem, d = mem.shape
    qc_blk = (2 * A_W) // C_W
    return pl.pallas_call(
        _mixer_c_kernel,
        grid=(b, seq // tq),
        in_specs=[pl.BlockSpec((None, tq, C_W), lambda bi, qi: (bi, qi, qc_blk)),
                  pl.BlockSpec((None, n_mem, d), lambda bi, qi: (bi, 0, 0)),
                  pl.BlockSpec((1, d), lambda bi, qi: (0, 0)),
                  pl.BlockSpec((d, 2 * C_W), lambda bi, qi: (0, 0)),
                  pl.BlockSpec((1, HEAD_DIM), lambda bi, qi: (0, 0))],
        out_specs=pl.BlockSpec((None, tq, C_W), lambda bi, qi: (bi, qi, 0)),
        out_shape=jax.ShapeDtypeStruct((b, seq, C_W), BF16),
        scratch_shapes=[pltpu.VMEM((n_mem, C_W), BF16), pltpu.VMEM((n_mem, C_W), BF16)],
        compiler_params=_cparams("parallel", "arbitrary"),
        name="mixer_c",
    )(qkn, mem, g_mem.reshape(1, d), w_kv, g_kc.reshape(1, HEAD_DIM))


def _merge_kernel(h_ref, wg0_ref, wg1_ref, wg2_ref, oa_ref, ob_ref, oc_ref, wa_ref, wb_ref, wc_ref, out_ref):
    h = h_ref[...]
    ga = jax.nn.sigmoid(jnp.dot(h, wg0_ref[...], preferred_element_type=F32))
    merged = ga * jnp.dot(oa_ref[...], wa_ref[...], preferred_element_type=F32)
    gb = jax.nn.sigmoid(jnp.dot(h, wg1_ref[...], preferred_element_type=F32))
    merged += gb * jnp.dot(ob_ref[...], wb_ref[...], preferred_element_type=F32)
    gc = jax.nn.sigmoid(jnp.dot(h, wg2_ref[...], preferred_element_type=F32))
    merged += gc * jnp.dot(oc_ref[...], wc_ref[...], preferred_element_type=F32)
    out_ref[...] = merged.astype(out_ref.dtype)


def _merge(h, w_gates, oa, ob, oc, wa, wb, wc, tm=1024, tn=1024):
    m, d = h.shape
    nj = d // tn
    row = lambda w: pl.BlockSpec((tm, w), lambda i, j: (i, 0))
    col = lambda k: pl.BlockSpec((k, tn), lambda i, j: (0, j))
    return pl.pallas_call(
        _merge_kernel,
        grid=(m // tm, nj),
        in_specs=[row(d), col(d), col(d), col(d), row(A_OUT_W), row(B_QW), row(C_W),
                  col(A_OUT_W), col(B_QW), col(C_W)],
        out_specs=pl.BlockSpec((tm, tn), lambda i, j: (i, j)),
        out_shape=jax.ShapeDtypeStruct((m, d), BF16),
        compiler_params=_cparams("parallel", "arbitrary", vmem_mb=62),
        name="merge",
    )(h, *w_gates, oa, ob, oc, wa, wb, wc)


def _out_proj_kernel(mg_ref, w_ref, x_ref, g_ref, x1_ref, h2_ref, *, r_sub):
    def matmul(c):
        return jnp.dot(mg_ref[c * r_sub:(c + 1) * r_sub, :], w_ref[...], preferred_element_type=F32)

    def epilogue(c, acc):
        rows = slice(c * r_sub, (c + 1) * r_sub)
        x1 = x_ref[rows, :] + acc
        x1_ref[rows, :] = x1
        ms = jnp.mean(x1 * x1, axis=-1, keepdims=True)
        h2_ref[rows, :] = (x1 * lax.rsqrt(ms + EPS) * g_ref[...]).astype(h2_ref.dtype)

    _software_pipeline(mg_ref.shape[0] // r_sub, matmul, epilogue)


def _out_proj(merged, w_o, x2d, g_ffn, tm=512, r_sub=512):
    m, d = x2d.shape
    return pl.pallas_call(
        functools.partial(_out_proj_kernel, r_sub=r_sub),
        grid=(m // tm,),
        in_specs=[pl.BlockSpec((tm, d), lambda i: (i, 0)),
                  pl.BlockSpec((d, d), lambda i: (0, 0)),
                  pl.BlockSpec((tm, d), lambda i: (i, 0)),
                  pl.BlockSpec((1, d), lambda i: (0, 0))],
        out_specs=[pl.BlockSpec((tm, d), lambda i: (i, 0))] * 2,
        out_shape=[jax.ShapeDtypeStruct((m, d), F32), jax.ShapeDtypeStruct((m, d), BF16)],
        compiler_params=_cparams("parallel"),
        name="out_proj",
    )(merged, w_o, x2d, g_ffn.reshape(1, d))


def _ffn_kernel(h_ref, wa_ref, wb_ref, wo_ref, x1_ref, out_ref, *, n_slabs):
    f = pl.program_id(1)
    slab_w = x1_ref.shape[1]

    @pl.when(f == 0)
    def _():
        out_ref[...] = jnp.zeros_like(out_ref)

    for s in range(n_slabs):
        @pl.when(f == s)
        def _():
            out_ref[:, s * slab_w:(s + 1) * slab_w] += x1_ref[...]

    h = h_ref[...]
    a = jnp.dot(h, wa_ref[...], preferred_element_type=F32)
    b = jnp.dot(h, wb_ref[...], preferred_element_type=F32)
    act = (a * jax.nn.sigmoid(a) * b).astype(BF16)
    out_ref[...] += jnp.dot(act, wo_ref[...], preferred_element_type=F32)


def _ffn(h2, w_a, w_b, w_out, x1, tm=1024, tf=512, slab_w=256):
    m, d = h2.shape
    d_ff = w_out.shape[0]
    nf = d_ff // tf
    n_slabs = d // slab_w
    assert n_slabs <= nf
    return pl.pallas_call(
        functools.partial(_ffn_kernel, n_slabs=n_slabs),
        grid=(m // tm, nf),
        in_specs=[pl.BlockSpec((tm, d), lambda i, f: (i, 0)),
                  pl.BlockSpec((d, tf), lambda i, f: (0, f)),
                  pl.BlockSpec((d, tf), lambda i, f: (0, f)),
                  pl.BlockSpec((tf, d), lambda i, f: (f, 0)),
                  pl.BlockSpec((tm, slab_w), lambda i, f: (i, jnp.minimum(f, n_slabs - 1)))],
        out_specs=pl.BlockSpec((tm, d), lambda i, f: (i, 0)),
        out_shape=jax.ShapeDtypeStruct((m, d), F32),
        compiler_params=_cparams("parallel", "arbitrary", vmem_mb=48),
        name="ffn",
    )(h2, w_a, w_b, w_out, x1)


def _rope_tables(seq):
    rows = seq // GRID_W
    r = np.repeat(np.arange(rows), GRID_W).astype(np.float64)
    c = np.tile(np.arange(GRID_W), rows).astype(np.float64)
    nf = HEAD_DIM // 4
    inv = ROPE_THETA ** (-np.arange(nf, dtype=np.float64) / nf)
    ang = np.concatenate([r[:, None] * inv, c[:, None] * inv], axis=-1)
    cos, sin = np.cos(ang).T, np.sin(ang).T
    return (np.concatenate([cos, cos], axis=0).astype(np.float32),
            np.concatenate([-sin, sin], axis=0).astype(np.float32))


def _layer(x2d, mem, bias_a, cos_t, sin_t, b, seq, g_mix, w_in, g_qa, g_ka, g_qb, g_kb, g_mem, w_mem_kv,
           g_qc, g_kc, w_br_a, w_br_b, w_br_c, w_o, g_ffn, w_ffn_in, w_ffn_out):
    d = x2d.shape[1]
    o_gt = IN_OFFSETS[-1]
    d_ff = w_ffn_out.shape[0]
    w_norm, w_b = _head_weights(w_in)
    g_norm = jnp.concatenate([jnp.tile(g_qa, A_HEADS), jnp.tile(g_ka, A_HEADS), jnp.tile(g_qc, C_HEADS)])
    deint = lambda g: g.reshape(HEAD_DIM // 2, 2).T.reshape(HEAD_DIM)

    qkn, h = _proj_norm(x2d, g_mix, w_norm, g_norm, lambda j: j, tn=2 * C_W)
    qkn = qkn.reshape(b, seq, -1)
    late_weights = [(w_ffn_out, 0, d), (w_o, 0, d), (w_br_a, 0, d), (w_br_b, 0, d), (w_br_c, 0, d),
                    (w_mem_kv, 0, 2 * C_W)]
    qbt, kb, vbt, va, w_out_b, w_o_b, wa_b, wb_b, wc_b, w_kv_b = _proj_t(
        h, w_b, deint(g_qb) * (SCALE * LOG2E), deint(g_kb), cos_t, sin_t, b, seq, late_weights)
    kb = kb.reshape(b, seq, B_KVW)
    va = va.reshape(b, seq, A_W)

    oa = _mixer_a(qkn, va, bias_a).reshape(b * seq, A_OUT_W)
    gate_jobs = [(w_in, int(o_gt) // d + br, d) for br in range(N_BRANCH)]
    ob, w_ffa, w_ffb, *w_gates = _mixer_b(qbt, kb, vbt, [(w_ffn_in, 0, d_ff), (w_ffn_in, 1, d_ff)] + gate_jobs)
    ob = ob.reshape(b * seq, B_QW)

    oc = _mixer_c(qkn, mem, g_mem, w_kv_b, g_kc).reshape(b * seq, C_W)

    merged = _merge(h, w_gates, oa, ob, oc, wa_b, wb_b, wc_b)
    x1, h2 = _out_proj(merged, w_o_b, x2d, g_ffn)
    return _ffn(h2, w_ffa, w_ffb, w_out_b, x1)


def kernel(x, mem, rel_bias, g_mix, w_in, g_qa, g_ka, g_qb, g_kb, g_mem, w_mem_kv, g_qc, g_kc,
           w_br_a, w_br_b, w_br_c, w_o, g_ffn, w_ffn_in, w_ffn_out):
    b, seq, d = x.shape
    depth = w_in.shape[0]
    cos_t, sin_t = _rope_tables(seq)
    bias_a = _a_bias(rel_bias)
    x2d = x.reshape(b * seq, d)
    for layer in range(depth):
        x2d = _layer(x2d, mem, bias_a, cos_t, sin_t, b, seq,
                     g_mix[layer], w_in[layer], g_qa[layer], g_ka[layer], g_qb[layer], g_kb[layer],
                     g_mem[layer], w_mem_kv[layer], g_qc[layer], g_kc[layer],
                     w_br_a[layer], w_br_b[layer], w_br_c[layer], w_o[layer], g_ffn[layer],
                     w_ffn_in[layer], w_ffn_out[layer])
    return x2d.reshape(b, seq, d)
```
